```python
import math
import jax
import jax.numpy as jnp
from jax import lax
import numpy as np

D_MODEL = 1024
BATCH = 16
SEQ = 256
DEPTH = 2
DEC_BATCH = 2
DEC_SEQ = 2048
PAST_LEN = 256

GRID_W = 64
HEAD_DIM = 64
NA_HEADS = 6
NA_WIN_ROWS = 8
NA_WIN_COLS = 16
ML_HEADS = 4
ML_CHUNK = 64
GQA_HEADS = 6
GQA_KV_HEADS = 2
Q_BLOCK = 128
ROPE_THETA = 10000.0
N_EXPERTS = 16
EC_CAPACITY = 2
EXPERT_HIDDEN = 2816
NORM_EPS = 1e-6
NA_W = NA_HEADS * HEAD_DIM
ML_W = ML_HEADS * HEAD_DIM
GQA_W = GQA_HEADS * HEAD_DIM
GQA_KV_W = GQA_KV_HEADS * HEAD_DIM
MIX_W = NA_W + ML_W + GQA_W
N_GATES = 4 * ML_HEADS
PROJ_SIZES = (NA_W, NA_W, NA_W, ML_W, ML_W, ML_W, ML_W, N_GATES, GQA_W, GQA_KV_W, GQA_KV_W)
PROJ_W = 3 * NA_W + 4 * ML_W + N_GATES + GQA_W + 2 * GQA_KV_W
DEEPNORM_ALPHA = (2 * DEPTH) ** 0.25
DEEPNORM_BETA = (8 * DEPTH) ** -0.25
ATTN_SCALE = HEAD_DIM ** -0.5
F32 = jnp.float32

kernel_name = 'hybrid_diffusion_na_mlstm_gqa_ec_step'


def layer_norm(x, g, b):
    xf = x.astype(F32)
    mu = jnp.mean(xf, -1, keepdims=True)
    var = jnp.mean(jnp.square(xf - mu), -1, keepdims=True)
    return ((xf - mu) * lax.rsqrt(var + NORM_EPS) * g + b).astype(x.dtype)


def rms_norm(x, g):
    xf = x.astype(F32)
    return (xf * lax.rsqrt(jnp.mean(xf * xf, -1, keepdims=True) + NORM_EPS) * g).astype(x.dtype)


def adaln_modulation(cond, ada_w, ada_b):
    m = jax.nn.silu(cond) @ ada_w + ada_b
    return jnp.split(m, 6, axis=-1)


def modulate(x, shift, scale):
    return x * (1.0 + scale[:, None, :]) + shift[:, None, :]


def heads(a):
    return a.reshape(*a.shape[:-1], a.shape[-1] // HEAD_DIM, HEAD_DIM)


def split_projection(z):
    parts, start = [], 0
    for size in PROJ_SIZES:
        parts.append(z[..., start:start + size])
        start += size
    return parts


def axial_rope(n_tokens):
    t = jnp.arange(n_tokens)
    row = (t // GRID_W).astype(F32)
    col = (t % GRID_W).astype(F32)
    half = HEAD_DIM // 2
    inv = ROPE_THETA ** (-jnp.arange(0, half, 2, dtype=F32) / half)
    ang_r = row[:, None] * inv
    ang_c = col[:, None] * inv
    ang = jnp.concatenate([ang_r, ang_r, ang_c, ang_c], -1)
    return jnp.cos(ang), jnp.sin(ang)


def rotate_half(x):
    x1, x2 = jnp.split(x, 2, -1)
    return jnp.concatenate([-x2, x1], -1)


def apply_axial_rope(x, cos, sin):
    xf = x.astype(F32)
    xr, xc = jnp.split(xf, 2, -1)
    rot = jnp.concatenate([rotate_half(xr), rotate_half(xc)], -1)
    return (xf * cos[:, None, :] + rot * sin[:, None, :]).astype(x.dtype)


def dense_attention(q, k, v):
    B, S, Hq, d = q.shape
    Hk = k.shape[2]
    qg = q.reshape(B, S, Hk, Hq // Hk, d)
    s = jnp.einsum('bqhgd,bkhd->bhgqk', qg, k).astype(F32) * ATTN_SCALE
    p = jax.nn.softmax(s, -1).astype(v.dtype)
    o = jnp.einsum('bhgqk,bkhd->bqhgd', p, v)
    return o.reshape(B, S, Hq * d)


def neighbourhood_attention(q, k, v, rpb, ctx_k, ctx_v):
    B, L, H, d = q.shape
    rows = L // GRID_W
    kr = min(NA_WIN_ROWS, rows)
    kc = NA_WIN_COLS
    r = jnp.arange(rows)
    row_start = jnp.clip(r - kr // 2, 0, rows - kr)
    row_idx = row_start[:, None] + jnp.arange(kr)
    cq = jnp.arange(GRID_W)
    col_start = jnp.clip(cq - kc // 2, 0, GRID_W - kc)
    ck = jnp.arange(GRID_W)
    in_win = (ck[None, :] >= col_start[:, None]) & (ck[None, :] < col_start[:, None] + kc)
    dr = row_idx - r[:, None]
    dc = jnp.clip(ck[None, :] - cq[:, None], -(kc - 1), kc - 1)
    bias = rpb[:, dr[:, None, :, None] + NA_WIN_ROWS - 1, dc[None, :, None, :] + NA_WIN_COLS - 1]
    bias = jnp.transpose(bias, (1, 2, 0, 3, 4)).astype(F32)
    qg = q.reshape(B, rows, GRID_W, H, d)
    kg = k.reshape(B, rows, GRID_W, H, d)[:, row_idx]
    vg = v.reshape(B, rows, GRID_W, H, d)[:, row_idx]
    s_loc = jnp.einsum('brwhd,brjkhd->brwhjk', qg, kg).astype(F32) * ATTN_SCALE + bias[None]
    s_loc = jnp.where(in_win[None, None, :, None, None, :], s_loc, -jnp.inf)
    s_ctx = jnp.einsum('brwhd,bphd->brwhp', qg, ctx_k).astype(F32) * ATTN_SCALE
    n_loc = kr * GRID_W
    s = jnp.concatenate([s_loc.reshape(B, rows, GRID_W, H, n_loc), s_ctx], -1)
    p = jax.nn.softmax(s, -1).astype(v.dtype)
    p_loc = p[..., :n_loc].reshape(B, rows, GRID_W, H, kr, GRID_W)
    p_ctx = p[..., n_loc:]
    o = jnp.einsum('brwhjk,brjkhd->brwhd', p_loc, vg) + jnp.einsum('brwhp,bphd->brwhd', p_ctx, ctx_v)
    return o.reshape(B, L, H * d)


def gqa_latent(q, k, v, ctx_k, ctx_v):
    B, L, Hq, d = q.shape
    Hk = k.shape[2]
    keys = jnp.concatenate([ctx_k, k], 1)
    vals = jnp.concatenate([ctx_v, v], 1)
    qb = q.reshape(B, L // Q_BLOCK, Q_BLOCK, Hk, Hq // Hk, d).transpose(1, 0, 2, 3, 4, 5)

    def block(qi):
        s = jnp.einsum('bqhgd,bkhd->bhgqk', qi, keys).astype(F32) * ATTN_SCALE
        p = jax.nn.softmax(s, -1).astype(vals.dtype)
        return jnp.einsum('bhgqk,bkhd->bqhgd', p, vals)

    o = lax.map(block, qb)
    return o.transpose(1, 0, 2, 3, 4, 5).reshape(B, L, Hq * d)


def mlstm_chunked(q, k, v, log_i, log_f, c0, n0, m0):
    B, H, L, d = q.shape
    T = ML_CHUNK
    nc = L // T

    def to_chunks(a):
        return jnp.moveaxis(a.reshape(B, H, nc, T, *a.shape[3:]), 2, 0)

    xs = tuple(to_chunks(a) for a in (q, k, v, log_i, log_f))
    causal = jnp.tril(jnp.ones((T, T), bool))

    def step(carry, inp):
        c, n, m = carry
        qc, kc, vc, ic, fc = inp
        b = jnp.cumsum(fc, -1)
        dmat = b[..., :, None] - b[..., None, :] + ic[..., None, :]
        dmat = jnp.where(causal, dmat, -jnp.inf)
        m_inter = b + m[..., None]
        m_j = jnp.maximum(m_inter, jnp.max(dmat, -1))
        w = jnp.exp(dmat - m_j[..., None])
        decay = jnp.exp(m_inter - m_j)
        qk = jnp.einsum('bhjd,bhsd->bhjs', qc, kc) * w
        num = decay[..., None] * jnp.einsum('bhjk,bhkv->bhjv', qc, c) + jnp.einsum('bhjs,bhsv->bhjv', qk, vc)
        den = decay * jnp.einsum('bhjk,bhk->bhj', qc, n) + jnp.sum(qk, -1)
        h = num / jnp.maximum(jnp.abs(den), jnp.exp(-m_j))[..., None]
        b_last = b[..., -1]
        g = b_last[..., None] - b + ic
        m_new = jnp.maximum(b_last + m, jnp.max(g, -1))
        ws = jnp.exp(g - m_new[..., None])
        d_last = jnp.exp(b_last + m - m_new)
        c_new = d_last[..., None, None] * c + jnp.einsum('bhs,bhsk,bhsv->bhkv', ws, kc, vc)
        n_new = d_last[..., None] * n + jnp.einsum('bhs,bhsk->bhk', ws, kc)
        return (c_new, n_new, m_new), h

    (c, n, m), h = lax.scan(step, (c0, n0, m0), xs)
    h = jnp.moveaxis(h, 0, 2).reshape(B, H, L, d)
    return h, c, n, m


def mlstm_mix(ml, norm_g, c0, n0, m0):
    q, k, v, o, g = ml
    gi_f, gf_f, gi_b, gf_b = [jnp.swapaxes(a, 1, 2) for a in jnp.split(g.astype(F32), 4, -1)]
    c0, n0, m0 = c0.astype(F32), n0.astype(F32), m0.astype(F32)
    h_f, cf, nf, mf = mlstm_chunked(q, k, v, gi_f, jax.nn.log_sigmoid(gf_f), c0[:, 0], n0[:, 0], m0[:, 0])
    flip = lambda a: jnp.flip(a, 2)
    h_b, cb, nb, mb = mlstm_chunked(flip(q), flip(k), flip(v), flip(gi_b), jax.nn.log_sigmoid(flip(gf_b)),
                                    c0[:, 1], n0[:, 1], m0[:, 1])
    h = h_f + flip(h_b)
    mu = jnp.mean(h, -1, keepdims=True)
    var = jnp.mean(jnp.square(h - mu), -1, keepdims=True)
    h = (h - mu) * lax.rsqrt(var + NORM_EPS)
    B, H, L, d = h.shape
    h = jnp.swapaxes(h, 1, 2).reshape(B, L, H * d) * norm_g
    out = (h * jax.nn.sigmoid(o.astype(F32))).astype(o.dtype)
    return out, (jnp.stack([cf, cb], 1), jnp.stack([nf, nb], 1), jnp.stack([mf, mb], 1))


def mixer_inputs(h, w_in, b_gate, qk_norm_g):
    z = h @ w_in
    na_q, na_k, na_v, ml_q, ml_k, ml_v, ml_o, ml_g, gq_q, gq_k, gq_v = split_projection(z)
    to_bhld = lambda a: jnp.swapaxes(heads(a), 1, 2).astype(F32)
    na = (heads(na_q), heads(na_k), heads(na_v))
    ml = (to_bhld(ml_q), to_bhld(ml_k) * ATTN_SCALE, to_bhld(ml_v), ml_o, ml_g + b_gate)
    gq = (rms_norm(heads(gq_q), qk_norm_g[0]), rms_norm(heads(gq_k), qk_norm_g[1]), heads(gq_v))
    return na, ml, gq


def context_mixers(h, w_in, b_gate, qk_norm_g, ml_norm_g):
    na, ml, gq = mixer_inputs(h, w_in, b_gate, qk_norm_g)
    B = h.shape[0]
    zc = jnp.zeros((B, 2, ML_HEADS, HEAD_DIM, HEAD_DIM), F32)
    zn = jnp.zeros((B, 2, ML_HEADS, HEAD_DIM), F32)
    zm = jnp.zeros((B, 2, ML_HEADS), F32)
    out_a = dense_attention(*na)
    out_b, (sc, sn, sm) = mlstm_mix(ml, ml_norm_g, zc, zn, zm)
    out_c = dense_attention(*gq)
    mixed = jnp.concatenate([out_a, out_b, out_c], -1)
    return mixed, (na[1], na[2], gq[1], gq[2], sc, sn, sm)


def latent_mixers(h, w_in, b_gate, qk_norm_g, ml_norm_g, rpb, ck_a, cv_a, ck_c, cv_c, sc, sn, sm):
    na, ml, gq = mixer_inputs(h, w_in, b_gate, qk_norm_g)
    out_a = neighbourhood_attention(na[0], na[1], na[2], rpb, ck_a, cv_a)
    out_b, _ = mlstm_mix(ml, ml_norm_g, sc, sn, sm)
    cos, sin = axial_rope(h.shape[1])
    out_c = gqa_latent(apply_axial_rope(gq[0], cos, sin), apply_axial_rope(gq[1], cos, sin), gq[2], ck_c, cv_c)
    return jnp.concatenate([out_a, out_b, out_c], -1)


def expert_choice_ffn(x, router_w, w_gate, w_up, w_down):
    n = x.shape[0]
    cap = EC_CAPACITY * n // N_EXPERTS
    aff = jax.nn.softmax((x @ router_w).astype(F32), -1)
    g, idx = lax.top_k(aff.T, cap)
    xe = x[idx]
    hid = jax.nn.silu(jnp.einsum('ecd,edf->ecf', xe, w_gate)) * jnp.einsum('ecd,edf->ecf', xe, w_up)
    ye = jnp.einsum('ecf,efd->ecd', hid, w_down) * g[..., None].astype(x.dtype)
    return jnp.zeros_like(x).at[idx.reshape(-1)].add(ye.reshape(-1, x.shape[-1]))


def finish_layer(x, mixed, mods, w_out, ln_g, ln_b, router_w, w_gate, w_up, w_down):
    _, _, gate1, shift2, scale2, gate2 = mods
    x = layer_norm(DEEPNORM_ALPHA * x + gate1[:, None, :] * (mixed @ w_out), ln_g[0], ln_b[0])
    h = modulate(x, shift2, scale2)
    ffn = jax.vmap(expert_choice_ffn, in_axes=(0, None, None, None, None))(h, router_w, w_gate, w_up, w_down)
    return layer_norm(DEEPNORM_ALPHA * x + gate2[:, None, :] * ffn, ln_g[1], ln_b[1])


def setup_inputs(seed: int = 0) -> dict:
    key = jax.random.key(seed)
    ks = jax.random.split(key, 32)

    def nrm(i, shape, scale=1.0):
        return scale * jax.random.normal(ks[i], shape, F32)

    f_bias = jnp.linspace(3.0, 6.0, ML_HEADS, dtype=F32)
    b_gate = jnp.concatenate([nrm(13, (DEPTH, ML_HEADS), 0.1), f_bias + nrm(14, (DEPTH, ML_HEADS), 0.1),
                              nrm(15, (DEPTH, ML_HEADS), 0.1), f_bias + nrm(16, (DEPTH, ML_HEADS), 0.1)], -1)
    return {
        'x_prompt': nrm(0, (BATCH, SEQ, D_MODEL)),
        'x_sample': nrm(1, (DEC_BATCH, DEC_SEQ, D_MODEL)),
        'c': nrm(2, (DEC_BATCH, D_MODEL)),
        'cache_na_k': nrm(3, (DEC_BATCH, DEPTH, PAST_LEN, NA_HEADS, HEAD_DIM)),
        'cache_na_v': nrm(4, (DEC_BATCH, DEPTH, PAST_LEN, NA_HEADS, HEAD_DIM)),
        'cache_gqa_k': nrm(5, (DEC_BATCH, DEPTH, PAST_LEN, GQA_KV_HEADS, HEAD_DIM)),
        'cache_gqa_v': nrm(6, (DEC_BATCH, DEPTH, PAST_LEN, GQA_KV_HEADS, HEAD_DIM)),
        'state_mlstm_c': nrm(7, (DEC_BATCH, DEPTH, 2, ML_HEADS, HEAD_DIM, HEAD_DIM), 0.1),
        'state_mlstm_n': nrm(8, (DEC_BATCH, DEPTH, 2, ML_HEADS, HEAD_DIM), 0.1),
        'state_mlstm_m': nrm(9, (DEC_BATCH, DEPTH, 2, ML_HEADS)),
        'c_ctx': nrm(10, (D_MODEL,)),
        'ada_w': nrm(11, (DEPTH, D_MODEL, 6 * D_MODEL), 0.5 * D_MODEL ** -0.5),
        'ada_b': nrm(12, (DEPTH, 6 * D_MODEL), 0.02),
        'w_in': nrm(17, (DEPTH, D_MODEL, PROJ_W), D_MODEL ** -0.5),
        'b_gate': b_gate,
        'w_out': nrm(18, (DEPTH, MIX_W, D_MODEL), DEEPNORM_BETA * MIX_W ** -0.5),
        'na_rpb': nrm(19, (DEPTH, NA_HEADS, 2 * NA_WIN_ROWS - 1, 2 * NA_WIN_COLS - 1), 0.1),
        'qk_norm_g': 1.0 + nrm(20, (DEPTH, 2, HEAD_DIM), 0.1),
        'ml_norm_g': 1.0 + nrm(21, (DEPTH, ML_W), 0.1),
        'ln_g': 1.0 + nrm(22, (DEPTH, 2, D_MODEL), 0.1),
        'ln_b': nrm(23, (DEPTH, 2, D_MODEL), 0.02),
        'router_w': nrm(24, (DEPTH, D_MODEL, N_EXPERTS), D_MODEL ** -0.5),
        'w_gate': nrm(25, (DEPTH, N_EXPERTS, D_MODEL, EXPERT_HIDDEN), D_MODEL ** -0.5),
        'w_up': nrm(26, (DEPTH, N_EXPERTS, D_MODEL, EXPERT_HIDDEN), D_MODEL ** -0.5),
        'w_down': nrm(27, (DEPTH, N_EXPERTS, EXPERT_HIDDEN, D_MODEL), DEEPNORM_BETA * EXPERT_HIDDEN ** -0.5),
    }


def reference(x_prompt, x_sample, c, cache_na_k, cache_na_v, cache_gqa_k, cache_gqa_v,
              state_mlstm_c, state_mlstm_n, state_mlstm_m, c_ctx, ada_w, ada_b, w_in, b_gate,
              w_out, na_rpb, qk_norm_g, ml_norm_g, ln_g, ln_b, router_w, w_gate, w_up, w_down):
    xp = x_prompt
    collected = [[] for _ in range(7)]
    for l in range(DEPTH):
        mods = adaln_modulation(c_ctx[None, :], ada_w[l], ada_b[l])
        h = modulate(xp, mods[0], mods[1])
        mixed, ctx_tensors = context_mixers(h, w_in[l], b_gate[l], qk_norm_g[l], ml_norm_g[l])
        for lst, t in zip(collected, ctx_tensors):
            lst.append(t)
        xp = finish_layer(xp, mixed, mods, w_out[l], ln_g[l], ln_b[l], router_w[l], w_gate[l], w_up[l], w_down[l])
    y_prompt = xp
    new_na_k, new_na_v, new_gqa_k, new_gqa_v, new_mlstm_c, new_mlstm_n, new_mlstm_m = [
        jnp.stack(lst, 1).astype(x_prompt.dtype) for lst in collected]

    xs = x_sample
    for l in range(DEPTH):
        mods = adaln_modulation(c, ada_w[l], ada_b[l])
        h = modulate(xs, mods[0], mods[1])
        mixed = latent_mixers(h, w_in[l], b_gate[l], qk_norm_g[l], ml_norm_g[l], na_rpb[l],
                              cache_na_k[:, l], cache_na_v[:, l], cache_gqa_k[:, l], cache_gqa_v[:, l],
                              state_mlstm_c[:, l], state_mlstm_n[:, l], state_mlstm_m[:, l])
        xs = finish_layer(xs, mixed, mods, w_out[l], ln_g[l], ln_b[l], router_w[l], w_gate[l], w_up[l], w_down[l])
    y_sample = xs
    return (y_prompt, y_sample, new_na_k, new_na_v, new_gqa_k, new_gqa_v, new_mlstm_c, new_mlstm_n, new_mlstm_m)
```

```python
import functools

import jax
import jax.numpy as jnp
import numpy as np
from jax import lax
from jax.experimental import pallas as pl
from jax.experimental.pallas import tpu as pltpu

D_MODEL = 1024
BATCH = 16
SEQ = 256
DEPTH = 2
DEC_BATCH = 2
DEC_SEQ = 2048
PAST_LEN = 256
GRID_W = 64
GRID_H = DEC_SEQ // GRID_W
HEAD_DIM = 64
NA_HEADS = 6
NA_WIN_ROWS = 8
NA_WIN_COLS = 16
ML_HEADS = 4
GQA_HEADS = 6
GQA_KV_HEADS = 2
ROPE_THETA = 10000.0
N_EXPERTS = 16
EC_CAPACITY = 2
EXPERT_HIDDEN = 2816
NORM_EPS = 1e-6
NA_W = NA_HEADS * HEAD_DIM
ML_W = ML_HEADS * HEAD_DIM
GQA_W = GQA_HEADS * HEAD_DIM
GQA_KV_W = GQA_KV_HEADS * HEAD_DIM
N_GATES = 4 * ML_HEADS
DEEPNORM_ALPHA = (2 * DEPTH) ** 0.25
ATTN_SCALE = HEAD_DIM ** -0.5
F32 = jnp.float32
BF16 = jnp.bfloat16

LANE = 128
N_CTX_TOK = BATCH * SEQ
N_LAT_TOK = DEC_BATCH * DEC_SEQ
N_TOK = N_CTX_TOK + N_LAT_TOK
NEG = -1e30

OFF_NA_Q = 0
OFF_NA_K = OFF_NA_Q + NA_W
OFF_NA_V = OFF_NA_K + NA_W
OFF_ML_Q = OFF_NA_V + NA_W
OFF_ML_K = OFF_ML_Q + ML_W
OFF_ML_V = OFF_ML_K + ML_W
OFF_ML_O = OFF_ML_V + ML_W
OFF_ML_G = OFF_ML_O + ML_W
OFF_GQ_Q = OFF_ML_G + LANE
OFF_GQ_K = OFF_GQ_Q + GQA_W
OFF_GQ_V = OFF_GQ_K + GQA_KV_W
PROJ_PAD = OFF_GQ_V + GQA_KV_W

TM = 512
ML_CHUNK = 256
ML_NB = 8
EXP_TF = 256
CAP_CTX = EC_CAPACITY * SEQ // N_EXPERTS
CAP_LAT = EC_CAPACITY * DEC_SEQ // N_EXPERTS
SLOTS_CTX = BATCH * CAP_CTX
SLOTS_LAT = DEC_BATCH * CAP_LAT
VMEM_LIMIT = 56 * 1024 * 1024


def _cparams(*sem):
    return pltpu.CompilerParams(dimension_semantics=sem, vmem_limit_bytes=VMEM_LIMIT)


def _mod_row(i):
    n_ctx_tiles = N_CTX_TOK // TM
    per_lat = DEC_SEQ // TM
    return jnp.where(i < n_ctx_tiles, 0, 1 + (i - n_ctx_tiles) // per_lat)


def _sigmoid(x):
    return 1.0 / (1.0 + jnp.exp(-x))


def _dot(a, b):
    return jnp.dot(a, b, preferred_element_type=F32)


def _dot_nt(a, b):
    return lax.dot_general(a, b, (((1,), (1,)), ((), ())), preferred_element_type=F32)


def _adaln_kernel(c_ref, w_ref, b_ref, o_ref):
    c = c_ref[...]
    s = c * _sigmoid(c)
    o_ref[0] = _dot(s.astype(BF16), w_ref[0].astype(BF16)) + b_ref[0]


def adaln(cond8, ada_w, ada_b):
    tn = 1536
    return pl.pallas_call(
        _adaln_kernel,
        grid=(DEPTH, 6 * D_MODEL // tn),
        in_specs=[
            pl.BlockSpec((8, D_MODEL), lambda l, j: (0, 0)),
            pl.BlockSpec((1, D_MODEL, tn), lambda l, j: (l, 0, j)),
            pl.BlockSpec((1, 1, tn), lambda l, j: (l, 0, j)),
        ],
        out_specs=pl.BlockSpec((1, 8, tn), lambda l, j: (l, 0, j)),
        out_shape=jax.ShapeDtypeStruct((DEPTH, 8, 6 * D_MODEL), F32),
        compiler_params=_cparams("arbitrary", "arbitrary"),
        name="adaln",
    )(cond8, ada_w, ada_b.reshape(DEPTH, 1, 6 * D_MODEL))


def _inproj_kernel(x_ref, sh_ref, sc_ref, w_ref, b_ref, z_ref):
    h = x_ref[...] * (1.0 + sc_ref[0]) + sh_ref[0]
    z_ref[...] = _dot(h.astype(BF16), w_ref[...]) + b_ref[...]


def in_projection(x, mods3, w_pad, bias_row):
    return pl.pallas_call(
        _inproj_kernel,
        grid=(N_TOK // TM,),
        in_specs=[
            pl.BlockSpec((TM, D_MODEL), lambda i: (i, 0)),
            pl.BlockSpec((1, 1, D_MODEL), lambda i: (_mod_row(i), 0, 0)),
            pl.BlockSpec((1, 1, D_MODEL), lambda i: (_mod_row(i), 0, 1)),
            pl.BlockSpec((D_MODEL, PROJ_PAD), lambda i: (0, 0)),
            pl.BlockSpec((1, PROJ_PAD), lambda i: (0, 0)),
        ],
        out_specs=pl.BlockSpec((TM, PROJ_PAD), lambda i: (i, 0)),
        out_shape=jax.ShapeDtypeStruct((N_TOK, PROJ_PAD), F32),
        compiler_params=_cparams("arbitrary"),
        name="in_projection",
    )(x, mods3, mods3, w_pad, bias_row)


def _headnorm_kernel(x_ref, g_ref, *rest, rope):
    x = x_ref[...]
    ms = jnp.mean(x * x, axis=-1, keepdims=True)
    y = x * lax.rsqrt(ms + NORM_EPS) * g_ref[...]
    if rope:
        cos_ref, sin_ref, o_ref = rest
        q = HEAD_DIM // 4
        rot = jnp.concatenate([-y[..., q:2 * q], y[..., 0:q], -y[..., 3 * q:4 * q], y[..., 2 * q:3 * q]], axis=-1)
        y = y * cos_ref[...] + rot * sin_ref[...]
    else:
        (o_ref,) = rest
    o_ref[...] = y


def head_norm(x, gain, cos=None, sin=None):
    n, s, _ = x.shape
    nb = max(1, min(n, 2048 // s))
    while n % nb:
        nb -= 1
    rope = cos is not None
    in_specs = [
        pl.BlockSpec((nb, s, HEAD_DIM), lambda i: (i, 0, 0)),
        pl.BlockSpec((1, 1, HEAD_DIM), lambda i: (0, 0, 0)),
    ]
    args = [x, gain.reshape(1, 1, HEAD_DIM)]
    if rope:
        in_specs += [pl.BlockSpec((1, s, HEAD_DIM), lambda i: (0, 0, 0))] * 2
        args += [cos.reshape(1, s, HEAD_DIM), sin.reshape(1, s, HEAD_DIM)]
    return pl.pallas_call(
        functools.partial(_headnorm_kernel, rope=rope),
        grid=(n // nb,),
        in_specs=in_specs,
        out_specs=pl.BlockSpec((nb, s, HEAD_DIM), lambda i: (i, 0, 0)),
        out_shape=jax.ShapeDtypeStruct(x.shape, F32),
        compiler_params=_cparams("arbitrary"),
        name="head_norm_rope" if rope else "head_norm",
    )(*args)


def _attn_kernel(q_ref, k_ref, v_ref, o_ref):
    q = q_ref[0, 0].astype(BF16)
    k = k_ref[0, 0].astype(BF16)
    v = v_ref[0, 0].astype(BF16)
    s = _dot_nt(q, k) * ATTN_SCALE
    m = jnp.max(s, axis=-1, keepdims=True)
    p = jnp.exp(s - m)
    l = jnp.sum(p, axis=-1, keepdims=True)
    o_ref[0, 0] = _dot(p.astype(BF16), v) * (1.0 / l)


def dense_attention(q, k, v, tq):
    b, hq, sq, _ = q.shape
    hk, sk = k.shape[1], k.shape[2]
    g = hq // hk
    return pl.pallas_call(
        _attn_kernel,
        grid=(b, hq, sq // tq),
        in_specs=[
            pl.BlockSpec((1, 1, tq, HEAD_DIM), lambda bi, h, i: (bi, h, i, 0)),
            pl.BlockSpec((1, 1, sk, HEAD_DIM), lambda bi, h, i: (bi, h // g, 0, 0)),
            pl.BlockSpec((1, 1, sk, HEAD_DIM), lambda bi, h, i: (bi, h // g, 0, 0)),
        ],
        out_specs=pl.BlockSpec((1, 1, tq, HEAD_DIM), lambda bi, h, i: (bi, h, i, 0)),
        out_shape=jax.ShapeDtypeStruct(q.shape, F32),
        compiler_params=_cparams("arbitrary", "arbitrary", "arbitrary"),
        name="dense_attention",
    )(q, k, v)


NA_ROWS_PER_STEP = 4
NA_LOCAL = NA_WIN_ROWS * GRID_W


def _na_kernel(q_ref, k_ref, v_ref, ck_ref, cv_ref, bias_ref, o_ref):
    step = pl.program_id(2)
    ck = ck_ref[0, 0].astype(BF16)
    cv = cv_ref[0, 0].astype(BF16)
    for t in range(NA_ROWS_PER_STEP):
        r = step * NA_ROWS_PER_STEP + t
        rs = jnp.clip(r - NA_WIN_ROWS // 2, 0, GRID_H - NA_WIN_ROWS)
        start = pl.multiple_of(rs * GRID_W, GRID_W)
        q = q_ref[0, 0, t * GRID_W:(t + 1) * GRID_W, :].astype(BF16)
        kl = k_ref[0, 0, pl.ds(start, NA_LOCAL), :].astype(BF16)
        vl = v_ref[0, 0, pl.ds(start, NA_LOCAL), :].astype(BF16)
        s_loc = _dot_nt(q, kl) * ATTN_SCALE + bias_ref[0, r - rs]
        s_ctx = _dot_nt(q, ck) * ATTN_SCALE
        m = jnp.maximum(jnp.max(s_loc, axis=-1, keepdims=True), jnp.max(s_ctx, axis=-1, keepdims=True))
        p_loc = jnp.exp(s_loc - m)
        p_ctx = jnp.exp(s_ctx - m)
        l = jnp.sum(p_loc, axis=-1, keepdims=True) + jnp.sum(p_ctx, axis=-1, keepdims=True)
        o = _dot(p_loc.astype(BF16), vl) + _dot(p_ctx.astype(BF16), cv)
        o_ref[0, 0, t * GRID_W:(t + 1) * GRID_W, :] = o * (1.0 / l)


def na_bias_table(rpb):
    d = np.arange(NA_WIN_ROWS)
    j = np.arange(NA_WIN_ROWS)
    dr = j[None, :] - d[:, None] + NA_WIN_ROWS - 1
    cq = np.arange(GRID_W)
    ck = np.arange(GRID_W)
    dc = np.clip(ck[None, :] - cq[:, None], -(NA_WIN_COLS - 1), NA_WIN_COLS - 1) + NA_WIN_COLS - 1
    col_start = np.clip(cq - NA_WIN_COLS // 2, 0, GRID_W - NA_WIN_COLS)
    in_win = (ck[None, :] >= col_start[:, None]) & (ck[None, :] < col_start[:, None] + NA_WIN_COLS)
    row_sel = (dr[:, :, None] == np.arange(2 * NA_WIN_ROWS - 1)).astype(np.float32)
    col_sel = (dc[:, :, None] == np.arange(2 * NA_WIN_COLS - 1)).astype(np.float32)
    rows = jnp.einsum("djr,hrc->hdjc", row_sel, rpb.astype(F32), precision=lax.Precision.HIGHEST)
    bias = jnp.einsum("hdjc,qkc->hdqjk", rows, col_sel, precision=lax.Precision.HIGHEST)
    bias = jnp.where(in_win[None, None, :, None, :], bias, NEG)
    return bias.reshape(NA_HEADS, NA_WIN_ROWS, GRID_W, NA_LOCAL)


def neighbourhood_attention(q, k, v, ck, cv, bias):
    b, h, l, _ = q.shape
    tq = NA_ROWS_PER_STEP * GRID_W
    full = lambda bi, hi, i: (bi, hi, 0, 0)
    return pl.pallas_call(
        _na_kernel,
        grid=(b, h, l // tq),
        in_specs=[
            pl.BlockSpec((1, 1, tq, HEAD_DIM), lambda bi, hi, i: (bi, hi, i, 0)),
            pl.BlockSpec((1, 1, l, HEAD_DIM), full),
            pl.BlockSpec((1, 1, l, HEAD_DIM), full),
            pl.BlockSpec((1, 1, PAST_LEN, HEAD_DIM), full),
            pl.BlockSpec((1, 1, PAST_LEN, HEAD_DIM), full),
            pl.BlockSpec((1, NA_WIN_ROWS, GRID_W, NA_LOCAL), lambda bi, hi, i: (hi, 0, 0, 0)),
        ],
        out_specs=pl.BlockSpec((1, 1, tq, HEAD_DIM), lambda bi, hi, i: (bi, hi, i, 0)),
        out_shape=jax.ShapeDtypeStruct(q.shape, F32),
        compiler_params=_cparams("arbitrary", "arbitrary", "arbitrary"),
        name="neighbourhood_attention",
    )(q, k, v, ck, cv, bias)


def _log_sigmoid(x):
    return jnp.minimum(x, 0.0) - jnp.log1p(jnp.exp(-jnp.abs(x)))


def _mlstm_chunk(q, kt, v, ig, fg, c, n_row, m):
    t = q.shape[0]
    lf = _log_sigmoid(fg)
    jj = lax.broadcasted_iota(jnp.int32, (t, t), 0)
    ss = lax.broadcasted_iota(jnp.int32, (t, t), 1)
    causal = ss <= jj
    b_col = jnp.sum(jnp.where(causal, lf, 0.0), axis=1, keepdims=True)
    b_row = jnp.sum(jnp.where(jj == ss, b_col, 0.0), axis=0, keepdims=True)
    dmat = jnp.where(causal, b_col - b_row + ig, NEG)
    m_inter = b_col + m
    m_j = jnp.maximum(m_inter, jnp.max(dmat, axis=1, keepdims=True))
    w = jnp.exp(dmat - m_j)
    decay = jnp.exp(m_inter - m_j)
    qb = q.astype(BF16)
    vb = v.astype(BF16)
    qk = _dot(qb, kt.astype(BF16)) * w
    num = decay * _dot(qb, c.astype(BF16)) + _dot(qk.astype(BF16), vb)
    den = decay * jnp.sum(q * n_row, axis=1, keepdims=True) + jnp.sum(qk, axis=1, keepdims=True)
    h = num * (1.0 / jnp.maximum(jnp.abs(den), jnp.exp(-m_j)))
    b_last = b_col[t - 1:t, :]
    g_row = b_last - b_row + ig
    m_new = jnp.maximum(b_last + m, jnp.max(g_row, axis=1, keepdims=True))
    ws = jnp.exp(g_row - m_new)
    d_last = jnp.exp(b_last + m - m_new)
    kw = kt * ws
    c_new = d_last * c + _dot(kw.astype(BF16), vb)
    n_add_col = jnp.sum(kw, axis=1, keepdims=True)
    di = lax.broadcasted_iota(jnp.int32, (HEAD_DIM, HEAD_DIM), 0)
    dj = lax.broadcasted_iota(jnp.int32, (HEAD_DIM, HEAD_DIM), 1)
    n_add_row = jnp.sum(jnp.where(di == dj, n_add_col, 0.0), axis=0, keepdims=True)
    n_new = d_last * n_row + n_add_row
    return h, c_new, n_new, m_new


def _mlstm_kernel(q_ref, kt_ref, v_ref, ig_ref, fg_ref, c0_ref, n0_ref, m0_ref, h_ref, c_ref, n_ref, m_ref):
    chunk = pl.program_id(1)

    @pl.when(chunk == 0)
    def _():
        c_ref[...] = c0_ref[...]
        n_ref[...] = n0_ref[...]
        m_ref[...] = m0_ref[...]

    for i in range(ML_NB):
        h, c_new, n_new, m_new = _mlstm_chunk(
            q_ref[i], kt_ref[i] * ATTN_SCALE, v_ref[i], ig_ref[i], fg_ref[i], c_ref[i], n_ref[i], m_ref[i])
        h_ref[i] = h
        c_ref[i] = c_new
        n_ref[i] = n_new
        m_ref[i] = m_new


def mlstm(q, kt, v, ig, fg, c0, n0, m0):
    n, l, _ = q.shape
    t = ML_CHUNK
    seq = lambda i, c: (i, c, 0)
    gate = lambda i, c: (i, 0, c)
    state = lambda i, c: (i, 0, 0)
    return pl.pallas_call(
        _mlstm_kernel,
        grid=(n // ML_NB, l // t),
        in_specs=[
            pl.BlockSpec((ML_NB, t, HEAD_DIM), seq),
            pl.BlockSpec((ML_NB, HEAD_DIM, t), gate),
            pl.BlockSpec((ML_NB, t, HEAD_DIM), seq),
            pl.BlockSpec((ML_NB, 1, t), gate),
            pl.BlockSpec((ML_NB, 1, t), gate),
            pl.BlockSpec((ML_NB, HEAD_DIM, HEAD_DIM), state),
            pl.BlockSpec((ML_NB, 1, HEAD_DIM), state),
            pl.BlockSpec((ML_NB, 1, 1), state),
        ],
        out_specs=[
            pl.BlockSpec((ML_NB, t, HEAD_DIM), seq),
            pl.BlockSpec((ML_NB, HEAD_DIM, HEAD_DIM), state),
            pl.BlockSpec((ML_NB, 1, HEAD_DIM), state),
            pl.BlockSpec((ML_NB, 1, 1), state),
        ],
        out_shape=[
            jax.ShapeDtypeStruct((n, l, HEAD_DIM), F32),
            jax.ShapeDtypeStruct((n, HEAD_DIM, HEAD_DIM), F32),
            jax.ShapeDtypeStruct((n, 1, HEAD_DIM), F32),
            jax.ShapeDtypeStruct((n, 1, 1), F32),
        ],
        compiler_params=_cparams("arbitrary", "arbitrary"),
        name="mlstm",
    )(q, kt, v, ig, fg, c0, n0, m0)


def _mlstm_post_kernel(hf_ref, hb_ref, oa_ref, ob_ref, g_ref, o_ref):
    outs = []
    for hh in range(ML_HEADS):
        x = hf_ref[0, hh] + hb_ref[0, hh]
        mu = jnp.mean(x, axis=-1, keepdims=True)
        xc = x - mu
        var = jnp.mean(xc * xc, axis=-1, keepdims=True)
        outs.append(xc * lax.rsqrt(var + NORM_EPS))
    y = jnp.concatenate(outs, axis=-1) * g_ref[...]
    og = jnp.concatenate([oa_ref[...], ob_ref[...]], axis=-1)
    o_ref[...] = y * _sigmoid(og)


def mlstm_post(hf, hb, z, norm_g, row0):
    b, _, l, _ = hf.shape
    tl = 256
    per = l // tl
    r0 = row0 // tl
    ca = OFF_ML_O // LANE
    hspec = pl.BlockSpec((1, ML_HEADS, tl, HEAD_DIM), lambda i: (i // per, 0, i % per, 0))
    return pl.pallas_call(
        _mlstm_post_kernel,
        grid=(b * per,),
        in_specs=[
            hspec, hspec,
            pl.BlockSpec((tl, LANE), lambda i: (r0 + i, ca)),
            pl.BlockSpec((tl, LANE), lambda i: (r0 + i, ca + 1)),
            pl.BlockSpec((1, ML_W), lambda i: (0, 0)),
        ],
        out_specs=pl.BlockSpec((tl, ML_W), lambda i: (i, 0)),
        out_shape=jax.ShapeDtypeStruct((b * l, ML_W), F32),
        compiler_params=_cparams("arbitrary"),
        name="mlstm_post",
    )(hf, hb, z, z, norm_g.reshape(1, ML_W))


def _layer_norm(u, g, b):
    mu = jnp.mean(u, axis=-1, keepdims=True)
    uc = u - mu
    var = jnp.mean(uc * uc, axis=-1, keepdims=True)
    return uc * lax.rsqrt(var + NORM_EPS) * g + b


def _outproj_kernel(x_ref, a_ref, b_ref, c_ref, w_ref, g1_ref, sh2_ref, sc2_ref, lng_ref, lnb_ref, rw_ref,
                    x1_ref, h2_ref, lg_ref):
    mixed = jnp.concatenate([a_ref[...], b_ref[...], c_ref[...]], axis=-1).astype(BF16)
    y = _dot(mixed, w_ref[...])
    x1 = _layer_norm(DEEPNORM_ALPHA * x_ref[...] + g1_ref[0] * y, lng_ref[...], lnb_ref[...])
    h2 = (x1 * (1.0 + sc2_ref[0]) + sh2_ref[0]).astype(BF16)
    x1_ref[...] = x1
    h2_ref[...] = h2
    lg_ref[...] = _dot(h2, rw_ref[...])


def out_projection(x, out_a, out_b, out_c, w_out, mods3, ln_g, ln_b, router_pad):
    mod = lambda k: pl.BlockSpec((1, 1, D_MODEL), lambda i: (_mod_row(i), 0, k))
    row = lambda w: pl.BlockSpec((TM, w), lambda i: (i, 0))
    const = lambda r, c: pl.BlockSpec((r, c), lambda i: (0, 0))
    return pl.pallas_call(
        _outproj_kernel,
        grid=(N_TOK // TM,),
        in_specs=[row(D_MODEL), row(NA_W), row(ML_W), row(GQA_W), const(D_MODEL, D_MODEL),
                  mod(2), mod(3), mod(4), const(1, D_MODEL), const(1, D_MODEL), const(D_MODEL, LANE)],
        out_specs=[row(D_MODEL), row(D_MODEL), row(LANE)],
        out_shape=[
            jax.ShapeDtypeStruct((N_TOK, D_MODEL), F32),
            jax.ShapeDtypeStruct((N_TOK, D_MODEL), BF16),
            jax.ShapeDtypeStruct((N_TOK, LANE), F32),
        ],
        compiler_params=_cparams("arbitrary"),
        name="out_projection",
    )(x, out_a, out_b, out_c, w_out, mods3, mods3, mods3, ln_g.reshape(1, D_MODEL), ln_b.reshape(1, D_MODEL),
      router_pad)


ROUTE_BLK = 256


def _prefix_count(x, tri):
    n = x.shape[0]
    outs = []
    carry = jnp.zeros((1, LANE), F32)
    for i in range(n // ROUTE_BLK):
        blk = x[i * ROUTE_BLK:(i + 1) * ROUTE_BLK]
        outs.append(_dot(tri, blk) + carry)
        carry = carry + jnp.sum(blk.astype(F32), axis=0, keepdims=True)
    return jnp.concatenate(outs, axis=0) if len(outs) > 1 else outs[0]


def _route_kernel(lg_ref, aff_ref, code_ref, coder_ref, *, cap):
    lg = lg_ref[0]
    n = lg.shape[0]
    lane = lax.broadcasted_iota(jnp.int32, (n, LANE), 1)
    valid = lane < N_EXPERTS
    mx = jnp.max(jnp.where(valid, lg, NEG), axis=1, keepdims=True)
    ex = jnp.where(valid, jnp.exp(lg - mx), 0.0)
    aff = ex / jnp.sum(ex, axis=1, keepdims=True)
    def search(i, bits):
        cand = bits | lax.shift_left(jnp.int32(1), 30 - i)
        cnt = jnp.sum(jnp.where(aff >= pltpu.bitcast(cand, F32), 1.0, 0.0), axis=0, keepdims=True)
        return jnp.where(cnt >= cap, cand, bits)

    floor_bits = lax.fori_loop(0, 31, search, jnp.zeros((1, LANE), jnp.int32))
    thr = jnp.min(jnp.where(aff >= pltpu.bitcast(floor_bits, F32), aff, 2.0), axis=0, keepdims=True)
    gt = aff > thr
    eq = aff == thr
    need = cap - jnp.sum(jnp.where(gt, 1.0, 0.0), axis=0, keepdims=True)
    ti = lax.broadcasted_iota(jnp.int32, (ROUTE_BLK, ROUTE_BLK), 0)
    tj = lax.broadcasted_iota(jnp.int32, (ROUTE_BLK, ROUTE_BLK), 1)
    tri = jnp.where(tj < ti, 1.0, 0.0).astype(BF16)
    eq_rank = _prefix_count(jnp.where(eq, 1.0, 0.0).astype(BF16), tri)
    sel = jnp.where(gt, 1.0, jnp.where(eq & (eq_rank < need), 1.0, 0.0))
    pos = _prefix_count(sel.astype(BF16), tri)
    code = jnp.where((sel > 0.5) & valid, pos, -1.0)
    aff_ref[0] = aff
    code_ref[0] = code
    ei = lax.broadcasted_iota(jnp.int32, (N_EXPERTS, LANE), 0)
    ej = lax.broadcasted_iota(jnp.int32, (N_EXPERTS, LANE), 1)
    eye = jnp.where(ei == ej, 1.0, 0.0).astype(BF16)
    coder_ref[0] = _dot_nt(eye, code.astype(BF16))


def route(logits, n, cap):
    sets = logits.shape[0] // n
    lg = logits.reshape(sets, n, LANE)
    tok = pl.BlockSpec((1, n, LANE), lambda s: (s, 0, 0))
    return pl.pallas_call(
        functools.partial(_route_kernel, cap=cap),
        grid=(sets,),
        in_specs=[tok],
        out_specs=[tok, tok, pl.BlockSpec((1, N_EXPERTS, n), lambda s: (s, 0, 0))],
        out_shape=[
            jax.ShapeDtypeStruct((sets, n, LANE), F32),
            jax.ShapeDtypeStruct((sets, n, LANE), F32),
            jax.ShapeDtypeStruct((sets, N_EXPERTS, n), F32),
        ],
        compiler_params=_cparams("arbitrary"),
        name="route",
    )(lg)


def _gather_kernel(coder_ref, h_ref, xe_ref, *, cap):
    e = pl.program_id(1)
    row = coder_ref[0, pl.ds(e, 1), :]
    n = row.shape[1]
    ci = lax.broadcasted_iota(jnp.int32, (cap, n), 0).astype(F32)
    onehot = jnp.where(ci == row, 1.0, 0.0).astype(BF16)
    xe_ref[0] = _dot(onehot, h_ref[...]).astype(BF16)


def gather_tokens(code_rows, h2, n, cap, row0):
    sets = code_rows.shape[0]
    blk0 = row0 // n
    return pl.pallas_call(
        functools.partial(_gather_kernel, cap=cap),
        grid=(sets, N_EXPERTS),
        in_specs=[
            pl.BlockSpec((1, N_EXPERTS, n), lambda s, e: (s, 0, 0)),
            pl.BlockSpec((n, D_MODEL), lambda s, e: (blk0 + s, 0)),
        ],
        out_specs=pl.BlockSpec((1, cap, D_MODEL), lambda s, e: (e, s, 0)),
        out_shape=jax.ShapeDtypeStruct((N_EXPERTS, sets * cap, D_MODEL), BF16),
        compiler_params=_cparams("arbitrary", "arbitrary"),
        name="gather_tokens",
    )(code_rows, h2)


def _expert_kernel(xi_ref, xl_ref, wg_ref, wu_ref, wd_ref, yi_ref, yl_ref, acc_ref):
    j = pl.program_id(1)
    x = jnp.concatenate([xi_ref[0], xl_ref[0]], axis=0)
    a = _dot(x, wg_ref[0].astype(BF16))
    b = _dot(x, wu_ref[0].astype(BF16))
    hid = (a * _sigmoid(a) * b).astype(BF16)
    contrib = _dot(hid, wd_ref[0].astype(BF16))

    @pl.when(j == 0)
    def _():
        acc_ref[...] = contrib

    @pl.when(j > 0)
    def _():
        acc_ref[...] += contrib

    @pl.when(j == pl.num_programs(1) - 1)
    def _():
        yi_ref[0] = acc_ref[0:SLOTS_CTX].astype(BF16)
        yl_ref[0] = acc_ref[SLOTS_CTX:].astype(BF16)


def experts(xe_ctx, xe_lat, w_gate, w_up, w_down):
    xin = lambda s: pl.BlockSpec((1, s, D_MODEL), lambda e, j: (e, 0, 0))
    return pl.pallas_call(
        _expert_kernel,
        grid=(N_EXPERTS, EXPERT_HIDDEN // EXP_TF),
        in_specs=[
            xin(SLOTS_CTX), xin(SLOTS_LAT),
            pl.BlockSpec((1, D_MODEL, EXP_TF), lambda e, j: (e, 0, j)),
            pl.BlockSpec((1, D_MODEL, EXP_TF), lambda e, j: (e, 0, j)),
            pl.BlockSpec((1, EXP_TF, D_MODEL), lambda e, j: (e, j, 0)),
        ],
        out_specs=[xin(SLOTS_CTX), xin(SLOTS_LAT)],
        out_shape=[
            jax.ShapeDtypeStruct((N_EXPERTS, SLOTS_CTX, D_MODEL), BF16),
            jax.ShapeDtypeStruct((N_EXPERTS, SLOTS_LAT, D_MODEL), BF16),
        ],
        scratch_shapes=[pltpu.VMEM((SLOTS_CTX + SLOTS_LAT, D_MODEL), F32)],
        compiler_params=_cparams("arbitrary", "arbitrary"),
        name="experts",
    )(xe_ctx, xe_lat, w_gate, w_up, w_down)


COMB_TT = 256


def _combine_kernel(code_ref, aff_ref, ye_ref, x1_ref, g2_ref, lng_ref, lnb_ref, o_ref, *, cap):
    code = code_ref[0]
    aff = aff_ref[0]
    tt = code.shape[0]
    li = lax.broadcasted_iota(jnp.int32, (tt, cap), 1).astype(F32)
    acc = jnp.zeros((tt, D_MODEL), F32)
    for e in range(N_EXPERTS):
        onehot = jnp.where(li == code[:, e:e + 1], 1.0, 0.0).astype(BF16)
        acc = acc + aff[:, e:e + 1] * _dot(onehot, ye_ref[e])
    u = DEEPNORM_ALPHA * x1_ref[...] + g2_ref[0] * acc
    o_ref[...] = _layer_norm(u, lng_ref[...], lnb_ref[...])


def combine(code, aff, ye, x1, mods3, ln_g, ln_b, n, cap, row0, mod_row0):
    sets = code.shape[0]
    per = n // COMB_TT
    blk0 = row0 // COMB_TT
    tok = pl.BlockSpec((1, COMB_TT, LANE), lambda s, i: (s, i, 0))
    per_set_mod = mod_row0 > 0
    mrow = (lambda s, i: (mod_row0 + s, 0, 5)) if per_set_mod else (lambda s, i: (0, 0, 5))
    return pl.pallas_call(
        functools.partial(_combine_kernel, cap=cap),
        grid=(sets, per),
        in_specs=[
            tok, tok,
            pl.BlockSpec((N_EXPERTS, cap, D_MODEL), lambda s, i: (0, s, 0)),
            pl.BlockSpec((COMB_TT, D_MODEL), lambda s, i: (blk0 + s * per + i, 0)),
            pl.BlockSpec((1, 1, D_MODEL), mrow),
            pl.BlockSpec((1, D_MODEL), lambda s, i: (0, 0)),
            pl.BlockSpec((1, D_MODEL), lambda s, i: (0, 0)),
        ],
        out_specs=pl.BlockSpec((COMB_TT, D_MODEL), lambda s, i: (s * per + i, 0)),
        out_shape=jax.ShapeDtypeStruct((sets * n, D_MODEL), F32),
        compiler_params=_cparams("arbitrary", "arbitrary"),
        name="combine",
    )(code, aff, ye, x1, mods3, ln_g.reshape(1, D_MODEL), ln_b.reshape(1, D_MODEL))


def _to_heads(zcols, b, s):
    h = zcols.shape[1] // HEAD_DIM
    return zcols.reshape(b, s, h, HEAD_DIM).transpose(0, 2, 1, 3)


def _from_heads(x):
    b, h, s, d = x.shape
    return x.transpose(0, 2, 1, 3).reshape(b * s, h * d)


def _axial_rope_tables():
    t = jnp.arange(DEC_SEQ)
    row = (t // GRID_W).astype(F32)
    col = (t % GRID_W).astype(F32)
    half = HEAD_DIM // 2
    inv = ROPE_THETA ** (-jnp.arange(0, half, 2, dtype=F32) / half)
    ang_r = row[:, None] * inv
    ang_c = col[:, None] * inv
    ang = jnp.concatenate([ang_r, ang_r, ang_c, ang_c], -1)
    return jnp.cos(ang), jnp.sin(ang)


def _mlstm_inputs(z, b, s, c0, n0, m0):
    q = _to_heads(z[:, OFF_ML_Q:OFF_ML_Q + ML_W], b, s)
    k = _to_heads(z[:, OFF_ML_K:OFF_ML_K + ML_W], b, s)
    v = _to_heads(z[:, OFF_ML_V:OFF_ML_V + ML_W], b, s)
    g = z[:, OFF_ML_G:OFF_ML_G + N_GATES].reshape(b, s, 4, ML_HEADS).transpose(2, 0, 3, 1)
    both = lambda a: jnp.concatenate([a, jnp.flip(a, 2)], 0).reshape(2 * b * ML_HEADS, s, HEAD_DIM)
    q2, k2, v2 = both(q), both(k), both(v)
    kt2 = k2.transpose(0, 2, 1)
    ig = jnp.concatenate([g[0], jnp.flip(g[2], 2)], 0).reshape(2 * b * ML_HEADS, 1, s)
    fg = jnp.concatenate([g[1], jnp.flip(g[3], 2)], 0).reshape(2 * b * ML_HEADS, 1, s)
    nd = 2 * b * ML_HEADS
    c0 = c0.transpose(1, 0, 2, 3, 4).reshape(nd, HEAD_DIM, HEAD_DIM)
    n0 = n0.transpose(1, 0, 2, 3).reshape(nd, 1, HEAD_DIM)
    m0 = m0.transpose(1, 0, 2).reshape(nd, 1, 1)
    return q2, kt2, v2, ig, fg, c0, n0, m0


def kernel(x_prompt, x_sample, c, cache_na_k, cache_na_v, cache_gqa_k, cache_gqa_v, state_mlstm_c, state_mlstm_n,
           state_mlstm_m, c_ctx, ada_w, ada_b, w_in, b_gate, w_out, na_rpb, qk_norm_g, ml_norm_g, ln_g, ln_b,
           router_w, w_gate, w_up, w_down):
    cond8 = jnp.concatenate([c_ctx[None, :], c, jnp.zeros((8 - 1 - DEC_BATCH, D_MODEL), F32)], 0)
    mods_all = adaln(cond8, ada_w, ada_b)
    x = jnp.concatenate([x_prompt.reshape(N_CTX_TOK, D_MODEL), x_sample.reshape(N_LAT_TOK, D_MODEL)], 0)
    cos, sin = _axial_rope_tables()
    zeros_c = jnp.zeros((BATCH, 2, ML_HEADS, HEAD_DIM, HEAD_DIM), F32)
    zeros_n = jnp.zeros((BATCH, 2, ML_HEADS, HEAD_DIM), F32)
    zeros_m = jnp.zeros((BATCH, 2, ML_HEADS), F32)
    new = [[] for _ in range(7)]

    for l in range(DEPTH):
        mods3 = mods_all[l].reshape(8, 1, 6 * D_MODEL)
        wl = w_in[l]
        gate_pad = jnp.zeros((D_MODEL, LANE - N_GATES), F32)
        w_pad = jnp.concatenate([wl[:, :OFF_ML_G], wl[:, OFF_ML_G:OFF_ML_G + N_GATES], gate_pad,
                                 wl[:, OFF_ML_G + N_GATES:]], 1).astype(BF16)
        bias_row = jnp.zeros((1, PROJ_PAD), F32).at[0, OFF_ML_G:OFF_ML_G + N_GATES].set(b_gate[l])
        z = in_projection(x, mods3, w_pad, bias_row)
        zc, zl = z[:N_CTX_TOK], z[N_CTX_TOK:]

        na_q = _to_heads(zc[:, OFF_NA_Q:OFF_NA_Q + NA_W], BATCH, SEQ)
        na_k = _to_heads(zc[:, OFF_NA_K:OFF_NA_K + NA_W], BATCH, SEQ)
        na_v = _to_heads(zc[:, OFF_NA_V:OFF_NA_V + NA_W], BATCH, SEQ)
        out_a_c = _from_heads(dense_attention(na_q, na_k, na_v, SEQ))
        gq_q = _to_heads(zc[:, OFF_GQ_Q:OFF_GQ_Q + GQA_W], BATCH, SEQ)
        gq_k = _to_heads(zc[:, OFF_GQ_K:OFF_GQ_K + GQA_KV_W], BATCH, SEQ)
        gq_v = _to_heads(zc[:, OFF_GQ_V:OFF_GQ_V + GQA_KV_W], BATCH, SEQ)
        gq_qn = head_norm(gq_q.reshape(-1, SEQ, HEAD_DIM), qk_norm_g[l, 0]).reshape(gq_q.shape)
        gq_kn = head_norm(gq_k.reshape(-1, SEQ, HEAD_DIM), qk_norm_g[l, 1]).reshape(gq_k.shape)
        out_c_c = _from_heads(dense_attention(gq_qn, gq_kn, gq_v, SEQ))
        mi = _mlstm_inputs(zc, BATCH, SEQ, zeros_c, zeros_n, zeros_m)
        h_c, sc, sn, sm = mlstm(*mi)
        h_c = h_c.reshape(2, BATCH, ML_HEADS, SEQ, HEAD_DIM)
        out_b_c = mlstm_post(h_c[0], jnp.flip(h_c[1], 2), z, ml_norm_g[l], 0)
        new[0].append(zc[:, OFF_NA_K:OFF_NA_K + NA_W].reshape(BATCH, SEQ, NA_HEADS, HEAD_DIM))
        new[1].append(zc[:, OFF_NA_V:OFF_NA_V + NA_W].reshape(BATCH, SEQ, NA_HEADS, HEAD_DIM))
        new[2].append(gq_kn.transpose(0, 2, 1, 3))
        new[3].append(zc[:, OFF_GQ_V:OFF_GQ_V + GQA_KV_W].reshape(BATCH, SEQ, GQA_KV_HEADS, HEAD_DIM))
        new[4].append(sc.reshape(2, BATCH, ML_HEADS, HEAD_DIM, HEAD_DIM).transpose(1, 0, 2, 3, 4))
        new[5].append(sn.reshape(2, BATCH, ML_HEADS, HEAD_DIM).transpose(1, 0, 2, 3))
        new[6].append(sm.reshape(2, BATCH, ML_HEADS).transpose(1, 0, 2))

        lq = _to_heads(zl[:, OFF_NA_Q:OFF_NA_Q + NA_W], DEC_BATCH, DEC_SEQ)
        lk = _to_heads(zl[:, OFF_NA_K:OFF_NA_K + NA_W], DEC_BATCH, DEC_SEQ)
        lv = _to_heads(zl[:, OFF_NA_V:OFF_NA_V + NA_W], DEC_BATCH, DEC_SEQ)
        out_a_l = _from_heads(neighbourhood_attention(
            lq, lk, lv, cache_na_k[:, l].transpose(0, 2, 1, 3), cache_na_v[:, l].transpose(0, 2, 1, 3),
            na_bias_table(na_rpb[l])))
        gq_q = _to_heads(zl[:, OFF_GQ_Q:OFF_GQ_Q + GQA_W], DEC_BATCH, DEC_SEQ)
        gq_k = _to_heads(zl[:, OFF_GQ_K:OFF_GQ_K + GQA_KV_W], DEC_BATCH, DEC_SEQ)
        gq_v = _to_heads(zl[:, OFF_GQ_V:OFF_GQ_V + GQA_KV_W], DEC_BATCH, DEC_SEQ)
        gq_qr = head_norm(gq_q.reshape(-1, DEC_SEQ, HEAD_DIM), qk_norm_g[l, 0], cos, sin).reshape(gq_q.shape)
        gq_kr = head_norm(gq_k.reshape(-1, DEC_SEQ, HEAD_DIM), qk_norm_g[l, 1], cos, sin).reshape(gq_k.shape)
        keys = jnp.concatenate([cache_gqa_k[:, l].transpose(0, 2, 1, 3), gq_kr], 2)
        vals = jnp.concatenate([cache_gqa_v[:, l].transpose(0, 2, 1, 3), gq_v], 2)
        out_c_l = _from_heads(dense_attention(gq_qr, keys, vals, 512))
        mi = _mlstm_inputs(zl, DEC_BATCH, DEC_SEQ, state_mlstm_c[:, l], state_mlstm_n[:, l], state_mlstm_m[:, l])
        h_l = mlstm(*mi)[0].reshape(2, DEC_BATCH, ML_HEADS, DEC_SEQ, HEAD_DIM)
        out_b_l = mlstm_post(h_l[0], jnp.flip(h_l[1], 2), z, ml_norm_g[l], N_CTX_TOK)

        out_a = jnp.concatenate([out_a_c, out_a_l], 0)
        out_b = jnp.concatenate([out_b_c, out_b_l], 0)
        out_c = jnp.concatenate([out_c_c, out_c_l], 0)

        router_pad = jnp.concatenate([router_w[l], jnp.zeros((D_MODEL, LANE - N_EXPERTS), F32)], 1).astype(BF16)
        x1, h2, logits = out_projection(x, out_a, out_b, out_c, w_out[l].astype(BF16), mods3, ln_g[l, 0],
                                        ln_b[l, 0], router_pad)
        aff_c, code_c, crow_c = route(logits[:N_CTX_TOK], SEQ, CAP_CTX)
        aff_l, code_l, crow_l = route(logits[N_CTX_TOK:], DEC_SEQ, CAP_LAT)
        xe_c = gather_tokens(crow_c, h2, SEQ, CAP_CTX, 0)
        xe_l = gather_tokens(crow_l, h2, DEC_SEQ, CAP_LAT, N_CTX_TOK)
        ye_c, ye_l = experts(xe_c, xe_l, w_gate[l], w_up[l], w_down[l])
        y_c = combine(code_c, aff_c, ye_c, x1, mods3, ln_g[l, 1], ln_b[l, 1], SEQ, CAP_CTX, 0, 0)
        y_l = combine(code_l, aff_l, ye_l, x1, mods3, ln_g[l, 1], ln_b[l, 1], DEC_SEQ, CAP_LAT, N_CTX_TOK, 1)
        x = jnp.concatenate([y_c, y_l], 0)

    y_prompt = x[:N_CTX_TOK].reshape(BATCH, SEQ, D_MODEL)
    y_sample = x[N_CTX_TOK:].reshape(DEC_BATCH, DEC_SEQ, D_MODEL)
    outs = [jnp.stack(t, 1) for t in new]
    return (y_prompt, y_sample, *outs)
```

```python
import functools

import jax
import jax.numpy as jnp
import numpy as np
from jax import lax
from jax.experimental import pallas as pl
from jax.experimental.pallas import tpu as pltpu

D_MODEL = 1024
BATCH = 16
SEQ = 256
DEPTH = 2
DEC_BATCH = 2
DEC_SEQ = 2048
PAST_LEN = 256
GRID_W = 64
GRID_H = DEC_SEQ // GRID_W
HEAD_DIM = 64
NA_HEADS = 6
NA_WIN_ROWS = 8
NA_WIN_COLS = 16
ML_HEADS = 4
GQA_HEADS = 6
GQA_KV_HEADS = 2
ROPE_THETA = 10000.0
N_EXPERTS = 16
EC_CAPACITY = 2
EXPERT_HIDDEN = 2816
NORM_EPS = 1e-6
NA_W = NA_HEADS * HEAD_DIM
ML_W = ML_HEADS * HEAD_DIM
GQA_W = GQA_HEADS * HEAD_DIM
GQA_KV_W = GQA_KV_HEADS * HEAD_DIM
N_GATES = 4 * ML_HEADS
DEEPNORM_ALPHA = (2 * DEPTH) ** 0.25
ATTN_SCALE = HEAD_DIM ** -0.5
F32 = jnp.float32
BF16 = jnp.bfloat16

LANE = 128
N_CTX_TOK = BATCH * SEQ
N_LAT_TOK = DEC_BATCH * DEC_SEQ
NEG = -1e30

OFF_NA_Q = 0
OFF_NA_K = OFF_NA_Q + NA_W
OFF_NA_V = OFF_NA_K + NA_W
OFF_ML_Q = OFF_NA_V + NA_W
OFF_ML_K = OFF_ML_Q + ML_W
OFF_ML_V = OFF_ML_K + ML_W
OFF_ML_O = OFF_ML_V + ML_W
OFF_ML_G = OFF_ML_O + ML_W
OFF_GQ_Q = OFF_ML_G + LANE
OFF_GQ_K = OFF_GQ_Q + GQA_W
OFF_GQ_V = OFF_GQ_K + GQA_KV_W
PROJ_PAD = OFF_GQ_V + GQA_KV_W
ML_BLOCK_W = OFF_GQ_Q - OFF_ML_Q

TM = 512
ML_CHUNK = 256
EXP_TF = 256
CAP_CTX = EC_CAPACITY * SEQ // N_EXPERTS
CAP_LAT = EC_CAPACITY * DEC_SEQ // N_EXPERTS
SLOTS_CTX = BATCH * CAP_CTX
SLOTS_LAT = DEC_BATCH * CAP_LAT
VMEM_LIMIT = 56 * 1024 * 1024


def _cparams(*sem):
    return pltpu.CompilerParams(dimension_semantics=sem, vmem_limit_bytes=VMEM_LIMIT)


def _sigmoid(x):
    return 1.0 / (1.0 + jnp.exp(-x))


def _dot(a, b):
    return jnp.dot(a, b, preferred_element_type=F32)


def _dot_nt(a, b):
    return lax.dot_general(a, b, (((1,), (1,)), ((), ())), preferred_element_type=F32)


def _dot_tn(a, b):
    return lax.dot_general(a, b, (((0,), (0,)), ((), ())), preferred_element_type=F32)


def _low_half(rows):
    return lax.broadcasted_iota(jnp.int32, (rows, LANE), 1) < HEAD_DIM


def _pair_mean(x, low):
    s_lo = jnp.sum(jnp.where(low, x, 0.0), axis=-1, keepdims=True)
    s_hi = jnp.sum(jnp.where(low, 0.0, x), axis=-1, keepdims=True)
    return jnp.where(low, s_lo, s_hi) * (1.0 / HEAD_DIM)


def _adaln_kernel(c_ref, w_ref, b_ref, o_ref):
    c = c_ref[...]
    s = c * _sigmoid(c)
    o_ref[0] = _dot(s.astype(BF16), w_ref[0].astype(BF16)) + b_ref[0]


def adaln(cond8, ada_w, ada_b):
    tn = 1536
    return pl.pallas_call(
        _adaln_kernel,
        grid=(DEPTH, 6 * D_MODEL // tn),
        in_specs=[
            pl.BlockSpec((8, D_MODEL), lambda l, j: (0, 0)),
            pl.BlockSpec((1, D_MODEL, tn), lambda l, j: (l, 0, j)),
            pl.BlockSpec((1, 1, tn), lambda l, j: (l, 0, j)),
        ],
        out_specs=pl.BlockSpec((1, 8, tn), lambda l, j: (l, 0, j)),
        out_shape=jax.ShapeDtypeStruct((DEPTH, 8, 6 * D_MODEL), F32),
        compiler_params=_cparams("arbitrary", "arbitrary"),
        name="adaln",
    )(cond8, ada_w, ada_b.reshape(DEPTH, 1, 6 * D_MODEL))


def _inproj_kernel(x_ref, sh_ref, sc_ref, w_ref, b_ref, gq_ref, gk_ref, *rest, rope):
    if rope:
        cos_ref, sin_ref, z_ref = rest
    else:
        (z_ref,) = rest
    h = x_ref[...] * (1.0 + sc_ref[0]) + sh_ref[0]
    z = _dot(h.astype(BF16), w_ref[...]) + b_ref[...]
    z_ref[...] = z
    rows = z.shape[0]
    low = _low_half(rows)
    if rope:
        lane = lax.broadcasted_iota(jnp.int32, (rows, LANE), 1)
        first_quarter = (lane & (HEAD_DIM // 4)) == 0

    def norm_pair(x, gain):
        y = x * lax.rsqrt(_pair_mean(x * x, low) + NORM_EPS) * gain
        if rope:
            rot = jnp.where(first_quarter, -pltpu.roll(y, LANE - HEAD_DIM // 4, axis=1),
                            pltpu.roll(y, HEAD_DIM // 4, axis=1))
            y = y * cos_ref[...] + rot * sin_ref[...]
        return y

    for p in range(GQA_W // LANE):
        c0 = OFF_GQ_Q + p * LANE
        z_ref[:, c0:c0 + LANE] = norm_pair(z[:, c0:c0 + LANE], gq_ref[...])
    z_ref[:, OFF_GQ_K:OFF_GQ_K + LANE] = norm_pair(z[:, OFF_GQ_K:OFF_GQ_K + LANE], gk_ref[...])


def in_projection(x, mods3, mod_row, w_pad, bias_row, gain_q, gain_k, cos=None, sin=None):
    n = x.shape[0]
    rope = cos is not None
    const = lambda r, c: pl.BlockSpec((r, c), lambda i: (0, 0))
    in_specs = [
        pl.BlockSpec((TM, D_MODEL), lambda i: (i, 0)),
        pl.BlockSpec((1, 1, D_MODEL), lambda i: (mod_row(i), 0, 0)),
        pl.BlockSpec((1, 1, D_MODEL), lambda i: (mod_row(i), 0, 1)),
        const(D_MODEL, PROJ_PAD), const(1, PROJ_PAD), const(1, LANE), const(1, LANE),
    ]
    args = [x, mods3, mods3, w_pad, bias_row, gain_q, gain_k]
    if rope:
        per = cos.shape[0] // TM
        in_specs += [pl.BlockSpec((TM, LANE), lambda i: (i % per, 0))] * 2
        args += [cos, sin]
    return pl.pallas_call(
        functools.partial(_inproj_kernel, rope=rope),
        grid=(n // TM,),
        in_specs=in_specs,
        out_specs=pl.BlockSpec((TM, PROJ_PAD), lambda i: (i, 0)),
        out_shape=jax.ShapeDtypeStruct((n, PROJ_PAD), F32),
        compiler_params=_cparams("arbitrary"),
        name="in_projection_rope" if rope else "in_projection",
    )(*args)


def _softmax_pv(scores, values):
    m = jnp.max(scores[0], axis=-1, keepdims=True)
    for s in scores[1:]:
        m = jnp.maximum(m, jnp.max(s, axis=-1, keepdims=True))
    l = 0.0
    o = 0.0
    for s, v in zip(scores, values):
        p = jnp.exp(s - m)
        l = l + jnp.sum(p, axis=-1, keepdims=True)
        o = o + _dot(p.astype(BF16), v)
    return o * (1.0 / l)


def _ctx_attn_kernel(naq_ref, nak_ref, nav_ref, gq_ref, gk_ref, gv_ref, oa_ref, oc_ref):
    low = _low_half(SEQ)
    for p in range(NA_W // LANE):
        cols = slice(p * LANE, (p + 1) * LANE)
        q = naq_ref[:, cols]
        k = nak_ref[:, cols].astype(BF16)
        v = nav_ref[:, cols].astype(BF16)
        outs = []
        for half in range(2):
            qm = jnp.where(low if half == 0 else ~low, q, 0.0).astype(BF16)
            outs.append(_softmax_pv([_dot_nt(qm, k) * ATTN_SCALE], [v]))
        oa_ref[:, cols] = jnp.where(low, outs[0], outs[1])
    k = gk_ref[...]
    v = gv_ref[...]
    k_at = (pltpu.roll(k, HEAD_DIM, axis=1).astype(BF16), k.astype(BF16))
    v_at = (pltpu.roll(v, HEAD_DIM, axis=1).astype(BF16), v.astype(BF16))
    group = GQA_HEADS // GQA_KV_HEADS
    for p in range(GQA_W // LANE):
        cols = slice(p * LANE, (p + 1) * LANE)
        q = gq_ref[:, cols]
        outs = []
        for half in range(2):
            g = (2 * p + half) // group
            qm = jnp.where(low if half == 0 else ~low, q, 0.0).astype(BF16)
            outs.append(_softmax_pv([_dot_nt(qm, k_at[g == half]) * ATTN_SCALE], [v_at[g == half]]))
        oc_ref[:, cols] = jnp.where(low, outs[0], outs[1])


def ctx_attention(z):
    col = lambda w, off: pl.BlockSpec((SEQ, w), lambda b: (b, off // w))
    return pl.pallas_call(
        _ctx_attn_kernel,
        grid=(BATCH,),
        in_specs=[col(NA_W, OFF_NA_Q), col(NA_W, OFF_NA_K), col(NA_W, OFF_NA_V),
                  col(GQA_W, OFF_GQ_Q), col(LANE, OFF_GQ_K), col(LANE, OFF_GQ_V)],
        out_specs=[col(NA_W, 0), col(GQA_W, 0)],
        out_shape=[jax.ShapeDtypeStruct((N_CTX_TOK, NA_W), F32), jax.ShapeDtypeStruct((N_CTX_TOK, GQA_W), F32)],
        compiler_params=_cparams("arbitrary"),
        name="ctx_attention",
    )(z, z, z, z, z, z)


GQA_TQ = 512


def _lat_gqa_kernel(q_ref, kl_ref, vl_ref, kc_ref, vc_ref, o_ref):
    low = _low_half(GQA_TQ)
    at = lambda x: (pltpu.roll(x, HEAD_DIM, axis=1).astype(BF16), x.astype(BF16))
    kl_at, vl_at, kc_at, vc_at = at(kl_ref[...]), at(vl_ref[...]), at(kc_ref[0]), at(vc_ref[0])
    group = GQA_HEADS // GQA_KV_HEADS
    for p in range(GQA_W // LANE):
        cols = slice(p * LANE, (p + 1) * LANE)
        q = q_ref[:, cols]
        outs = []
        for half in range(2):
            same = ((2 * p + half) // group) == half
            qm = jnp.where(low if half == 0 else ~low, q, 0.0).astype(BF16)
            outs.append(_softmax_pv(
                [_dot_nt(qm, kc_at[same]) * ATTN_SCALE, _dot_nt(qm, kl_at[same]) * ATTN_SCALE],
                [vc_at[same], vl_at[same]]))
        o_ref[:, cols] = jnp.where(low, outs[0], outs[1])


def lat_gqa(z, cache_k, cache_v):
    per = DEC_SEQ // GQA_TQ
    kv = lambda off: pl.BlockSpec((DEC_SEQ, LANE), lambda b, i: (b, off // LANE))
    cache = pl.BlockSpec((1, PAST_LEN, LANE), lambda b, i: (b, 0, 0))
    return pl.pallas_call(
        _lat_gqa_kernel,
        grid=(DEC_BATCH, per),
        in_specs=[pl.BlockSpec((GQA_TQ, GQA_W), lambda b, i: (b * per + i, OFF_GQ_Q // GQA_W)),
                  kv(OFF_GQ_K), kv(OFF_GQ_V), cache, cache],
        out_specs=pl.BlockSpec((GQA_TQ, GQA_W), lambda b, i: (b * per + i, 0)),
        out_shape=jax.ShapeDtypeStruct((N_LAT_TOK, GQA_W), F32),
        compiler_params=_cparams("arbitrary", "arbitrary"),
        name="lat_gqa",
    )(z, z, z, cache_k, cache_v)


NA_RB = 4
NA_UNION = NA_RB + NA_WIN_ROWS - 1
NA_TQ = NA_RB * GRID_W
NA_TK = NA_UNION * GRID_W
NA_TYPES = 3


def _na_union_start(i):
    return jnp.clip(i * NA_RB - NA_WIN_ROWS // 2, 0, GRID_H - NA_UNION)


def _na_block_type(i):
    return jnp.where(i == 0, 0, jnp.where(i == GRID_H // NA_RB - 1, 2, 1))


def _lat_na_kernel(q_ref, k_ref, v_ref, kc_ref, vc_ref, bias_ref, o_ref):
    i = pl.program_id(1)
    start = pl.multiple_of(_na_union_start(i) * GRID_W, GRID_W)
    low = _low_half(NA_TQ)
    for p in range(NA_W // LANE):
        cols = slice(p * LANE, (p + 1) * LANE)
        q = q_ref[:, cols]
        kl = k_ref[pl.ds(start, NA_TK), cols].astype(BF16)
        vl = v_ref[pl.ds(start, NA_TK), cols].astype(BF16)
        kc = kc_ref[0, :, cols].astype(BF16)
        vc = vc_ref[0, :, cols].astype(BF16)
        outs = []
        for half in range(2):
            qm = jnp.where(low if half == 0 else ~low, q, 0.0).astype(BF16)
            s_loc = _dot_nt(qm, kl) * ATTN_SCALE + bias_ref[2 * p + half, 0]
            s_ctx = _dot_nt(qm, kc) * ATTN_SCALE
            outs.append(_softmax_pv([s_loc, s_ctx], [vl, vc]))
        o_ref[:, cols] = jnp.where(low, outs[0], outs[1])


def na_bias_table(rpb):
    blocks = (0, 1, GRID_H // NA_RB - 1)
    row_sel = np.zeros((NA_TYPES, NA_RB, NA_UNION, 2 * NA_WIN_ROWS - 1), np.float32)
    row_ok = np.zeros((NA_TYPES, NA_RB, NA_UNION), bool)
    for ty, i in enumerate(blocks):
        u0 = int(np.clip(i * NA_RB - NA_WIN_ROWS // 2, 0, GRID_H - NA_UNION))
        for t in range(NA_RB):
            r = i * NA_RB + t
            rs = int(np.clip(r - NA_WIN_ROWS // 2, 0, GRID_H - NA_WIN_ROWS))
            for j in range(NA_UNION):
                kr = u0 + j
                if rs <= kr < rs + NA_WIN_ROWS:
                    row_ok[ty, t, j] = True
                    row_sel[ty, t, j, kr - r + NA_WIN_ROWS - 1] = 1.0
    cq = np.arange(GRID_W)
    ck = np.arange(GRID_W)
    dc = np.clip(ck[None, :] - cq[:, None], -(NA_WIN_COLS - 1), NA_WIN_COLS - 1) + NA_WIN_COLS - 1
    col_start = np.clip(cq - NA_WIN_COLS // 2, 0, GRID_W - NA_WIN_COLS)
    in_win = (ck[None, :] >= col_start[:, None]) & (ck[None, :] < col_start[:, None] + NA_WIN_COLS)
    col_sel = (dc[:, :, None] == np.arange(2 * NA_WIN_COLS - 1)).astype(np.float32)
    rows = jnp.einsum("ytjr,hrc->hytjc", row_sel, rpb.astype(F32), precision=lax.Precision.HIGHEST)
    bias = jnp.einsum("hytjc,qkc->hytqjk", rows, col_sel, precision=lax.Precision.HIGHEST)
    ok = row_ok[:, :, None, :, None] & in_win[None, None, :, None, :]
    bias = jnp.where(ok[None], bias, NEG)
    return bias.reshape(NA_HEADS, NA_TYPES, NA_TQ, NA_TK)


def lat_na(z, cache_k, cache_v, bias):
    per = GRID_H // NA_RB
    kv = lambda off: pl.BlockSpec((DEC_SEQ, NA_W), lambda b, i: (b, off // NA_W))
    cache = pl.BlockSpec((1, PAST_LEN, NA_W), lambda b, i: (b, 0, 0))
    return pl.pallas_call(
        _lat_na_kernel,
        grid=(DEC_BATCH, per),
        in_specs=[pl.BlockSpec((NA_TQ, NA_W), lambda b, i: (b * per + i, 0)),
                  kv(OFF_NA_K), kv(OFF_NA_V), cache, cache,
                  pl.BlockSpec((NA_HEADS, 1, NA_TQ, NA_TK), lambda b, i: (0, _na_block_type(i), 0, 0))],
        out_specs=pl.BlockSpec((NA_TQ, NA_W), lambda b, i: (b * per + i, 0)),
        out_shape=jax.ShapeDtypeStruct((N_LAT_TOK, NA_W), F32),
        compiler_params=_cparams("arbitrary", "arbitrary"),
        name="lat_na",
    )(z, z, z, cache_k, cache_v, bias)


def _log_sigmoid(x):
    return jnp.minimum(x, 0.0) - jnp.log1p(jnp.exp(-jnp.abs(x)))


def _mlstm_head(q, k, v, ig, fg, c, n, m, half, fwd):
    t = q.shape[0]
    lf = _log_sigmoid(fg)
    jj = lax.broadcasted_iota(jnp.int32, (t, t), 0)
    ss = lax.broadcasted_iota(jnp.int32, (t, t), 1)
    eye = jj == ss
    b_row = jnp.sum(jnp.where((jj <= ss) if fwd else (jj >= ss), lf, 0.0), axis=0, keepdims=True)
    b_col = jnp.sum(jnp.where(eye, b_row, 0.0), axis=1, keepdims=True)
    ig_row = jnp.sum(jnp.where(eye, ig, 0.0), axis=0, keepdims=True)
    dmat = jnp.where((ss <= jj) if fwd else (ss >= jj), b_col - b_row + ig_row, NEG)
    m_inter = b_col + m
    m_j = jnp.maximum(m_inter, jnp.max(dmat, axis=1, keepdims=True))
    w = jnp.exp(dmat - m_j)
    decay = jnp.exp(m_inter - m_j)
    mine = _low_half(t) if half == 0 else ~_low_half(t)
    qm = jnp.where(mine, q, 0.0)
    qb = qm.astype(BF16)
    vb = v.astype(BF16)
    qk = _dot_nt(qb, k.astype(BF16)) * w
    num = decay * _dot(qb, c.astype(BF16)) + _dot(qk.astype(BF16), vb)
    den = decay * jnp.sum(qm * n, axis=1, keepdims=True) + jnp.sum(qk, axis=1, keepdims=True)
    h = jnp.where(mine, num * (1.0 / jnp.maximum(jnp.abs(den), jnp.exp(-m_j))), 0.0)
    last = t - 1 if fwd else 0
    b_last = b_col[last:last + 1, :]
    g = b_last - b_col + ig
    m_new = jnp.maximum(b_last + m, jnp.max(g, axis=0, keepdims=True))
    ws = jnp.exp(g - m_new)
    d_last = jnp.exp(b_last + m - m_new)
    kw = jnp.where(mine, k * ws, 0.0)
    c_add = _dot_tn(kw.astype(BF16), vb)
    n_add = jnp.sum(kw, axis=0, keepdims=True)
    return h, d_last, c_add, n_add, m_new


def _mlstm_kernel(zf_ref, zb_ref, c0_ref, n0_ref, m0_ref, hf_ref, hb_ref, co_ref, no_ref, mo_ref,
                  c_s, n_s, m_s):
    chunk = pl.program_id(1)
    pairs = ML_W // LANE
    zero = jnp.zeros((HEAD_DIM, HEAD_DIM), F32)

    @pl.when(chunk == 0)
    def _():
        for d in range(2):
            for p in range(pairs):
                top = jnp.concatenate([c0_ref[0, d, 2 * p], zero], axis=1)
                bot = jnp.concatenate([zero, c0_ref[0, d, 2 * p + 1]], axis=1)
                c_s[d * pairs + p] = jnp.concatenate([top, bot], axis=0)
                e = d * ML_HEADS + 2 * p
                n_s[d * pairs + p] = jnp.concatenate([n0_ref[0, e:e + 1, :], n0_ref[0, e + 1:e + 2, :]], axis=1)
            m_s[d] = m0_ref[0, d * ML_HEADS:(d + 1) * ML_HEADS, :]

    ri = lax.broadcasted_iota(jnp.int32, (LANE, LANE), 0) < HEAD_DIM
    ci = lax.broadcasted_iota(jnp.int32, (LANE, LANE), 1) < HEAD_DIM
    low_row = lax.broadcasted_iota(jnp.int32, (1, LANE), 1) < HEAD_DIM
    for d, (z_ref, h_ref) in enumerate(((zf_ref, hf_ref), (zb_ref, hb_ref))):
        gates = z_ref[:, OFF_ML_G - OFF_ML_Q:OFF_ML_G - OFF_ML_Q + LANE]
        m_all = m_s[d]
        m_rows = []
        for p in range(pairs):
            q = z_ref[:, p * LANE:(p + 1) * LANE]
            k = z_ref[:, ML_W + p * LANE:ML_W + (p + 1) * LANE] * ATTN_SCALE
            v = z_ref[:, 2 * ML_W + p * LANE:2 * ML_W + (p + 1) * LANE]
            c = c_s[d * pairs + p]
            n = n_s[d * pairs + p]
            res = []
            for half in range(2):
                hd = 2 * p + half
                gi = 2 * ML_HEADS * d + hd
                res.append(_mlstm_head(q, k, v, gates[:, gi:gi + 1], gates[:, gi + ML_HEADS:gi + ML_HEADS + 1],
                                       c, n, m_all[hd:hd + 1, 0:1], half, d == 0))
            (h0, dl0, ca0, na0, mn0), (h1, dl1, ca1, na1, mn1) = res
            h_ref[:, p * LANE:(p + 1) * LANE] = h0 + h1
            c_s[d * pairs + p] = jnp.where(ri & ci, dl0 * c + ca0, jnp.where(~ri & ~ci, dl1 * c + ca1, 0.0))
            n_s[d * pairs + p] = jnp.where(low_row, dl0 * n + na0, dl1 * n + na1)
            m_rows += [jnp.broadcast_to(mn0, (1, LANE)), jnp.broadcast_to(mn1, (1, LANE))]
        m_s[d] = jnp.concatenate(m_rows, axis=0)

    @pl.when(chunk == pl.num_programs(1) - 1)
    def _():
        for d in range(2):
            for p in range(pairs):
                c = c_s[d * pairs + p]
                n = n_s[d * pairs + p]
                co_ref[0, d, 2 * p] = c[:HEAD_DIM, :HEAD_DIM]
                co_ref[0, d, 2 * p + 1] = c[HEAD_DIM:, HEAD_DIM:]
                e = d * ML_HEADS + 2 * p
                no_ref[0, e:e + 1, :] = n[:, :HEAD_DIM]
                no_ref[0, e + 1:e + 2, :] = n[:, HEAD_DIM:]
            mo_ref[0, d * ML_HEADS:(d + 1) * ML_HEADS, :] = m_s[d]


def mlstm(z, batch, seq, c0, n0, m0):
    nc = seq // ML_CHUNK
    zcol = OFF_ML_Q // ML_BLOCK_W
    state_c = pl.BlockSpec((1, 2, ML_HEADS, HEAD_DIM, HEAD_DIM), lambda b, c: (b, 0, 0, 0, 0))
    state_n = pl.BlockSpec((1, 2 * ML_HEADS, HEAD_DIM), lambda b, c: (b, 0, 0))
    state_m = pl.BlockSpec((1, 2 * ML_HEADS, LANE), lambda b, c: (b, 0, 0))
    pairs = ML_W // LANE
    return pl.pallas_call(
        _mlstm_kernel,
        grid=(batch, nc),
        in_specs=[
            pl.BlockSpec((ML_CHUNK, ML_BLOCK_W), lambda b, c: (b * nc + c, zcol)),
            pl.BlockSpec((ML_CHUNK, ML_BLOCK_W), lambda b, c: (b * nc + nc - 1 - c, zcol)),
            state_c, state_n, state_m,
        ],
        out_specs=[
            pl.BlockSpec((ML_CHUNK, ML_W), lambda b, c: (b * nc + c, 0)),
            pl.BlockSpec((ML_CHUNK, ML_W), lambda b, c: (b * nc + nc - 1 - c, 0)),
            state_c, state_n, state_m,
        ],
        out_shape=[
            jax.ShapeDtypeStruct((batch * seq, ML_W), F32),
            jax.ShapeDtypeStruct((batch * seq, ML_W), F32),
            jax.ShapeDtypeStruct(c0.shape, F32),
            jax.ShapeDtypeStruct(n0.shape, F32),
            jax.ShapeDtypeStruct(m0.shape, F32),
        ],
        scratch_shapes=[
            pltpu.VMEM((2 * pairs, LANE, LANE), F32),
            pltpu.VMEM((2 * pairs, 1, LANE), F32),
            pltpu.VMEM((2, ML_HEADS, LANE), F32),
        ],
        compiler_params=_cparams("arbitrary", "arbitrary"),
        name="mlstm",
    )(z, z, c0, n0, m0)


def _layer_norm(u, g, b):
    mu = jnp.mean(u, axis=-1, keepdims=True)
    uc = u - mu
    var = jnp.mean(uc * uc, axis=-1, keepdims=True)
    return uc * lax.rsqrt(var + NORM_EPS) * g + b


def _outproj_kernel(x_ref, a_ref, hf_ref, hb_ref, og0_ref, og1_ref, c_ref, w_ref, mg_ref, g1_ref, sh2_ref, sc2_ref,
                    lng_ref, lnb_ref, rw_ref, x1_ref, h2_ref, lg_ref):
    low = _low_half(TM)
    mixed_b = []
    for p, og_ref in enumerate((og0_ref, og1_ref)):
        cols = slice(p * LANE, (p + 1) * LANE)
        h = hf_ref[:, cols] + hb_ref[:, cols]
        hc = h - _pair_mean(h, low)
        hn = hc * lax.rsqrt(_pair_mean(hc * hc, low) + NORM_EPS)
        mixed_b.append(hn * mg_ref[:, cols] * _sigmoid(og_ref[...]))
    mixed = jnp.concatenate([a_ref[...]] + mixed_b + [c_ref[...]], axis=-1).astype(BF16)
    y = _dot(mixed, w_ref[...])
    x1 = _layer_norm(DEEPNORM_ALPHA * x_ref[...] + g1_ref[0] * y, lng_ref[...], lnb_ref[...])
    h2 = (x1 * (1.0 + sc2_ref[0]) + sh2_ref[0]).astype(BF16)
    x1_ref[...] = x1
    h2_ref[...] = h2
    lg_ref[...] = _dot(h2, rw_ref[...])


def out_projection(x, out_a, hf, hb, z, out_c, w_out, ml_gain, mods3, mod_row, ln_g, ln_b, router_pad):
    n = x.shape[0]
    mod = lambda k: pl.BlockSpec((1, 1, D_MODEL), lambda i: (mod_row(i), 0, k))
    row = lambda w: pl.BlockSpec((TM, w), lambda i: (i, 0))
    zcol = lambda off: pl.BlockSpec((TM, LANE), lambda i: (i, off // LANE))
    const = lambda r, c: pl.BlockSpec((r, c), lambda i: (0, 0))
    return pl.pallas_call(
        _outproj_kernel,
        grid=(n // TM,),
        in_specs=[row(D_MODEL), row(NA_W), row(ML_W), row(ML_W), zcol(OFF_ML_O), zcol(OFF_ML_O + LANE), row(GQA_W),
                  const(D_MODEL, D_MODEL), const(1, ML_W), mod(2), mod(3), mod(4), const(1, D_MODEL),
                  const(1, D_MODEL), const(D_MODEL, LANE)],
        out_specs=[row(D_MODEL), row(D_MODEL), row(LANE)],
        out_shape=[
            jax.ShapeDtypeStruct((n, D_MODEL), F32),
            jax.ShapeDtypeStruct((n, D_MODEL), BF16),
            jax.ShapeDtypeStruct((n, LANE), F32),
        ],
        compiler_params=_cparams("arbitrary"),
        name="out_projection",
    )(x, out_a, hf, hb, z, z, out_c, w_out, ml_gain, mods3, mods3, mods3, ln_g.reshape(1, D_MODEL),
      ln_b.reshape(1, D_MODEL), router_pad)


ROUTE_BLK = 256


def _prefix_count(x, tri):
    n = x.shape[0]
    outs = []
    carry = jnp.zeros((1, LANE), F32)
    for i in range(n // ROUTE_BLK):
        blk = x[i * ROUTE_BLK:(i + 1) * ROUTE_BLK]
        outs.append(_dot(tri, blk) + carry)
        carry = carry + jnp.sum(blk.astype(F32), axis=0, keepdims=True)
    return jnp.concatenate(outs, axis=0) if len(outs) > 1 else outs[0]


def _route_kernel(lg_ref, aff_ref, code_ref, coder_ref, *, cap):
    lg = lg_ref[0]
    n = lg.shape[0]
    lane = lax.broadcasted_iota(jnp.int32, (n, LANE), 1)
    valid = lane < N_EXPERTS
    mx = jnp.max(jnp.where(valid, lg, NEG), axis=1, keepdims=True)
    ex = jnp.where(valid, jnp.exp(lg - mx), 0.0)
    aff = ex / jnp.sum(ex, axis=1, keepdims=True)

    def search(i, bits):
        cand = bits | lax.shift_left(jnp.int32(1), 30 - i)
        cnt = jnp.sum(jnp.where(aff >= pltpu.bitcast(cand, F32), 1.0, 0.0), axis=0, keepdims=True)
        return jnp.where(cnt >= cap, cand, bits)

    floor_bits = lax.fori_loop(0, 31, search, jnp.zeros((1, LANE), jnp.int32))
    thr = jnp.min(jnp.where(aff >= pltpu.bitcast(floor_bits, F32), aff, 2.0), axis=0, keepdims=True)
    gt = aff > thr
    eq = aff == thr
    need = cap - jnp.sum(jnp.where(gt, 1.0, 0.0), axis=0, keepdims=True)
    ti = lax.broadcasted_iota(jnp.int32, (ROUTE_BLK, ROUTE_BLK), 0)
    tj = lax.broadcasted_iota(jnp.int32, (ROUTE_BLK, ROUTE_BLK), 1)
    tri = jnp.where(tj < ti, 1.0, 0.0).astype(BF16)
    eq_rank = _prefix_count(jnp.where(eq, 1.0, 0.0).astype(BF16), tri)
    sel = jnp.where(gt, 1.0, jnp.where(eq & (eq_rank < need), 1.0, 0.0))
    pos = _prefix_count(sel.astype(BF16), tri)
    code = jnp.where((sel > 0.5) & valid, pos, -1.0)
    aff_ref[0] = aff
    code_ref[0] = code
    ei = lax.broadcasted_iota(jnp.int32, (N_EXPERTS, LANE), 0)
    ej = lax.broadcasted_iota(jnp.int32, (N_EXPERTS, LANE), 1)
    eye = jnp.where(ei == ej, 1.0, 0.0).astype(BF16)
    coder_ref[0] = _dot_nt(eye, code.astype(BF16))


def route(logits, n, cap):
    sets = logits.shape[0] // n
    lg = logits.reshape(sets, n, LANE)
    tok = pl.BlockSpec((1, n, LANE), lambda s: (s, 0, 0))
    return pl.pallas_call(
        functools.partial(_route_kernel, cap=cap),
        grid=(sets,),
        in_specs=[tok],
        out_specs=[tok, tok, pl.BlockSpec((1, N_EXPERTS, n), lambda s: (s, 0, 0))],
        out_shape=[
            jax.ShapeDtypeStruct((sets, n, LANE), F32),
            jax.ShapeDtypeStruct((sets, n, LANE), F32),
            jax.ShapeDtypeStruct((sets, N_EXPERTS, n), F32),
        ],
        compiler_params=_cparams("arbitrary"),
        name="route",
    )(lg)


def _gather_kernel(coder_ref, h_ref, xe_ref, *, cap):
    e = pl.program_id(1)
    row = coder_ref[0, pl.ds(e, 1), :]
    n = row.shape[1]
    ci = lax.broadcasted_iota(jnp.int32, (cap, n), 0).astype(F32)
    onehot = jnp.where(ci == row, 1.0, 0.0).astype(BF16)
    xe_ref[0] = _dot(onehot, h_ref[...]).astype(BF16)


def gather_tokens(code_rows, h2, n, cap):
    sets = code_rows.shape[0]
    return pl.pallas_call(
        functools.partial(_gather_kernel, cap=cap),
        grid=(sets, N_EXPERTS),
        in_specs=[
            pl.BlockSpec((1, N_EXPERTS, n), lambda s, e: (s, 0, 0)),
            pl.BlockSpec((n, D_MODEL), lambda s, e: (s, 0)),
        ],
        out_specs=pl.BlockSpec((1, cap, D_MODEL), lambda s, e: (e, s, 0)),
        out_shape=jax.ShapeDtypeStruct((N_EXPERTS, sets * cap, D_MODEL), BF16),
        compiler_params=_cparams("arbitrary", "arbitrary"),
        name="gather_tokens",
    )(code_rows, h2)


def _expert_kernel(xi_ref, xl_ref, wg_ref, wu_ref, wd_ref, yi_ref, yl_ref, acc_ref):
    j = pl.program_id(1)
    x = jnp.concatenate([xi_ref[0], xl_ref[0]], axis=0)
    a = _dot(x, wg_ref[0].astype(BF16))
    b = _dot(x, wu_ref[0].astype(BF16))
    hid = (a * _sigmoid(a) * b).astype(BF16)
    contrib = _dot(hid, wd_ref[0].astype(BF16))

    @pl.when(j == 0)
    def _():
        acc_ref[...] = contrib

    @pl.when(j > 0)
    def _():
        acc_ref[...] += contrib

    @pl.when(j == pl.num_programs(1) - 1)
    def _():
        yi_ref[0] = acc_ref[0:SLOTS_CTX].astype(BF16)
        yl_ref[0] = acc_ref[SLOTS_CTX:].astype(BF16)


def experts(xe_ctx, xe_lat, w_gate, w_up, w_down):
    xin = lambda s: pl.BlockSpec((1, s, D_MODEL), lambda e, j: (e, 0, 0))
    return pl.pallas_call(
        _expert_kernel,
        grid=(N_EXPERTS, EXPERT_HIDDEN // EXP_TF),
        in_specs=[
            xin(SLOTS_CTX), xin(SLOTS_LAT),
            pl.BlockSpec((1, D_MODEL, EXP_TF), lambda e, j: (e, 0, j)),
            pl.BlockSpec((1, D_MODEL, EXP_TF), lambda e, j: (e, 0, j)),
            pl.BlockSpec((1, EXP_TF, D_MODEL), lambda e, j: (e, j, 0)),
        ],
        out_specs=[xin(SLOTS_CTX), xin(SLOTS_LAT)],
        out_shape=[
            jax.ShapeDtypeStruct((N_EXPERTS, SLOTS_CTX, D_MODEL), BF16),
            jax.ShapeDtypeStruct((N_EXPERTS, SLOTS_LAT, D_MODEL), BF16),
        ],
        scratch_shapes=[pltpu.VMEM((SLOTS_CTX + SLOTS_LAT, D_MODEL), F32)],
        compiler_params=_cparams("arbitrary", "arbitrary"),
        name="experts",
    )(xe_ctx, xe_lat, w_gate, w_up, w_down)


COMB_TT = 256


def _combine_kernel(code_ref, aff_ref, ye_ref, x1_ref, g2_ref, lng_ref, lnb_ref, o_ref, *, cap):
    code = code_ref[0]
    aff = aff_ref[0]
    tt = code.shape[0]
    li = lax.broadcasted_iota(jnp.int32, (tt, cap), 1).astype(F32)
    acc = jnp.zeros((tt, D_MODEL), F32)
    for e in range(N_EXPERTS):
        onehot = jnp.where(li == code[:, e:e + 1], 1.0, 0.0).astype(BF16)
        acc = acc + aff[:, e:e + 1] * _dot(onehot, ye_ref[e])
    u = DEEPNORM_ALPHA * x1_ref[...] + g2_ref[0] * acc
    o_ref[...] = _layer_norm(u, lng_ref[...], lnb_ref[...])


def combine(code, aff, ye, x1, mods3, set_mod_row, ln_g, ln_b, n, cap):
    sets = code.shape[0]
    per = n // COMB_TT
    tok = pl.BlockSpec((1, COMB_TT, LANE), lambda s, i: (s, i, 0))
    return pl.pallas_call(
        functools.partial(_combine_kernel, cap=cap),
        grid=(sets, per),
        in_specs=[
            tok, tok,
            pl.BlockSpec((N_EXPERTS, cap, D_MODEL), lambda s, i: (0, s, 0)),
            pl.BlockSpec((COMB_TT, D_MODEL), lambda s, i: (s * per + i, 0)),
            pl.BlockSpec((1, 1, D_MODEL), lambda s, i: (set_mod_row(s), 0, 5)),
            pl.BlockSpec((1, D_MODEL), lambda s, i: (0, 0)),
            pl.BlockSpec((1, D_MODEL), lambda s, i: (0, 0)),
        ],
        out_specs=pl.BlockSpec((COMB_TT, D_MODEL), lambda s, i: (s * per + i, 0)),
        out_shape=jax.ShapeDtypeStruct((sets * n, D_MODEL), F32),
        compiler_params=_cparams("arbitrary", "arbitrary"),
        name="combine",
    )(code, aff, ye, x1, mods3, ln_g.reshape(1, D_MODEL), ln_b.reshape(1, D_MODEL))


def _axial_rope_tables():
    t = jnp.arange(DEC_SEQ)
    row = (t // GRID_W).astype(F32)
    col = (t % GRID_W).astype(F32)
    half = HEAD_DIM // 2
    inv = ROPE_THETA ** (-jnp.arange(0, half, 2, dtype=F32) / half)
    ang_r = row[:, None] * inv
    ang_c = col[:, None] * inv
    ang = jnp.concatenate([ang_r, ang_r, ang_c, ang_c] * 2, -1)
    return jnp.cos(ang), jnp.sin(ang)


def _ctx_mod_row(i):
    return 0


def _lat_tile_mod_row(i):
    return 1 + i // (DEC_SEQ // TM)


def _lat_set_mod_row(s):
    return 1 + s


def kernel(x_prompt, x_sample, c, cache_na_k, cache_na_v, cache_gqa_k, cache_gqa_v, state_mlstm_c, state_mlstm_n,
           state_mlstm_m, c_ctx, ada_w, ada_b, w_in, b_gate, w_out, na_rpb, qk_norm_g, ml_norm_g, ln_g, ln_b,
           router_w, w_gate, w_up, w_down):
    cond8 = jnp.concatenate([c_ctx[None, :], c, jnp.zeros((8 - 1 - DEC_BATCH, D_MODEL), F32)], 0)
    mods_all = adaln(cond8, ada_w, ada_b)
    xc = x_prompt.reshape(N_CTX_TOK, D_MODEL)
    xl = x_sample.reshape(N_LAT_TOK, D_MODEL)
    cos, sin = _axial_rope_tables()
    zeros_c = jnp.zeros((BATCH, 2, ML_HEADS, HEAD_DIM, HEAD_DIM), F32)
    zeros_n = jnp.zeros((BATCH, 2 * ML_HEADS, HEAD_DIM), F32)
    zeros_m = jnp.zeros((BATCH, 2 * ML_HEADS, LANE), F32)
    new = [[] for _ in range(7)]

    for l in range(DEPTH):
        mods3 = mods_all[l].reshape(8, 1, 6 * D_MODEL)
        wl = w_in[l]
        gate_pad = jnp.zeros((D_MODEL, LANE - N_GATES), F32)
        w_pad = jnp.concatenate([wl[:, :OFF_ML_G + N_GATES], gate_pad, wl[:, OFF_ML_G + N_GATES:]], 1).astype(BF16)
        bias_row = jnp.zeros((1, PROJ_PAD), F32).at[0, OFF_ML_G:OFF_ML_G + N_GATES].set(b_gate[l])
        gain_q = jnp.tile(qk_norm_g[l, 0], 2).reshape(1, LANE)
        gain_k = jnp.tile(qk_norm_g[l, 1], 2).reshape(1, LANE)
        zc = in_projection(xc, mods3, _ctx_mod_row, w_pad, bias_row, gain_q, gain_k)
        zl = in_projection(xl, mods3, _lat_tile_mod_row, w_pad, bias_row, gain_q, gain_k, cos, sin)

        out_a_c, out_c_c = ctx_attention(zc)
        hf_c, hb_c, sc, sn, sm = mlstm(zc, BATCH, SEQ, zeros_c, zeros_n, zeros_m)
        out_a_l = lat_na(zl, cache_na_k[:, l].reshape(DEC_BATCH, PAST_LEN, NA_W),
                         cache_na_v[:, l].reshape(DEC_BATCH, PAST_LEN, NA_W), na_bias_table(na_rpb[l]))
        out_c_l = lat_gqa(zl, cache_gqa_k[:, l].reshape(DEC_BATCH, PAST_LEN, GQA_KV_W),
                          cache_gqa_v[:, l].reshape(DEC_BATCH, PAST_LEN, GQA_KV_W))
        m0 = jnp.broadcast_to(state_mlstm_m[:, l].reshape(DEC_BATCH, 2 * ML_HEADS, 1), (DEC_BATCH, 2 * ML_HEADS, LANE))
        hf_l, hb_l = mlstm(zl, DEC_BATCH, DEC_SEQ, state_mlstm_c[:, l],
                           state_mlstm_n[:, l].reshape(DEC_BATCH, 2 * ML_HEADS, HEAD_DIM), m0)[:2]

        new[0].append(zc[:, OFF_NA_K:OFF_NA_K + NA_W].reshape(BATCH, SEQ, NA_HEADS, HEAD_DIM))
        new[1].append(zc[:, OFF_NA_V:OFF_NA_V + NA_W].reshape(BATCH, SEQ, NA_HEADS, HEAD_DIM))
        new[2].append(zc[:, OFF_GQ_K:OFF_GQ_K + GQA_KV_W].reshape(BATCH, SEQ, GQA_KV_HEADS, HEAD_DIM))
        new[3].append(zc[:, OFF_GQ_V:OFF_GQ_V + GQA_KV_W].reshape(BATCH, SEQ, GQA_KV_HEADS, HEAD_DIM))
        new[4].append(sc)
        new[5].append(sn.reshape(BATCH, 2, ML_HEADS, HEAD_DIM))
        new[6].append(sm[:, :, 0].reshape(BATCH, 2, ML_HEADS))

        router_pad = jnp.concatenate([router_w[l], jnp.zeros((D_MODEL, LANE - N_EXPERTS), F32)], 1).astype(BF16)
        w_out_b = w_out[l].astype(BF16)
        ml_gain = ml_norm_g[l].reshape(1, ML_W)
        x1_c, h2_c, lg_c = out_projection(xc, out_a_c, hf_c, hb_c, zc, out_c_c, w_out_b, ml_gain, mods3,
                                          _ctx_mod_row, ln_g[l, 0], ln_b[l, 0], router_pad)
        x1_l, h2_l, lg_l = out_projection(xl, out_a_l, hf_l, hb_l, zl, out_c_l, w_out_b, ml_gain, mods3,
                                          _lat_tile_mod_row, ln_g[l, 0], ln_b[l, 0], router_pad)
        aff_c, code_c, crow_c = route(lg_c, SEQ, CAP_CTX)
        aff_l, code_l, crow_l = route(lg_l, DEC_SEQ, CAP_LAT)
        xe_c = gather_tokens(crow_c, h2_c, SEQ, CAP_CTX)
        xe_l = gather_tokens(crow_l, h2_l, DEC_SEQ, CAP_LAT)
        ye_c, ye_l = experts(xe_c, xe_l, w_gate[l], w_up[l], w_down[l])
        xc = combine(code_c, aff_c, ye_c, x1_c, mods3, _ctx_mod_row, ln_g[l, 1], ln_b[l, 1], SEQ, CAP_CTX)
        xl = combine(code_l, aff_l, ye_l, x1_l, mods3, _lat_set_mod_row, ln_g[l, 1], ln_b[l, 1], DEC_SEQ, CAP_LAT)

    y_prompt = xc.reshape(BATCH, SEQ, D_MODEL)
    y_sample = xl.reshape(DEC_BATCH, DEC_SEQ, D_MODEL)
    outs = [jnp.stack(t, 1) for t in new]
    return (y_prompt, y_sample, *outs)
```

```python
import functools

import jax
import jax.numpy as jnp
import numpy as np
from jax import lax
from jax.experimental import pallas as pl
from jax.experimental.pallas import tpu as pltpu

D_MODEL = 1024
BATCH = 16
SEQ = 256
DEPTH = 2
DEC_BATCH = 2
DEC_SEQ = 2048
PAST_LEN = 256
GRID_W = 64
GRID_H = DEC_SEQ // GRID_W
HEAD_DIM = 64
NA_HEADS = 6
NA_WIN_ROWS = 8
NA_WIN_COLS = 16
ML_HEADS = 4
GQA_HEADS = 6
GQA_KV_HEADS = 2
ROPE_THETA = 10000.0
N_EXPERTS = 16
EC_CAPACITY = 2
EXPERT_HIDDEN = 2816
NORM_EPS = 1e-6
NA_W = NA_HEADS * HEAD_DIM
ML_W = ML_HEADS * HEAD_DIM
GQA_W = GQA_HEADS * HEAD_DIM
GQA_KV_W = GQA_KV_HEADS * HEAD_DIM
N_GATES = 4 * ML_HEADS
DEEPNORM_ALPHA = (2 * DEPTH) ** 0.25
ATTN_SCALE = HEAD_DIM ** -0.5
F32 = jnp.float32
BF16 = jnp.bfloat16

LANE = 128
N_CTX_TOK = BATCH * SEQ
N_LAT_TOK = DEC_BATCH * DEC_SEQ
NEG = -1e30

OFF_NA_Q = 0
OFF_NA_K = OFF_NA_Q + NA_W
OFF_NA_V = OFF_NA_K + NA_W
OFF_ML_Q = OFF_NA_V + NA_W
OFF_ML_K = OFF_ML_Q + ML_W
OFF_ML_V = OFF_ML_K + ML_W
OFF_ML_O = OFF_ML_V + ML_W
OFF_ML_G = OFF_ML_O + ML_W
OFF_GQ_Q = OFF_ML_G + LANE
OFF_GQ_K = OFF_GQ_Q + GQA_W
OFF_GQ_V = OFF_GQ_K + GQA_KV_W
PROJ_PAD = OFF_GQ_V + GQA_KV_W
ML_BLOCK_W = OFF_GQ_Q - OFF_ML_Q

TM = 512
ML_CHUNK = 256
EXP_TF = EXPERT_HIDDEN // 2
EXP_SUB = 256
CAP_CTX = EC_CAPACITY * SEQ // N_EXPERTS
CAP_LAT = EC_CAPACITY * DEC_SEQ // N_EXPERTS
SLOTS_CTX = BATCH * CAP_CTX
SLOTS_LAT = DEC_BATCH * CAP_LAT
VMEM_LIMIT = 56 * 1024 * 1024


def _cparams(*sem):
    return pltpu.CompilerParams(dimension_semantics=sem, vmem_limit_bytes=VMEM_LIMIT)


def _sigmoid(x):
    return 1.0 / (1.0 + jnp.exp(-x))


def _dot(a, b):
    return jnp.dot(a, b, preferred_element_type=F32)


def _dot_nt(a, b):
    return lax.dot_general(a, b, (((1,), (1,)), ((), ())), preferred_element_type=F32)


def _dot_tn(a, b):
    return lax.dot_general(a, b, (((0,), (0,)), ((), ())), preferred_element_type=F32)


def _low_half(rows):
    return lax.broadcasted_iota(jnp.int32, (rows, LANE), 1) < HEAD_DIM


def _pair_mean(x, low):
    s_lo = jnp.sum(jnp.where(low, x, 0.0), axis=-1, keepdims=True)
    s_hi = jnp.sum(jnp.where(low, 0.0, x), axis=-1, keepdims=True)
    return jnp.where(low, s_lo, s_hi) * (1.0 / HEAD_DIM)


def _adaln_kernel(c_ref, w_ref, b_ref, o_ref):
    c = c_ref[...]
    s = c * _sigmoid(c)
    o_ref[0] = _dot(s.astype(BF16), w_ref[0].astype(BF16)) + b_ref[0]


def adaln(cond8, ada_w, ada_b):
    tn = 1536
    return pl.pallas_call(
        _adaln_kernel,
        grid=(DEPTH, 6 * D_MODEL // tn),
        in_specs=[
            pl.BlockSpec((8, D_MODEL), lambda l, j: (0, 0)),
            pl.BlockSpec((1, D_MODEL, tn), lambda l, j: (l, 0, j)),
            pl.BlockSpec((1, 1, tn), lambda l, j: (l, 0, j)),
        ],
        out_specs=pl.BlockSpec((1, 8, tn), lambda l, j: (l, 0, j)),
        out_shape=jax.ShapeDtypeStruct((DEPTH, 8, 6 * D_MODEL), F32),
        compiler_params=_cparams("arbitrary", "arbitrary"),
        name="adaln",
    )(cond8, ada_w, ada_b.reshape(DEPTH, 1, 6 * D_MODEL))


def _inproj_kernel(x_ref, sh_ref, sc_ref, w_ref, b_ref, gq_ref, gk_ref, *rest, rope):
    if rope:
        cos_ref, sin_ref, z_ref = rest
    else:
        (z_ref,) = rest
    h = x_ref[...] * (1.0 + sc_ref[0]) + sh_ref[0]
    z = _dot(h.astype(BF16), w_ref[...]) + b_ref[...]
    z_ref[...] = z
    rows = z.shape[0]
    low = _low_half(rows)
    if rope:
        lane = lax.broadcasted_iota(jnp.int32, (rows, LANE), 1)
        first_quarter = (lane & (HEAD_DIM // 4)) == 0

    def norm_pair(x, gain):
        y = x * lax.rsqrt(_pair_mean(x * x, low) + NORM_EPS) * gain
        if rope:
            rot = jnp.where(first_quarter, -pltpu.roll(y, LANE - HEAD_DIM // 4, axis=1),
                            pltpu.roll(y, HEAD_DIM // 4, axis=1))
            y = y * cos_ref[...] + rot * sin_ref[...]
        return y

    for p in range(GQA_W // LANE):
        c0 = OFF_GQ_Q + p * LANE
        z_ref[:, c0:c0 + LANE] = norm_pair(z[:, c0:c0 + LANE], gq_ref[...])
    z_ref[:, OFF_GQ_K:OFF_GQ_K + LANE] = norm_pair(z[:, OFF_GQ_K:OFF_GQ_K + LANE], gk_ref[...])


def in_projection(x, mods3, mod_row, w_pad, bias_row, gain_q, gain_k, cos=None, sin=None):
    n = x.shape[0]
    rope = cos is not None
    const = lambda r, c: pl.BlockSpec((r, c), lambda i: (0, 0))
    in_specs = [
        pl.BlockSpec((TM, D_MODEL), lambda i: (i, 0)),
        pl.BlockSpec((1, 1, D_MODEL), lambda i: (mod_row(i), 0, 0)),
        pl.BlockSpec((1, 1, D_MODEL), lambda i: (mod_row(i), 0, 1)),
        const(D_MODEL, PROJ_PAD), const(1, PROJ_PAD), const(1, LANE), const(1, LANE),
    ]
    args = [x, mods3, mods3, w_pad, bias_row, gain_q, gain_k]
    if rope:
        per = cos.shape[0] // TM
        in_specs += [pl.BlockSpec((TM, LANE), lambda i: (i % per, 0))] * 2
        args += [cos, sin]
    return pl.pallas_call(
        functools.partial(_inproj_kernel, rope=rope),
        grid=(n // TM,),
        in_specs=in_specs,
        out_specs=pl.BlockSpec((TM, PROJ_PAD), lambda i: (i, 0)),
        out_shape=jax.ShapeDtypeStruct((n, PROJ_PAD), F32),
        compiler_params=_cparams("arbitrary"),
        name="in_projection_rope" if rope else "in_projection",
    )(*args)


def _softmax_pv(scores, values):
    m = jnp.max(scores[0], axis=-1, keepdims=True)
    for s in scores[1:]:
        m = jnp.maximum(m, jnp.max(s, axis=-1, keepdims=True))
    l = 0.0
    o = 0.0
    for s, v in zip(scores, values):
        p = jnp.exp(s - m)
        l = l + jnp.sum(p, axis=-1, keepdims=True)
        o = o + _dot(p.astype(BF16), v)
    return o * (1.0 / l)


def _ctx_attn_kernel(naq_ref, nak_ref, nav_ref, gq_ref, gk_ref, gv_ref, oa_ref, oc_ref):
    low = _low_half(SEQ)
    for p in range(NA_W // LANE):
        cols = slice(p * LANE, (p + 1) * LANE)
        q = naq_ref[:, cols]
        k = nak_ref[:, cols].astype(BF16)
        v = nav_ref[:, cols].astype(BF16)
        outs = []
        for half in range(2):
            qm = jnp.where(low if half == 0 else ~low, q, 0.0).astype(BF16)
            outs.append(_softmax_pv([_dot_nt(qm, k) * ATTN_SCALE], [v]))
        oa_ref[:, cols] = jnp.where(low, outs[0], outs[1])
    k = gk_ref[...]
    v = gv_ref[...]
    k_at = (pltpu.roll(k, HEAD_DIM, axis=1).astype(BF16), k.astype(BF16))
    v_at = (pltpu.roll(v, HEAD_DIM, axis=1).astype(BF16), v.astype(BF16))
    group = GQA_HEADS // GQA_KV_HEADS
    for p in range(GQA_W // LANE):
        cols = slice(p * LANE, (p + 1) * LANE)
        q = gq_ref[:, cols]
        outs = []
        for half in range(2):
            g = (2 * p + half) // group
            qm = jnp.where(low if half == 0 else ~low, q, 0.0).astype(BF16)
            outs.append(_softmax_pv([_dot_nt(qm, k_at[g == half]) * ATTN_SCALE], [v_at[g == half]]))
        oc_ref[:, cols] = jnp.where(low, outs[0], outs[1])


def ctx_attention(z):
    col = lambda w, off: pl.BlockSpec((SEQ, w), lambda b: (b, off // w))
    return pl.pallas_call(
        _ctx_attn_kernel,
        grid=(BATCH,),
        in_specs=[col(NA_W, OFF_NA_Q), col(NA_W, OFF_NA_K), col(NA_W, OFF_NA_V),
                  col(GQA_W, OFF_GQ_Q), col(LANE, OFF_GQ_K), col(LANE, OFF_GQ_V)],
        out_specs=[col(NA_W, 0), col(GQA_W, 0)],
        out_shape=[jax.ShapeDtypeStruct((N_CTX_TOK, NA_W), F32), jax.ShapeDtypeStruct((N_CTX_TOK, GQA_W), F32)],
        compiler_params=_cparams("arbitrary"),
        name="ctx_attention",
    )(z, z, z, z, z, z)


GQA_TQ = 512


def _lat_gqa_kernel(q_ref, kl_ref, vl_ref, kc_ref, vc_ref, o_ref):
    low = _low_half(GQA_TQ)
    at = lambda x: (pltpu.roll(x, HEAD_DIM, axis=1).astype(BF16), x.astype(BF16))
    kl_at, vl_at, kc_at, vc_at = at(kl_ref[...]), at(vl_ref[...]), at(kc_ref[0]), at(vc_ref[0])
    group = GQA_HEADS // GQA_KV_HEADS
    for p in range(GQA_W // LANE):
        cols = slice(p * LANE, (p + 1) * LANE)
        q = q_ref[:, cols]
        outs = []
        for half in range(2):
            same = ((2 * p + half) // group) == half
            qm = jnp.where(low if half == 0 else ~low, q, 0.0).astype(BF16)
            outs.append(_softmax_pv(
                [_dot_nt(qm, kc_at[same]) * ATTN_SCALE, _dot_nt(qm, kl_at[same]) * ATTN_SCALE],
                [vc_at[same], vl_at[same]]))
        o_ref[:, cols] = jnp.where(low, outs[0], outs[1])


def lat_gqa(z, cache_k, cache_v):
    per = DEC_SEQ // GQA_TQ
    kv = lambda off: pl.BlockSpec((DEC_SEQ, LANE), lambda b, i: (b, off // LANE))
    cache = pl.BlockSpec((1, PAST_LEN, LANE), lambda b, i: (b, 0, 0))
    return pl.pallas_call(
        _lat_gqa_kernel,
        grid=(DEC_BATCH, per),
        in_specs=[pl.BlockSpec((GQA_TQ, GQA_W), lambda b, i: (b * per + i, OFF_GQ_Q // GQA_W)),
                  kv(OFF_GQ_K), kv(OFF_GQ_V), cache, cache],
        out_specs=pl.BlockSpec((GQA_TQ, GQA_W), lambda b, i: (b * per + i, 0)),
        out_shape=jax.ShapeDtypeStruct((N_LAT_TOK, GQA_W), F32),
        compiler_params=_cparams("arbitrary", "arbitrary"),
        name="lat_gqa",
    )(z, z, z, cache_k, cache_v)


NA_RB = 4
NA_UNION = NA_RB + NA_WIN_ROWS - 1
NA_TQ = NA_RB * GRID_W
NA_TK = NA_UNION * GRID_W
NA_TYPES = 3


def _na_union_start(i):
    return jnp.clip(i * NA_RB - NA_WIN_ROWS // 2, 0, GRID_H - NA_UNION)


def _na_block_type(i):
    return jnp.where(i == 0, 0, jnp.where(i == GRID_H // NA_RB - 1, 2, 1))


def _lat_na_kernel(q_ref, k_ref, v_ref, kc_ref, vc_ref, bias_ref, o_ref):
    i = pl.program_id(1)
    start = pl.multiple_of(_na_union_start(i) * GRID_W, GRID_W)
    low = _low_half(NA_TQ)
    for p in range(NA_W // LANE):
        cols = slice(p * LANE, (p + 1) * LANE)
        q = q_ref[:, cols]
        kl = k_ref[pl.ds(start, NA_TK), cols].astype(BF16)
        vl = v_ref[pl.ds(start, NA_TK), cols].astype(BF16)
        kc = kc_ref[0, :, cols].astype(BF16)
        vc = vc_ref[0, :, cols].astype(BF16)
        outs = []
        for half in range(2):
            qm = jnp.where(low if half == 0 else ~low, q, 0.0).astype(BF16)
            s_loc = _dot_nt(qm, kl) * ATTN_SCALE + bias_ref[2 * p + half, 0]
            s_ctx = _dot_nt(qm, kc) * ATTN_SCALE
            outs.append(_softmax_pv([s_loc, s_ctx], [vl, vc]))
        o_ref[:, cols] = jnp.where(low, outs[0], outs[1])


def na_bias_table(rpb):
    blocks = (0, 1, GRID_H // NA_RB - 1)
    row_sel = np.zeros((NA_TYPES, NA_RB, NA_UNION, 2 * NA_WIN_ROWS - 1), np.float32)
    row_ok = np.zeros((NA_TYPES, NA_RB, NA_UNION), bool)
    for ty, i in enumerate(blocks):
        u0 = int(np.clip(i * NA_RB - NA_WIN_ROWS // 2, 0, GRID_H - NA_UNION))
        for t in range(NA_RB):
            r = i * NA_RB + t
            rs = int(np.clip(r - NA_WIN_ROWS // 2, 0, GRID_H - NA_WIN_ROWS))
            for j in range(NA_UNION):
                kr = u0 + j
                if rs <= kr < rs + NA_WIN_ROWS:
                    row_ok[ty, t, j] = True
                    row_sel[ty, t, j, kr - r + NA_WIN_ROWS - 1] = 1.0
    cq = np.arange(GRID_W)
    ck = np.arange(GRID_W)
    dc = np.clip(ck[None, :] - cq[:, None], -(NA_WIN_COLS - 1), NA_WIN_COLS - 1) + NA_WIN_COLS - 1
    col_start = np.clip(cq - NA_WIN_COLS // 2, 0, GRID_W - NA_WIN_COLS)
    in_win = (ck[None, :] >= col_start[:, None]) & (ck[None, :] < col_start[:, None] + NA_WIN_COLS)
    col_sel = (dc[:, :, None] == np.arange(2 * NA_WIN_COLS - 1)).astype(np.float32)
    rows = jnp.einsum("ytjr,hrc->hytjc", row_sel, rpb.astype(F32), precision=lax.Precision.HIGHEST)
    bias = jnp.einsum("hytjc,qkc->hytqjk", rows, col_sel, precision=lax.Precision.HIGHEST)
    ok = row_ok[:, :, None, :, None] & in_win[None, None, :, None, :]
    bias = jnp.where(ok[None], bias, NEG)
    return bias.reshape(NA_HEADS, NA_TYPES, NA_TQ, NA_TK)


def lat_na(z, cache_k, cache_v, bias):
    per = GRID_H // NA_RB
    kv = lambda off: pl.BlockSpec((DEC_SEQ, NA_W), lambda b, i: (b, off // NA_W))
    cache = pl.BlockSpec((1, PAST_LEN, NA_W), lambda b, i: (b, 0, 0))
    return pl.pallas_call(
        _lat_na_kernel,
        grid=(DEC_BATCH, per),
        in_specs=[pl.BlockSpec((NA_TQ, NA_W), lambda b, i: (b * per + i, 0)),
                  kv(OFF_NA_K), kv(OFF_NA_V), cache, cache,
                  pl.BlockSpec((NA_HEADS, 1, NA_TQ, NA_TK), lambda b, i: (0, _na_block_type(i), 0, 0))],
        out_specs=pl.BlockSpec((NA_TQ, NA_W), lambda b, i: (b * per + i, 0)),
        out_shape=jax.ShapeDtypeStruct((N_LAT_TOK, NA_W), F32),
        compiler_params=_cparams("arbitrary", "arbitrary"),
        name="lat_na",
    )(z, z, z, cache_k, cache_v, bias)


def _log_sigmoid(x):
    return jnp.minimum(x, 0.0) - jnp.log1p(jnp.exp(-jnp.abs(x)))


def _split3(x):
    hi = x.astype(BF16)
    r = x - hi.astype(F32)
    mid = r.astype(BF16)
    lo = (r - mid.astype(F32)).astype(BF16)
    return hi, mid, lo


def _mlstm_gate_tables(gates, fwd):
    t = gates.shape[0]
    jj = lax.broadcasted_iota(jnp.int32, (t, t), 0)
    ss = lax.broadcasted_iota(jnp.int32, (t, t), 1)
    allowed = (ss <= jj) if fwd else (ss >= jj)
    allowed_b = jnp.where(allowed, 1.0, 0.0).astype(BF16)
    feeds_b = jnp.where((jj <= ss) if fwd else (jj >= ss), 1.0, 0.0).astype(BF16)
    eye_b = jnp.where(jj == ss, 1.0, 0.0).astype(BF16)
    lf = _split3(_log_sigmoid(gates))
    gs = _split3(gates)
    b_col = _dot(allowed_b, lf[0]) + _dot(allowed_b, lf[1]) + _dot(allowed_b, lf[2])
    b_row = _dot_tn(lf[0], feeds_b) + _dot_tn(lf[1], feeds_b) + _dot_tn(lf[2], feeds_b)
    ig_row = _dot_tn(gs[0], eye_b) + _dot_tn(gs[1], eye_b) + _dot_tn(gs[2], eye_b)
    return ig_row, b_row, b_col, allowed


def _mlstm_head(q, k, v, ig, ig_row, b_col, b_row, allowed, c, n, m, half, fwd):
    t = q.shape[0]
    dmat = jnp.where(allowed, b_col - b_row + ig_row, NEG)
    m_inter = b_col + m
    m_j = jnp.maximum(m_inter, jnp.max(dmat, axis=1, keepdims=True))
    w = jnp.exp(dmat - m_j)
    decay = jnp.exp(m_inter - m_j)
    mine = _low_half(t) if half == 0 else ~_low_half(t)
    qm = jnp.where(mine, q, 0.0)
    qb = qm.astype(BF16)
    vb = v.astype(BF16)
    qk = _dot_nt(qb, k.astype(BF16)) * w
    num = decay * _dot(qb, c.astype(BF16)) + _dot(qk.astype(BF16), vb)
    den = decay * jnp.sum(qm * n, axis=1, keepdims=True) + jnp.sum(qk, axis=1, keepdims=True)
    h = jnp.where(mine, num * (1.0 / jnp.maximum(jnp.abs(den), jnp.exp(-m_j))), 0.0)
    last = t - 1 if fwd else 0
    b_last = b_col[last:last + 1, :]
    g = b_last - b_col + ig
    m_new = jnp.maximum(b_last + m, jnp.max(g, axis=0, keepdims=True))
    ws = jnp.exp(g - m_new)
    d_last = jnp.exp(b_last + m - m_new)
    kw = jnp.where(mine, k * ws, 0.0)
    c_add = _dot_tn(kw.astype(BF16), vb)
    n_add = jnp.sum(kw, axis=0, keepdims=True)
    return h, d_last, c_add, n_add, m_new


def _mlstm_kernel(zf_ref, zb_ref, c0_ref, n0_ref, m0_ref, hf_ref, hb_ref, co_ref, no_ref, mo_ref,
                  c_s, n_s, m_s):
    chunk = pl.program_id(1)
    pairs = ML_W // LANE
    zero = jnp.zeros((HEAD_DIM, HEAD_DIM), F32)

    @pl.when(chunk == 0)
    def _():
        for d in range(2):
            for p in range(pairs):
                top = jnp.concatenate([c0_ref[0, d, 2 * p], zero], axis=1)
                bot = jnp.concatenate([zero, c0_ref[0, d, 2 * p + 1]], axis=1)
                c_s[d * pairs + p] = jnp.concatenate([top, bot], axis=0)
                e = d * ML_HEADS + 2 * p
                n_s[d * pairs + p] = jnp.concatenate([n0_ref[0, e:e + 1, :], n0_ref[0, e + 1:e + 2, :]], axis=1)
            m_s[d] = m0_ref[0, d * ML_HEADS:(d + 1) * ML_HEADS, :]

    ri = lax.broadcasted_iota(jnp.int32, (LANE, LANE), 0) < HEAD_DIM
    ci = lax.broadcasted_iota(jnp.int32, (LANE, LANE), 1) < HEAD_DIM
    low_row = lax.broadcasted_iota(jnp.int32, (1, LANE), 1) < HEAD_DIM
    for d, (z_ref, h_ref) in enumerate(((zf_ref, hf_ref), (zb_ref, hb_ref))):
        gates = z_ref[:, OFF_ML_G - OFF_ML_Q:OFF_ML_G - OFF_ML_Q + LANE]
        ig_rows, b_rows, b_cols, allowed = _mlstm_gate_tables(gates, d == 0)
        m_all = m_s[d]
        m_rows = []
        for p in range(pairs):
            q = z_ref[:, p * LANE:(p + 1) * LANE]
            k = z_ref[:, ML_W + p * LANE:ML_W + (p + 1) * LANE] * ATTN_SCALE
            v = z_ref[:, 2 * ML_W + p * LANE:2 * ML_W + (p + 1) * LANE]
            c = c_s[d * pairs + p]
            n = n_s[d * pairs + p]
            res = []
            for half in range(2):
                hd = 2 * p + half
                gi = 2 * ML_HEADS * d + hd
                gf = gi + ML_HEADS
                res.append(_mlstm_head(q, k, v, gates[:, gi:gi + 1], ig_rows[gi:gi + 1, :], b_cols[:, gf:gf + 1],
                                       b_rows[gf:gf + 1, :], allowed, c, n, m_all[hd:hd + 1, 0:1], half, d == 0))
            (h0, dl0, ca0, na0, mn0), (h1, dl1, ca1, na1, mn1) = res
            h_ref[:, p * LANE:(p + 1) * LANE] = h0 + h1
            c_s[d * pairs + p] = jnp.where(ri & ci, dl0 * c + ca0, jnp.where(~ri & ~ci, dl1 * c + ca1, 0.0))
            n_s[d * pairs + p] = jnp.where(low_row, dl0 * n + na0, dl1 * n + na1)
            m_rows += [jnp.broadcast_to(mn0, (1, LANE)), jnp.broadcast_to(mn1, (1, LANE))]
        m_s[d] = jnp.concatenate(m_rows, axis=0)

    @pl.when(chunk == pl.num_programs(1) - 1)
    def _():
        for d in range(2):
            for p in range(pairs):
                c = c_s[d * pairs + p]
                n = n_s[d * pairs + p]
                co_ref[0, d, 2 * p] = c[:HEAD_DIM, :HEAD_DIM]
                co_ref[0, d, 2 * p + 1] = c[HEAD_DIM:, HEAD_DIM:]
                e = d * ML_HEADS + 2 * p
                no_ref[0, e:e + 1, :] = n[:, :HEAD_DIM]
                no_ref[0, e + 1:e + 2, :] = n[:, HEAD_DIM:]
            mo_ref[0, d * ML_HEADS:(d + 1) * ML_HEADS, :] = m_s[d]


def mlstm(z, batch, seq, c0, n0, m0):
    nc = seq // ML_CHUNK
    zcol = OFF_ML_Q // ML_BLOCK_W
    state_c = pl.BlockSpec((1, 2, ML_HEADS, HEAD_DIM, HEAD_DIM), lambda b, c: (b, 0, 0, 0, 0))
    state_n = pl.BlockSpec((1, 2 * ML_HEADS, HEAD_DIM), lambda b, c: (b, 0, 0))
    state_m = pl.BlockSpec((1, 2 * ML_HEADS, LANE), lambda b, c: (b, 0, 0))
    pairs = ML_W // LANE
    return pl.pallas_call(
        _mlstm_kernel,
        grid=(batch, nc),
        in_specs=[
            pl.BlockSpec((ML_CHUNK, ML_BLOCK_W), lambda b, c: (b * nc + c, zcol)),
            pl.BlockSpec((ML_CHUNK, ML_BLOCK_W), lambda b, c: (b * nc + nc - 1 - c, zcol)),
            state_c, state_n, state_m,
        ],
        out_specs=[
            pl.BlockSpec((ML_CHUNK, ML_W), lambda b, c: (b * nc + c, 0)),
            pl.BlockSpec((ML_CHUNK, ML_W), lambda b, c: (b * nc + nc - 1 - c, 0)),
            state_c, state_n, state_m,
        ],
        out_shape=[
            jax.ShapeDtypeStruct((batch * seq, ML_W), F32),
            jax.ShapeDtypeStruct((batch * seq, ML_W), F32),
            jax.ShapeDtypeStruct(c0.shape, F32),
            jax.ShapeDtypeStruct(n0.shape, F32),
            jax.ShapeDtypeStruct(m0.shape, F32),
        ],
        scratch_shapes=[
            pltpu.VMEM((2 * pairs, LANE, LANE), F32),
            pltpu.VMEM((2 * pairs, 1, LANE), F32),
            pltpu.VMEM((2, ML_HEADS, LANE), F32),
        ],
        compiler_params=_cparams("arbitrary", "arbitrary"),
        name="mlstm",
    )(z, z, c0, n0, m0)


def _layer_norm(u, g, b):
    mu = jnp.mean(u, axis=-1, keepdims=True)
    uc = u - mu
    var = jnp.mean(uc * uc, axis=-1, keepdims=True)
    return uc * lax.rsqrt(var + NORM_EPS) * g + b


def _outproj_kernel(x_ref, a_ref, hf_ref, hb_ref, og0_ref, og1_ref, c_ref, w_ref, mg_ref, g1_ref, sh2_ref, sc2_ref,
                    lng_ref, lnb_ref, rw_ref, x1_ref, h2_ref, lg_ref):
    low = _low_half(TM)
    mixed_b = []
    for p, og_ref in enumerate((og0_ref, og1_ref)):
        cols = slice(p * LANE, (p + 1) * LANE)
        h = hf_ref[:, cols] + hb_ref[:, cols]
        hc = h - _pair_mean(h, low)
        hn = hc * lax.rsqrt(_pair_mean(hc * hc, low) + NORM_EPS)
        mixed_b.append(hn * mg_ref[:, cols] * _sigmoid(og_ref[...]))
    mixed = jnp.concatenate([a_ref[...]] + mixed_b + [c_ref[...]], axis=-1).astype(BF16)
    y = _dot(mixed, w_ref[...])
    x1 = _layer_norm(DEEPNORM_ALPHA * x_ref[...] + g1_ref[0] * y, lng_ref[...], lnb_ref[...])
    h2 = (x1 * (1.0 + sc2_ref[0]) + sh2_ref[0]).astype(BF16)
    x1_ref[...] = x1
    h2_ref[...] = h2
    lg_ref[...] = _dot(h2, rw_ref[...])


def out_projection(x, out_a, hf, hb, z, out_c, w_out, ml_gain, mods3, mod_row, ln_g, ln_b, router_pad):
    n = x.shape[0]
    mod = lambda k: pl.BlockSpec((1, 1, D_MODEL), lambda i: (mod_row(i), 0, k))
    row = lambda w: pl.BlockSpec((TM, w), lambda i: (i, 0))
    zcol = lambda off: pl.BlockSpec((TM, LANE), lambda i: (i, off // LANE))
    const = lambda r, c: pl.BlockSpec((r, c), lambda i: (0, 0))
    return pl.pallas_call(
        _outproj_kernel,
        grid=(n // TM,),
        in_specs=[row(D_MODEL), row(NA_W), row(ML_W), row(ML_W), zcol(OFF_ML_O), zcol(OFF_ML_O + LANE), row(GQA_W),
                  const(D_MODEL, D_MODEL), const(1, ML_W), mod(2), mod(3), mod(4), const(1, D_MODEL),
                  const(1, D_MODEL), const(D_MODEL, LANE)],
        out_specs=[row(D_MODEL), row(D_MODEL), row(LANE)],
        out_shape=[
            jax.ShapeDtypeStruct((n, D_MODEL), F32),
            jax.ShapeDtypeStruct((n, D_MODEL), BF16),
            jax.ShapeDtypeStruct((n, LANE), F32),
        ],
        compiler_params=_cparams("arbitrary"),
        name="out_projection",
    )(x, out_a, hf, hb, z, z, out_c, w_out, ml_gain, mods3, mods3, mods3, ln_g.reshape(1, D_MODEL),
      ln_b.reshape(1, D_MODEL), router_pad)


ROUTE_BLK = 256


def _prefix_count(x, tri):
    n = x.shape[0]
    outs = []
    carry = jnp.zeros((1, LANE), F32)
    for i in range(n // ROUTE_BLK):
        blk = x[i * ROUTE_BLK:(i + 1) * ROUTE_BLK]
        outs.append(_dot(tri, blk) + carry)
        carry = carry + jnp.sum(blk.astype(F32), axis=0, keepdims=True)
    return jnp.concatenate(outs, axis=0) if len(outs) > 1 else outs[0]


def _route_kernel(lg_ref, aff_ref, code_ref, coder_ref, *, cap):
    lg = lg_ref[0]
    n = lg.shape[0]
    lane = lax.broadcasted_iota(jnp.int32, (n, LANE), 1)
    valid = lane < N_EXPERTS
    mx = jnp.max(jnp.where(valid, lg, NEG), axis=1, keepdims=True)
    ex = jnp.where(valid, jnp.exp(lg - mx), 0.0)
    aff = ex / jnp.sum(ex, axis=1, keepdims=True)

    def search(i, bits):
        cand = bits | lax.shift_left(jnp.int32(1), 30 - i)
        cnt = jnp.sum(jnp.where(aff >= pltpu.bitcast(cand, F32), 1.0, 0.0), axis=0, keepdims=True)
        return jnp.where(cnt >= cap, cand, bits)

    floor_bits = lax.fori_loop(0, 31, search, jnp.zeros((1, LANE), jnp.int32))
    thr = jnp.min(jnp.where(aff >= pltpu.bitcast(floor_bits, F32), aff, 2.0), axis=0, keepdims=True)
    gt = aff > thr
    eq = aff == thr
    need = cap - jnp.sum(jnp.where(gt, 1.0, 0.0), axis=0, keepdims=True)
    ti = lax.broadcasted_iota(jnp.int32, (ROUTE_BLK, ROUTE_BLK), 0)
    tj = lax.broadcasted_iota(jnp.int32, (ROUTE_BLK, ROUTE_BLK), 1)
    tri = jnp.where(tj < ti, 1.0, 0.0).astype(BF16)
    eq_rank = _prefix_count(jnp.where(eq, 1.0, 0.0).astype(BF16), tri)
    sel = jnp.where(gt, 1.0, jnp.where(eq & (eq_rank < need), 1.0, 0.0))
    pos = _prefix_count(sel.astype(BF16), tri)
    code = jnp.where((sel > 0.5) & valid, pos, -1.0)
    aff_ref[0] = aff
    code_ref[0] = code
    ei = lax.broadcasted_iota(jnp.int32, (N_EXPERTS, LANE), 0)
    ej = lax.broadcasted_iota(jnp.int32, (N_EXPERTS, LANE), 1)
    eye = jnp.where(ei == ej, 1.0, 0.0).astype(BF16)
    coder_ref[0] = _dot_nt(eye, code.astype(BF16))


def route(logits, n, cap):
    sets = logits.shape[0] // n
    lg = logits.reshape(sets, n, LANE)
    tok = pl.BlockSpec((1, n, LANE), lambda s: (s, 0, 0))
    return pl.pallas_call(
        functools.partial(_route_kernel, cap=cap),
        grid=(sets,),
        in_specs=[tok],
        out_specs=[tok, tok, pl.BlockSpec((1, N_EXPERTS, n), lambda s: (s, 0, 0))],
        out_shape=[
            jax.ShapeDtypeStruct((sets, n, LANE), F32),
            jax.ShapeDtypeStruct((sets, n, LANE), F32),
            jax.ShapeDtypeStruct((sets, N_EXPERTS, n), F32),
        ],
        compiler_params=_cparams("arbitrary"),
        name="route",
    )(lg)


def _gather_kernel(coder_ref, h_ref, xe_ref, *, cap, epb):
    e0 = pl.program_id(1) * epb
    n = coder_ref.shape[2]
    ci = lax.broadcasted_iota(jnp.int32, (cap, n), 0).astype(F32)
    onehot = [jnp.where(ci == coder_ref[0, pl.ds(e0 + k, 1), :], 1.0, 0.0).astype(BF16) for k in range(epb)]
    onehot = jnp.concatenate(onehot, axis=0) if epb > 1 else onehot[0]
    rows = _dot(onehot, h_ref[...]).astype(BF16)
    for k in range(epb):
        xe_ref[k] = rows[k * cap:(k + 1) * cap]


def gather_tokens(code_rows, h2, n, cap):
    sets = code_rows.shape[0]
    epb = max(1, 512 // cap)
    return pl.pallas_call(
        functools.partial(_gather_kernel, cap=cap, epb=epb),
        grid=(sets, N_EXPERTS // epb),
        in_specs=[
            pl.BlockSpec((1, N_EXPERTS, n), lambda s, e: (s, 0, 0)),
            pl.BlockSpec((n, D_MODEL), lambda s, e: (s, 0)),
        ],
        out_specs=pl.BlockSpec((epb, cap, D_MODEL), lambda s, e: (e, s, 0)),
        out_shape=jax.ShapeDtypeStruct((N_EXPERTS, sets * cap, D_MODEL), BF16),
        compiler_params=_cparams("arbitrary", "arbitrary"),
        name="gather_tokens",
    )(code_rows, h2)


def _expert_kernel(xi_ref, xl_ref, wg_ref, wu_ref, wd_ref, yi_ref, yl_ref, acci_ref, accl_ref):
    j = pl.program_id(1)

    @pl.when(j == 0)
    def _():
        acci_ref[...] = jnp.zeros_like(acci_ref)
        accl_ref[...] = jnp.zeros_like(accl_ref)

    xs = (xi_ref[0], xl_ref[0])
    sums = [None, None]
    for s0 in range(0, EXP_TF, EXP_SUB):
        s1 = min(s0 + EXP_SUB, EXP_TF)
        wg = wg_ref[0, 0, :, s0:s1].astype(BF16)
        wu = wu_ref[0, 0, :, s0:s1].astype(BF16)
        wd = wd_ref[0, 0, s0:s1, :].astype(BF16)
        for g in range(2):
            a = _dot(xs[g], wg)
            b = _dot(xs[g], wu)
            hid = (a * _sigmoid(a) * b).astype(BF16)
            part = _dot(hid, wd)
            sums[g] = part if sums[g] is None else sums[g] + part
    acci_ref[...] += sums[0]
    accl_ref[...] += sums[1]

    @pl.when(j == pl.num_programs(1) - 1)
    def _():
        yi_ref[0] = acci_ref[...].astype(BF16)
        yl_ref[0] = accl_ref[...].astype(BF16)


def experts(xe_ctx, xe_lat, w_gate, w_up, w_down, layer):
    xin = lambda s: pl.BlockSpec((1, s, D_MODEL), lambda e, j: (e, 0, 0))
    return pl.pallas_call(
        _expert_kernel,
        grid=(N_EXPERTS, EXPERT_HIDDEN // EXP_TF),
        in_specs=[
            xin(SLOTS_CTX), xin(SLOTS_LAT),
            pl.BlockSpec((1, 1, D_MODEL, EXP_TF), lambda e, j: (layer, e, 0, j)),
            pl.BlockSpec((1, 1, D_MODEL, EXP_TF), lambda e, j: (layer, e, 0, j)),
            pl.BlockSpec((1, 1, EXP_TF, D_MODEL), lambda e, j: (layer, e, j, 0)),
        ],
        out_specs=[xin(SLOTS_CTX), xin(SLOTS_LAT)],
        out_shape=[
            jax.ShapeDtypeStruct((N_EXPERTS, SLOTS_CTX, D_MODEL), BF16),
            jax.ShapeDtypeStruct((N_EXPERTS, SLOTS_LAT, D_MODEL), BF16),
        ],
        scratch_shapes=[pltpu.VMEM((SLOTS_CTX, D_MODEL), F32), pltpu.VMEM((SLOTS_LAT, D_MODEL), F32)],
        compiler_params=_cparams("arbitrary", "arbitrary"),
        name="experts",
    )(xe_ctx, xe_lat, w_gate, w_up, w_down)


COMB_TT = 256


def _combine_kernel(code_ref, aff_ref, ye_ref, x1_ref, g2_ref, lng_ref, lnb_ref, o_ref, *, cap):
    code = code_ref[0]
    aff = aff_ref[0]
    tt = code.shape[0]
    li = lax.broadcasted_iota(jnp.int32, (tt, cap), 1).astype(F32)
    acc = jnp.zeros((tt, D_MODEL), F32)
    for e in range(N_EXPERTS):
        onehot = jnp.where(li == code[:, e:e + 1], 1.0, 0.0).astype(BF16)
        acc = acc + aff[:, e:e + 1] * _dot(onehot, ye_ref[e])
    u = DEEPNORM_ALPHA * x1_ref[...] + g2_ref[0] * acc
    o_ref[...] = _layer_norm(u, lng_ref[...], lnb_ref[...])


def combine(code, aff, ye, x1, mods3, set_mod_row, ln_g, ln_b, n, cap):
    sets = code.shape[0]
    per = n // COMB_TT
    tok = pl.BlockSpec((1, COMB_TT, LANE), lambda s, i: (s, i, 0))
    return pl.pallas_call(
        functools.partial(_combine_kernel, cap=cap),
        grid=(sets, per),
        in_specs=[
            tok, tok,
            pl.BlockSpec((N_EXPERTS, cap, D_MODEL), lambda s, i: (0, s, 0)),
            pl.BlockSpec((COMB_TT, D_MODEL), lambda s, i: (s * per + i, 0)),
            pl.BlockSpec((1, 1, D_MODEL), lambda s, i: (set_mod_row(s), 0, 5)),
            pl.BlockSpec((1, D_MODEL), lambda s, i: (0, 0)),
            pl.BlockSpec((1, D_MODEL), lambda s, i: (0, 0)),
        ],
        out_specs=pl.BlockSpec((COMB_TT, D_MODEL), lambda s, i: (s * per + i, 0)),
        out_shape=jax.ShapeDtypeStruct((sets * n, D_MODEL), F32),
        compiler_params=_cparams("arbitrary", "arbitrary"),
        name="combine",
    )(code, aff, ye, x1, mods3, ln_g.reshape(1, D_MODEL), ln_b.reshape(1, D_MODEL))


def _axial_rope_tables():
    t = jnp.arange(DEC_SEQ)
    row = (t // GRID_W).astype(F32)
    col = (t % GRID_W).astype(F32)
    half = HEAD_DIM // 2
    inv = ROPE_THETA ** (-jnp.arange(0, half, 2, dtype=F32) / half)
    ang_r = row[:, None] * inv
    ang_c = col[:, None] * inv
    ang = jnp.concatenate([ang_r, ang_r, ang_c, ang_c] * 2, -1)
    return jnp.cos(ang), jnp.sin(ang)


def _ctx_mod_row(i):
    return 0


def _lat_tile_mod_row(i):
    return 1 + i // (DEC_SEQ // TM)


def _lat_set_mod_row(s):
    return 1 + s


def kernel(x_prompt, x_sample, c, cache_na_k, cache_na_v, cache_gqa_k, cache_gqa_v, state_mlstm_c, state_mlstm_n,
           state_mlstm_m, c_ctx, ada_w, ada_b, w_in, b_gate, w_out, na_rpb, qk_norm_g, ml_norm_g, ln_g, ln_b,
           router_w, w_gate, w_up, w_down):
    cond8 = jnp.concatenate([c_ctx[None, :], c, jnp.zeros((8 - 1 - DEC_BATCH, D_MODEL), F32)], 0)
    mods_all = adaln(cond8, ada_w, ada_b)
    xc = x_prompt.reshape(N_CTX_TOK, D_MODEL)
    xl = x_sample.reshape(N_LAT_TOK, D_MODEL)
    cos, sin = _axial_rope_tables()
    zeros_c = jnp.zeros((BATCH, 2, ML_HEADS, HEAD_DIM, HEAD_DIM), F32)
    zeros_n = jnp.zeros((BATCH, 2 * ML_HEADS, HEAD_DIM), F32)
    zeros_m = jnp.zeros((BATCH, 2 * ML_HEADS, LANE), F32)
    new = [[] for _ in range(7)]

    for l in range(DEPTH):
        mods3 = mods_all[l].reshape(8, 1, 6 * D_MODEL)
        wl = w_in[l]
        gate_pad = jnp.zeros((D_MODEL, LANE - N_GATES), F32)
        w_pad = jnp.concatenate([wl[:, :OFF_ML_G + N_GATES], gate_pad, wl[:, OFF_ML_G + N_GATES:]], 1).astype(BF16)
        bias_row = jnp.zeros((1, PROJ_PAD), F32).at[0, OFF_ML_G:OFF_ML_G + N_GATES].set(b_gate[l])
        gain_q = jnp.tile(qk_norm_g[l, 0], 2).reshape(1, LANE)
        gain_k = jnp.tile(qk_norm_g[l, 1], 2).reshape(1, LANE)
        zc = in_projection(xc, mods3, _ctx_mod_row, w_pad, bias_row, gain_q, gain_k)
        zl = in_projection(xl, mods3, _lat_tile_mod_row, w_pad, bias_row, gain_q, gain_k, cos, sin)

        out_a_c, out_c_c = ctx_attention(zc)
        hf_c, hb_c, sc, sn, sm = mlstm(zc, BATCH, SEQ, zeros_c, zeros_n, zeros_m)
        out_a_l = lat_na(zl, cache_na_k[:, l].reshape(DEC_BATCH, PAST_LEN, NA_W),
                         cache_na_v[:, l].reshape(DEC_BATCH, PAST_LEN, NA_W), na_bias_table(na_rpb[l]))
        out_c_l = lat_gqa(zl, cache_gqa_k[:, l].reshape(DEC_BATCH, PAST_LEN, GQA_KV_W),
                          cache_gqa_v[:, l].reshape(DEC_BATCH, PAST_LEN, GQA_KV_W))
        m0 = jnp.broadcast_to(state_mlstm_m[:, l].reshape(DEC_BATCH, 2 * ML_HEADS, 1), (DEC_BATCH, 2 * ML_HEADS, LANE))
        hf_l, hb_l = mlstm(zl, DEC_BATCH, DEC_SEQ, state_mlstm_c[:, l],
                           state_mlstm_n[:, l].reshape(DEC_BATCH, 2 * ML_HEADS, HEAD_DIM), m0)[:2]

        new[0].append(zc[:, OFF_NA_K:OFF_NA_K + NA_W].reshape(BATCH, SEQ, NA_HEADS, HEAD_DIM))
        new[1].append(zc[:, OFF_NA_V:OFF_NA_V + NA_W].reshape(BATCH, SEQ, NA_HEADS, HEAD_DIM))
        new[2].append(zc[:, OFF_GQ_K:OFF_GQ_K + GQA_KV_W].reshape(BATCH, SEQ, GQA_KV_HEADS, HEAD_DIM))
        new[3].append(zc[:, OFF_GQ_V:OFF_GQ_V + GQA_KV_W].reshape(BATCH, SEQ, GQA_KV_HEADS, HEAD_DIM))
        new[4].append(sc)
        new[5].append(sn.reshape(BATCH, 2, ML_HEADS, HEAD_DIM))
        new[6].append(sm[:, :, 0].reshape(BATCH, 2, ML_HEADS))

        router_pad = jnp.concatenate([router_w[l], jnp.zeros((D_MODEL, LANE - N_EXPERTS), F32)], 1).astype(BF16)
        w_out_b = w_out[l].astype(BF16)
        ml_gain = ml_norm_g[l].reshape(1, ML_W)
        x1_c, h2_c, lg_c = out_projection(xc, out_a_c, hf_c, hb_c, zc, out_c_c, w_out_b, ml_gain, mods3,
                                          _ctx_mod_row, ln_g[l, 0], ln_b[l, 0], router_pad)
        x1_l, h2_l, lg_l = out_projection(xl, out_a_l, hf_l, hb_l, zl, out_c_l, w_out_b, ml_gain, mods3,
                                          _lat_tile_mod_row, ln_g[l, 0], ln_b[l, 0], router_pad)
        aff_c, code_c, crow_c = route(lg_c, SEQ, CAP_CTX)
        aff_l, code_l, crow_l = route(lg_l, DEC_SEQ, CAP_LAT)
        xe_c = gather_tokens(crow_c, h2_c, SEQ, CAP_CTX)
        xe_l = gather_tokens(crow_l, h2_l, DEC_SEQ, CAP_LAT)
        ye_c, ye_l = experts(xe_c, xe_l, w_gate, w_up, w_down, l)
        xc = combine(code_c, aff_c, ye_c, x1_c, mods3, _ctx_mod_row, ln_g[l, 1], ln_b[l, 1], SEQ, CAP_CTX)
        xl = combine(code_l, aff_l, ye_l, x1_l, mods3, _lat_set_mod_row, ln_g[l, 1], ln_b[l, 1], DEC_SEQ, CAP_LAT)

    y_prompt = xc.reshape(BATCH, SEQ, D_MODEL)
    y_sample = xl.reshape(DEC_BATCH, DEC_SEQ, D_MODEL)
    outs = [jnp.stack(t, 1) for t in new]
    return (y_prompt, y_sample, *outs)
```

```python
import functools

import jax
import jax.numpy as jnp
import numpy as np
from jax import lax
from jax.experimental import pallas as pl
from jax.experimental.pallas import tpu as pltpu

D_MODEL = 1024
BATCH = 16
SEQ = 256
DEPTH = 2
DEC_BATCH = 2
DEC_SEQ = 2048
PAST_LEN = 256
GRID_W = 64
GRID_H = DEC_SEQ // GRID_W
HEAD_DIM = 64
NA_HEADS = 6
NA_WIN_ROWS = 8
NA_WIN_COLS = 16
ML_HEADS = 4
GQA_HEADS = 6
GQA_KV_HEADS = 2
ROPE_THETA = 10000.0
N_EXPERTS = 16
EC_CAPACITY = 2
EXPERT_HIDDEN = 2816
NORM_EPS = 1e-6
NA_W = NA_HEADS * HEAD_DIM
ML_W = ML_HEADS * HEAD_DIM
GQA_W = GQA_HEADS * HEAD_DIM
GQA_KV_W = GQA_KV_HEADS * HEAD_DIM
N_GATES = 4 * ML_HEADS
DEEPNORM_ALPHA = (2 * DEPTH) ** 0.25
ATTN_SCALE = HEAD_DIM ** -0.5
F32 = jnp.float32
BF16 = jnp.bfloat16

LANE = 128
N_CTX_TOK = BATCH * SEQ
N_LAT_TOK = DEC_BATCH * DEC_SEQ
NEG = -1e30

OFF_NA_Q = 0
OFF_NA_K = OFF_NA_Q + NA_W
OFF_NA_V = OFF_NA_K + NA_W
OFF_ML_Q = OFF_NA_V + NA_W
OFF_ML_K = OFF_ML_Q + ML_W
OFF_ML_V = OFF_ML_K + ML_W
OFF_ML_O = OFF_ML_V + ML_W
OFF_ML_G = OFF_ML_O + ML_W
OFF_GQ_Q = OFF_ML_G + LANE
OFF_GQ_K = OFF_GQ_Q + GQA_W
OFF_GQ_V = OFF_GQ_K + GQA_KV_W
PROJ_PAD = OFF_GQ_V + GQA_KV_W
ML_BLOCK_W = OFF_GQ_Q - OFF_ML_Q

TM = 512
ML_CHUNK = 256
EXP_TF = EXPERT_HIDDEN // 2
EXP_SUB = 256
CAP_CTX = EC_CAPACITY * SEQ // N_EXPERTS
CAP_LAT = EC_CAPACITY * DEC_SEQ // N_EXPERTS
SLOTS_CTX = BATCH * CAP_CTX
SLOTS_LAT = DEC_BATCH * CAP_LAT
VMEM_LIMIT = 56 * 1024 * 1024


def _cparams(*sem):
    return pltpu.CompilerParams(dimension_semantics=sem, vmem_limit_bytes=VMEM_LIMIT)


def _sigmoid(x):
    return 1.0 / (1.0 + jnp.exp(-x))


def _dot(a, b):
    return jnp.dot(a, b, preferred_element_type=F32)


def _dot_nt(a, b):
    return lax.dot_general(a, b, (((1,), (1,)), ((), ())), preferred_element_type=F32)


def _dot_tn(a, b):
    return lax.dot_general(a, b, (((0,), (0,)), ((), ())), preferred_element_type=F32)


def _low_half(rows):
    return lax.broadcasted_iota(jnp.int32, (rows, LANE), 1) < HEAD_DIM


def _pair_mean(x, low):
    s_lo = jnp.sum(jnp.where(low, x, 0.0), axis=-1, keepdims=True)
    s_hi = jnp.sum(jnp.where(low, 0.0, x), axis=-1, keepdims=True)
    return jnp.where(low, s_lo, s_hi) * (1.0 / HEAD_DIM)


def _adaln_kernel(c_ref, w_ref, b_ref, o_ref):
    c = c_ref[...]
    s = c * _sigmoid(c)
    o_ref[0] = _dot(s.astype(BF16), w_ref[0].astype(BF16)) + b_ref[0]


def adaln(cond8, ada_w, ada_b):
    tn = 1536
    return pl.pallas_call(
        _adaln_kernel,
        grid=(DEPTH, 6 * D_MODEL // tn),
        in_specs=[
            pl.BlockSpec((8, D_MODEL), lambda l, j: (0, 0)),
            pl.BlockSpec((1, D_MODEL, tn), lambda l, j: (l, 0, j)),
            pl.BlockSpec((1, 1, tn), lambda l, j: (l, 0, j)),
        ],
        out_specs=pl.BlockSpec((1, 8, tn), lambda l, j: (l, 0, j)),
        out_shape=jax.ShapeDtypeStruct((DEPTH, 8, 6 * D_MODEL), F32),
        compiler_params=_cparams("arbitrary", "arbitrary"),
        name="adaln",
    )(cond8, ada_w, ada_b.reshape(DEPTH, 1, 6 * D_MODEL))


def _inproj_kernel(x_ref, sh_ref, sc_ref, w_ref, b_ref, gq_ref, gk_ref, *rest, rope):
    if rope:
        cos_ref, sin_ref, z_ref = rest
    else:
        (z_ref,) = rest
    h = x_ref[...] * (1.0 + sc_ref[0]) + sh_ref[0]
    z = _dot(h.astype(BF16), w_ref[...]) + b_ref[...]
    z_ref[...] = z
    rows = z.shape[0]
    low = _low_half(rows)
    if rope:
        lane = lax.broadcasted_iota(jnp.int32, (rows, LANE), 1)
        first_quarter = (lane & (HEAD_DIM // 4)) == 0

    def norm_pair(x, gain):
        y = x * lax.rsqrt(_pair_mean(x * x, low) + NORM_EPS) * gain
        if rope:
            rot = jnp.where(first_quarter, -pltpu.roll(y, LANE - HEAD_DIM // 4, axis=1),
                            pltpu.roll(y, HEAD_DIM // 4, axis=1))
            y = y * cos_ref[...] + rot * sin_ref[...]
        return y

    for p in range(GQA_W // LANE):
        c0 = OFF_GQ_Q + p * LANE
        z_ref[:, c0:c0 + LANE] = norm_pair(z[:, c0:c0 + LANE], gq_ref[...])
    z_ref[:, OFF_GQ_K:OFF_GQ_K + LANE] = norm_pair(z[:, OFF_GQ_K:OFF_GQ_K + LANE], gk_ref[...])


def in_projection(x, mods3, mod_row, w_pad, bias_row, gain_q, gain_k, cos=None, sin=None):
    n = x.shape[0]
    rope = cos is not None
    const = lambda r, c: pl.BlockSpec((r, c), lambda i: (0, 0))
    in_specs = [
        pl.BlockSpec((TM, D_MODEL), lambda i: (i, 0)),
        pl.BlockSpec((1, 1, D_MODEL), lambda i: (mod_row(i), 0, 0)),
        pl.BlockSpec((1, 1, D_MODEL), lambda i: (mod_row(i), 0, 1)),
        const(D_MODEL, PROJ_PAD), const(1, PROJ_PAD), const(1, LANE), const(1, LANE),
    ]
    args = [x, mods3, mods3, w_pad, bias_row, gain_q, gain_k]
    if rope:
        per = cos.shape[0] // TM
        in_specs += [pl.BlockSpec((TM, LANE), lambda i: (i % per, 0))] * 2
        args += [cos, sin]
    return pl.pallas_call(
        functools.partial(_inproj_kernel, rope=rope),
        grid=(n // TM,),
        in_specs=in_specs,
        out_specs=pl.BlockSpec((TM, PROJ_PAD), lambda i: (i, 0)),
        out_shape=jax.ShapeDtypeStruct((n, PROJ_PAD), F32),
        compiler_params=_cparams("arbitrary"),
        name="in_projection_rope" if rope else "in_projection",
    )(*args)


def _softmax_pv(scores, values):
    m = jnp.max(scores[0], axis=-1, keepdims=True)
    for s in scores[1:]:
        m = jnp.maximum(m, jnp.max(s, axis=-1, keepdims=True))
    l = 0.0
    o = 0.0
    for s, v in zip(scores, values):
        p = jnp.exp(s - m)
        l = l + jnp.sum(p, axis=-1, keepdims=True)
        o = o + _dot(p.astype(BF16), v)
    return o * (1.0 / l)


def _ctx_attn_kernel(naq_ref, nak_ref, nav_ref, gq_ref, gk_ref, gv_ref, oa_ref, oc_ref):
    low = _low_half(SEQ)
    for p in range(NA_W // LANE):
        cols = slice(p * LANE, (p + 1) * LANE)
        q = naq_ref[:, cols]
        k = nak_ref[:, cols].astype(BF16)
        v = nav_ref[:, cols].astype(BF16)
        outs = []
        for half in range(2):
            qm = jnp.where(low if half == 0 else ~low, q, 0.0).astype(BF16)
            outs.append(_softmax_pv([_dot_nt(qm, k) * ATTN_SCALE], [v]))
        oa_ref[:, cols] = jnp.where(low, outs[0], outs[1])
    k = gk_ref[...]
    v = gv_ref[...]
    k_at = (pltpu.roll(k, HEAD_DIM, axis=1).astype(BF16), k.astype(BF16))
    v_at = (pltpu.roll(v, HEAD_DIM, axis=1).astype(BF16), v.astype(BF16))
    group = GQA_HEADS // GQA_KV_HEADS
    for p in range(GQA_W // LANE):
        cols = slice(p * LANE, (p + 1) * LANE)
        q = gq_ref[:, cols]
        outs = []
        for half in range(2):
            g = (2 * p + half) // group
            qm = jnp.where(low if half == 0 else ~low, q, 0.0).astype(BF16)
            outs.append(_softmax_pv([_dot_nt(qm, k_at[g == half]) * ATTN_SCALE], [v_at[g == half]]))
        oc_ref[:, cols] = jnp.where(low, outs[0], outs[1])


def ctx_attention(z):
    col = lambda w, off: pl.BlockSpec((SEQ, w), lambda b: (b, off // w))
    return pl.pallas_call(
        _ctx_attn_kernel,
        grid=(BATCH,),
        in_specs=[col(NA_W, OFF_NA_Q), col(NA_W, OFF_NA_K), col(NA_W, OFF_NA_V),
                  col(GQA_W, OFF_GQ_Q), col(LANE, OFF_GQ_K), col(LANE, OFF_GQ_V)],
        out_specs=[col(NA_W, 0), col(GQA_W, 0)],
        out_shape=[jax.ShapeDtypeStruct((N_CTX_TOK, NA_W), F32), jax.ShapeDtypeStruct((N_CTX_TOK, GQA_W), F32)],
        compiler_params=_cparams("arbitrary"),
        name="ctx_attention",
    )(z, z, z, z, z, z)


GQA_TQ = 512


def _lat_gqa_kernel(q_ref, kl_ref, vl_ref, kc_ref, vc_ref, o_ref):
    low = _low_half(GQA_TQ)
    at = lambda x: (pltpu.roll(x, HEAD_DIM, axis=1).astype(BF16), x.astype(BF16))
    kl_at, vl_at, kc_at, vc_at = at(kl_ref[...]), at(vl_ref[...]), at(kc_ref[0]), at(vc_ref[0])
    group = GQA_HEADS // GQA_KV_HEADS
    for p in range(GQA_W // LANE):
        cols = slice(p * LANE, (p + 1) * LANE)
        q = q_ref[:, cols]
        outs = []
        for half in range(2):
            same = ((2 * p + half) // group) == half
            qm = jnp.where(low if half == 0 else ~low, q, 0.0).astype(BF16)
            outs.append(_softmax_pv(
                [_dot_nt(qm, kc_at[same]) * ATTN_SCALE, _dot_nt(qm, kl_at[same]) * ATTN_SCALE],
                [vc_at[same], vl_at[same]]))
        o_ref[:, cols] = jnp.where(low, outs[0], outs[1])


def lat_gqa(z, cache_k, cache_v):
    per = DEC_SEQ // GQA_TQ
    kv = lambda off: pl.BlockSpec((DEC_SEQ, LANE), lambda b, i: (b, off // LANE))
    cache = pl.BlockSpec((1, PAST_LEN, LANE), lambda b, i: (b, 0, 0))
    return pl.pallas_call(
        _lat_gqa_kernel,
        grid=(DEC_BATCH, per),
        in_specs=[pl.BlockSpec((GQA_TQ, GQA_W), lambda b, i: (b * per + i, OFF_GQ_Q // GQA_W)),
                  kv(OFF_GQ_K), kv(OFF_GQ_V), cache, cache],
        out_specs=pl.BlockSpec((GQA_TQ, GQA_W), lambda b, i: (b * per + i, 0)),
        out_shape=jax.ShapeDtypeStruct((N_LAT_TOK, GQA_W), F32),
        compiler_params=_cparams("arbitrary", "arbitrary"),
        name="lat_gqa",
    )(z, z, z, cache_k, cache_v)


NA_RB = 4
NA_UNION = NA_RB + NA_WIN_ROWS
NA_TQ = NA_RB * GRID_W
NA_TK = NA_UNION * GRID_W
NA_TILES = 2 * NA_WIN_ROWS


def _na_union_start(i):
    return jnp.clip(i * NA_RB - NA_WIN_ROWS // 2, 0, GRID_H - NA_UNION)


def _na_tile_ids(i):
    u0 = _na_union_start(i)
    ids = []
    for t in range(NA_RB):
        r = i * NA_RB + t
        rs = jnp.clip(r - NA_WIN_ROWS // 2, 0, GRID_H - NA_WIN_ROWS)
        row_ids = []
        for j in range(NA_UNION):
            kr = u0 + j
            inside = (kr >= rs) & (kr < rs + NA_WIN_ROWS)
            row_ids.append(jnp.where(inside, kr - r + NA_WIN_ROWS - 1, NA_TILES - 1))
        ids.append(row_ids)
    return ids


def _na_bias(tab_ref, head, ids):
    rows = [jnp.concatenate([tab_ref[head, 0, row_ids[j]] + tab_ref[head, 1, row_ids[j + 1]]
                             for j in range(0, NA_UNION, 2)], axis=1) for row_ids in ids]
    return jnp.concatenate(rows, axis=0)


def _lat_na_kernel(q_ref, k_ref, v_ref, kc_ref, vc_ref, tab_ref, o_ref):
    i = pl.program_id(1)
    start = pl.multiple_of(_na_union_start(i) * GRID_W, GRID_W)
    low = _low_half(NA_TQ)
    ids = _na_tile_ids(i)
    for p in range(NA_W // LANE):
        cols = slice(p * LANE, (p + 1) * LANE)
        q = q_ref[:, cols]
        kl = k_ref[pl.ds(start, NA_TK), cols].astype(BF16)
        vl = v_ref[pl.ds(start, NA_TK), cols].astype(BF16)
        kc = kc_ref[0, :, cols].astype(BF16)
        vc = vc_ref[0, :, cols].astype(BF16)
        outs = []
        for half in range(2):
            qm = jnp.where(low if half == 0 else ~low, q, 0.0).astype(BF16)
            s_loc = _dot_nt(qm, kl) * ATTN_SCALE + _na_bias(tab_ref, 2 * p + half, ids)
            s_ctx = _dot_nt(qm, kc) * ATTN_SCALE
            outs.append(_softmax_pv([s_loc, s_ctx], [vl, vc]))
        o_ref[:, cols] = jnp.where(low, outs[0], outs[1])


def na_bias_table(rpb):
    cq = np.arange(GRID_W)
    ck = np.arange(GRID_W)
    dc = np.clip(ck[None, :] - cq[:, None], -(NA_WIN_COLS - 1), NA_WIN_COLS - 1) + NA_WIN_COLS - 1
    col_start = np.clip(cq - NA_WIN_COLS // 2, 0, GRID_W - NA_WIN_COLS)
    in_win = (ck[None, :] >= col_start[:, None]) & (ck[None, :] < col_start[:, None] + NA_WIN_COLS)
    col_sel = (dc[:, :, None] == np.arange(2 * NA_WIN_COLS - 1)).astype(np.float32)
    tiles = jnp.einsum("lhrc,qkc->lhrqk", rpb.astype(F32), col_sel, precision=lax.Precision.HIGHEST)
    tiles = jnp.where(in_win, tiles, NEG)
    masked = jnp.full(tiles.shape[:2] + (1, GRID_W, GRID_W), NEG, F32)
    tiles = jnp.concatenate([tiles, masked], axis=2)
    zero = jnp.zeros_like(tiles)
    return jnp.stack([jnp.concatenate([tiles, zero], -1), jnp.concatenate([zero, tiles], -1)], axis=2)


def lat_na(z, cache_k, cache_v, tab, layer):
    per = GRID_H // NA_RB
    kv = lambda off: pl.BlockSpec((DEC_SEQ, NA_W), lambda b, i: (b, off // NA_W))
    cache = pl.BlockSpec((1, PAST_LEN, NA_W), lambda b, i: (b, 0, 0))
    return pl.pallas_call(
        _lat_na_kernel,
        grid=(DEC_BATCH, per),
        in_specs=[pl.BlockSpec((NA_TQ, NA_W), lambda b, i: (b * per + i, 0)),
                  kv(OFF_NA_K), kv(OFF_NA_V), cache, cache,
                  pl.BlockSpec((None, NA_HEADS, 2, NA_TILES, GRID_W, LANE), lambda b, i: (layer, 0, 0, 0, 0, 0))],
        out_specs=pl.BlockSpec((NA_TQ, NA_W), lambda b, i: (b * per + i, 0)),
        out_shape=jax.ShapeDtypeStruct((N_LAT_TOK, NA_W), F32),
        compiler_params=_cparams("arbitrary", "arbitrary"),
        name="lat_na",
    )(z, z, z, cache_k, cache_v, tab)


def _log_sigmoid(x):
    return jnp.minimum(x, 0.0) - jnp.log1p(jnp.exp(-jnp.abs(x)))


def _split3(x):
    hi = x.astype(BF16)
    r = x - hi.astype(F32)
    mid = r.astype(BF16)
    lo = (r - mid.astype(F32)).astype(BF16)
    return hi, mid, lo


def _mlstm_gate_tables(gates, fwd):
    t = gates.shape[0]
    jj = lax.broadcasted_iota(jnp.int32, (t, t), 0)
    ss = lax.broadcasted_iota(jnp.int32, (t, t), 1)
    feeds = (jj <= ss) if fwd else (jj >= ss)
    feeds_b = jnp.where(feeds, 1.0, 0.0).astype(BF16)
    reached_b = jnp.where((ss <= jj) if fwd else (ss >= jj), 1.0, 0.0).astype(BF16)
    eye_b = jnp.where(jj == ss, 1.0, 0.0).astype(BF16)
    lf = _split3(_log_sigmoid(gates))
    gs = _split3(gates)
    b_col = _dot(reached_b, lf[0]) + _dot(reached_b, lf[1]) + _dot(reached_b, lf[2])
    b_row = _dot_tn(lf[0], feeds_b) + _dot_tn(lf[1], feeds_b) + _dot_tn(lf[2], feeds_b)
    ig_row = _dot_tn(gs[0], eye_b) + _dot_tn(gs[1], eye_b) + _dot_tn(gs[2], eye_b)
    return ig_row, b_row, b_col, feeds


def _mlstm_head(k, qt, vt, kt, u, b_row, ig_row, feeds, ct, n, m, half, fwd):
    t = k.shape[0]
    dmat = jnp.where(feeds, b_row + u, NEG)
    m_inter = b_row + m
    m_j = jnp.maximum(m_inter, jnp.max(dmat, axis=0, keepdims=True))
    w = jnp.exp(dmat - m_j)
    decay = jnp.exp(m_inter - m_j)
    first = lax.broadcasted_iota(jnp.int32, (LANE, t), 0) < HEAD_DIM
    mine = first if half == 0 else ~first
    qm = jnp.where(mine, qt, 0.0)
    qb = qm.astype(BF16)
    kb = k.astype(BF16)
    qk = _dot(kb, qb) * w
    num = decay * _dot(ct.astype(BF16), qb) + _dot(vt.astype(BF16), qk.astype(BF16))
    den = decay * jnp.sum(qm * n, axis=0, keepdims=True) + jnp.sum(qk, axis=0, keepdims=True)
    h = jnp.where(mine, num * (1.0 / jnp.maximum(jnp.abs(den), jnp.exp(-m_j))), 0.0)
    last = t - 1 if fwd else 0
    b_last = b_row[:, last:last + 1]
    g = b_last - b_row + ig_row
    m_new = jnp.maximum(b_last + m, jnp.max(g, axis=1, keepdims=True))
    ws = jnp.exp(g - m_new)
    d_last = jnp.exp(b_last + m - m_new)
    ct_add = _dot((vt * ws).astype(BF16), kb)
    n_add = jnp.sum(jnp.where(mine, kt * ws, 0.0), axis=1, keepdims=True)
    return h, d_last, ct_add, n_add, m_new


def _mlstm_kernel(zf_ref, zb_ref, c0_ref, n0_ref, m0_ref, hf_ref, hb_ref, co_ref, no_ref, mo_ref,
                  c_s, n_s, m_s):
    chunk = pl.program_id(1)
    pairs = ML_W // LANE
    zero = jnp.zeros((HEAD_DIM, HEAD_DIM), F32)

    ri = lax.broadcasted_iota(jnp.int32, (LANE, LANE), 0)
    ci = lax.broadcasted_iota(jnp.int32, (LANE, LANE), 1)
    eye = ri == ci
    first_block = (ri < HEAD_DIM) & (ci < HEAD_DIM)
    second_block = (ri >= HEAD_DIM) & (ci >= HEAD_DIM)
    first_rows = lax.broadcasted_iota(jnp.int32, (LANE, 1), 0) < HEAD_DIM

    @pl.when(chunk == 0)
    def _():
        for d in range(2):
            for p in range(pairs):
                top = jnp.concatenate([c0_ref[0, d, 2 * p], zero], axis=1)
                bot = jnp.concatenate([zero, c0_ref[0, d, 2 * p + 1]], axis=1)
                c_s[d * pairs + p] = jnp.concatenate([top, bot], axis=0).T
                e = d * ML_HEADS + 2 * p
                n_row = jnp.concatenate([n0_ref[0, e:e + 1, :], n0_ref[0, e + 1:e + 2, :]], axis=1)
                n_s[d * pairs + p] = jnp.sum(jnp.where(eye, n_row, 0.0), axis=1, keepdims=True)
            m_s[d] = m0_ref[0, d * ML_HEADS:(d + 1) * ML_HEADS, :]

    for d, (z_ref, h_ref) in enumerate(((zf_ref, hf_ref), (zb_ref, hb_ref))):
        gates = z_ref[:, OFF_ML_G - OFF_ML_Q:OFF_ML_G - OFF_ML_Q + LANE]
        ig_rows, b_rows, b_cols, feeds = _mlstm_gate_tables(gates, d == 0)
        u = gates - pltpu.roll(b_cols, LANE - ML_HEADS, axis=1)
        m_all = m_s[d]
        m_rows = []
        for p in range(pairs):
            k = z_ref[:, ML_W + p * LANE:ML_W + (p + 1) * LANE] * ATTN_SCALE
            qt = z_ref[:, p * LANE:(p + 1) * LANE].T
            vt = z_ref[:, 2 * ML_W + p * LANE:2 * ML_W + (p + 1) * LANE].T
            kt = k.T
            ct = c_s[d * pairs + p]
            n = n_s[d * pairs + p]
            res = []
            for half in range(2):
                hd = 2 * p + half
                gi = 2 * ML_HEADS * d + hd
                gf = gi + ML_HEADS
                res.append(_mlstm_head(k, qt, vt, kt, u[:, gi:gi + 1], b_rows[gf:gf + 1, :], ig_rows[gi:gi + 1, :],
                                       feeds, ct, n, m_all[hd:hd + 1, 0:1], half, d == 0))
            (h0, dl0, ca0, na0, mn0), (h1, dl1, ca1, na1, mn1) = res
            h_ref[:, p * LANE:(p + 1) * LANE] = (h0 + h1).T
            c_s[d * pairs + p] = jnp.where(first_block, dl0 * ct + ca0, jnp.where(second_block, dl1 * ct + ca1, 0.0))
            n_s[d * pairs + p] = jnp.where(first_rows, dl0 * n + na0, dl1 * n + na1)
            m_rows += [jnp.broadcast_to(mn0, (1, LANE)), jnp.broadcast_to(mn1, (1, LANE))]
        m_s[d] = jnp.concatenate(m_rows, axis=0)

    @pl.when(chunk == pl.num_programs(1) - 1)
    def _():
        for d in range(2):
            for p in range(pairs):
                c = c_s[d * pairs + p].T
                n_row = jnp.sum(jnp.where(eye, n_s[d * pairs + p], 0.0), axis=0, keepdims=True)
                co_ref[0, d, 2 * p] = c[:HEAD_DIM, :HEAD_DIM]
                co_ref[0, d, 2 * p + 1] = c[HEAD_DIM:, HEAD_DIM:]
                e = d * ML_HEADS + 2 * p
                no_ref[0, e:e + 1, :] = n_row[:, :HEAD_DIM]
                no_ref[0, e + 1:e + 2, :] = n_row[:, HEAD_DIM:]
            mo_ref[0, d * ML_HEADS:(d + 1) * ML_HEADS, :] = m_s[d]


def mlstm(z, batch, seq, c0, n0, m0):
    nc = seq // ML_CHUNK
    zcol = OFF_ML_Q // ML_BLOCK_W
    state_c = pl.BlockSpec((1, 2, ML_HEADS, HEAD_DIM, HEAD_DIM), lambda b, c: (b, 0, 0, 0, 0))
    state_n = pl.BlockSpec((1, 2 * ML_HEADS, HEAD_DIM), lambda b, c: (b, 0, 0))
    state_m = pl.BlockSpec((1, 2 * ML_HEADS, LANE), lambda b, c: (b, 0, 0))
    pairs = ML_W // LANE
    return pl.pallas_call(
        _mlstm_kernel,
        grid=(batch, nc),
        in_specs=[
            pl.BlockSpec((ML_CHUNK, ML_BLOCK_W), lambda b, c: (b * nc + c, zcol)),
            pl.BlockSpec((ML_CHUNK, ML_BLOCK_W), lambda b, c: (b * nc + nc - 1 - c, zcol)),
            state_c, state_n, state_m,
        ],
        out_specs=[
            pl.BlockSpec((ML_CHUNK, ML_W), lambda b, c: (b * nc + c, 0)),
            pl.BlockSpec((ML_CHUNK, ML_W), lambda b, c: (b * nc + nc - 1 - c, 0)),
            state_c, state_n, state_m,
        ],
        out_shape=[
            jax.ShapeDtypeStruct((batch * seq, ML_W), F32),
            jax.ShapeDtypeStruct((batch * seq, ML_W), F32),
            jax.ShapeDtypeStruct(c0.shape, F32),
            jax.ShapeDtypeStruct(n0.shape, F32),
            jax.ShapeDtypeStruct(m0.shape, F32),
        ],
        scratch_shapes=[
            pltpu.VMEM((2 * pairs, LANE, LANE), F32),
            pltpu.VMEM((2 * pairs, LANE, 1), F32),
            pltpu.VMEM((2, ML_HEADS, LANE), F32),
        ],
        compiler_params=_cparams("arbitrary", "arbitrary"),
        name="mlstm",
    )(z, z, c0, n0, m0)


def _layer_norm(u, g, b):
    mu = jnp.mean(u, axis=-1, keepdims=True)
    uc = u - mu
    var = jnp.mean(uc * uc, axis=-1, keepdims=True)
    return uc * lax.rsqrt(var + NORM_EPS) * g + b


def _outproj_kernel(x_ref, a_ref, hf_ref, hb_ref, og0_ref, og1_ref, c_ref, w_ref, mg_ref, g1_ref, sh2_ref, sc2_ref,
                    lng_ref, lnb_ref, rw_ref, x1_ref, h2_ref, lg_ref):
    low = _low_half(TM)
    mixed_b = []
    for p, og_ref in enumerate((og0_ref, og1_ref)):
        cols = slice(p * LANE, (p + 1) * LANE)
        h = hf_ref[:, cols] + hb_ref[:, cols]
        hc = h - _pair_mean(h, low)
        hn = hc * lax.rsqrt(_pair_mean(hc * hc, low) + NORM_EPS)
        mixed_b.append(hn * mg_ref[:, cols] * _sigmoid(og_ref[...]))
    mixed = jnp.concatenate([a_ref[...]] + mixed_b + [c_ref[...]], axis=-1).astype(BF16)
    y = _dot(mixed, w_ref[...])
    x1 = _layer_norm(DEEPNORM_ALPHA * x_ref[...] + g1_ref[0] * y, lng_ref[...], lnb_ref[...])
    h2 = (x1 * (1.0 + sc2_ref[0]) + sh2_ref[0]).astype(BF16)
    x1_ref[...] = x1
    h2_ref[...] = h2
    lg_ref[...] = _dot(h2, rw_ref[...])


def out_projection(x, out_a, hf, hb, z, out_c, w_out, ml_gain, mods3, mod_row, ln_g, ln_b, router_pad):
    n = x.shape[0]
    mod = lambda k: pl.BlockSpec((1, 1, D_MODEL), lambda i: (mod_row(i), 0, k))
    row = lambda w: pl.BlockSpec((TM, w), lambda i: (i, 0))
    zcol = lambda off: pl.BlockSpec((TM, LANE), lambda i: (i, off // LANE))
    const = lambda r, c: pl.BlockSpec((r, c), lambda i: (0, 0))
    return pl.pallas_call(
        _outproj_kernel,
        grid=(n // TM,),
        in_specs=[row(D_MODEL), row(NA_W), row(ML_W), row(ML_W), zcol(OFF_ML_O), zcol(OFF_ML_O + LANE), row(GQA_W),
                  const(D_MODEL, D_MODEL), const(1, ML_W), mod(2), mod(3), mod(4), const(1, D_MODEL),
                  const(1, D_MODEL), const(D_MODEL, LANE)],
        out_specs=[row(D_MODEL), row(D_MODEL), row(LANE)],
        out_shape=[
            jax.ShapeDtypeStruct((n, D_MODEL), F32),
            jax.ShapeDtypeStruct((n, D_MODEL), BF16),
            jax.ShapeDtypeStruct((n, LANE), F32),
        ],
        compiler_params=_cparams("arbitrary"),
        name="out_projection",
    )(x, out_a, hf, hb, z, z, out_c, w_out, ml_gain, mods3, mods3, mods3, ln_g.reshape(1, D_MODEL),
      ln_b.reshape(1, D_MODEL), router_pad)


ROUTE_BLK = 256


def _prefix_count(x, tri):
    n = x.shape[0]
    outs = []
    carry = jnp.zeros((1, LANE), F32)
    for i in range(n // ROUTE_BLK):
        blk = x[i * ROUTE_BLK:(i + 1) * ROUTE_BLK]
        outs.append(_dot(tri, blk) + carry)
        carry = carry + jnp.sum(blk.astype(F32), axis=0, keepdims=True)
    return jnp.concatenate(outs, axis=0) if len(outs) > 1 else outs[0]


def _route_kernel(lg_ref, aff_ref, code_ref, coder_ref, *, cap):
    lg = lg_ref[0]
    n = lg.shape[0]
    lane = lax.broadcasted_iota(jnp.int32, (n, LANE), 1)
    valid = lane < N_EXPERTS
    mx = jnp.max(jnp.where(valid, lg, NEG), axis=1, keepdims=True)
    ex = jnp.where(valid, jnp.exp(lg - mx), 0.0)
    aff = ex / jnp.sum(ex, axis=1, keepdims=True)

    def search(i, bits):
        cand = bits | lax.shift_left(jnp.int32(1), 30 - i)
        cnt = jnp.sum(jnp.where(aff >= pltpu.bitcast(cand, F32), 1.0, 0.0), axis=0, keepdims=True)
        return jnp.where(cnt >= cap, cand, bits)

    floor_bits = lax.fori_loop(0, 31, search, jnp.zeros((1, LANE), jnp.int32))
    thr = jnp.min(jnp.where(aff >= pltpu.bitcast(floor_bits, F32), aff, 2.0), axis=0, keepdims=True)
    gt = aff > thr
    eq = aff == thr
    need = cap - jnp.sum(jnp.where(gt, 1.0, 0.0), axis=0, keepdims=True)
    ti = lax.broadcasted_iota(jnp.int32, (ROUTE_BLK, ROUTE_BLK), 0)
    tj = lax.broadcasted_iota(jnp.int32, (ROUTE_BLK, ROUTE_BLK), 1)
    tri = jnp.where(tj < ti, 1.0, 0.0).astype(BF16)
    eq_rank = _prefix_count(jnp.where(eq, 1.0, 0.0).astype(BF16), tri)
    sel = jnp.where(gt, 1.0, jnp.where(eq & (eq_rank < need), 1.0, 0.0))
    pos = _prefix_count(sel.astype(BF16), tri)
    code = jnp.where((sel > 0.5) & valid, pos, -1.0)
    aff_ref[0] = aff
    code_ref[0] = code
    ei = lax.broadcasted_iota(jnp.int32, (N_EXPERTS, LANE), 0)
    ej = lax.broadcasted_iota(jnp.int32, (N_EXPERTS, LANE), 1)
    eye = jnp.where(ei == ej, 1.0, 0.0).astype(BF16)
    coder_ref[0] = _dot_nt(eye, code.astype(BF16))


def route(logits, n, cap):
    sets = logits.shape[0] // n
    lg = logits.reshape(sets, n, LANE)
    tok = pl.BlockSpec((1, n, LANE), lambda s: (s, 0, 0))
    return pl.pallas_call(
        functools.partial(_route_kernel, cap=cap),
        grid=(sets,),
        in_specs=[tok],
        out_specs=[tok, tok, pl.BlockSpec((1, N_EXPERTS, n), lambda s: (s, 0, 0))],
        out_shape=[
            jax.ShapeDtypeStruct((sets, n, LANE), F32),
            jax.ShapeDtypeStruct((sets, n, LANE), F32),
            jax.ShapeDtypeStruct((sets, N_EXPERTS, n), F32),
        ],
        compiler_params=_cparams("arbitrary"),
        name="route",
    )(lg)


def _gather_kernel(coder_ref, h_ref, xe_ref, *, cap, epb):
    e0 = pl.program_id(1) * epb
    n = coder_ref.shape[2]
    ci = lax.broadcasted_iota(jnp.int32, (cap, n), 0).astype(F32)
    onehot = [jnp.where(ci == coder_ref[0, pl.ds(e0 + k, 1), :], 1.0, 0.0).astype(BF16) for k in range(epb)]
    onehot = jnp.concatenate(onehot, axis=0) if epb > 1 else onehot[0]
    rows = _dot(onehot, h_ref[...]).astype(BF16)
    for k in range(epb):
        xe_ref[k] = rows[k * cap:(k + 1) * cap]


def gather_tokens(code_rows, h2, n, cap):
    sets = code_rows.shape[0]
    epb = max(1, 512 // cap)
    return pl.pallas_call(
        functools.partial(_gather_kernel, cap=cap, epb=epb),
        grid=(sets, N_EXPERTS // epb),
        in_specs=[
            pl.BlockSpec((1, N_EXPERTS, n), lambda s, e: (s, 0, 0)),
            pl.BlockSpec((n, D_MODEL), lambda s, e: (s, 0)),
        ],
        out_specs=pl.BlockSpec((epb, cap, D_MODEL), lambda s, e: (e, s, 0)),
        out_shape=jax.ShapeDtypeStruct((N_EXPERTS, sets * cap, D_MODEL), BF16),
        compiler_params=_cparams("arbitrary", "arbitrary"),
        name="gather_tokens",
    )(code_rows, h2)


def _expert_kernel(xi_ref, xl_ref, wg_ref, wu_ref, wd_ref, yi_ref, yl_ref, acci_ref, accl_ref):
    j = pl.program_id(1)

    @pl.when(j == 0)
    def _():
        acci_ref[...] = jnp.zeros_like(acci_ref)
        accl_ref[...] = jnp.zeros_like(accl_ref)

    xs = (xi_ref[0], xl_ref[0])
    sums = [None, None]
    for s0 in range(0, EXP_TF, EXP_SUB):
        s1 = min(s0 + EXP_SUB, EXP_TF)
        wg = wg_ref[0, 0, :, s0:s1].astype(BF16)
        wu = wu_ref[0, 0, :, s0:s1].astype(BF16)
        wd = wd_ref[0, 0, s0:s1, :].astype(BF16)
        for g in range(2):
            a = _dot(xs[g], wg)
            b = _dot(xs[g], wu)
            hid = (a * _sigmoid(a) * b).astype(BF16)
            part = _dot(hid, wd)
            sums[g] = part if sums[g] is None else sums[g] + part
    acci_ref[...] += sums[0]
    accl_ref[...] += sums[1]

    @pl.when(j == pl.num_programs(1) - 1)
    def _():
        yi_ref[0] = acci_ref[...].astype(BF16)
        yl_ref[0] = accl_ref[...].astype(BF16)


def experts(xe_ctx, xe_lat, w_gate, w_up, w_down, layer):
    xin = lambda s: pl.BlockSpec((1, s, D_MODEL), lambda e, j: (e, 0, 0))
    return pl.pallas_call(
        _expert_kernel,
        grid=(N_EXPERTS, EXPERT_HIDDEN // EXP_TF),
        in_specs=[
            xin(SLOTS_CTX), xin(SLOTS_LAT),
            pl.BlockSpec((1, 1, D_MODEL, EXP_TF), lambda e, j: (layer, e, 0, j)),
            pl.BlockSpec((1, 1, D_MODEL, EXP_TF), lambda e, j: (layer, e, 0, j)),
            pl.BlockSpec((1, 1, EXP_TF, D_MODEL), lambda e, j: (layer, e, j, 0)),
        ],
        out_specs=[xin(SLOTS_CTX), xin(SLOTS_LAT)],
        out_shape=[
            jax.ShapeDtypeStruct((N_EXPERTS, SLOTS_CTX, D_MODEL), BF16),
            jax.ShapeDtypeStruct((N_EXPERTS, SLOTS_LAT, D_MODEL), BF16),
        ],
        scratch_shapes=[pltpu.VMEM((SLOTS_CTX, D_MODEL), F32), pltpu.VMEM((SLOTS_LAT, D_MODEL), F32)],
        compiler_params=_cparams("arbitrary", "arbitrary"),
        name="experts",
    )(xe_ctx, xe_lat, w_gate, w_up, w_down)


COMB_TT = 256


def _combine_kernel(code_ref, aff_ref, ye_ref, x1_ref, g2_ref, lng_ref, lnb_ref, o_ref, *, cap):
    code = code_ref[0]
    aff = aff_ref[0]
    tt = code.shape[0]
    li = lax.broadcasted_iota(jnp.int32, (tt, cap), 1).astype(F32)
    acc = jnp.zeros((tt, D_MODEL), F32)
    for e in range(N_EXPERTS):
        onehot = jnp.where(li == code[:, e:e + 1], 1.0, 0.0).astype(BF16)
        acc = acc + aff[:, e:e + 1] * _dot(onehot, ye_ref[e])
    u = DEEPNORM_ALPHA * x1_ref[...] + g2_ref[0] * acc
    o_ref[...] = _layer_norm(u, lng_ref[...], lnb_ref[...])


def combine(code, aff, ye, x1, mods3, set_mod_row, ln_g, ln_b, n, cap):
    sets = code.shape[0]
    per = n // COMB_TT
    tok = pl.BlockSpec((1, COMB_TT, LANE), lambda s, i: (s, i, 0))
    return pl.pallas_call(
        functools.partial(_combine_kernel, cap=cap),
        grid=(sets, per),
        in_specs=[
            tok, tok,
            pl.BlockSpec((N_EXPERTS, cap, D_MODEL), lambda s, i: (0, s, 0)),
            pl.BlockSpec((COMB_TT, D_MODEL), lambda s, i: (s * per + i, 0)),
            pl.BlockSpec((1, 1, D_MODEL), lambda s, i: (set_mod_row(s), 0, 5)),
            pl.BlockSpec((1, D_MODEL), lambda s, i: (0, 0)),
            pl.BlockSpec((1, D_MODEL), lambda s, i: (0, 0)),
        ],
        out_specs=pl.BlockSpec((COMB_TT, D_MODEL), lambda s, i: (s * per + i, 0)),
        out_shape=jax.ShapeDtypeStruct((sets * n, D_MODEL), F32),
        compiler_params=_cparams("arbitrary", "arbitrary"),
        name="combine",
    )(code, aff, ye, x1, mods3, ln_g.reshape(1, D_MODEL), ln_b.reshape(1, D_MODEL))


def _axial_rope_tables():
    t = jnp.arange(DEC_SEQ)
    row = (t // GRID_W).astype(F32)
    col = (t % GRID_W).astype(F32)
    half = HEAD_DIM // 2
    inv = ROPE_THETA ** (-jnp.arange(0, half, 2, dtype=F32) / half)
    ang_r = row[:, None] * inv
    ang_c = col[:, None] * inv
    ang = jnp.concatenate([ang_r, ang_r, ang_c, ang_c] * 2, -1)
    return jnp.cos(ang), jnp.sin(ang)


def _ctx_mod_row(i):
    return 0


def _lat_tile_mod_row(i):
    return 1 + i // (DEC_SEQ // TM)


def _lat_set_mod_row(s):
    return 1 + s


def kernel(x_prompt, x_sample, c, cache_na_k, cache_na_v, cache_gqa_k, cache_gqa_v, state_mlstm_c, state_mlstm_n,
           state_mlstm_m, c_ctx, ada_w, ada_b, w_in, b_gate, w_out, na_rpb, qk_norm_g, ml_norm_g, ln_g, ln_b,
           router_w, w_gate, w_up, w_down):
    cond8 = jnp.concatenate([c_ctx[None, :], c, jnp.zeros((8 - 1 - DEC_BATCH, D_MODEL), F32)], 0)
    mods_all = adaln(cond8, ada_w, ada_b)
    xc = x_prompt.reshape(N_CTX_TOK, D_MODEL)
    xl = x_sample.reshape(N_LAT_TOK, D_MODEL)
    cos, sin = _axial_rope_tables()
    na_tab = na_bias_table(na_rpb)
    zeros_c = jnp.zeros((BATCH, 2, ML_HEADS, HEAD_DIM, HEAD_DIM), F32)
    zeros_n = jnp.zeros((BATCH, 2 * ML_HEADS, HEAD_DIM), F32)
    zeros_m = jnp.zeros((BATCH, 2 * ML_HEADS, LANE), F32)
    new = [[] for _ in range(7)]

    for l in range(DEPTH):
        mods3 = mods_all[l].reshape(8, 1, 6 * D_MODEL)
        wl = w_in[l]
        gate_pad = jnp.zeros((D_MODEL, LANE - N_GATES), F32)
        w_pad = jnp.concatenate([wl[:, :OFF_ML_G + N_GATES], gate_pad, wl[:, OFF_ML_G + N_GATES:]], 1).astype(BF16)
        bias_row = jnp.zeros((1, PROJ_PAD), F32).at[0, OFF_ML_G:OFF_ML_G + N_GATES].set(b_gate[l])
        gain_q = jnp.tile(qk_norm_g[l, 0], 2).reshape(1, LANE)
        gain_k = jnp.tile(qk_norm_g[l, 1], 2).reshape(1, LANE)
        zc = in_projection(xc, mods3, _ctx_mod_row, w_pad, bias_row, gain_q, gain_k)
        zl = in_projection(xl, mods3, _lat_tile_mod_row, w_pad, bias_row, gain_q, gain_k, cos, sin)

        out_a_c, out_c_c = ctx_attention(zc)
        hf_c, hb_c, sc, sn, sm = mlstm(zc, BATCH, SEQ, zeros_c, zeros_n, zeros_m)
        out_a_l = lat_na(zl, cache_na_k[:, l].reshape(DEC_BATCH, PAST_LEN, NA_W),
                         cache_na_v[:, l].reshape(DEC_BATCH, PAST_LEN, NA_W), na_tab, l)
        out_c_l = lat_gqa(zl, cache_gqa_k[:, l].reshape(DEC_BATCH, PAST_LEN, GQA_KV_W),
                          cache_gqa_v[:, l].reshape(DEC_BATCH, PAST_LEN, GQA_KV_W))
        m0 = jnp.broadcast_to(state_mlstm_m[:, l].reshape(DEC_BATCH, 2 * ML_HEADS, 1), (DEC_BATCH, 2 * ML_HEADS, LANE))
        hf_l, hb_l = mlstm(zl, DEC_BATCH, DEC_SEQ, state_mlstm_c[:, l],
                           state_mlstm_n[:, l].reshape(DEC_BATCH, 2 * ML_HEADS, HEAD_DIM), m0)[:2]

        new[0].append(zc[:, OFF_NA_K:OFF_NA_K + NA_W].reshape(BATCH, SEQ, NA_HEADS, HEAD_DIM))
        new[1].append(zc[:, OFF_NA_V:OFF_NA_V + NA_W].reshape(BATCH, SEQ, NA_HEADS, HEAD_DIM))
        new[2].append(zc[:, OFF_GQ_K:OFF_GQ_K + GQA_KV_W].reshape(BATCH, SEQ, GQA_KV_HEADS, HEAD_DIM))
        new[3].append(zc[:, OFF_GQ_V:OFF_GQ_V + GQA_KV_W].reshape(BATCH, SEQ, GQA_KV_HEADS, HEAD_DIM))
        new[4].append(sc)
        new[5].append(sn.reshape(BATCH, 2, ML_HEADS, HEAD_DIM))
        new[6].append(sm[:, :, 0].reshape(BATCH, 2, ML_HEADS))

        router_pad = jnp.concatenate([router_w[l], jnp.zeros((D_MODEL, LANE - N_EXPERTS), F32)], 1).astype(BF16)
        w_out_b = w_out[l].astype(BF16)
        ml_gain = ml_norm_g[l].reshape(1, ML_W)
        x1_c, h2_c, lg_c = out_projection(xc, out_a_c, hf_c, hb_c, zc, out_c_c, w_out_b, ml_gain, mods3,
                                          _ctx_mod_row, ln_g[l, 0], ln_b[l, 0], router_pad)
        x1_l, h2_l, lg_l = out_projection(xl, out_a_l, hf_l, hb_l, zl, out_c_l, w_out_b, ml_gain, mods3,
                                          _lat_tile_mod_row, ln_g[l, 0], ln_b[l, 0], router_pad)
        aff_c, code_c, crow_c = route(lg_c, SEQ, CAP_CTX)
        aff_l, code_l, crow_l = route(lg_l, DEC_SEQ, CAP_LAT)
        xe_c = gather_tokens(crow_c, h2_c, SEQ, CAP_CTX)
        xe_l = gather_tokens(crow_l, h2_l, DEC_SEQ, CAP_LAT)
        ye_c, ye_l = experts(xe_c, xe_l, w_gate, w_up, w_down, l)
        xc = combine(code_c, aff_c, ye_c, x1_c, mods3, _ctx_mod_row, ln_g[l, 1], ln_b[l, 1], SEQ, CAP_CTX)
        xl = combine(code_l, aff_l, ye_l, x1_l, mods3, _lat_set_mod_row, ln_g[l, 1], ln_b[l, 1], DEC_SEQ, CAP_LAT)

    y_prompt = xc.reshape(BATCH, SEQ, D_MODEL)
    y_sample = xl.reshape(DEC_BATCH, DEC_SEQ, D_MODEL)
    outs = [jnp.stack(t, 1) for t in new]
    return (y_prompt, y_sample, *outs)
```

```python
import functools

import jax
import jax.numpy as jnp
import numpy as np
from jax import lax
from jax.experimental import pallas as pl
from jax.experimental.pallas import tpu as pltpu

D_MODEL = 1024
BATCH = 16
SEQ = 256
DEPTH = 2
DEC_BATCH = 2
DEC_SEQ = 2048
PAST_LEN = 256
GRID_W = 64
GRID_H = DEC_SEQ // GRID_W
HEAD_DIM = 64
NA_HEADS = 6
NA_WIN_ROWS = 8
NA_WIN_COLS = 16
ML_HEADS = 4
GQA_HEADS = 6
GQA_KV_HEADS = 2
ROPE_THETA = 10000.0
N_EXPERTS = 16
EC_CAPACITY = 2
EXPERT_HIDDEN = 2816
NORM_EPS = 1e-6
NA_W = NA_HEADS * HEAD_DIM
ML_W = ML_HEADS * HEAD_DIM
GQA_W = GQA_HEADS * HEAD_DIM
GQA_KV_W = GQA_KV_HEADS * HEAD_DIM
N_GATES = 4 * ML_HEADS
DEEPNORM_ALPHA = (2 * DEPTH) ** 0.25
ATTN_SCALE = HEAD_DIM ** -0.5
F32 = jnp.float32
BF16 = jnp.bfloat16

LANE = 128
N_CTX_TOK = BATCH * SEQ
N_LAT_TOK = DEC_BATCH * DEC_SEQ
NEG = -1e30

OFF_NA_Q = 0
OFF_NA_K = OFF_NA_Q + NA_W
OFF_NA_V = OFF_NA_K + NA_W
OFF_ML_Q = OFF_NA_V + NA_W
OFF_ML_K = OFF_ML_Q + ML_W
OFF_ML_V = OFF_ML_K + ML_W
OFF_ML_O = OFF_ML_V + ML_W
OFF_ML_G = OFF_ML_O + ML_W
OFF_GQ_Q = OFF_ML_G + LANE
OFF_GQ_K = OFF_GQ_Q + GQA_W
OFF_GQ_V = OFF_GQ_K + GQA_KV_W
PROJ_PAD = OFF_GQ_V + GQA_KV_W
ML_BLOCK_W = OFF_GQ_Q - OFF_ML_Q

TM = 512
ML_CHUNK = 256
EXP_TF = EXPERT_HIDDEN // 2
EXP_SUB = 256
CAP_CTX = EC_CAPACITY * SEQ // N_EXPERTS
CAP_LAT = EC_CAPACITY * DEC_SEQ // N_EXPERTS
SLOTS_CTX = BATCH * CAP_CTX
SLOTS_LAT = DEC_BATCH * CAP_LAT
VMEM_LIMIT = 56 * 1024 * 1024


def _cparams(*sem):
    return pltpu.CompilerParams(dimension_semantics=sem, vmem_limit_bytes=VMEM_LIMIT)


def _sigmoid(x):
    return 1.0 / (1.0 + jnp.exp(-x))


def _dot(a, b):
    return jnp.dot(a, b, preferred_element_type=F32)


def _dot_nt(a, b):
    return lax.dot_general(a, b, (((1,), (1,)), ((), ())), preferred_element_type=F32)


def _dot_tn(a, b):
    return lax.dot_general(a, b, (((0,), (0,)), ((), ())), preferred_element_type=F32)


def _low_half(rows):
    return lax.broadcasted_iota(jnp.int32, (rows, LANE), 1) < HEAD_DIM


def _pair_mean(x, low):
    s_lo = jnp.sum(jnp.where(low, x, 0.0), axis=-1, keepdims=True)
    s_hi = jnp.sum(jnp.where(low, 0.0, x), axis=-1, keepdims=True)
    return jnp.where(low, s_lo, s_hi) * (1.0 / HEAD_DIM)


def _adaln_kernel(c_ref, w_ref, b_ref, o_ref):
    c = c_ref[...]
    s = c * _sigmoid(c)
    o_ref[0] = _dot(s.astype(BF16), w_ref[0].astype(BF16)) + b_ref[0]


def adaln(cond8, ada_w, ada_b):
    tn = 1536
    return pl.pallas_call(
        _adaln_kernel,
        grid=(DEPTH, 6 * D_MODEL // tn),
        in_specs=[
            pl.BlockSpec((8, D_MODEL), lambda l, j: (0, 0)),
            pl.BlockSpec((1, D_MODEL, tn), lambda l, j: (l, 0, j)),
            pl.BlockSpec((1, 1, tn), lambda l, j: (l, 0, j)),
        ],
        out_specs=pl.BlockSpec((1, 8, tn), lambda l, j: (l, 0, j)),
        out_shape=jax.ShapeDtypeStruct((DEPTH, 8, 6 * D_MODEL), F32),
        compiler_params=_cparams("arbitrary", "arbitrary"),
        name="adaln",
    )(cond8, ada_w, ada_b.reshape(DEPTH, 1, 6 * D_MODEL))


def _inproj_kernel(x_ref, sh_ref, sc_ref, w_ref, b_ref, gq_ref, gk_ref, *rest, rope):
    if rope:
        cos_ref, sin_ref, z_ref = rest
    else:
        (z_ref,) = rest
    h = x_ref[...] * (1.0 + sc_ref[0]) + sh_ref[0]
    z = _dot(h.astype(BF16), w_ref[...]) + b_ref[...]
    z_ref[...] = z
    rows = z.shape[0]
    low = _low_half(rows)
    if rope:
        lane = lax.broadcasted_iota(jnp.int32, (rows, LANE), 1)
        first_quarter = (lane & (HEAD_DIM // 4)) == 0

    def norm_pair(x, gain):
        y = x * lax.rsqrt(_pair_mean(x * x, low) + NORM_EPS) * gain
        if rope:
            rot = jnp.where(first_quarter, -pltpu.roll(y, LANE - HEAD_DIM // 4, axis=1),
                            pltpu.roll(y, HEAD_DIM // 4, axis=1))
            y = y * cos_ref[...] + rot * sin_ref[...]
        return y

    for p in range(GQA_W // LANE):
        c0 = OFF_GQ_Q + p * LANE
        z_ref[:, c0:c0 + LANE] = norm_pair(z[:, c0:c0 + LANE], gq_ref[...])
    z_ref[:, OFF_GQ_K:OFF_GQ_K + LANE] = norm_pair(z[:, OFF_GQ_K:OFF_GQ_K + LANE], gk_ref[...])


def in_projection(x, layer, mods, mod_row, w_pad, bias_row, gains, cos=None, sin=None):
    n = x.shape[0]
    rope = cos is not None
    in_specs = [
        pl.BlockSpec((TM, D_MODEL), lambda i: (i, 0)),
        pl.BlockSpec((None, 1, 1, D_MODEL), lambda i: (layer, mod_row(i), 0, 0)),
        pl.BlockSpec((None, 1, 1, D_MODEL), lambda i: (layer, mod_row(i), 0, 1)),
        pl.BlockSpec((None, D_MODEL, PROJ_PAD), lambda i: (layer, 0, 0)),
        pl.BlockSpec((None, 1, PROJ_PAD), lambda i: (layer, 0, 0)),
        pl.BlockSpec((None, None, 1, LANE), lambda i: (layer, 0, 0, 0)),
        pl.BlockSpec((None, None, 1, LANE), lambda i: (layer, 1, 0, 0)),
    ]
    args = [x, mods, mods, w_pad, bias_row, gains, gains]
    if rope:
        per = cos.shape[0] // TM
        in_specs += [pl.BlockSpec((TM, LANE), lambda i: (i % per, 0))] * 2
        args += [cos, sin]
    return pl.pallas_call(
        functools.partial(_inproj_kernel, rope=rope),
        grid=(n // TM,),
        in_specs=in_specs,
        out_specs=pl.BlockSpec((TM, PROJ_PAD), lambda i: (i, 0)),
        out_shape=jax.ShapeDtypeStruct((n, PROJ_PAD), F32),
        compiler_params=_cparams("arbitrary"),
        name="in_projection_rope" if rope else "in_projection",
    )(*args)


def _softmax_pv(scores, values):
    m = jnp.max(scores[0], axis=-1, keepdims=True)
    for s in scores[1:]:
        m = jnp.maximum(m, jnp.max(s, axis=-1, keepdims=True))
    l = 0.0
    o = 0.0
    for s, v in zip(scores, values):
        p = jnp.exp(s - m)
        l = l + jnp.sum(p, axis=-1, keepdims=True)
        o = o + _dot(p.astype(BF16), v)
    return o * (1.0 / l)


def _ctx_attn_kernel(naq_ref, nak_ref, nav_ref, gq_ref, gk_ref, gv_ref, oa_ref, oc_ref):
    low = _low_half(SEQ)
    for p in range(NA_W // LANE):
        cols = slice(p * LANE, (p + 1) * LANE)
        q = naq_ref[:, cols]
        k = nak_ref[:, cols].astype(BF16)
        v = nav_ref[:, cols].astype(BF16)
        outs = []
        for half in range(2):
            qm = jnp.where(low if half == 0 else ~low, q, 0.0).astype(BF16)
            outs.append(_softmax_pv([_dot_nt(qm, k) * ATTN_SCALE], [v]))
        oa_ref[:, cols] = jnp.where(low, outs[0], outs[1])
    k = gk_ref[...]
    v = gv_ref[...]
    k_at = (pltpu.roll(k, HEAD_DIM, axis=1).astype(BF16), k.astype(BF16))
    v_at = (pltpu.roll(v, HEAD_DIM, axis=1).astype(BF16), v.astype(BF16))
    group = GQA_HEADS // GQA_KV_HEADS
    for p in range(GQA_W // LANE):
        cols = slice(p * LANE, (p + 1) * LANE)
        q = gq_ref[:, cols]
        outs = []
        for half in range(2):
            g = (2 * p + half) // group
            qm = jnp.where(low if half == 0 else ~low, q, 0.0).astype(BF16)
            outs.append(_softmax_pv([_dot_nt(qm, k_at[g == half]) * ATTN_SCALE], [v_at[g == half]]))
        oc_ref[:, cols] = jnp.where(low, outs[0], outs[1])


def ctx_attention(z):
    col = lambda w, off: pl.BlockSpec((SEQ, w), lambda b: (b, off // w))
    return pl.pallas_call(
        _ctx_attn_kernel,
        grid=(BATCH,),
        in_specs=[col(NA_W, OFF_NA_Q), col(NA_W, OFF_NA_K), col(NA_W, OFF_NA_V),
                  col(GQA_W, OFF_GQ_Q), col(LANE, OFF_GQ_K), col(LANE, OFF_GQ_V)],
        out_specs=[col(NA_W, 0), col(GQA_W, 0)],
        out_shape=[jax.ShapeDtypeStruct((N_CTX_TOK, NA_W), F32), jax.ShapeDtypeStruct((N_CTX_TOK, GQA_W), F32)],
        compiler_params=_cparams("arbitrary"),
        name="ctx_attention",
    )(z, z, z, z, z, z)


GQA_TQ = 512


def _lat_gqa_kernel(q_ref, kl_ref, vl_ref, kc_ref, vc_ref, o_ref):
    low = _low_half(GQA_TQ)
    at = lambda x: (pltpu.roll(x, HEAD_DIM, axis=1).astype(BF16), x.astype(BF16))
    kl_at, vl_at, kc_at, vc_at = at(kl_ref[...]), at(vl_ref[...]), at(kc_ref[0]), at(vc_ref[0])
    group = GQA_HEADS // GQA_KV_HEADS
    for p in range(GQA_W // LANE):
        cols = slice(p * LANE, (p + 1) * LANE)
        q = q_ref[:, cols]
        outs = []
        for half in range(2):
            same = ((2 * p + half) // group) == half
            qm = jnp.where(low if half == 0 else ~low, q, 0.0).astype(BF16)
            outs.append(_softmax_pv(
                [_dot_nt(qm, kc_at[same]) * ATTN_SCALE, _dot_nt(qm, kl_at[same]) * ATTN_SCALE],
                [vc_at[same], vl_at[same]]))
        o_ref[:, cols] = jnp.where(low, outs[0], outs[1])


def lat_gqa(z, cache_k, cache_v, layer):
    per = DEC_SEQ // GQA_TQ
    kv = lambda off: pl.BlockSpec((DEC_SEQ, LANE), lambda b, i: (b, off // LANE))
    cache = pl.BlockSpec((1, None, PAST_LEN, LANE), lambda b, i: (b, layer, 0, 0))
    return pl.pallas_call(
        _lat_gqa_kernel,
        grid=(DEC_BATCH, per),
        in_specs=[pl.BlockSpec((GQA_TQ, GQA_W), lambda b, i: (b * per + i, OFF_GQ_Q // GQA_W)),
                  kv(OFF_GQ_K), kv(OFF_GQ_V), cache, cache],
        out_specs=pl.BlockSpec((GQA_TQ, GQA_W), lambda b, i: (b * per + i, 0)),
        out_shape=jax.ShapeDtypeStruct((N_LAT_TOK, GQA_W), F32),
        compiler_params=_cparams("arbitrary", "arbitrary"),
        name="lat_gqa",
    )(z, z, z, cache_k, cache_v)


NA_RB = 4
NA_UNION = NA_RB + NA_WIN_ROWS
NA_TQ = NA_RB * GRID_W
NA_TK = NA_UNION * GRID_W
NA_TILES = 2 * NA_WIN_ROWS


def _na_union_start(i):
    return jnp.clip(i * NA_RB - NA_WIN_ROWS // 2, 0, GRID_H - NA_UNION)


def _na_tile_ids(i):
    u0 = _na_union_start(i)
    ids = []
    for t in range(NA_RB):
        r = i * NA_RB + t
        rs = jnp.clip(r - NA_WIN_ROWS // 2, 0, GRID_H - NA_WIN_ROWS)
        row_ids = []
        for j in range(NA_UNION):
            kr = u0 + j
            inside = (kr >= rs) & (kr < rs + NA_WIN_ROWS)
            row_ids.append(jnp.where(inside, kr - r + NA_WIN_ROWS - 1, NA_TILES - 1))
        ids.append(row_ids)
    return ids


def _na_bias(tab_ref, head, ids):
    rows = [jnp.concatenate([tab_ref[head, 0, row_ids[j]] + tab_ref[head, 1, row_ids[j + 1]]
                             for j in range(0, NA_UNION, 2)], axis=1) for row_ids in ids]
    return jnp.concatenate(rows, axis=0)


def _lat_na_kernel(q_ref, k_ref, v_ref, kc_ref, vc_ref, tab_ref, o_ref):
    i = pl.program_id(1)
    start = pl.multiple_of(_na_union_start(i) * GRID_W, GRID_W)
    low = _low_half(NA_TQ)
    ids = _na_tile_ids(i)
    for p in range(NA_W // LANE):
        cols = slice(p * LANE, (p + 1) * LANE)
        q = q_ref[:, cols]
        kl = k_ref[pl.ds(start, NA_TK), cols].astype(BF16)
        vl = v_ref[pl.ds(start, NA_TK), cols].astype(BF16)
        kc = kc_ref[0, :, cols].astype(BF16)
        vc = vc_ref[0, :, cols].astype(BF16)
        outs = []
        for half in range(2):
            qm = jnp.where(low if half == 0 else ~low, q, 0.0).astype(BF16)
            s_loc = _dot_nt(qm, kl) * ATTN_SCALE + _na_bias(tab_ref, 2 * p + half, ids)
            s_ctx = _dot_nt(qm, kc) * ATTN_SCALE
            outs.append(_softmax_pv([s_loc, s_ctx], [vl, vc]))
        o_ref[:, cols] = jnp.where(low, outs[0], outs[1])


def na_bias_table(rpb):
    cq = np.arange(GRID_W)
    ck = np.arange(GRID_W)
    dc = np.clip(ck[None, :] - cq[:, None], -(NA_WIN_COLS - 1), NA_WIN_COLS - 1) + NA_WIN_COLS - 1
    col_start = np.clip(cq - NA_WIN_COLS // 2, 0, GRID_W - NA_WIN_COLS)
    in_win = (ck[None, :] >= col_start[:, None]) & (ck[None, :] < col_start[:, None] + NA_WIN_COLS)
    col_sel = (dc[:, :, None] == np.arange(2 * NA_WIN_COLS - 1)).astype(np.float32)
    tiles = jnp.einsum("lhrc,qkc->lhrqk", rpb.astype(F32), col_sel, precision=lax.Precision.HIGHEST)
    tiles = jnp.where(in_win, tiles, NEG)
    masked = jnp.full(tiles.shape[:2] + (1, GRID_W, GRID_W), NEG, F32)
    tiles = jnp.concatenate([tiles, masked], axis=2)
    zero = jnp.zeros_like(tiles)
    return jnp.stack([jnp.concatenate([tiles, zero], -1), jnp.concatenate([zero, tiles], -1)], axis=2)


def lat_na(z, cache_k, cache_v, tab, layer):
    per = GRID_H // NA_RB
    kv = lambda off: pl.BlockSpec((DEC_SEQ, NA_W), lambda b, i: (b, off // NA_W))
    cache = pl.BlockSpec((1, None, PAST_LEN, NA_W), lambda b, i: (b, layer, 0, 0))
    return pl.pallas_call(
        _lat_na_kernel,
        grid=(DEC_BATCH, per),
        in_specs=[pl.BlockSpec((NA_TQ, NA_W), lambda b, i: (b * per + i, 0)),
                  kv(OFF_NA_K), kv(OFF_NA_V), cache, cache,
                  pl.BlockSpec((None, NA_HEADS, 2, NA_TILES, GRID_W, LANE), lambda b, i: (layer, 0, 0, 0, 0, 0))],
        out_specs=pl.BlockSpec((NA_TQ, NA_W), lambda b, i: (b * per + i, 0)),
        out_shape=jax.ShapeDtypeStruct((N_LAT_TOK, NA_W), F32),
        compiler_params=_cparams("arbitrary", "arbitrary"),
        name="lat_na",
    )(z, z, z, cache_k, cache_v, tab)


def _log_sigmoid(x):
    return jnp.minimum(x, 0.0) - jnp.log1p(jnp.exp(-jnp.abs(x)))


def _split3(x):
    hi = x.astype(BF16)
    r = x - hi.astype(F32)
    mid = r.astype(BF16)
    lo = (r - mid.astype(F32)).astype(BF16)
    return hi, mid, lo


def _mlstm_gate_tables(gates, fwd):
    t = gates.shape[0]
    jj = lax.broadcasted_iota(jnp.int32, (t, t), 0)
    ss = lax.broadcasted_iota(jnp.int32, (t, t), 1)
    feeds = (jj <= ss) if fwd else (jj >= ss)
    feeds_b = jnp.where(feeds, 1.0, 0.0).astype(BF16)
    reached_b = jnp.where((ss <= jj) if fwd else (ss >= jj), 1.0, 0.0).astype(BF16)
    eye_b = jnp.where(jj == ss, 1.0, 0.0).astype(BF16)
    lf = _split3(_log_sigmoid(gates))
    gs = _split3(gates)
    b_col = _dot(reached_b, lf[0]) + _dot(reached_b, lf[1]) + _dot(reached_b, lf[2])
    b_row = _dot_tn(lf[0], feeds_b) + _dot_tn(lf[1], feeds_b) + _dot_tn(lf[2], feeds_b)
    ig_row = _dot_tn(gs[0], eye_b) + _dot_tn(gs[1], eye_b) + _dot_tn(gs[2], eye_b)
    return ig_row, b_row, b_col, feeds


def _mlstm_head(k, qt, vt, kt, u, b_row, ig_row, feeds, ct, n, m, half, fwd):
    t = k.shape[0]
    dmat = jnp.where(feeds, b_row + u, NEG)
    m_inter = b_row + m
    m_j = jnp.maximum(m_inter, jnp.max(dmat, axis=0, keepdims=True))
    w = jnp.exp(dmat - m_j)
    decay = jnp.exp(m_inter - m_j)
    first = lax.broadcasted_iota(jnp.int32, (LANE, t), 0) < HEAD_DIM
    mine = first if half == 0 else ~first
    qm = jnp.where(mine, qt, 0.0)
    qb = qm.astype(BF16)
    kb = k.astype(BF16)
    qk = _dot(kb, qb) * w
    num = decay * _dot(ct.astype(BF16), qb) + _dot(vt.astype(BF16), qk.astype(BF16))
    den = decay * jnp.sum(qm * n, axis=0, keepdims=True) + jnp.sum(qk, axis=0, keepdims=True)
    h = jnp.where(mine, num * (1.0 / jnp.maximum(jnp.abs(den), jnp.exp(-m_j))), 0.0)
    last = t - 1 if fwd else 0
    b_last = b_row[:, last:last + 1]
    g = b_last - b_row + ig_row
    m_new = jnp.maximum(b_last + m, jnp.max(g, axis=1, keepdims=True))
    ws = jnp.exp(g - m_new)
    d_last = jnp.exp(b_last + m - m_new)
    ct_add = _dot((vt * ws).astype(BF16), kb)
    n_add = jnp.sum(jnp.where(mine, kt * ws, 0.0), axis=1, keepdims=True)
    return h, d_last, ct_add, n_add, m_new


def _mlstm_kernel(zf_ref, zb_ref, c0_ref, n0_ref, m0_ref, hf_ref, hb_ref, co_ref, no_ref, mo_ref,
                  c_s, n_s, m_s):
    chunk = pl.program_id(1)
    pairs = ML_W // LANE
    zero = jnp.zeros((HEAD_DIM, HEAD_DIM), F32)

    ri = lax.broadcasted_iota(jnp.int32, (LANE, LANE), 0)
    ci = lax.broadcasted_iota(jnp.int32, (LANE, LANE), 1)
    eye = ri == ci
    first_block = (ri < HEAD_DIM) & (ci < HEAD_DIM)
    second_block = (ri >= HEAD_DIM) & (ci >= HEAD_DIM)
    first_rows = lax.broadcasted_iota(jnp.int32, (LANE, 1), 0) < HEAD_DIM

    @pl.when(chunk == 0)
    def _():
        for d in range(2):
            for p in range(pairs):
                top = jnp.concatenate([c0_ref[0, d, 2 * p], zero], axis=1)
                bot = jnp.concatenate([zero, c0_ref[0, d, 2 * p + 1]], axis=1)
                c_s[d * pairs + p] = jnp.concatenate([top, bot], axis=0).T
                e = d * ML_HEADS + 2 * p
                n_row = jnp.concatenate([n0_ref[0, e:e + 1, :], n0_ref[0, e + 1:e + 2, :]], axis=1)
                n_s[d * pairs + p] = jnp.sum(jnp.where(eye, n_row, 0.0), axis=1, keepdims=True)
            m_s[d] = m0_ref[0, d * ML_HEADS:(d + 1) * ML_HEADS, :]

    for d, (z_ref, h_ref) in enumerate(((zf_ref, hf_ref), (zb_ref, hb_ref))):
        gates = z_ref[:, OFF_ML_G - OFF_ML_Q:OFF_ML_G - OFF_ML_Q + LANE]
        ig_rows, b_rows, b_cols, feeds = _mlstm_gate_tables(gates, d == 0)
        u = gates - pltpu.roll(b_cols, LANE - ML_HEADS, axis=1)
        m_all = m_s[d]
        m_rows = []
        for p in range(pairs):
            k = z_ref[:, ML_W + p * LANE:ML_W + (p + 1) * LANE] * ATTN_SCALE
            qt = z_ref[:, p * LANE:(p + 1) * LANE].T
            vt = z_ref[:, 2 * ML_W + p * LANE:2 * ML_W + (p + 1) * LANE].T
            kt = k.T
            ct = c_s[d * pairs + p]
            n = n_s[d * pairs + p]
            res = []
            for half in range(2):
                hd = 2 * p + half
                gi = 2 * ML_HEADS * d + hd
                gf = gi + ML_HEADS
                res.append(_mlstm_head(k, qt, vt, kt, u[:, gi:gi + 1], b_rows[gf:gf + 1, :], ig_rows[gi:gi + 1, :],
                                       feeds, ct, n, m_all[hd:hd + 1, 0:1], half, d == 0))
            (h0, dl0, ca0, na0, mn0), (h1, dl1, ca1, na1, mn1) = res
            h_ref[:, p * LANE:(p + 1) * LANE] = (h0 + h1).T
            c_s[d * pairs + p] = jnp.where(first_block, dl0 * ct + ca0, jnp.where(second_block, dl1 * ct + ca1, 0.0))
            n_s[d * pairs + p] = jnp.where(first_rows, dl0 * n + na0, dl1 * n + na1)
            m_rows += [jnp.broadcast_to(mn0, (1, LANE)), jnp.broadcast_to(mn1, (1, LANE))]
        m_s[d] = jnp.concatenate(m_rows, axis=0)

    @pl.when(chunk == pl.num_programs(1) - 1)
    def _():
        for d in range(2):
            for p in range(pairs):
                c = c_s[d * pairs + p].T
                n_row = jnp.sum(jnp.where(eye, n_s[d * pairs + p], 0.0), axis=0, keepdims=True)
                co_ref[0, d, 2 * p] = c[:HEAD_DIM, :HEAD_DIM]
                co_ref[0, d, 2 * p + 1] = c[HEAD_DIM:, HEAD_DIM:]
                e = d * ML_HEADS + 2 * p
                no_ref[0, e:e + 1, :] = n_row[:, :HEAD_DIM]
                no_ref[0, e + 1:e + 2, :] = n_row[:, HEAD_DIM:]
            mo_ref[0, d * ML_HEADS:(d + 1) * ML_HEADS, :] = m_s[d]


def mlstm(z, batch, seq, c0, n0, m0, layer=None):
    nc = seq // ML_CHUNK
    zcol = OFF_ML_Q // ML_BLOCK_W
    state_c = pl.BlockSpec((1, 2, ML_HEADS, HEAD_DIM, HEAD_DIM), lambda b, c: (b, 0, 0, 0, 0))
    state_n = pl.BlockSpec((1, 2 * ML_HEADS, HEAD_DIM), lambda b, c: (b, 0, 0))
    state_m = pl.BlockSpec((1, 2 * ML_HEADS, LANE), lambda b, c: (b, 0, 0))
    if layer is None:
        init_c, init_n, init_m = state_c, state_n, state_m
    else:
        init_c = pl.BlockSpec((1, None, 2, ML_HEADS, HEAD_DIM, HEAD_DIM), lambda b, c: (b, layer, 0, 0, 0, 0))
        init_n = pl.BlockSpec((1, None, 2 * ML_HEADS, HEAD_DIM), lambda b, c: (b, layer, 0, 0))
        init_m = pl.BlockSpec((1, None, 2 * ML_HEADS, LANE), lambda b, c: (b, layer, 0, 0))
    pairs = ML_W // LANE
    return pl.pallas_call(
        _mlstm_kernel,
        grid=(batch, nc),
        in_specs=[
            pl.BlockSpec((ML_CHUNK, ML_BLOCK_W), lambda b, c: (b * nc + c, zcol)),
            pl.BlockSpec((ML_CHUNK, ML_BLOCK_W), lambda b, c: (b * nc + nc - 1 - c, zcol)),
            init_c, init_n, init_m,
        ],
        out_specs=[
            pl.BlockSpec((ML_CHUNK, ML_W), lambda b, c: (b * nc + c, 0)),
            pl.BlockSpec((ML_CHUNK, ML_W), lambda b, c: (b * nc + nc - 1 - c, 0)),
            state_c, state_n, state_m,
        ],
        out_shape=[
            jax.ShapeDtypeStruct((batch * seq, ML_W), F32),
            jax.ShapeDtypeStruct((batch * seq, ML_W), F32),
            jax.ShapeDtypeStruct((batch, 2, ML_HEADS, HEAD_DIM, HEAD_DIM), F32),
            jax.ShapeDtypeStruct((batch, 2 * ML_HEADS, HEAD_DIM), F32),
            jax.ShapeDtypeStruct((batch, 2 * ML_HEADS, LANE), F32),
        ],
        scratch_shapes=[
            pltpu.VMEM((2 * pairs, LANE, LANE), F32),
            pltpu.VMEM((2 * pairs, LANE, 1), F32),
            pltpu.VMEM((2, ML_HEADS, LANE), F32),
        ],
        compiler_params=_cparams("arbitrary", "arbitrary"),
        name="mlstm",
    )(z, z, c0, n0, m0)


def _layer_norm(u, g, b):
    mu = jnp.mean(u, axis=-1, keepdims=True)
    uc = u - mu
    var = jnp.mean(uc * uc, axis=-1, keepdims=True)
    return uc * lax.rsqrt(var + NORM_EPS) * g + b


def _outproj_kernel(x_ref, a_ref, hf_ref, hb_ref, og0_ref, og1_ref, c_ref, w_ref, mg_ref, g1_ref, sh2_ref, sc2_ref,
                    lng_ref, lnb_ref, rw_ref, x1_ref, h2_ref, lg_ref):
    low = _low_half(TM)
    mixed_b = []
    for p, og_ref in enumerate((og0_ref, og1_ref)):
        cols = slice(p * LANE, (p + 1) * LANE)
        h = hf_ref[:, cols] + hb_ref[:, cols]
        hc = h - _pair_mean(h, low)
        hn = hc * lax.rsqrt(_pair_mean(hc * hc, low) + NORM_EPS)
        mixed_b.append(hn * mg_ref[:, cols] * _sigmoid(og_ref[...]))
    mixed = jnp.concatenate([a_ref[...]] + mixed_b + [c_ref[...]], axis=-1).astype(BF16)
    y = _dot(mixed, w_ref[...])
    x1 = _layer_norm(DEEPNORM_ALPHA * x_ref[...] + g1_ref[0] * y, lng_ref[...], lnb_ref[...])
    h2 = (x1 * (1.0 + sc2_ref[0]) + sh2_ref[0]).astype(BF16)
    x1_ref[...] = x1
    h2_ref[...] = h2
    lg_ref[...] = _dot_nt(rw_ref[...], h2)


def out_projection(x, out_a, hf, hb, z, out_c, layer, w_out, ml_gain, mods, mod_row, ln_g, ln_b, router_t):
    n = x.shape[0]
    mod = lambda k: pl.BlockSpec((None, 1, 1, D_MODEL), lambda i: (layer, mod_row(i), 0, k))
    row = lambda w: pl.BlockSpec((TM, w), lambda i: (i, 0))
    zcol = lambda off: pl.BlockSpec((TM, LANE), lambda i: (i, off // LANE))
    per_layer = lambda r, c: pl.BlockSpec((None, r, c), lambda i: (layer, 0, 0))
    norm0 = pl.BlockSpec((None, None, 1, D_MODEL), lambda i: (layer, 0, 0, 0))
    return pl.pallas_call(
        _outproj_kernel,
        grid=(n // TM,),
        in_specs=[row(D_MODEL), row(NA_W), row(ML_W), row(ML_W), zcol(OFF_ML_O), zcol(OFF_ML_O + LANE), row(GQA_W),
                  per_layer(D_MODEL, D_MODEL), per_layer(1, ML_W), mod(2), mod(3), mod(4), norm0, norm0,
                  per_layer(N_EXPERTS, D_MODEL)],
        out_specs=[row(D_MODEL), row(D_MODEL), pl.BlockSpec((N_EXPERTS, TM), lambda i: (0, i))],
        out_shape=[
            jax.ShapeDtypeStruct((n, D_MODEL), F32),
            jax.ShapeDtypeStruct((n, D_MODEL), BF16),
            jax.ShapeDtypeStruct((N_EXPERTS, n), F32),
        ],
        compiler_params=_cparams("arbitrary"),
        name="out_projection",
    )(x, out_a, hf, hb, z, z, out_c, w_out, ml_gain, mods, mods, mods, ln_g, ln_b, router_t)


ROUTE_BLK = 256


def _prefix_count(x, tri):
    n = x.shape[1]
    outs = []
    carry = jnp.zeros((N_EXPERTS, 1), F32)
    for i in range(n // ROUTE_BLK):
        blk = x[:, i * ROUTE_BLK:(i + 1) * ROUTE_BLK]
        outs.append(_dot(blk, tri) + carry)
        carry = carry + jnp.sum(blk.astype(F32), axis=1, keepdims=True)
    return jnp.concatenate(outs, axis=1) if len(outs) > 1 else outs[0]


def _route_kernel(lg_ref, coder_ref, codet_ref, afft_ref, *, n, cap):
    sets = lg_ref.shape[1] // n
    lg = lg_ref[...]
    ex = jnp.exp(lg - jnp.max(lg, axis=0, keepdims=True))
    aff_all = ex / jnp.sum(ex, axis=0, keepdims=True)
    affs = [aff_all[:, s * n:(s + 1) * n] for s in range(sets)]

    def search(i, bits):
        bit = lax.shift_left(jnp.int32(1), 30 - i)
        out = []
        for aff, b in zip(affs, bits):
            cand = b | bit
            cnt = jnp.sum(jnp.where(aff >= pltpu.bitcast(cand, F32), 1.0, 0.0), axis=1, keepdims=True)
            out.append(jnp.where(cnt >= cap, cand, b))
        return tuple(out)

    floors = lax.fori_loop(0, 31, search, tuple(jnp.zeros((N_EXPERTS, 1), jnp.int32) for _ in range(sets)))
    ti = lax.broadcasted_iota(jnp.int32, (ROUTE_BLK, ROUTE_BLK), 0)
    tj = lax.broadcasted_iota(jnp.int32, (ROUTE_BLK, ROUTE_BLK), 1)
    tri = jnp.where(ti < tj, 1.0, 0.0).astype(BF16)
    pad = jnp.zeros((LANE - N_EXPERTS, n), F32)
    for s, (aff, floor_bits) in enumerate(zip(affs, floors)):
        thr = jnp.min(jnp.where(aff >= pltpu.bitcast(floor_bits, F32), aff, 2.0), axis=1, keepdims=True)
        gt = aff > thr
        eq = aff == thr
        need = cap - jnp.sum(jnp.where(gt, 1.0, 0.0), axis=1, keepdims=True)
        eq_rank = _prefix_count(jnp.where(eq, 1.0, 0.0).astype(BF16), tri)
        sel = jnp.where(gt, 1.0, jnp.where(eq & (eq_rank < need), 1.0, 0.0))
        pos = _prefix_count(sel.astype(BF16), tri)
        code = jnp.where(sel > 0.5, pos, -1.0)
        coder_ref[s] = code
        codet_ref[s] = jnp.concatenate([code, pad], axis=0).T
        afft_ref[s] = jnp.concatenate([aff, pad], axis=0).T


def route(logits_t, n, cap):
    tokens = logits_t.shape[1]
    sets = tokens // n
    whole = lambda shape: pl.BlockSpec(shape, lambda i: (0,) * len(shape))
    return pl.pallas_call(
        functools.partial(_route_kernel, n=n, cap=cap),
        grid=(1,),
        in_specs=[whole((N_EXPERTS, tokens))],
        out_specs=[whole((sets, N_EXPERTS, n)), whole((sets, n, LANE)), whole((sets, n, LANE))],
        out_shape=[
            jax.ShapeDtypeStruct((sets, N_EXPERTS, n), F32),
            jax.ShapeDtypeStruct((sets, n, LANE), F32),
            jax.ShapeDtypeStruct((sets, n, LANE), F32),
        ],
        compiler_params=_cparams("arbitrary"),
        name="route",
    )(logits_t)


def _gather_kernel(coder_ref, h_ref, xe_ref, *, cap, epb):
    e0 = pl.program_id(1) * epb
    n = coder_ref.shape[2]
    ci = lax.broadcasted_iota(jnp.int32, (cap, n), 0).astype(F32)
    onehot = [jnp.where(ci == coder_ref[0, pl.ds(e0 + k, 1), :], 1.0, 0.0).astype(BF16) for k in range(epb)]
    onehot = jnp.concatenate(onehot, axis=0) if epb > 1 else onehot[0]
    rows = _dot(onehot, h_ref[...]).astype(BF16)
    for k in range(epb):
        xe_ref[k] = rows[k * cap:(k + 1) * cap]


def gather_tokens(code_rows, h2, n, cap):
    sets = code_rows.shape[0]
    epb = max(1, 512 // cap)
    return pl.pallas_call(
        functools.partial(_gather_kernel, cap=cap, epb=epb),
        grid=(sets, N_EXPERTS // epb),
        in_specs=[
            pl.BlockSpec((1, N_EXPERTS, n), lambda s, e: (s, 0, 0)),
            pl.BlockSpec((n, D_MODEL), lambda s, e: (s, 0)),
        ],
        out_specs=pl.BlockSpec((epb, cap, D_MODEL), lambda s, e: (e, s, 0)),
        out_shape=jax.ShapeDtypeStruct((N_EXPERTS, sets * cap, D_MODEL), BF16),
        compiler_params=_cparams("arbitrary", "arbitrary"),
        name="gather_tokens",
    )(code_rows, h2)


def _expert_kernel(xi_ref, xl_ref, wg_ref, wu_ref, wd_ref, yi_ref, yl_ref, acci_ref, accl_ref):
    j = pl.program_id(1)

    @pl.when(j == 0)
    def _():
        acci_ref[...] = jnp.zeros_like(acci_ref)
        accl_ref[...] = jnp.zeros_like(accl_ref)

    xs = (xi_ref[0], xl_ref[0])
    sums = [None, None]
    for s0 in range(0, EXP_TF, EXP_SUB):
        s1 = min(s0 + EXP_SUB, EXP_TF)
        wg = wg_ref[0, 0, :, s0:s1].astype(BF16)
        wu = wu_ref[0, 0, :, s0:s1].astype(BF16)
        wd = wd_ref[0, 0, s0:s1, :].astype(BF16)
        for g in range(2):
            a = _dot(xs[g], wg)
            b = _dot(xs[g], wu)
            hid = (a * _sigmoid(a) * b).astype(BF16)
            part = _dot(hid, wd)
            sums[g] = part if sums[g] is None else sums[g] + part
    acci_ref[...] += sums[0]
    accl_ref[...] += sums[1]

    @pl.when(j == pl.num_programs(1) - 1)
    def _():
        yi_ref[0] = acci_ref[...].astype(BF16)
        yl_ref[0] = accl_ref[...].astype(BF16)


def experts(xe_ctx, xe_lat, w_gate, w_up, w_down, layer):
    xin = lambda s: pl.BlockSpec((1, s, D_MODEL), lambda e, j: (e, 0, 0))
    return pl.pallas_call(
        _expert_kernel,
        grid=(N_EXPERTS, EXPERT_HIDDEN // EXP_TF),
        in_specs=[
            xin(SLOTS_CTX), xin(SLOTS_LAT),
            pl.BlockSpec((1, 1, D_MODEL, EXP_TF), lambda e, j: (layer, e, 0, j)),
            pl.BlockSpec((1, 1, D_MODEL, EXP_TF), lambda e, j: (layer, e, 0, j)),
            pl.BlockSpec((1, 1, EXP_TF, D_MODEL), lambda e, j: (layer, e, j, 0)),
        ],
        out_specs=[xin(SLOTS_CTX), xin(SLOTS_LAT)],
        out_shape=[
            jax.ShapeDtypeStruct((N_EXPERTS, SLOTS_CTX, D_MODEL), BF16),
            jax.ShapeDtypeStruct((N_EXPERTS, SLOTS_LAT, D_MODEL), BF16),
        ],
        scratch_shapes=[pltpu.VMEM((SLOTS_CTX, D_MODEL), F32), pltpu.VMEM((SLOTS_LAT, D_MODEL), F32)],
        compiler_params=_cparams("arbitrary", "arbitrary"),
        name="experts",
    )(xe_ctx, xe_lat, w_gate, w_up, w_down)


COMB_TT = 256


def _scatter_per_expert(code, aff, ye_ref, cap):
    tt = code.shape[0]
    li = lax.broadcasted_iota(jnp.int32, (tt, cap), 1).astype(F32)
    acc = jnp.zeros((tt, D_MODEL), F32)
    for e in range(N_EXPERTS):
        onehot = jnp.where(li == code[:, e:e + 1], 1.0, 0.0).astype(BF16)
        acc = acc + aff[:, e:e + 1] * _dot(onehot, ye_ref[e])
    return acc


def _scatter_merged(code, aff, ye_ref, cap):
    tt = code.shape[0]
    slots = N_EXPERTS * cap
    shift = cap.bit_length() - 1
    ei = lax.broadcasted_iota(jnp.int32, (LANE, slots), 0)
    si = lax.broadcasted_iota(jnp.int32, (LANE, slots), 1)
    expand = jnp.where(lax.shift_right_logical(si, shift) == ei, 1.0, 0.0).astype(BF16)
    slot = (lax.broadcasted_iota(jnp.int32, (tt, slots), 1) & (cap - 1)).astype(F32)
    hit = _dot(code.astype(BF16), expand) == slot
    a_hi = aff.astype(BF16)
    a_lo = (aff - a_hi.astype(F32)).astype(BF16)
    ye = ye_ref[...].reshape(slots, D_MODEL)
    acc = _dot(jnp.where(hit, _dot(a_hi, expand), 0.0).astype(BF16), ye)
    return acc + _dot(jnp.where(hit, _dot(a_lo, expand), 0.0).astype(BF16), ye)


def _combine_kernel(code_ref, aff_ref, ye_ref, x1_ref, g2_ref, lng_ref, lnb_ref, o_ref, *, cap):
    scatter = _scatter_merged if cap < LANE else _scatter_per_expert
    acc = scatter(code_ref[0], aff_ref[0], ye_ref, cap)
    u = DEEPNORM_ALPHA * x1_ref[...] + g2_ref[0] * acc
    o_ref[...] = _layer_norm(u, lng_ref[...], lnb_ref[...])


def combine(code, aff, ye, x1, layer, mods, set_mod_row, ln_g, ln_b, n, cap):
    sets = code.shape[0]
    per = n // COMB_TT
    tok = pl.BlockSpec((1, COMB_TT, LANE), lambda s, i: (s, i, 0))
    norm1 = pl.BlockSpec((None, None, 1, D_MODEL), lambda s, i: (layer, 1, 0, 0))
    return pl.pallas_call(
        functools.partial(_combine_kernel, cap=cap),
        grid=(sets, per),
        in_specs=[
            tok, tok,
            pl.BlockSpec((N_EXPERTS, cap, D_MODEL), lambda s, i: (0, s, 0)),
            pl.BlockSpec((COMB_TT, D_MODEL), lambda s, i: (s * per + i, 0)),
            pl.BlockSpec((None, 1, 1, D_MODEL), lambda s, i: (layer, set_mod_row(s), 0, 5)),
            norm1, norm1,
        ],
        out_specs=pl.BlockSpec((COMB_TT, D_MODEL), lambda s, i: (s * per + i, 0)),
        out_shape=jax.ShapeDtypeStruct((sets * n, D_MODEL), F32),
        compiler_params=_cparams("arbitrary", "arbitrary"),
        name="combine",
    )(code, aff, ye, x1, mods, ln_g, ln_b)


def _axial_rope_tables():
    t = jnp.arange(DEC_SEQ)
    row = (t // GRID_W).astype(F32)
    col = (t % GRID_W).astype(F32)
    half = HEAD_DIM // 2
    inv = ROPE_THETA ** (-jnp.arange(0, half, 2, dtype=F32) / half)
    ang_r = row[:, None] * inv
    ang_c = col[:, None] * inv
    ang = jnp.concatenate([ang_r, ang_r, ang_c, ang_c] * 2, -1)
    return jnp.cos(ang), jnp.sin(ang)


def _ctx_mod_row(i):
    return 0


def _lat_tile_mod_row(i):
    return 1 + i // (DEC_SEQ // TM)


def _lat_set_mod_row(s):
    return 1 + s


def kernel(x_prompt, x_sample, c, cache_na_k, cache_na_v, cache_gqa_k, cache_gqa_v, state_mlstm_c, state_mlstm_n,
           state_mlstm_m, c_ctx, ada_w, ada_b, w_in, b_gate, w_out, na_rpb, qk_norm_g, ml_norm_g, ln_g, ln_b,
           router_w, w_gate, w_up, w_down):
    cond8 = jnp.concatenate([c_ctx[None, :], c, jnp.zeros((8 - 1 - DEC_BATCH, D_MODEL), F32)], 0)
    mods = adaln(cond8, ada_w, ada_b).reshape(DEPTH, 8, 1, 6 * D_MODEL)
    xc = x_prompt.reshape(N_CTX_TOK, D_MODEL)
    xl = x_sample.reshape(N_LAT_TOK, D_MODEL)
    cos, sin = _axial_rope_tables()
    na_tab = na_bias_table(na_rpb)

    split = OFF_ML_G + N_GATES
    w_pad = jnp.concatenate([w_in[:, :, :split], jnp.zeros((DEPTH, D_MODEL, LANE - N_GATES), F32),
                             w_in[:, :, split:]], 2).astype(BF16)
    bias_row = jnp.pad(b_gate, ((0, 0), (OFF_ML_G, PROJ_PAD - OFF_ML_G - N_GATES))).reshape(DEPTH, 1, PROJ_PAD)
    gains = jnp.tile(qk_norm_g, (1, 1, 2)).reshape(DEPTH, 2, 1, LANE)
    w_out_b = w_out.astype(BF16)
    router_t = jnp.swapaxes(router_w, 1, 2).astype(BF16)
    ml_gain = ml_norm_g.reshape(DEPTH, 1, ML_W)
    ln_g4 = ln_g.reshape(DEPTH, 2, 1, D_MODEL)
    ln_b4 = ln_b.reshape(DEPTH, 2, 1, D_MODEL)
    na_ck = cache_na_k.reshape(DEC_BATCH, DEPTH, PAST_LEN, NA_W)
    na_cv = cache_na_v.reshape(DEC_BATCH, DEPTH, PAST_LEN, NA_W)
    gq_ck = cache_gqa_k.reshape(DEC_BATCH, DEPTH, PAST_LEN, GQA_KV_W)
    gq_cv = cache_gqa_v.reshape(DEC_BATCH, DEPTH, PAST_LEN, GQA_KV_W)
    lat_n0 = state_mlstm_n.reshape(DEC_BATCH, DEPTH, 2 * ML_HEADS, HEAD_DIM)
    lat_m0 = jnp.broadcast_to(state_mlstm_m.reshape(DEC_BATCH, DEPTH, 2 * ML_HEADS, 1),
                              (DEC_BATCH, DEPTH, 2 * ML_HEADS, LANE))
    zeros_c = jnp.zeros((BATCH, 2, ML_HEADS, HEAD_DIM, HEAD_DIM), F32)
    zeros_n = jnp.zeros((BATCH, 2 * ML_HEADS, HEAD_DIM), F32)
    zeros_m = jnp.zeros((BATCH, 2 * ML_HEADS, LANE), F32)
    zcs, states = [], []

    for l in range(DEPTH):
        zc = in_projection(xc, l, mods, _ctx_mod_row, w_pad, bias_row, gains)
        zl = in_projection(xl, l, mods, _lat_tile_mod_row, w_pad, bias_row, gains, cos, sin)

        out_a_c, out_c_c = ctx_attention(zc)
        hf_c, hb_c, sc, sn, sm = mlstm(zc, BATCH, SEQ, zeros_c, zeros_n, zeros_m)
        out_a_l = lat_na(zl, na_ck, na_cv, na_tab, l)
        out_c_l = lat_gqa(zl, gq_ck, gq_cv, l)
        hf_l, hb_l = mlstm(zl, DEC_BATCH, DEC_SEQ, state_mlstm_c, lat_n0, lat_m0, layer=l)[:2]
        zcs.append(zc)
        states.append((sc, sn, sm))

        x1_c, h2_c, lg_c = out_projection(xc, out_a_c, hf_c, hb_c, zc, out_c_c, l, w_out_b, ml_gain, mods,
                                          _ctx_mod_row, ln_g4, ln_b4, router_t)
        x1_l, h2_l, lg_l = out_projection(xl, out_a_l, hf_l, hb_l, zl, out_c_l, l, w_out_b, ml_gain, mods,
                                          _lat_tile_mod_row, ln_g4, ln_b4, router_t)
        crow_c, code_c, aff_c = route(lg_c, SEQ, CAP_CTX)
        crow_l, code_l, aff_l = route(lg_l, DEC_SEQ, CAP_LAT)
        xe_c = gather_tokens(crow_c, h2_c, SEQ, CAP_CTX)
        xe_l = gather_tokens(crow_l, h2_l, DEC_SEQ, CAP_LAT)
        ye_c, ye_l = experts(xe_c, xe_l, w_gate, w_up, w_down, l)
        xc = combine(code_c, aff_c, ye_c, x1_c, l, mods, _ctx_mod_row, ln_g4, ln_b4, SEQ, CAP_CTX)
        xl = combine(code_l, aff_l, ye_l, x1_l, l, mods, _lat_set_mod_row, ln_g4, ln_b4, DEC_SEQ, CAP_LAT)

    y_prompt = xc.reshape(BATCH, SEQ, D_MODEL)
    y_sample = xl.reshape(DEC_BATCH, DEC_SEQ, D_MODEL)
    cols = lambda off, heads: jnp.concatenate(
        [zc[:, off:off + heads * HEAD_DIM].reshape(BATCH, 1, SEQ, heads, HEAD_DIM) for zc in zcs], 1)
    new_c = jnp.stack([s[0] for s in states], 1)
    new_n = jnp.stack([s[1] for s in states], 1).reshape(BATCH, DEPTH, 2, ML_HEADS, HEAD_DIM)
    new_m = jnp.stack([s[2][:, :, 0] for s in states], 1).reshape(BATCH, DEPTH, 2, ML_HEADS)
    return (y_prompt, y_sample, cols(OFF_NA_K, NA_HEADS), cols(OFF_NA_V, NA_HEADS), cols(OFF_GQ_K, GQA_KV_HEADS),
            cols(OFF_GQ_V, GQA_KV_HEADS), new_c, new_n, new_m)
```

```python
import functools

import jax
import jax.numpy as jnp
import numpy as np
from jax import lax
from jax.experimental import pallas as pl
from jax.experimental.pallas import tpu as pltpu

D_MODEL = 1024
BATCH = 16
SEQ = 256
DEPTH = 2
DEC_BATCH = 2
DEC_SEQ = 2048
PAST_LEN = 256
GRID_W = 64
GRID_H = DEC_SEQ // GRID_W
HEAD_DIM = 64
NA_HEADS = 6
NA_WIN_ROWS = 8
NA_WIN_COLS = 16
ML_HEADS = 4
GQA_HEADS = 6
GQA_KV_HEADS = 2
ROPE_THETA = 10000.0
N_EXPERTS = 16
EC_CAPACITY = 2
EXPERT_HIDDEN = 2816
NORM_EPS = 1e-6
NA_W = NA_HEADS * HEAD_DIM
ML_W = ML_HEADS * HEAD_DIM
GQA_W = GQA_HEADS * HEAD_DIM
GQA_KV_W = GQA_KV_HEADS * HEAD_DIM
N_GATES = 4 * ML_HEADS
DEEPNORM_ALPHA = (2 * DEPTH) ** 0.25
ATTN_SCALE = HEAD_DIM ** -0.5
F32 = jnp.float32
BF16 = jnp.bfloat16

LANE = 128
N_CTX_TOK = BATCH * SEQ
N_LAT_TOK = DEC_BATCH * DEC_SEQ
NEG = -1e30

OFF_NA_Q = 0
OFF_NA_K = OFF_NA_Q + NA_W
OFF_NA_V = OFF_NA_K + NA_W
OFF_ML_Q = OFF_NA_V + NA_W
OFF_ML_K = OFF_ML_Q + ML_W
OFF_ML_V = OFF_ML_K + ML_W
OFF_ML_O = OFF_ML_V + ML_W
OFF_ML_G = OFF_ML_O + ML_W
OFF_GQ_Q = OFF_ML_G + LANE
OFF_GQ_K = OFF_GQ_Q + GQA_W
OFF_GQ_V = OFF_GQ_K + GQA_KV_W
PROJ_PAD = OFF_GQ_V + GQA_KV_W
ML_BLOCK_W = OFF_GQ_Q - OFF_ML_Q

TM = 512
ML_CHUNK = 256
EXP_SUB = 256
EXP_TF = 5 * EXP_SUB
EXP_STEPS = -(-EXPERT_HIDDEN // EXP_TF)
EXP_LAST_SUBS = (EXPERT_HIDDEN - (EXP_STEPS - 1) * EXP_TF) // EXP_SUB
CAP_CTX = EC_CAPACITY * SEQ // N_EXPERTS
CAP_LAT = EC_CAPACITY * DEC_SEQ // N_EXPERTS
SLOTS_CTX = BATCH * CAP_CTX
SLOTS_LAT = DEC_BATCH * CAP_LAT
VMEM_LIMIT = 56 * 1024 * 1024


def _cparams(*sem):
    return pltpu.CompilerParams(dimension_semantics=sem, vmem_limit_bytes=VMEM_LIMIT)


def _sigmoid(x):
    return 1.0 / (1.0 + jnp.exp(-x))


def _dot(a, b):
    return jnp.dot(a, b, preferred_element_type=F32)


def _dot_nt(a, b):
    return lax.dot_general(a, b, (((1,), (1,)), ((), ())), preferred_element_type=F32)


def _dot_tn(a, b):
    return lax.dot_general(a, b, (((0,), (0,)), ((), ())), preferred_element_type=F32)


def _low_half(rows):
    return lax.broadcasted_iota(jnp.int32, (rows, LANE), 1) < HEAD_DIM


def _pair_mean(x, low):
    s_lo = jnp.sum(jnp.where(low, x, 0.0), axis=-1, keepdims=True)
    s_hi = jnp.sum(jnp.where(low, 0.0, x), axis=-1, keepdims=True)
    return jnp.where(low, s_lo, s_hi) * (1.0 / HEAD_DIM)


def _adaln_kernel(c_ref, w_ref, b_ref, o_ref):
    c = c_ref[...]
    s = c * _sigmoid(c)
    o_ref[0] = _dot(s.astype(BF16), w_ref[0].astype(BF16)) + b_ref[0]


def adaln(cond8, ada_w, ada_b):
    tn = 1536
    return pl.pallas_call(
        _adaln_kernel,
        grid=(DEPTH, 6 * D_MODEL // tn),
        in_specs=[
            pl.BlockSpec((8, D_MODEL), lambda l, j: (0, 0)),
            pl.BlockSpec((1, D_MODEL, tn), lambda l, j: (l, 0, j)),
            pl.BlockSpec((1, 1, tn), lambda l, j: (l, 0, j)),
        ],
        out_specs=pl.BlockSpec((1, 8, tn), lambda l, j: (l, 0, j)),
        out_shape=jax.ShapeDtypeStruct((DEPTH, 8, 6 * D_MODEL), F32),
        compiler_params=_cparams("arbitrary", "arbitrary"),
        name="adaln",
    )(cond8, ada_w, ada_b.reshape(DEPTH, 1, 6 * D_MODEL))


def _inproj_kernel(x_ref, sh_ref, sc_ref, w_ref, b_ref, gq_ref, gk_ref, *rest, rope):
    if rope:
        cos_ref, sin_ref, z_ref = rest
    else:
        (z_ref,) = rest
    h = x_ref[...] * (1.0 + sc_ref[0]) + sh_ref[0]
    z = _dot(h.astype(BF16), w_ref[...]) + b_ref[...]
    z_ref[...] = z
    rows = z.shape[0]
    low = _low_half(rows)
    if rope:
        lane = lax.broadcasted_iota(jnp.int32, (rows, LANE), 1)
        first_quarter = (lane & (HEAD_DIM // 4)) == 0

    def norm_pair(x, gain):
        y = x * lax.rsqrt(_pair_mean(x * x, low) + NORM_EPS) * gain
        if rope:
            rot = jnp.where(first_quarter, -pltpu.roll(y, LANE - HEAD_DIM // 4, axis=1),
                            pltpu.roll(y, HEAD_DIM // 4, axis=1))
            y = y * cos_ref[...] + rot * sin_ref[...]
        return y

    for p in range(GQA_W // LANE):
        c0 = OFF_GQ_Q + p * LANE
        z_ref[:, c0:c0 + LANE] = norm_pair(z[:, c0:c0 + LANE], gq_ref[...])
    z_ref[:, OFF_GQ_K:OFF_GQ_K + LANE] = norm_pair(z[:, OFF_GQ_K:OFF_GQ_K + LANE], gk_ref[...])


def in_projection(x, layer, mods, mod_row, w_pad, bias_row, gains, cos=None, sin=None):
    n = x.shape[0]
    rope = cos is not None
    in_specs = [
        pl.BlockSpec((TM, D_MODEL), lambda i: (i, 0)),
        pl.BlockSpec((None, 1, 1, D_MODEL), lambda i: (layer, mod_row(i), 0, 0)),
        pl.BlockSpec((None, 1, 1, D_MODEL), lambda i: (layer, mod_row(i), 0, 1)),
        pl.BlockSpec((None, D_MODEL, PROJ_PAD), lambda i: (layer, 0, 0)),
        pl.BlockSpec((None, 1, PROJ_PAD), lambda i: (layer, 0, 0)),
        pl.BlockSpec((None, None, 1, LANE), lambda i: (layer, 0, 0, 0)),
        pl.BlockSpec((None, None, 1, LANE), lambda i: (layer, 1, 0, 0)),
    ]
    args = [x, mods, mods, w_pad, bias_row, gains, gains]
    if rope:
        per = cos.shape[0] // TM
        in_specs += [pl.BlockSpec((TM, LANE), lambda i: (i % per, 0))] * 2
        args += [cos, sin]
    return pl.pallas_call(
        functools.partial(_inproj_kernel, rope=rope),
        grid=(n // TM,),
        in_specs=in_specs,
        out_specs=pl.BlockSpec((TM, PROJ_PAD), lambda i: (i, 0)),
        out_shape=jax.ShapeDtypeStruct((n, PROJ_PAD), F32),
        compiler_params=_cparams("arbitrary"),
        name="in_projection_rope" if rope else "in_projection",
    )(*args)


def _softmax_pv(scores, values):
    m = jnp.max(scores[0], axis=-1, keepdims=True)
    for s in scores[1:]:
        m = jnp.maximum(m, jnp.max(s, axis=-1, keepdims=True))
    l = 0.0
    o = 0.0
    for s, v in zip(scores, values):
        p = jnp.exp(s - m)
        l = l + jnp.sum(p, axis=-1, keepdims=True)
        o = o + _dot(p.astype(BF16), v)
    return o * (1.0 / l)


def _ctx_attn_kernel(naq_ref, nak_ref, nav_ref, gq_ref, gk_ref, gv_ref, oa_ref, oc_ref):
    low = _low_half(SEQ)
    for p in range(NA_W // LANE):
        cols = slice(p * LANE, (p + 1) * LANE)
        q = naq_ref[:, cols] * ATTN_SCALE
        k = nak_ref[:, cols].astype(BF16)
        v = nav_ref[:, cols].astype(BF16)
        outs = []
        for half in range(2):
            qm = jnp.where(low if half == 0 else ~low, q, 0.0).astype(BF16)
            outs.append(_softmax_pv([_dot_nt(qm, k)], [v]))
        oa_ref[:, cols] = jnp.where(low, outs[0], outs[1])
    k = gk_ref[...]
    v = gv_ref[...]
    k_at = (pltpu.roll(k, HEAD_DIM, axis=1).astype(BF16), k.astype(BF16))
    v_at = (pltpu.roll(v, HEAD_DIM, axis=1).astype(BF16), v.astype(BF16))
    group = GQA_HEADS // GQA_KV_HEADS
    for p in range(GQA_W // LANE):
        cols = slice(p * LANE, (p + 1) * LANE)
        q = gq_ref[:, cols] * ATTN_SCALE
        outs = []
        for half in range(2):
            g = (2 * p + half) // group
            qm = jnp.where(low if half == 0 else ~low, q, 0.0).astype(BF16)
            outs.append(_softmax_pv([_dot_nt(qm, k_at[g == half])], [v_at[g == half]]))
        oc_ref[:, cols] = jnp.where(low, outs[0], outs[1])


def ctx_attention(z):
    col = lambda w, off: pl.BlockSpec((SEQ, w), lambda b: (b, off // w))
    return pl.pallas_call(
        _ctx_attn_kernel,
        grid=(BATCH,),
        in_specs=[col(NA_W, OFF_NA_Q), col(NA_W, OFF_NA_K), col(NA_W, OFF_NA_V),
                  col(GQA_W, OFF_GQ_Q), col(LANE, OFF_GQ_K), col(LANE, OFF_GQ_V)],
        out_specs=[col(NA_W, 0), col(GQA_W, 0)],
        out_shape=[jax.ShapeDtypeStruct((N_CTX_TOK, NA_W), F32), jax.ShapeDtypeStruct((N_CTX_TOK, GQA_W), F32)],
        compiler_params=_cparams("arbitrary"),
        name="ctx_attention",
    )(z, z, z, z, z, z)


GQA_TQ = 512


def _lat_gqa_kernel(q_ref, kl_ref, vl_ref, kc_ref, vc_ref, o_ref):
    low = _low_half(GQA_TQ)
    at = lambda x: (pltpu.roll(x, HEAD_DIM, axis=1).astype(BF16), x.astype(BF16))
    kl_at, vl_at, kc_at, vc_at = at(kl_ref[...]), at(vl_ref[...]), at(kc_ref[0]), at(vc_ref[0])
    group = GQA_HEADS // GQA_KV_HEADS
    for p in range(GQA_W // LANE):
        cols = slice(p * LANE, (p + 1) * LANE)
        q = q_ref[:, cols] * ATTN_SCALE
        outs = []
        for half in range(2):
            same = ((2 * p + half) // group) == half
            qm = jnp.where(low if half == 0 else ~low, q, 0.0).astype(BF16)
            outs.append(_softmax_pv([_dot_nt(qm, kc_at[same]), _dot_nt(qm, kl_at[same])],
                                    [vc_at[same], vl_at[same]]))
        o_ref[:, cols] = jnp.where(low, outs[0], outs[1])


def lat_gqa(z, cache_k, cache_v, layer):
    per = DEC_SEQ // GQA_TQ
    kv = lambda off: pl.BlockSpec((DEC_SEQ, LANE), lambda b, i: (b, off // LANE))
    cache = pl.BlockSpec((1, None, PAST_LEN, LANE), lambda b, i: (b, layer, 0, 0))
    return pl.pallas_call(
        _lat_gqa_kernel,
        grid=(DEC_BATCH, per),
        in_specs=[pl.BlockSpec((GQA_TQ, GQA_W), lambda b, i: (b * per + i, OFF_GQ_Q // GQA_W)),
                  kv(OFF_GQ_K), kv(OFF_GQ_V), cache, cache],
        out_specs=pl.BlockSpec((GQA_TQ, GQA_W), lambda b, i: (b * per + i, 0)),
        out_shape=jax.ShapeDtypeStruct((N_LAT_TOK, GQA_W), F32),
        compiler_params=_cparams("arbitrary", "arbitrary"),
        name="lat_gqa",
    )(z, z, z, cache_k, cache_v)


NA_RB = 4
NA_UNION = NA_RB + NA_WIN_ROWS
NA_TQ = NA_RB * GRID_W
NA_TK = NA_UNION * GRID_W
NA_TILES = 2 * NA_WIN_ROWS


def _na_union_start(i):
    return jnp.clip(i * NA_RB - NA_WIN_ROWS // 2, 0, GRID_H - NA_UNION)


def _na_tile_ids(i):
    u0 = _na_union_start(i)
    ids = []
    for t in range(NA_RB):
        r = i * NA_RB + t
        rs = jnp.clip(r - NA_WIN_ROWS // 2, 0, GRID_H - NA_WIN_ROWS)
        row_ids = []
        for j in range(NA_UNION):
            kr = u0 + j
            inside = (kr >= rs) & (kr < rs + NA_WIN_ROWS)
            row_ids.append(jnp.where(inside, kr - r + NA_WIN_ROWS - 1, NA_TILES - 1))
        ids.append(row_ids)
    return ids


def _na_bias(tab_ref, head, ids):
    rows = [jnp.concatenate([tab_ref[head, 0, row_ids[j]] + tab_ref[head, 1, row_ids[j + 1]]
                             for j in range(0, NA_UNION, 2)], axis=1) for row_ids in ids]
    return jnp.concatenate(rows, axis=0)


def _lat_na_kernel(q_ref, k_ref, v_ref, kc_ref, vc_ref, tab_ref, o_ref):
    i = pl.program_id(1)
    start = pl.multiple_of(_na_union_start(i) * GRID_W, GRID_W)
    low = _low_half(NA_TQ)
    ids = _na_tile_ids(i)
    for p in range(NA_W // LANE):
        cols = slice(p * LANE, (p + 1) * LANE)
        q = q_ref[:, cols] * ATTN_SCALE
        kl = k_ref[pl.ds(start, NA_TK), cols].astype(BF16)
        vl = v_ref[pl.ds(start, NA_TK), cols].astype(BF16)
        kc = kc_ref[0, :, cols].astype(BF16)
        vc = vc_ref[0, :, cols].astype(BF16)
        outs = []
        for half in range(2):
            qm = jnp.where(low if half == 0 else ~low, q, 0.0).astype(BF16)
            s_loc = _dot_nt(qm, kl) + _na_bias(tab_ref, 2 * p + half, ids)
            s_ctx = _dot_nt(qm, kc)
            outs.append(_softmax_pv([s_loc, s_ctx], [vl, vc]))
        o_ref[:, cols] = jnp.where(low, outs[0], outs[1])


def na_bias_table(rpb):
    cq = np.arange(GRID_W)
    ck = np.arange(GRID_W)
    dc = np.clip(ck[None, :] - cq[:, None], -(NA_WIN_COLS - 1), NA_WIN_COLS - 1) + NA_WIN_COLS - 1
    col_start = np.clip(cq - NA_WIN_COLS // 2, 0, GRID_W - NA_WIN_COLS)
    in_win = (ck[None, :] >= col_start[:, None]) & (ck[None, :] < col_start[:, None] + NA_WIN_COLS)
    col_sel = (dc[:, :, None] == np.arange(2 * NA_WIN_COLS - 1)).astype(np.float32)
    tiles = jnp.einsum("lhrc,qkc->lhrqk", rpb.astype(F32), col_sel, precision=lax.Precision.HIGHEST)
    tiles = jnp.where(in_win, tiles, NEG)
    masked = jnp.full(tiles.shape[:2] + (1, GRID_W, GRID_W), NEG, F32)
    tiles = jnp.concatenate([tiles, masked], axis=2)
    zero = jnp.zeros_like(tiles)
    return jnp.stack([jnp.concatenate([tiles, zero], -1), jnp.concatenate([zero, tiles], -1)], axis=2)


def lat_na(z, cache_k, cache_v, tab, layer):
    per = GRID_H // NA_RB
    kv = lambda off: pl.BlockSpec((DEC_SEQ, NA_W), lambda b, i: (b, off // NA_W))
    cache = pl.BlockSpec((1, None, PAST_LEN, NA_W), lambda b, i: (b, layer, 0, 0))
    return pl.pallas_call(
        _lat_na_kernel,
        grid=(DEC_BATCH, per),
        in_specs=[pl.BlockSpec((NA_TQ, NA_W), lambda b, i: (b * per + i, 0)),
                  kv(OFF_NA_K), kv(OFF_NA_V), cache, cache,
                  pl.BlockSpec((None, NA_HEADS, 2, NA_TILES, GRID_W, LANE), lambda b, i: (layer, 0, 0, 0, 0, 0))],
        out_specs=pl.BlockSpec((NA_TQ, NA_W), lambda b, i: (b * per + i, 0)),
        out_shape=jax.ShapeDtypeStruct((N_LAT_TOK, NA_W), F32),
        compiler_params=_cparams("arbitrary", "arbitrary"),
        name="lat_na",
    )(z, z, z, cache_k, cache_v, tab)


def _log_sigmoid(x):
    return jnp.minimum(x, 0.0) - jnp.log1p(jnp.exp(-jnp.abs(x)))


def _split3(x):
    hi = x.astype(BF16)
    r = x - hi.astype(F32)
    mid = r.astype(BF16)
    lo = (r - mid.astype(F32)).astype(BF16)
    return hi, mid, lo


def _mlstm_gate_tables(gates, fwd):
    t = gates.shape[0]
    jj = lax.broadcasted_iota(jnp.int32, (t, t), 0)
    ss = lax.broadcasted_iota(jnp.int32, (t, t), 1)
    feeds = (jj <= ss) if fwd else (jj >= ss)
    feeds_b = jnp.where(feeds, 1.0, 0.0).astype(BF16)
    reached_b = jnp.where((ss <= jj) if fwd else (ss >= jj), 1.0, 0.0).astype(BF16)
    eye_b = jnp.where(jj == ss, 1.0, 0.0).astype(BF16)
    lf = _split3(_log_sigmoid(gates))
    gs = _split3(gates)
    b_col = _dot(reached_b, lf[0]) + _dot(reached_b, lf[1]) + _dot(reached_b, lf[2])
    b_row = _dot_tn(lf[0], feeds_b) + _dot_tn(lf[1], feeds_b) + _dot_tn(lf[2], feeds_b)
    ig_row = _dot_tn(gs[0], eye_b) + _dot_tn(gs[1], eye_b) + _dot_tn(gs[2], eye_b)
    return ig_row, b_row, b_col, feeds


def _mlstm_head(k, qt, vt, kt, u, b_row, ig_row, feeds, ct, n, m, half, fwd):
    t = k.shape[0]
    dmat = jnp.where(feeds, b_row + u, NEG)
    m_inter = b_row + m
    m_j = jnp.maximum(m_inter, jnp.max(dmat, axis=0, keepdims=True))
    w = jnp.exp(dmat - m_j)
    decay = jnp.exp(m_inter - m_j)
    first = lax.broadcasted_iota(jnp.int32, (LANE, t), 0) < HEAD_DIM
    mine = first if half == 0 else ~first
    qm = jnp.where(mine, qt, 0.0)
    qb = qm.astype(BF16)
    kb = k.astype(BF16)
    qk = _dot(kb, qb) * w
    num = decay * _dot(ct.astype(BF16), qb) + _dot(vt.astype(BF16), qk.astype(BF16))
    den = decay * jnp.sum(qm * n, axis=0, keepdims=True) + jnp.sum(qk, axis=0, keepdims=True)
    h = jnp.where(mine, num * (1.0 / jnp.maximum(jnp.abs(den), jnp.exp(-m_j))), 0.0)
    last = t - 1 if fwd else 0
    b_last = b_row[:, last:last + 1]
    g = b_last - b_row + ig_row
    m_new = jnp.maximum(b_last + m, jnp.max(g, axis=1, keepdims=True))
    ws = jnp.exp(g - m_new)
    d_last = jnp.exp(b_last + m - m_new)
    ct_add = _dot((vt * ws).astype(BF16), kb)
    n_add = jnp.sum(jnp.where(mine, kt * ws, 0.0), axis=1, keepdims=True)
    return h, d_last, ct_add, n_add, m_new


def _mlstm_kernel(zf_ref, zb_ref, *refs, has_init):
    if has_init:
        c0_ref, n0_ref, m0_ref, hf_ref, hb_ref, co_ref, no_ref, mo_ref, c_s, n_s, m_s = refs
    else:
        hf_ref, hb_ref, co_ref, no_ref, mo_ref, c_s, n_s, m_s = refs
    chunk = pl.program_id(1)
    pairs = ML_W // LANE
    zero = jnp.zeros((HEAD_DIM, HEAD_DIM), F32)

    ri = lax.broadcasted_iota(jnp.int32, (LANE, LANE), 0)
    ci = lax.broadcasted_iota(jnp.int32, (LANE, LANE), 1)
    eye = ri == ci
    first_block = (ri < HEAD_DIM) & (ci < HEAD_DIM)
    second_block = (ri >= HEAD_DIM) & (ci >= HEAD_DIM)
    first_rows = lax.broadcasted_iota(jnp.int32, (LANE, 1), 0) < HEAD_DIM

    @pl.when(chunk == 0)
    def _():
        if not has_init:
            c_s[...] = jnp.zeros_like(c_s)
            n_s[...] = jnp.zeros_like(n_s)
            m_s[...] = jnp.zeros_like(m_s)
            return
        for d in range(2):
            for p in range(pairs):
                top = jnp.concatenate([c0_ref[0, d, 2 * p], zero], axis=1)
                bot = jnp.concatenate([zero, c0_ref[0, d, 2 * p + 1]], axis=1)
                c_s[d * pairs + p] = jnp.concatenate([top, bot], axis=0).T
                e = d * ML_HEADS + 2 * p
                n_row = jnp.concatenate([n0_ref[0, e:e + 1, :], n0_ref[0, e + 1:e + 2, :]], axis=1)
                n_s[d * pairs + p] = jnp.sum(jnp.where(eye, n_row, 0.0), axis=1, keepdims=True)
            m_s[d] = m0_ref[0, d * ML_HEADS:(d + 1) * ML_HEADS, :]

    for d, (z_ref, h_ref) in enumerate(((zf_ref, hf_ref), (zb_ref, hb_ref))):
        gates = z_ref[:, OFF_ML_G - OFF_ML_Q:OFF_ML_G - OFF_ML_Q + LANE]
        ig_rows, b_rows, b_cols, feeds = _mlstm_gate_tables(gates, d == 0)
        u = gates - pltpu.roll(b_cols, LANE - ML_HEADS, axis=1)
        m_all = m_s[d]
        m_rows = []
        for p in range(pairs):
            k = z_ref[:, ML_W + p * LANE:ML_W + (p + 1) * LANE] * ATTN_SCALE
            qt = z_ref[:, p * LANE:(p + 1) * LANE].T
            vt = z_ref[:, 2 * ML_W + p * LANE:2 * ML_W + (p + 1) * LANE].T
            kt = k.T
            ct = c_s[d * pairs + p]
            n = n_s[d * pairs + p]
            res = []
            for half in range(2):
                hd = 2 * p + half
                gi = 2 * ML_HEADS * d + hd
                gf = gi + ML_HEADS
                res.append(_mlstm_head(k, qt, vt, kt, u[:, gi:gi + 1], b_rows[gf:gf + 1, :], ig_rows[gi:gi + 1, :],
                                       feeds, ct, n, m_all[hd:hd + 1, 0:1], half, d == 0))
            (h0, dl0, ca0, na0, mn0), (h1, dl1, ca1, na1, mn1) = res
            h_ref[:, p * LANE:(p + 1) * LANE] = (h0 + h1).T
            c_s[d * pairs + p] = jnp.where(first_block, dl0 * ct + ca0, jnp.where(second_block, dl1 * ct + ca1, 0.0))
            n_s[d * pairs + p] = jnp.where(first_rows, dl0 * n + na0, dl1 * n + na1)
            m_rows += [jnp.broadcast_to(mn0, (1, LANE)), jnp.broadcast_to(mn1, (1, LANE))]
        m_s[d] = jnp.concatenate(m_rows, axis=0)

    @pl.when(chunk == pl.num_programs(1) - 1)
    def _():
        for d in range(2):
            for p in range(pairs):
                c = c_s[d * pairs + p].T
                n_row = jnp.sum(jnp.where(eye, n_s[d * pairs + p], 0.0), axis=0, keepdims=True)
                co_ref[0, d, 2 * p] = c[:HEAD_DIM, :HEAD_DIM]
                co_ref[0, d, 2 * p + 1] = c[HEAD_DIM:, HEAD_DIM:]
                e = d * ML_HEADS + 2 * p
                no_ref[0, e:e + 1, :] = n_row[:, :HEAD_DIM]
                no_ref[0, e + 1:e + 2, :] = n_row[:, HEAD_DIM:]
            mo_ref[0, d * ML_HEADS:(d + 1) * ML_HEADS, :] = m_s[d]


def mlstm(z, batch, seq, init=None, layer=None):
    nc = seq // ML_CHUNK
    zcol = OFF_ML_Q // ML_BLOCK_W
    state_c = pl.BlockSpec((1, 2, ML_HEADS, HEAD_DIM, HEAD_DIM), lambda b, c: (b, 0, 0, 0, 0))
    state_n = pl.BlockSpec((1, 2 * ML_HEADS, HEAD_DIM), lambda b, c: (b, 0, 0))
    state_m = pl.BlockSpec((1, 2 * ML_HEADS, LANE), lambda b, c: (b, 0, 0))
    init_specs = [] if init is None else [
        pl.BlockSpec((1, None, 2, ML_HEADS, HEAD_DIM, HEAD_DIM), lambda b, c: (b, layer, 0, 0, 0, 0)),
        pl.BlockSpec((1, None, 2 * ML_HEADS, HEAD_DIM), lambda b, c: (b, layer, 0, 0)),
        pl.BlockSpec((1, None, 2 * ML_HEADS, LANE), lambda b, c: (b, layer, 0, 0)),
    ]
    pairs = ML_W // LANE
    return pl.pallas_call(
        functools.partial(_mlstm_kernel, has_init=init is not None),
        grid=(batch, nc),
        in_specs=[
            pl.BlockSpec((ML_CHUNK, ML_BLOCK_W), lambda b, c: (b * nc + c, zcol)),
            pl.BlockSpec((ML_CHUNK, ML_BLOCK_W), lambda b, c: (b * nc + nc - 1 - c, zcol)),
        ] + init_specs,
        out_specs=[
            pl.BlockSpec((ML_CHUNK, ML_W), lambda b, c: (b * nc + c, 0)),
            pl.BlockSpec((ML_CHUNK, ML_W), lambda b, c: (b * nc + nc - 1 - c, 0)),
            state_c, state_n, state_m,
        ],
        out_shape=[
            jax.ShapeDtypeStruct((batch * seq, ML_W), F32),
            jax.ShapeDtypeStruct((batch * seq, ML_W), F32),
            jax.ShapeDtypeStruct((batch, 2, ML_HEADS, HEAD_DIM, HEAD_DIM), F32),
            jax.ShapeDtypeStruct((batch, 2 * ML_HEADS, HEAD_DIM), F32),
            jax.ShapeDtypeStruct((batch, 2 * ML_HEADS, LANE), F32),
        ],
        scratch_shapes=[
            pltpu.VMEM((2 * pairs, LANE, LANE), F32),
            pltpu.VMEM((2 * pairs, LANE, 1), F32),
            pltpu.VMEM((2, ML_HEADS, LANE), F32),
        ],
        compiler_params=_cparams("arbitrary", "arbitrary"),
        name="mlstm",
    )(z, z, *(() if init is None else init))


def _layer_norm(u, g, b):
    mu = jnp.mean(u, axis=-1, keepdims=True)
    uc = u - mu
    var = jnp.mean(uc * uc, axis=-1, keepdims=True)
    return uc * lax.rsqrt(var + NORM_EPS) * g + b


def _outproj_kernel(x_ref, a_ref, hf_ref, hb_ref, og0_ref, og1_ref, c_ref, w_ref, mg_ref, g1_ref, sh2_ref, sc2_ref,
                    lng_ref, lnb_ref, rw_ref, x1_ref, h2_ref, lg_ref):
    low = _low_half(TM)
    mixed_b = []
    for p, og_ref in enumerate((og0_ref, og1_ref)):
        cols = slice(p * LANE, (p + 1) * LANE)
        h = hf_ref[:, cols] + hb_ref[:, cols]
        hc = h - _pair_mean(h, low)
        hn = hc * lax.rsqrt(_pair_mean(hc * hc, low) + NORM_EPS)
        mixed_b.append(hn * mg_ref[:, cols] * _sigmoid(og_ref[...]))
    mixed = jnp.concatenate([a_ref[...]] + mixed_b + [c_ref[...]], axis=-1).astype(BF16)
    y = _dot(mixed, w_ref[...])
    x1 = _layer_norm(DEEPNORM_ALPHA * x_ref[...] + g1_ref[0] * y, lng_ref[...], lnb_ref[...])
    h2 = (x1 * (1.0 + sc2_ref[0]) + sh2_ref[0]).astype(BF16)
    x1_ref[...] = x1
    h2_ref[...] = h2
    lg_ref[...] = _dot_nt(rw_ref[...], h2)


def out_projection(x, out_a, hf, hb, z, out_c, layer, w_out, ml_gain, mods, mod_row, ln_g, ln_b, router_t):
    n = x.shape[0]
    mod = lambda k: pl.BlockSpec((None, 1, 1, D_MODEL), lambda i: (layer, mod_row(i), 0, k))
    row = lambda w: pl.BlockSpec((TM, w), lambda i: (i, 0))
    zcol = lambda off: pl.BlockSpec((TM, LANE), lambda i: (i, off // LANE))
    per_layer = lambda r, c: pl.BlockSpec((None, r, c), lambda i: (layer, 0, 0))
    norm0 = pl.BlockSpec((None, None, 1, D_MODEL), lambda i: (layer, 0, 0, 0))
    return pl.pallas_call(
        _outproj_kernel,
        grid=(n // TM,),
        in_specs=[row(D_MODEL), row(NA_W), row(ML_W), row(ML_W), zcol(OFF_ML_O), zcol(OFF_ML_O + LANE), row(GQA_W),
                  per_layer(D_MODEL, D_MODEL), per_layer(1, ML_W), mod(2), mod(3), mod(4), norm0, norm0,
                  per_layer(N_EXPERTS, D_MODEL)],
        out_specs=[row(D_MODEL), row(D_MODEL), pl.BlockSpec((N_EXPERTS, TM), lambda i: (0, i))],
        out_shape=[
            jax.ShapeDtypeStruct((n, D_MODEL), F32),
            jax.ShapeDtypeStruct((n, D_MODEL), BF16),
            jax.ShapeDtypeStruct((N_EXPERTS, n), F32),
        ],
        compiler_params=_cparams("arbitrary"),
        name="out_projection",
    )(x, out_a, hf, hb, z, z, out_c, w_out, ml_gain, mods, mods, mods, ln_g, ln_b, router_t)


ROUTE_BLK = 256


def _prefix_count(x, tri):
    n = x.shape[1]
    outs = []
    carry = jnp.zeros((N_EXPERTS, 1), F32)
    for i in range(n // ROUTE_BLK):
        blk = x[:, i * ROUTE_BLK:(i + 1) * ROUTE_BLK]
        outs.append(_dot(blk, tri) + carry)
        carry = carry + jnp.sum(blk.astype(F32), axis=1, keepdims=True)
    return jnp.concatenate(outs, axis=1) if len(outs) > 1 else outs[0]


def _route_kernel(lg_ref, coder_ref, codet_ref, afft_ref, *, n, cap):
    sets = lg_ref.shape[1] // n
    lg = lg_ref[...]
    ex = jnp.exp(lg - jnp.max(lg, axis=0, keepdims=True))
    aff_all = ex / jnp.sum(ex, axis=0, keepdims=True)
    affs = [aff_all[:, s * n:(s + 1) * n] for s in range(sets)]

    def search(i, bits):
        bit = lax.shift_left(jnp.int32(1), 30 - i)
        out = []
        for aff, b in zip(affs, bits):
            cand = b | bit
            cnt = jnp.sum(jnp.where(aff >= pltpu.bitcast(cand, F32), 1.0, 0.0), axis=1, keepdims=True)
            out.append(jnp.where(cnt >= cap, cand, b))
        return tuple(out)

    floors = lax.fori_loop(0, 31, search, tuple(jnp.zeros((N_EXPERTS, 1), jnp.int32) for _ in range(sets)))
    ti = lax.broadcasted_iota(jnp.int32, (ROUTE_BLK, ROUTE_BLK), 0)
    tj = lax.broadcasted_iota(jnp.int32, (ROUTE_BLK, ROUTE_BLK), 1)
    tri = jnp.where(ti < tj, 1.0, 0.0).astype(BF16)
    pad = jnp.zeros((LANE - N_EXPERTS, n), F32)
    for s, (aff, floor_bits) in enumerate(zip(affs, floors)):
        thr = jnp.min(jnp.where(aff >= pltpu.bitcast(floor_bits, F32), aff, 2.0), axis=1, keepdims=True)
        gt = aff > thr
        eq = aff == thr
        need = cap - jnp.sum(jnp.where(gt, 1.0, 0.0), axis=1, keepdims=True)
        eq_rank = _prefix_count(jnp.where(eq, 1.0, 0.0).astype(BF16), tri)
        sel = jnp.where(gt, 1.0, jnp.where(eq & (eq_rank < need), 1.0, 0.0))
        pos = _prefix_count(sel.astype(BF16), tri)
        code = jnp.where(sel > 0.5, pos, -1.0)
        coder_ref[s] = code
        codet_ref[s] = jnp.concatenate([code, pad], axis=0).T
        afft_ref[s] = jnp.concatenate([aff, pad], axis=0).T


def route(logits_t, n, cap):
    tokens = logits_t.shape[1]
    sets = tokens // n
    whole = lambda shape: pl.BlockSpec(shape, lambda i: (0,) * len(shape))
    return pl.pallas_call(
        functools.partial(_route_kernel, n=n, cap=cap),
        grid=(1,),
        in_specs=[whole((N_EXPERTS, tokens))],
        out_specs=[whole((sets, N_EXPERTS, n)), whole((sets, n, LANE)), whole((sets, n, LANE))],
        out_shape=[
            jax.ShapeDtypeStruct((sets, N_EXPERTS, n), F32),
            jax.ShapeDtypeStruct((sets, n, LANE), F32),
            jax.ShapeDtypeStruct((sets, n, LANE), F32),
        ],
        compiler_params=_cparams("arbitrary"),
        name="route",
    )(logits_t)


def _gather_kernel(coder_ref, h_ref, xe_ref, *, cap, epb):
    e0 = pl.program_id(1) * epb
    n = coder_ref.shape[2]
    ci = lax.broadcasted_iota(jnp.int32, (cap, n), 0).astype(F32)
    onehot = [jnp.where(ci == coder_ref[0, pl.ds(e0 + k, 1), :], 1.0, 0.0).astype(BF16) for k in range(epb)]
    onehot = jnp.concatenate(onehot, axis=0) if epb > 1 else onehot[0]
    rows = _dot(onehot, h_ref[...]).astype(BF16)
    for k in range(epb):
        xe_ref[k] = rows[k * cap:(k + 1) * cap]


def gather_tokens(code_rows, h2, n, cap):
    sets = code_rows.shape[0]
    epb = max(1, 512 // cap)
    return pl.pallas_call(
        functools.partial(_gather_kernel, cap=cap, epb=epb),
        grid=(sets, N_EXPERTS // epb),
        in_specs=[
            pl.BlockSpec((1, N_EXPERTS, n), lambda s, e: (s, 0, 0)),
            pl.BlockSpec((n, D_MODEL), lambda s, e: (s, 0)),
        ],
        out_specs=pl.BlockSpec((epb, cap, D_MODEL), lambda s, e: (e, s, 0)),
        out_shape=jax.ShapeDtypeStruct((N_EXPERTS, sets * cap, D_MODEL), BF16),
        compiler_params=_cparams("arbitrary", "arbitrary"),
        name="gather_tokens",
    )(code_rows, h2)


def _expert_kernel(xi_ref, xl_ref, wg_ref, wu_ref, wd_ref, yi_ref, yl_ref, acci_ref, accl_ref):
    j = pl.program_id(1)

    @pl.when(j == 0)
    def _():
        acci_ref[...] = jnp.zeros_like(acci_ref)
        accl_ref[...] = jnp.zeros_like(accl_ref)

    xs = (xi_ref[0], xl_ref[0])

    def run(sub_tiles):
        sums = [None, None]
        for s in sub_tiles:
            cols = slice(s * EXP_SUB, (s + 1) * EXP_SUB)
            wg = wg_ref[0, 0, :, cols].astype(BF16)
            wu = wu_ref[0, 0, :, cols].astype(BF16)
            wd = wd_ref[0, 0, cols, :].astype(BF16)
            for g in range(2):
                a = _dot(xs[g], wg)
                b = _dot(xs[g], wu)
                hid = (a * _sigmoid(a) * b).astype(BF16)
                part = _dot(hid, wd)
                sums[g] = part if sums[g] is None else sums[g] + part
        acci_ref[...] += sums[0]
        accl_ref[...] += sums[1]

    in_last = EXP_LAST_SUBS
    run(range(in_last))
    if in_last < EXP_TF // EXP_SUB:
        @pl.when(j < pl.num_programs(1) - 1)
        def _():
            run(range(in_last, EXP_TF // EXP_SUB))

    @pl.when(j == pl.num_programs(1) - 1)
    def _():
        yi_ref[0] = acci_ref[...].astype(BF16)
        yl_ref[0] = accl_ref[...].astype(BF16)


def experts(xe_ctx, xe_lat, w_gate, w_up, w_down, layer):
    xin = lambda s: pl.BlockSpec((1, s, D_MODEL), lambda e, j: (e, 0, 0))
    return pl.pallas_call(
        _expert_kernel,
        grid=(N_EXPERTS, EXP_STEPS),
        in_specs=[
            xin(SLOTS_CTX), xin(SLOTS_LAT),
            pl.BlockSpec((1, 1, D_MODEL, EXP_TF), lambda e, j: (layer, e, 0, j)),
            pl.BlockSpec((1, 1, D_MODEL, EXP_TF), lambda e, j: (layer, e, 0, j)),
            pl.BlockSpec((1, 1, EXP_TF, D_MODEL), lambda e, j: (layer, e, j, 0)),
        ],
        out_specs=[xin(SLOTS_CTX), xin(SLOTS_LAT)],
        out_shape=[
            jax.ShapeDtypeStruct((N_EXPERTS, SLOTS_CTX, D_MODEL), BF16),
            jax.ShapeDtypeStruct((N_EXPERTS, SLOTS_LAT, D_MODEL), BF16),
        ],
        scratch_shapes=[pltpu.VMEM((SLOTS_CTX, D_MODEL), F32), pltpu.VMEM((SLOTS_LAT, D_MODEL), F32)],
        compiler_params=_cparams("arbitrary", "arbitrary"),
        name="experts",
    )(xe_ctx, xe_lat, w_gate, w_up, w_down)


COMB_TT = 256


def _scatter_per_expert(code, aff, ye_ref, cap):
    tt = code.shape[0]
    li = lax.broadcasted_iota(jnp.int32, (tt, cap), 1).astype(F32)
    acc = jnp.zeros((tt, D_MODEL), F32)
    for e in range(N_EXPERTS):
        onehot = jnp.where(li == code[:, e:e + 1], 1.0, 0.0).astype(BF16)
        acc = acc + aff[:, e:e + 1] * _dot(onehot, ye_ref[e])
    return acc


def _scatter_merged(code, aff, ye_ref, cap):
    tt = code.shape[0]
    slots = N_EXPERTS * cap
    shift = cap.bit_length() - 1
    ei = lax.broadcasted_iota(jnp.int32, (LANE, slots), 0)
    si = lax.broadcasted_iota(jnp.int32, (LANE, slots), 1)
    expand = jnp.where(lax.shift_right_logical(si, shift) == ei, 1.0, 0.0).astype(BF16)
    slot = (lax.broadcasted_iota(jnp.int32, (tt, slots), 1) & (cap - 1)).astype(F32)
    hit = _dot(code.astype(BF16), expand) == slot
    a_hi = aff.astype(BF16)
    a_lo = (aff - a_hi.astype(F32)).astype(BF16)
    ye = ye_ref[...].reshape(slots, D_MODEL)
    acc = _dot(jnp.where(hit, _dot(a_hi, expand), 0.0).astype(BF16), ye)
    return acc + _dot(jnp.where(hit, _dot(a_lo, expand), 0.0).astype(BF16), ye)


def _combine_kernel(code_ref, aff_ref, ye_ref, x1_ref, g2_ref, lng_ref, lnb_ref, o_ref, *, cap):
    scatter = _scatter_merged if cap < LANE else _scatter_per_expert
    acc = scatter(code_ref[0], aff_ref[0], ye_ref, cap)
    u = DEEPNORM_ALPHA * x1_ref[...] + g2_ref[0] * acc
    o_ref[...] = _layer_norm(u, lng_ref[...], lnb_ref[...])


def combine(code, aff, ye, x1, layer, mods, set_mod_row, ln_g, ln_b, n, cap):
    sets = code.shape[0]
    per = n // COMB_TT
    tok = pl.BlockSpec((1, COMB_TT, LANE), lambda s, i: (s, i, 0))
    norm1 = pl.BlockSpec((None, None, 1, D_MODEL), lambda s, i: (layer, 1, 0, 0))
    return pl.pallas_call(
        functools.partial(_combine_kernel, cap=cap),
        grid=(sets, per),
        in_specs=[
            tok, tok,
            pl.BlockSpec((N_EXPERTS, cap, D_MODEL), lambda s, i: (0, s, 0)),
            pl.BlockSpec((COMB_TT, D_MODEL), lambda s, i: (s * per + i, 0)),
            pl.BlockSpec((None, 1, 1, D_MODEL), lambda s, i: (layer, set_mod_row(s), 0, 5)),
            norm1, norm1,
        ],
        out_specs=pl.BlockSpec((COMB_TT, D_MODEL), lambda s, i: (s * per + i, 0)),
        out_shape=jax.ShapeDtypeStruct((sets * n, D_MODEL), F32),
        compiler_params=_cparams("arbitrary", "arbitrary"),
        name="combine",
    )(code, aff, ye, x1, mods, ln_g, ln_b)


def _axial_rope_tables():
    t = jnp.arange(DEC_SEQ)
    row = (t // GRID_W).astype(F32)
    col = (t % GRID_W).astype(F32)
    half = HEAD_DIM // 2
    inv = ROPE_THETA ** (-jnp.arange(0, half, 2, dtype=F32) / half)
    ang_r = row[:, None] * inv
    ang_c = col[:, None] * inv
    ang = jnp.concatenate([ang_r, ang_r, ang_c, ang_c] * 2, -1)
    return jnp.cos(ang), jnp.sin(ang)


def _ctx_mod_row(i):
    return 0


def _lat_tile_mod_row(i):
    return 1 + i // (DEC_SEQ // TM)


def _lat_set_mod_row(s):
    return 1 + s


def kernel(x_prompt, x_sample, c, cache_na_k, cache_na_v, cache_gqa_k, cache_gqa_v, state_mlstm_c, state_mlstm_n,
           state_mlstm_m, c_ctx, ada_w, ada_b, w_in, b_gate, w_out, na_rpb, qk_norm_g, ml_norm_g, ln_g, ln_b,
           router_w, w_gate, w_up, w_down):
    cond8 = jnp.concatenate([c_ctx[None, :], c, jnp.zeros((8 - 1 - DEC_BATCH, D_MODEL), F32)], 0)
    mods = adaln(cond8, ada_w, ada_b).reshape(DEPTH, 8, 1, 6 * D_MODEL)
    xc = x_prompt.reshape(N_CTX_TOK, D_MODEL)
    xl = x_sample.reshape(N_LAT_TOK, D_MODEL)
    cos, sin = _axial_rope_tables()
    na_tab = na_bias_table(na_rpb)

    split = OFF_ML_G + N_GATES
    w_pad = jnp.concatenate([w_in[:, :, :split], jnp.zeros((DEPTH, D_MODEL, LANE - N_GATES), F32),
                             w_in[:, :, split:]], 2).astype(BF16)
    bias_row = jnp.pad(b_gate, ((0, 0), (OFF_ML_G, PROJ_PAD - OFF_ML_G - N_GATES))).reshape(DEPTH, 1, PROJ_PAD)
    gains = jnp.tile(qk_norm_g, (1, 1, 2)).reshape(DEPTH, 2, 1, LANE)
    w_out_b = w_out.astype(BF16)
    router_t = jnp.swapaxes(router_w, 1, 2).astype(BF16)
    ml_gain = ml_norm_g.reshape(DEPTH, 1, ML_W)
    ln_g4 = ln_g.reshape(DEPTH, 2, 1, D_MODEL)
    ln_b4 = ln_b.reshape(DEPTH, 2, 1, D_MODEL)
    na_ck = cache_na_k.reshape(DEC_BATCH, DEPTH, PAST_LEN, NA_W)
    na_cv = cache_na_v.reshape(DEC_BATCH, DEPTH, PAST_LEN, NA_W)
    gq_ck = cache_gqa_k.reshape(DEC_BATCH, DEPTH, PAST_LEN, GQA_KV_W)
    gq_cv = cache_gqa_v.reshape(DEC_BATCH, DEPTH, PAST_LEN, GQA_KV_W)
    lat_n0 = state_mlstm_n.reshape(DEC_BATCH, DEPTH, 2 * ML_HEADS, HEAD_DIM)
    lat_m0 = jnp.broadcast_to(state_mlstm_m.reshape(DEC_BATCH, DEPTH, 2 * ML_HEADS, 1),
                              (DEC_BATCH, DEPTH, 2 * ML_HEADS, LANE))
    zcs, states = [], []

    for l in range(DEPTH):
        zc = in_projection(xc, l, mods, _ctx_mod_row, w_pad, bias_row, gains)
        zl = in_projection(xl, l, mods, _lat_tile_mod_row, w_pad, bias_row, gains, cos, sin)

        out_a_c, out_c_c = ctx_attention(zc)
        hf_c, hb_c, sc, sn, sm = mlstm(zc, BATCH, SEQ)
        out_a_l = lat_na(zl, na_ck, na_cv, na_tab, l)
        out_c_l = lat_gqa(zl, gq_ck, gq_cv, l)
        hf_l, hb_l = mlstm(zl, DEC_BATCH, DEC_SEQ, (state_mlstm_c, lat_n0, lat_m0), l)[:2]
        zcs.append(zc)
        states.append((sc, sn, sm))

        x1_c, h2_c, lg_c = out_projection(xc, out_a_c, hf_c, hb_c, zc, out_c_c, l, w_out_b, ml_gain, mods,
                                          _ctx_mod_row, ln_g4, ln_b4, router_t)
        x1_l, h2_l, lg_l = out_projection(xl, out_a_l, hf_l, hb_l, zl, out_c_l, l, w_out_b, ml_gain, mods,
                                          _lat_tile_mod_row, ln_g4, ln_b4, router_t)
        crow_c, code_c, aff_c = route(lg_c, SEQ, CAP_CTX)
        crow_l, code_l, aff_l = route(lg_l, DEC_SEQ, CAP_LAT)
        xe_c = gather_tokens(crow_c, h2_c, SEQ, CAP_CTX)
        xe_l = gather_tokens(crow_l, h2_l, DEC_SEQ, CAP_LAT)
        ye_c, ye_l = experts(xe_c, xe_l, w_gate, w_up, w_down, l)
        xc = combine(code_c, aff_c, ye_c, x1_c, l, mods, _ctx_mod_row, ln_g4, ln_b4, SEQ, CAP_CTX)
        xl = combine(code_l, aff_l, ye_l, x1_l, l, mods, _lat_set_mod_row, ln_g4, ln_b4, DEC_SEQ, CAP_LAT)

    y_prompt = xc.reshape(BATCH, SEQ, D_MODEL)
    y_sample = xl.reshape(DEC_BATCH, DEC_SEQ, D_MODEL)
    cols = lambda off, heads: jnp.concatenate(
        [zc[:, off:off + heads * HEAD_DIM].reshape(BATCH, 1, SEQ, heads, HEAD_DIM) for zc in zcs], 1)
    new_c = jnp.stack([s[0] for s in states], 1)
    new_n = jnp.stack([s[1] for s in states], 1).reshape(BATCH, DEPTH, 2, ML_HEADS, HEAD_DIM)
    new_m = jnp.stack([s[2][:, :, 0] for s in states], 1).reshape(BATCH, DEPTH, 2, ML_HEADS)
    return (y_prompt, y_sample, cols(OFF_NA_K, NA_HEADS), cols(OFF_NA_V, NA_HEADS), cols(OFF_GQ_K, GQA_KV_HEADS),
            cols(OFF_GQ_V, GQA_KV_HEADS), new_c, new_n, new_m)
```

```python
import functools

import jax
import jax.numpy as jnp
import numpy as np
from jax import lax
from jax.experimental import pallas as pl
from jax.experimental.pallas import tpu as pltpu

D_MODEL = 1024
BATCH = 16
SEQ = 256
DEPTH = 2
DEC_BATCH = 2
DEC_SEQ = 2048
PAST_LEN = 256
GRID_W = 64
GRID_H = DEC_SEQ // GRID_W
HEAD_DIM = 64
NA_HEADS = 6
NA_WIN_ROWS = 8
NA_WIN_COLS = 16
ML_HEADS = 4
GQA_HEADS = 6
GQA_KV_HEADS = 2
ROPE_THETA = 10000.0
N_EXPERTS = 16
EC_CAPACITY = 2
EXPERT_HIDDEN = 2816
NORM_EPS = 1e-6
NA_W = NA_HEADS * HEAD_DIM
ML_W = ML_HEADS * HEAD_DIM
GQA_W = GQA_HEADS * HEAD_DIM
GQA_KV_W = GQA_KV_HEADS * HEAD_DIM
N_GATES = 4 * ML_HEADS
DEEPNORM_ALPHA = (2 * DEPTH) ** 0.25
ATTN_SCALE = HEAD_DIM ** -0.5
F32 = jnp.float32
BF16 = jnp.bfloat16

LANE = 128
N_CTX_TOK = BATCH * SEQ
N_LAT_TOK = DEC_BATCH * DEC_SEQ
NEG = -1e30

OFF_NA_Q = 0
OFF_NA_K = OFF_NA_Q + NA_W
OFF_NA_V = OFF_NA_K + NA_W
OFF_ML_Q = OFF_NA_V + NA_W
OFF_ML_K = OFF_ML_Q + ML_W
OFF_ML_V = OFF_ML_K + ML_W
OFF_ML_O = OFF_ML_V + ML_W
OFF_ML_G = OFF_ML_O + ML_W
OFF_GQ_Q = OFF_ML_G + LANE
OFF_GQ_K = OFF_GQ_Q + GQA_W
OFF_GQ_V = OFF_GQ_K + GQA_KV_W
PROJ_PAD = OFF_GQ_V + GQA_KV_W
ML_BLOCK_W = OFF_GQ_Q - OFF_ML_Q

TM = 512
ML_CHUNK = 256
EXP_SUB = 256
EXP_TF = EXPERT_HIDDEN // 2
CAP_CTX = EC_CAPACITY * SEQ // N_EXPERTS
CAP_LAT = EC_CAPACITY * DEC_SEQ // N_EXPERTS
SLOTS_CTX = BATCH * CAP_CTX
SLOTS_LAT = DEC_BATCH * CAP_LAT
VMEM_LIMIT = 56 * 1024 * 1024


def _cparams(*sem):
    return pltpu.CompilerParams(dimension_semantics=sem, vmem_limit_bytes=VMEM_LIMIT)


def _sigmoid(x):
    return 1.0 / (1.0 + jnp.exp(-x))


def _dot(a, b):
    return jnp.dot(a, b, preferred_element_type=F32)


def _dot_nt(a, b):
    return lax.dot_general(a, b, (((1,), (1,)), ((), ())), preferred_element_type=F32)


def _dot_tn(a, b):
    return lax.dot_general(a, b, (((0,), (0,)), ((), ())), preferred_element_type=F32)


def _low_half(rows):
    return lax.broadcasted_iota(jnp.int32, (rows, LANE), 1) < HEAD_DIM


def _pair_mean(x, low):
    s_lo = jnp.sum(jnp.where(low, x, 0.0), axis=-1, keepdims=True)
    s_hi = jnp.sum(jnp.where(low, 0.0, x), axis=-1, keepdims=True)
    return jnp.where(low, s_lo, s_hi) * (1.0 / HEAD_DIM)


def _adaln_kernel(c_ref, w_ref, b_ref, o_ref):
    c = c_ref[...]
    s = c * _sigmoid(c)
    o_ref[0] = _dot(s.astype(BF16), w_ref[0].astype(BF16)) + b_ref[0]


def adaln(cond8, ada_w, ada_b):
    tn = 1536
    return pl.pallas_call(
        _adaln_kernel,
        grid=(DEPTH, 6 * D_MODEL // tn),
        in_specs=[
            pl.BlockSpec((8, D_MODEL), lambda l, j: (0, 0)),
            pl.BlockSpec((1, D_MODEL, tn), lambda l, j: (l, 0, j)),
            pl.BlockSpec((1, 1, tn), lambda l, j: (l, 0, j)),
        ],
        out_specs=pl.BlockSpec((1, 8, tn), lambda l, j: (l, 0, j)),
        out_shape=jax.ShapeDtypeStruct((DEPTH, 8, 6 * D_MODEL), F32),
        compiler_params=_cparams("arbitrary", "arbitrary"),
        name="adaln",
    )(cond8, ada_w, ada_b.reshape(DEPTH, 1, 6 * D_MODEL))


def _inproj_kernel(x_ref, sh_ref, sc_ref, w_ref, b_ref, gq_ref, gk_ref, *rest, rope):
    if rope:
        cos_ref, sin_ref, z_ref = rest
    else:
        (z_ref,) = rest
    h = x_ref[...] * (1.0 + sc_ref[0]) + sh_ref[0]
    z = _dot(h.astype(BF16), w_ref[...]) + b_ref[...]
    z_ref[...] = z
    rows = z.shape[0]
    low = _low_half(rows)
    if rope:
        lane = lax.broadcasted_iota(jnp.int32, (rows, LANE), 1)
        first_quarter = (lane & (HEAD_DIM // 4)) == 0

    def norm_pair(x, gain):
        y = x * lax.rsqrt(_pair_mean(x * x, low) + NORM_EPS) * gain
        if rope:
            rot = jnp.where(first_quarter, -pltpu.roll(y, LANE - HEAD_DIM // 4, axis=1),
                            pltpu.roll(y, HEAD_DIM // 4, axis=1))
            y = y * cos_ref[...] + rot * sin_ref[...]
        return y

    for p in range(GQA_W // LANE):
        c0 = OFF_GQ_Q + p * LANE
        z_ref[:, c0:c0 + LANE] = norm_pair(z[:, c0:c0 + LANE], gq_ref[...])
    z_ref[:, OFF_GQ_K:OFF_GQ_K + LANE] = norm_pair(z[:, OFF_GQ_K:OFF_GQ_K + LANE], gk_ref[...])


def in_projection(x, layer, mods, mod_row, w_pad, bias_row, gains, cos=None, sin=None):
    n = x.shape[0]
    rope = cos is not None
    in_specs = [
        pl.BlockSpec((TM, D_MODEL), lambda i: (i, 0)),
        pl.BlockSpec((None, 1, 1, D_MODEL), lambda i: (layer, mod_row(i), 0, 0)),
        pl.BlockSpec((None, 1, 1, D_MODEL), lambda i: (layer, mod_row(i), 0, 1)),
        pl.BlockSpec((None, D_MODEL, PROJ_PAD), lambda i: (layer, 0, 0)),
        pl.BlockSpec((None, 1, PROJ_PAD), lambda i: (layer, 0, 0)),
        pl.BlockSpec((None, None, 1, LANE), lambda i: (layer, 0, 0, 0)),
        pl.BlockSpec((None, None, 1, LANE), lambda i: (layer, 1, 0, 0)),
    ]
    args = [x, mods, mods, w_pad, bias_row, gains, gains]
    if rope:
        per = cos.shape[0] // TM
        in_specs += [pl.BlockSpec((TM, LANE), lambda i: (i % per, 0))] * 2
        args += [cos, sin]
    return pl.pallas_call(
        functools.partial(_inproj_kernel, rope=rope),
        grid=(n // TM,),
        in_specs=in_specs,
        out_specs=pl.BlockSpec((TM, PROJ_PAD), lambda i: (i, 0)),
        out_shape=jax.ShapeDtypeStruct((n, PROJ_PAD), F32),
        compiler_params=_cparams("arbitrary"),
        name="in_projection_rope" if rope else "in_projection",
    )(*args)


def _softmax_pv(scores, values):
    m = jnp.max(scores[0], axis=-1, keepdims=True)
    for s in scores[1:]:
        m = jnp.maximum(m, jnp.max(s, axis=-1, keepdims=True))
    l = 0.0
    o = 0.0
    for s, v in zip(scores, values):
        p = jnp.exp(s - m)
        l = l + jnp.sum(p, axis=-1, keepdims=True)
        o = o + _dot(p.astype(BF16), v)
    return o * (1.0 / l)


def _ctx_attn_kernel(naq_ref, nak_ref, nav_ref, gq_ref, gk_ref, gv_ref, oa_ref, oc_ref):
    low = _low_half(SEQ)
    for p in range(NA_W // LANE):
        cols = slice(p * LANE, (p + 1) * LANE)
        q = naq_ref[:, cols] * ATTN_SCALE
        k = nak_ref[:, cols].astype(BF16)
        v = nav_ref[:, cols].astype(BF16)
        outs = []
        for half in range(2):
            qm = jnp.where(low if half == 0 else ~low, q, 0.0).astype(BF16)
            outs.append(_softmax_pv([_dot_nt(qm, k)], [v]))
        oa_ref[:, cols] = jnp.where(low, outs[0], outs[1])
    k = gk_ref[...]
    v = gv_ref[...]
    k_at = (pltpu.roll(k, HEAD_DIM, axis=1).astype(BF16), k.astype(BF16))
    v_at = (pltpu.roll(v, HEAD_DIM, axis=1).astype(BF16), v.astype(BF16))
    group = GQA_HEADS // GQA_KV_HEADS
    for p in range(GQA_W // LANE):
        cols = slice(p * LANE, (p + 1) * LANE)
        q = gq_ref[:, cols] * ATTN_SCALE
        outs = []
        for half in range(2):
            g = (2 * p + half) // group
            qm = jnp.where(low if half == 0 else ~low, q, 0.0).astype(BF16)
            outs.append(_softmax_pv([_dot_nt(qm, k_at[g == half])], [v_at[g == half]]))
        oc_ref[:, cols] = jnp.where(low, outs[0], outs[1])


def ctx_attention(z):
    col = lambda w, off: pl.BlockSpec((SEQ, w), lambda b: (b, off // w))
    return pl.pallas_call(
        _ctx_attn_kernel,
        grid=(BATCH,),
        in_specs=[col(NA_W, OFF_NA_Q), col(NA_W, OFF_NA_K), col(NA_W, OFF_NA_V),
                  col(GQA_W, OFF_GQ_Q), col(LANE, OFF_GQ_K), col(LANE, OFF_GQ_V)],
        out_specs=[col(NA_W, 0), col(GQA_W, 0)],
        out_shape=[jax.ShapeDtypeStruct((N_CTX_TOK, NA_W), F32), jax.ShapeDtypeStruct((N_CTX_TOK, GQA_W), F32)],
        compiler_params=_cparams("arbitrary"),
        name="ctx_attention",
    )(z, z, z, z, z, z)


GQA_TQ = 512


def _lat_gqa_kernel(q_ref, kl_ref, vl_ref, kc_ref, vc_ref, o_ref):
    low = _low_half(GQA_TQ)
    at = lambda x: (pltpu.roll(x, HEAD_DIM, axis=1).astype(BF16), x.astype(BF16))
    kl_at, vl_at, kc_at, vc_at = at(kl_ref[...]), at(vl_ref[...]), at(kc_ref[0]), at(vc_ref[0])
    group = GQA_HEADS // GQA_KV_HEADS
    for p in range(GQA_W // LANE):
        cols = slice(p * LANE, (p + 1) * LANE)
        q = q_ref[:, cols] * ATTN_SCALE
        outs = []
        for half in range(2):
            same = ((2 * p + half) // group) == half
            qm = jnp.where(low if half == 0 else ~low, q, 0.0).astype(BF16)
            outs.append(_softmax_pv([_dot_nt(qm, kc_at[same]), _dot_nt(qm, kl_at[same])],
                                    [vc_at[same], vl_at[same]]))
        o_ref[:, cols] = jnp.where(low, outs[0], outs[1])


def lat_gqa(z, cache_k, cache_v, layer):
    per = DEC_SEQ // GQA_TQ
    kv = lambda off: pl.BlockSpec((DEC_SEQ, LANE), lambda b, i: (b, off // LANE))
    cache = pl.BlockSpec((1, None, PAST_LEN, LANE), lambda b, i: (b, layer, 0, 0))
    return pl.pallas_call(
        _lat_gqa_kernel,
        grid=(DEC_BATCH, per),
        in_specs=[pl.BlockSpec((GQA_TQ, GQA_W), lambda b, i: (b * per + i, OFF_GQ_Q // GQA_W)),
                  kv(OFF_GQ_K), kv(OFF_GQ_V), cache, cache],
        out_specs=pl.BlockSpec((GQA_TQ, GQA_W), lambda b, i: (b * per + i, 0)),
        out_shape=jax.ShapeDtypeStruct((N_LAT_TOK, GQA_W), F32),
        compiler_params=_cparams("arbitrary", "arbitrary"),
        name="lat_gqa",
    )(z, z, z, cache_k, cache_v)


NA_RB = 4
NA_UNION = NA_RB + NA_WIN_ROWS
NA_TQ = NA_RB * GRID_W
NA_TK = NA_UNION * GRID_W
NA_TILES = 2 * NA_WIN_ROWS


def _na_union_start(i):
    return jnp.clip(i * NA_RB - NA_WIN_ROWS // 2, 0, GRID_H - NA_UNION)


def _na_tile_ids(i):
    u0 = _na_union_start(i)
    ids = []
    for t in range(NA_RB):
        r = i * NA_RB + t
        rs = jnp.clip(r - NA_WIN_ROWS // 2, 0, GRID_H - NA_WIN_ROWS)
        row_ids = []
        for j in range(NA_UNION):
            kr = u0 + j
            inside = (kr >= rs) & (kr < rs + NA_WIN_ROWS)
            row_ids.append(jnp.where(inside, kr - r + NA_WIN_ROWS - 1, NA_TILES - 1))
        ids.append(row_ids)
    return ids


def _na_bias(tab_ref, head, ids):
    rows = [jnp.concatenate([tab_ref[head, 0, row_ids[j]] + tab_ref[head, 1, row_ids[j + 1]]
                             for j in range(0, NA_UNION, 2)], axis=1) for row_ids in ids]
    return jnp.concatenate(rows, axis=0)


def _lat_na_kernel(q_ref, k_ref, v_ref, kc_ref, vc_ref, tab_ref, o_ref):
    i = pl.program_id(1)
    start = pl.multiple_of(_na_union_start(i) * GRID_W, GRID_W)
    low = _low_half(NA_TQ)
    ids = _na_tile_ids(i)
    for p in range(NA_W // LANE):
        cols = slice(p * LANE, (p + 1) * LANE)
        q = q_ref[:, cols] * ATTN_SCALE
        kl = k_ref[pl.ds(start, NA_TK), cols].astype(BF16)
        vl = v_ref[pl.ds(start, NA_TK), cols].astype(BF16)
        kc = kc_ref[0, :, cols].astype(BF16)
        vc = vc_ref[0, :, cols].astype(BF16)
        outs = []
        for half in range(2):
            qm = jnp.where(low if half == 0 else ~low, q, 0.0).astype(BF16)
            s_loc = _dot_nt(qm, kl) + _na_bias(tab_ref, 2 * p + half, ids)
            s_ctx = _dot_nt(qm, kc)
            outs.append(_softmax_pv([s_loc, s_ctx], [vl, vc]))
        o_ref[:, cols] = jnp.where(low, outs[0], outs[1])


def na_bias_table(rpb):
    cq = np.arange(GRID_W)
    ck = np.arange(GRID_W)
    dc = np.clip(ck[None, :] - cq[:, None], -(NA_WIN_COLS - 1), NA_WIN_COLS - 1) + NA_WIN_COLS - 1
    col_start = np.clip(cq - NA_WIN_COLS // 2, 0, GRID_W - NA_WIN_COLS)
    in_win = (ck[None, :] >= col_start[:, None]) & (ck[None, :] < col_start[:, None] + NA_WIN_COLS)
    col_sel = (dc[:, :, None] == np.arange(2 * NA_WIN_COLS - 1)).astype(np.float32)
    tiles = jnp.einsum("lhrc,qkc->lhrqk", rpb.astype(F32), col_sel, precision=lax.Precision.HIGHEST)
    tiles = jnp.where(in_win, tiles, NEG)
    masked = jnp.full(tiles.shape[:2] + (1, GRID_W, GRID_W), NEG, F32)
    tiles = jnp.concatenate([tiles, masked], axis=2)
    zero = jnp.zeros_like(tiles)
    return jnp.stack([jnp.concatenate([tiles, zero], -1), jnp.concatenate([zero, tiles], -1)], axis=2)


def lat_na(z, cache_k, cache_v, tab, layer):
    per = GRID_H // NA_RB
    kv = lambda off: pl.BlockSpec((DEC_SEQ, NA_W), lambda b, i: (b, off // NA_W))
    cache = pl.BlockSpec((1, None, PAST_LEN, NA_W), lambda b, i: (b, layer, 0, 0))
    return pl.pallas_call(
        _lat_na_kernel,
        grid=(DEC_BATCH, per),
        in_specs=[pl.BlockSpec((NA_TQ, NA_W), lambda b, i: (b * per + i, 0)),
                  kv(OFF_NA_K), kv(OFF_NA_V), cache, cache,
                  pl.BlockSpec((None, NA_HEADS, 2, NA_TILES, GRID_W, LANE), lambda b, i: (layer, 0, 0, 0, 0, 0))],
        out_specs=pl.BlockSpec((NA_TQ, NA_W), lambda b, i: (b * per + i, 0)),
        out_shape=jax.ShapeDtypeStruct((N_LAT_TOK, NA_W), F32),
        compiler_params=_cparams("arbitrary", "arbitrary"),
        name="lat_na",
    )(z, z, z, cache_k, cache_v, tab)


def _log_sigmoid(x):
    return jnp.minimum(x, 0.0) - jnp.log1p(jnp.exp(-jnp.abs(x)))


def _split3(x):
    hi = x.astype(BF16)
    r = x - hi.astype(F32)
    mid = r.astype(BF16)
    lo = (r - mid.astype(F32)).astype(BF16)
    return hi, mid, lo


def _mlstm_gate_tables(gates, fwd):
    t = gates.shape[0]
    jj = lax.broadcasted_iota(jnp.int32, (t, t), 0)
    ss = lax.broadcasted_iota(jnp.int32, (t, t), 1)
    feeds = (jj <= ss) if fwd else (jj >= ss)
    feeds_b = jnp.where(feeds, 1.0, 0.0).astype(BF16)
    reached_b = jnp.where((ss <= jj) if fwd else (ss >= jj), 1.0, 0.0).astype(BF16)
    eye_b = jnp.where(jj == ss, 1.0, 0.0).astype(BF16)
    lf = _split3(_log_sigmoid(gates))
    gs = _split3(gates)
    b_col = _dot(reached_b, lf[0]) + _dot(reached_b, lf[1]) + _dot(reached_b, lf[2])
    b_row = _dot_tn(lf[0], feeds_b) + _dot_tn(lf[1], feeds_b) + _dot_tn(lf[2], feeds_b)
    ig_row = _dot_tn(gs[0], eye_b) + _dot_tn(gs[1], eye_b) + _dot_tn(gs[2], eye_b)
    return ig_row, b_row, b_col, feeds


def _mlstm_head(k, qt, vt, kt, u, b_row, ig_row, feeds, ct, n, m, half, fwd):
    t = k.shape[0]
    dmat = jnp.where(feeds, b_row + u, NEG)
    m_inter = b_row + m
    m_j = jnp.maximum(m_inter, jnp.max(dmat, axis=0, keepdims=True))
    w = jnp.exp(dmat - m_j)
    decay = jnp.exp(m_inter - m_j)
    first = lax.broadcasted_iota(jnp.int32, (LANE, t), 0) < HEAD_DIM
    mine = first if half == 0 else ~first
    qm = jnp.where(mine, qt, 0.0)
    qb = qm.astype(BF16)
    kb = k.astype(BF16)
    qk = _dot(kb, qb) * w
    num = decay * _dot(ct.astype(BF16), qb) + _dot(vt.astype(BF16), qk.astype(BF16))
    den = decay * jnp.sum(qm * n, axis=0, keepdims=True) + jnp.sum(qk, axis=0, keepdims=True)
    h = jnp.where(mine, num * (1.0 / jnp.maximum(jnp.abs(den), jnp.exp(-m_j))), 0.0)
    last = t - 1 if fwd else 0
    b_last = b_row[:, last:last + 1]
    g = b_last - b_row + ig_row
    m_new = jnp.maximum(b_last + m, jnp.max(g, axis=1, keepdims=True))
    ws = jnp.exp(g - m_new)
    d_last = jnp.exp(b_last + m - m_new)
    ct_add = _dot((vt * ws).astype(BF16), kb)
    n_add = jnp.sum(jnp.where(mine, kt * ws, 0.0), axis=1, keepdims=True)
    return h, d_last, ct_add, n_add, m_new


def _mlstm_kernel(zf_ref, zb_ref, *refs, has_init):
    if has_init:
        c0_ref, n0_ref, m0_ref, hf_ref, hb_ref, co_ref, no_ref, mo_ref, c_s, n_s, m_s = refs
    else:
        hf_ref, hb_ref, co_ref, no_ref, mo_ref, c_s, n_s, m_s = refs
    chunk = pl.program_id(1)
    pairs = ML_W // LANE
    zero = jnp.zeros((HEAD_DIM, HEAD_DIM), F32)

    ri = lax.broadcasted_iota(jnp.int32, (LANE, LANE), 0)
    ci = lax.broadcasted_iota(jnp.int32, (LANE, LANE), 1)
    eye = ri == ci
    first_block = (ri < HEAD_DIM) & (ci < HEAD_DIM)
    second_block = (ri >= HEAD_DIM) & (ci >= HEAD_DIM)
    first_rows = lax.broadcasted_iota(jnp.int32, (LANE, 1), 0) < HEAD_DIM

    @pl.when(chunk == 0)
    def _():
        if not has_init:
            c_s[...] = jnp.zeros_like(c_s)
            n_s[...] = jnp.zeros_like(n_s)
            m_s[...] = jnp.zeros_like(m_s)
            return
        for d in range(2):
            for p in range(pairs):
                top = jnp.concatenate([c0_ref[0, d, 2 * p], zero], axis=1)
                bot = jnp.concatenate([zero, c0_ref[0, d, 2 * p + 1]], axis=1)
                c_s[d * pairs + p] = jnp.concatenate([top, bot], axis=0).T
                e = d * ML_HEADS + 2 * p
                n_row = jnp.concatenate([n0_ref[0, e:e + 1, :], n0_ref[0, e + 1:e + 2, :]], axis=1)
                n_s[d * pairs + p] = jnp.sum(jnp.where(eye, n_row, 0.0), axis=1, keepdims=True)
            m_s[d] = m0_ref[0, d * ML_HEADS:(d + 1) * ML_HEADS, :]

    for d, (z_ref, h_ref) in enumerate(((zf_ref, hf_ref), (zb_ref, hb_ref))):
        gates = z_ref[:, OFF_ML_G - OFF_ML_Q:OFF_ML_G - OFF_ML_Q + LANE]
        ig_rows, b_rows, b_cols, feeds = _mlstm_gate_tables(gates, d == 0)
        u = gates - pltpu.roll(b_cols, LANE - ML_HEADS, axis=1)
        m_all = m_s[d]
        m_rows = []
        for p in range(pairs):
            k = z_ref[:, ML_W + p * LANE:ML_W + (p + 1) * LANE] * ATTN_SCALE
            qt = z_ref[:, p * LANE:(p + 1) * LANE].T
            vt = z_ref[:, 2 * ML_W + p * LANE:2 * ML_W + (p + 1) * LANE].T
            kt = k.T
            ct = c_s[d * pairs + p]
            n = n_s[d * pairs + p]
            res = []
            for half in range(2):
                hd = 2 * p + half
                gi = 2 * ML_HEADS * d + hd
                gf = gi + ML_HEADS
                res.append(_mlstm_head(k, qt, vt, kt, u[:, gi:gi + 1], b_rows[gf:gf + 1, :], ig_rows[gi:gi + 1, :],
                                       feeds, ct, n, m_all[hd:hd + 1, 0:1], half, d == 0))
            (h0, dl0, ca0, na0, mn0), (h1, dl1, ca1, na1, mn1) = res
            h_ref[:, p * LANE:(p + 1) * LANE] = (h0 + h1).T
            c_s[d * pairs + p] = jnp.where(first_block, dl0 * ct + ca0, jnp.where(second_block, dl1 * ct + ca1, 0.0))
            n_s[d * pairs + p] = jnp.where(first_rows, dl0 * n + na0, dl1 * n + na1)
            m_rows += [jnp.broadcast_to(mn0, (1, LANE)), jnp.broadcast_to(mn1, (1, LANE))]
        m_s[d] = jnp.concatenate(m_rows, axis=0)

    @pl.when(chunk == pl.num_programs(1) - 1)
    def _():
        for d in range(2):
            for p in range(pairs):
                c = c_s[d * pairs + p].T
                n_row = jnp.sum(jnp.where(eye, n_s[d * pairs + p], 0.0), axis=0, keepdims=True)
                co_ref[0, d, 2 * p] = c[:HEAD_DIM, :HEAD_DIM]
                co_ref[0, d, 2 * p + 1] = c[HEAD_DIM:, HEAD_DIM:]
                e = d * ML_HEADS + 2 * p
                no_ref[0, e:e + 1, :] = n_row[:, :HEAD_DIM]
                no_ref[0, e + 1:e + 2, :] = n_row[:, HEAD_DIM:]
            mo_ref[0, d * ML_HEADS:(d + 1) * ML_HEADS, :] = m_s[d]


def mlstm(z, batch, seq, init=None, layer=None):
    nc = seq // ML_CHUNK
    zcol = OFF_ML_Q // ML_BLOCK_W
    state_c = pl.BlockSpec((1, 2, ML_HEADS, HEAD_DIM, HEAD_DIM), lambda b, c: (b, 0, 0, 0, 0))
    state_n = pl.BlockSpec((1, 2 * ML_HEADS, HEAD_DIM), lambda b, c: (b, 0, 0))
    state_m = pl.BlockSpec((1, 2 * ML_HEADS, LANE), lambda b, c: (b, 0, 0))
    init_specs = [] if init is None else [
        pl.BlockSpec((1, None, 2, ML_HEADS, HEAD_DIM, HEAD_DIM), lambda b, c: (b, layer, 0, 0, 0, 0)),
        pl.BlockSpec((1, None, 2 * ML_HEADS, HEAD_DIM), lambda b, c: (b, layer, 0, 0)),
        pl.BlockSpec((1, None, 2 * ML_HEADS, LANE), lambda b, c: (b, layer, 0, 0)),
    ]
    pairs = ML_W // LANE
    return pl.pallas_call(
        functools.partial(_mlstm_kernel, has_init=init is not None),
        grid=(batch, nc),
        in_specs=[
            pl.BlockSpec((ML_CHUNK, ML_BLOCK_W), lambda b, c: (b * nc + c, zcol)),
            pl.BlockSpec((ML_CHUNK, ML_BLOCK_W), lambda b, c: (b * nc + nc - 1 - c, zcol)),
        ] + init_specs,
        out_specs=[
            pl.BlockSpec((ML_CHUNK, ML_W), lambda b, c: (b * nc + c, 0)),
            pl.BlockSpec((ML_CHUNK, ML_W), lambda b, c: (b * nc + nc - 1 - c, 0)),
            state_c, state_n, state_m,
        ],
        out_shape=[
            jax.ShapeDtypeStruct((batch * seq, ML_W), F32),
            jax.ShapeDtypeStruct((batch * seq, ML_W), F32),
            jax.ShapeDtypeStruct((batch, 2, ML_HEADS, HEAD_DIM, HEAD_DIM), F32),
            jax.ShapeDtypeStruct((batch, 2 * ML_HEADS, HEAD_DIM), F32),
            jax.ShapeDtypeStruct((batch, 2 * ML_HEADS, LANE), F32),
        ],
        scratch_shapes=[
            pltpu.VMEM((2 * pairs, LANE, LANE), F32),
            pltpu.VMEM((2 * pairs, LANE, 1), F32),
            pltpu.VMEM((2, ML_HEADS, LANE), F32),
        ],
        compiler_params=_cparams("arbitrary", "arbitrary"),
        name="mlstm",
    )(z, z, *(() if init is None else init))


def _layer_norm(u, g, b):
    mu = jnp.mean(u, axis=-1, keepdims=True)
    uc = u - mu
    var = jnp.mean(uc * uc, axis=-1, keepdims=True)
    return uc * lax.rsqrt(var + NORM_EPS) * g + b


def _outproj_kernel(x_ref, a_ref, hf_ref, hb_ref, og0_ref, og1_ref, c_ref, w_ref, mg_ref, g1_ref, sh2_ref, sc2_ref,
                    lng_ref, lnb_ref, rw_ref, x1_ref, h2_ref, lg_ref):
    low = _low_half(TM)
    mixed_b = []
    for p, og_ref in enumerate((og0_ref, og1_ref)):
        cols = slice(p * LANE, (p + 1) * LANE)
        h = hf_ref[:, cols] + hb_ref[:, cols]
        hc = h - _pair_mean(h, low)
        hn = hc * lax.rsqrt(_pair_mean(hc * hc, low) + NORM_EPS)
        mixed_b.append(hn * mg_ref[:, cols] * _sigmoid(og_ref[...]))
    mixed = jnp.concatenate([a_ref[...]] + mixed_b + [c_ref[...]], axis=-1).astype(BF16)
    y = _dot(mixed, w_ref[...])
    x1 = _layer_norm(DEEPNORM_ALPHA * x_ref[...] + g1_ref[0] * y, lng_ref[...], lnb_ref[...])
    h2 = (x1 * (1.0 + sc2_ref[0]) + sh2_ref[0]).astype(BF16)
    x1_ref[...] = x1
    h2_ref[...] = h2
    lg_ref[...] = _dot_nt(rw_ref[...], h2)


def out_projection(x, out_a, hf, hb, z, out_c, layer, w_out, ml_gain, mods, mod_row, ln_g, ln_b, router_t):
    n = x.shape[0]
    mod = lambda k: pl.BlockSpec((None, 1, 1, D_MODEL), lambda i: (layer, mod_row(i), 0, k))
    row = lambda w: pl.BlockSpec((TM, w), lambda i: (i, 0))
    zcol = lambda off: pl.BlockSpec((TM, LANE), lambda i: (i, off // LANE))
    per_layer = lambda r, c: pl.BlockSpec((None, r, c), lambda i: (layer, 0, 0))
    norm0 = pl.BlockSpec((None, None, 1, D_MODEL), lambda i: (layer, 0, 0, 0))
    return pl.pallas_call(
        _outproj_kernel,
        grid=(n // TM,),
        in_specs=[row(D_MODEL), row(NA_W), row(ML_W), row(ML_W), zcol(OFF_ML_O), zcol(OFF_ML_O + LANE), row(GQA_W),
                  per_layer(D_MODEL, D_MODEL), per_layer(1, ML_W), mod(2), mod(3), mod(4), norm0, norm0,
                  per_layer(N_EXPERTS, D_MODEL)],
        out_specs=[row(D_MODEL), row(D_MODEL), pl.BlockSpec((N_EXPERTS, TM), lambda i: (0, i))],
        out_shape=[
            jax.ShapeDtypeStruct((n, D_MODEL), F32),
            jax.ShapeDtypeStruct((n, D_MODEL), BF16),
            jax.ShapeDtypeStruct((N_EXPERTS, n), F32),
        ],
        compiler_params=_cparams("arbitrary"),
        name="out_projection",
    )(x, out_a, hf, hb, z, z, out_c, w_out, ml_gain, mods, mods, mods, ln_g, ln_b, router_t)


ROUTE_BLK = 256


def _prefix_count(x, tri):
    n = x.shape[1]
    outs = []
    carry = jnp.zeros((N_EXPERTS, 1), F32)
    for i in range(n // ROUTE_BLK):
        blk = x[:, i * ROUTE_BLK:(i + 1) * ROUTE_BLK]
        outs.append(_dot(blk, tri) + carry)
        carry = carry + jnp.sum(blk.astype(F32), axis=1, keepdims=True)
    return jnp.concatenate(outs, axis=1) if len(outs) > 1 else outs[0]


def _route_kernel(lg_ref, coder_ref, codet_ref, afft_ref, *, n, cap):
    sets = lg_ref.shape[1] // n
    lg = lg_ref[...]
    ex = jnp.exp(lg - jnp.max(lg, axis=0, keepdims=True))
    aff_all = ex / jnp.sum(ex, axis=0, keepdims=True)
    affs = [aff_all[:, s * n:(s + 1) * n] for s in range(sets)]

    def search(i, bits):
        bit = lax.shift_left(jnp.int32(1), 30 - i)
        out = []
        for aff, b in zip(affs, bits):
            cand = b | bit
            cnt = jnp.sum(jnp.where(aff >= pltpu.bitcast(cand, F32), 1.0, 0.0), axis=1, keepdims=True)
            out.append(jnp.where(cnt >= cap, cand, b))
        return tuple(out)

    floors = lax.fori_loop(0, 31, search, tuple(jnp.zeros((N_EXPERTS, 1), jnp.int32) for _ in range(sets)))
    ti = lax.broadcasted_iota(jnp.int32, (ROUTE_BLK, ROUTE_BLK), 0)
    tj = lax.broadcasted_iota(jnp.int32, (ROUTE_BLK, ROUTE_BLK), 1)
    tri = jnp.where(ti < tj, 1.0, 0.0).astype(BF16)
    pad = jnp.zeros((LANE - N_EXPERTS, n), F32)
    for s, (aff, floor_bits) in enumerate(zip(affs, floors)):
        thr = jnp.min(jnp.where(aff >= pltpu.bitcast(floor_bits, F32), aff, 2.0), axis=1, keepdims=True)
        gt = aff > thr
        eq = aff == thr
        need = cap - jnp.sum(jnp.where(gt, 1.0, 0.0), axis=1, keepdims=True)
        eq_rank = _prefix_count(jnp.where(eq, 1.0, 0.0).astype(BF16), tri)
        sel = jnp.where(gt, 1.0, jnp.where(eq & (eq_rank < need), 1.0, 0.0))
        pos = _prefix_count(sel.astype(BF16), tri)
        code = jnp.where(sel > 0.5, pos, -1.0)
        coder_ref[s] = code
        codet_ref[s] = jnp.concatenate([code, pad], axis=0).T
        afft_ref[s] = jnp.concatenate([aff, pad], axis=0).T


def route(logits_t, n, cap):
    tokens = logits_t.shape[1]
    sets = tokens // n
    whole = lambda shape: pl.BlockSpec(shape, lambda i: (0,) * len(shape))
    return pl.pallas_call(
        functools.partial(_route_kernel, n=n, cap=cap),
        grid=(1,),
        in_specs=[whole((N_EXPERTS, tokens))],
        out_specs=[whole((sets, N_EXPERTS, n)), whole((sets, n, LANE)), whole((sets, n, LANE))],
        out_shape=[
            jax.ShapeDtypeStruct((sets, N_EXPERTS, n), F32),
            jax.ShapeDtypeStruct((sets, n, LANE), F32),
            jax.ShapeDtypeStruct((sets, n, LANE), F32),
        ],
        compiler_params=_cparams("arbitrary"),
        name="route",
    )(logits_t)


def _gather_kernel(coder_ref, h_ref, xe_ref, *, cap, epb):
    e0 = pl.program_id(1) * epb
    n = coder_ref.shape[2]
    ci = lax.broadcasted_iota(jnp.int32, (cap, n), 0).astype(F32)
    onehot = [jnp.where(ci == coder_ref[0, pl.ds(e0 + k, 1), :], 1.0, 0.0).astype(BF16) for k in range(epb)]
    onehot = jnp.concatenate(onehot, axis=0) if epb > 1 else onehot[0]
    rows = _dot(onehot, h_ref[...]).astype(BF16)
    for k in range(epb):
        xe_ref[k] = rows[k * cap:(k + 1) * cap]


def gather_tokens(code_rows, h2, n, cap):
    sets = code_rows.shape[0]
    epb = max(1, 512 // cap)
    return pl.pallas_call(
        functools.partial(_gather_kernel, cap=cap, epb=epb),
        grid=(sets, N_EXPERTS // epb),
        in_specs=[
            pl.BlockSpec((1, N_EXPERTS, n), lambda s, e: (s, 0, 0)),
            pl.BlockSpec((n, D_MODEL), lambda s, e: (s, 0)),
        ],
        out_specs=pl.BlockSpec((epb, cap, D_MODEL), lambda s, e: (e, s, 0)),
        out_shape=jax.ShapeDtypeStruct((N_EXPERTS, sets * cap, D_MODEL), BF16),
        compiler_params=_cparams("arbitrary", "arbitrary"),
        name="gather_tokens",
    )(code_rows, h2)


def _expert_kernel(xi_ref, xl_ref, wg_ref, wu_ref, wd_ref, yi_ref, yl_ref, acci_ref, accl_ref):
    j = pl.program_id(1)

    @pl.when(j == 0)
    def _():
        acci_ref[...] = jnp.zeros_like(acci_ref)
        accl_ref[...] = jnp.zeros_like(accl_ref)

    xs = (xi_ref[0], xl_ref[0])

    sums = [None, None]
    for s0 in range(0, EXP_TF, EXP_SUB):
        cols = slice(s0, min(s0 + EXP_SUB, EXP_TF))
        wg = wg_ref[0, 0, :, cols].astype(BF16)
        wu = wu_ref[0, 0, :, cols].astype(BF16)
        wd = wd_ref[0, 0, cols, :].astype(BF16)
        for g in range(2):
            a = _dot(xs[g], wg)
            b = _dot(xs[g], wu)
            hid = (a * _sigmoid(a) * b).astype(BF16)
            part = _dot(hid, wd)
            sums[g] = part if sums[g] is None else sums[g] + part
    acci_ref[...] += sums[0]
    accl_ref[...] += sums[1]

    @pl.when(j == pl.num_programs(1) - 1)
    def _():
        yi_ref[0] = acci_ref[...].astype(BF16)
        yl_ref[0] = accl_ref[...].astype(BF16)


def experts(xe_ctx, xe_lat, w_gate, w_up, w_down, layer):
    xin = lambda s: pl.BlockSpec((1, s, D_MODEL), lambda e, j: (e, 0, 0))
    return pl.pallas_call(
        _expert_kernel,
        grid=(N_EXPERTS, EXPERT_HIDDEN // EXP_TF),
        in_specs=[
            xin(SLOTS_CTX), xin(SLOTS_LAT),
            pl.BlockSpec((1, 1, D_MODEL, EXP_TF), lambda e, j: (layer, e, 0, j)),
            pl.BlockSpec((1, 1, D_MODEL, EXP_TF), lambda e, j: (layer, e, 0, j)),
            pl.BlockSpec((1, 1, EXP_TF, D_MODEL), lambda e, j: (layer, e, j, 0)),
        ],
        out_specs=[xin(SLOTS_CTX), xin(SLOTS_LAT)],
        out_shape=[
            jax.ShapeDtypeStruct((N_EXPERTS, SLOTS_CTX, D_MODEL), BF16),
            jax.ShapeDtypeStruct((N_EXPERTS, SLOTS_LAT, D_MODEL), BF16),
        ],
        scratch_shapes=[pltpu.VMEM((SLOTS_CTX, D_MODEL), F32), pltpu.VMEM((SLOTS_LAT, D_MODEL), F32)],
        compiler_params=_cparams("arbitrary", "arbitrary"),
        name="experts",
    )(xe_ctx, xe_lat, w_gate, w_up, w_down)


COMB_TT = 256


def _scatter_per_expert(code, aff, ye_ref, cap):
    tt = code.shape[0]
    li = lax.broadcasted_iota(jnp.int32, (tt, cap), 1).astype(F32)
    acc = jnp.zeros((tt, D_MODEL), F32)
    for e in range(N_EXPERTS):
        onehot = jnp.where(li == code[:, e:e + 1], 1.0, 0.0).astype(BF16)
        acc = acc + aff[:, e:e + 1] * _dot(onehot, ye_ref[e])
    return acc


def _scatter_merged(code, aff, ye_ref, cap):
    tt = code.shape[0]
    slots = N_EXPERTS * cap
    shift = cap.bit_length() - 1
    ei = lax.broadcasted_iota(jnp.int32, (LANE, slots), 0)
    si = lax.broadcasted_iota(jnp.int32, (LANE, slots), 1)
    expand = jnp.where(lax.shift_right_logical(si, shift) == ei, 1.0, 0.0).astype(BF16)
    slot = (lax.broadcasted_iota(jnp.int32, (tt, slots), 1) & (cap - 1)).astype(F32)
    hit = _dot(code.astype(BF16), expand) == slot
    a_hi = aff.astype(BF16)
    a_lo = (aff - a_hi.astype(F32)).astype(BF16)
    ye = ye_ref[...].reshape(slots, D_MODEL)
    acc = _dot(jnp.where(hit, _dot(a_hi, expand), 0.0).astype(BF16), ye)
    return acc + _dot(jnp.where(hit, _dot(a_lo, expand), 0.0).astype(BF16), ye)


def _combine_kernel(code_ref, aff_ref, ye_ref, x1_ref, g2_ref, lng_ref, lnb_ref, o_ref, *, cap):
    scatter = _scatter_merged if cap < LANE else _scatter_per_expert
    acc = scatter(code_ref[0], aff_ref[0], ye_ref, cap)
    u = DEEPNORM_ALPHA * x1_ref[...] + g2_ref[0] * acc
    o_ref[...] = _layer_norm(u, lng_ref[...], lnb_ref[...])


def combine(code, aff, ye, x1, layer, mods, set_mod_row, ln_g, ln_b, n, cap):
    sets = code.shape[0]
    per = n // COMB_TT
    tok = pl.BlockSpec((1, COMB_TT, LANE), lambda s, i: (s, i, 0))
    norm1 = pl.BlockSpec((None, None, 1, D_MODEL), lambda s, i: (layer, 1, 0, 0))
    return pl.pallas_call(
        functools.partial(_combine_kernel, cap=cap),
        grid=(sets, per),
        in_specs=[
            tok, tok,
            pl.BlockSpec((N_EXPERTS, cap, D_MODEL), lambda s, i: (0, s, 0)),
            pl.BlockSpec((COMB_TT, D_MODEL), lambda s, i: (s * per + i, 0)),
            pl.BlockSpec((None, 1, 1, D_MODEL), lambda s, i: (layer, set_mod_row(s), 0, 5)),
            norm1, norm1,
        ],
        out_specs=pl.BlockSpec((COMB_TT, D_MODEL), lambda s, i: (s * per + i, 0)),
        out_shape=jax.ShapeDtypeStruct((sets * n, D_MODEL), F32),
        compiler_params=_cparams("arbitrary", "arbitrary"),
        name="combine",
    )(code, aff, ye, x1, mods, ln_g, ln_b)


def _axial_rope_tables():
    t = np.arange(DEC_SEQ)
    row = (t // GRID_W).astype(np.float32)
    col = (t % GRID_W).astype(np.float32)
    half = HEAD_DIM // 2
    inv = (ROPE_THETA ** (-np.arange(0, half, 2, dtype=np.float32) / half)).astype(np.float32)
    ang_r = row[:, None] * inv
    ang_c = col[:, None] * inv
    ang = np.concatenate([ang_r, ang_r, ang_c, ang_c] * 2, -1)
    return jnp.asarray(np.cos(ang), F32), jnp.asarray(np.sin(ang), F32)


def _ctx_mod_row(i):
    return 0


def _lat_tile_mod_row(i):
    return 1 + i // (DEC_SEQ // TM)


def _lat_set_mod_row(s):
    return 1 + s


def kernel(x_prompt, x_sample, c, cache_na_k, cache_na_v, cache_gqa_k, cache_gqa_v, state_mlstm_c, state_mlstm_n,
           state_mlstm_m, c_ctx, ada_w, ada_b, w_in, b_gate, w_out, na_rpb, qk_norm_g, ml_norm_g, ln_g, ln_b,
           router_w, w_gate, w_up, w_down):
    cond8 = jnp.concatenate([c_ctx[None, :], c, jnp.zeros((8 - 1 - DEC_BATCH, D_MODEL), F32)], 0)
    mods = adaln(cond8, ada_w, ada_b).reshape(DEPTH, 8, 1, 6 * D_MODEL)
    xc = x_prompt.reshape(N_CTX_TOK, D_MODEL)
    xl = x_sample.reshape(N_LAT_TOK, D_MODEL)
    cos, sin = _axial_rope_tables()
    na_tab = na_bias_table(na_rpb)

    split = OFF_ML_G + N_GATES
    w_pad = jnp.concatenate([w_in[:, :, :split], jnp.zeros((DEPTH, D_MODEL, LANE - N_GATES), F32),
                             w_in[:, :, split:]], 2).astype(BF16)
    bias_row = jnp.pad(b_gate, ((0, 0), (OFF_ML_G, PROJ_PAD - OFF_ML_G - N_GATES))).reshape(DEPTH, 1, PROJ_PAD)
    gains = jnp.tile(qk_norm_g, (1, 1, 2)).reshape(DEPTH, 2, 1, LANE)
    w_out_b = w_out.astype(BF16)
    router_t = jnp.swapaxes(router_w, 1, 2).astype(BF16)
    ml_gain = ml_norm_g.reshape(DEPTH, 1, ML_W)
    ln_g4 = ln_g.reshape(DEPTH, 2, 1, D_MODEL)
    ln_b4 = ln_b.reshape(DEPTH, 2, 1, D_MODEL)
    na_ck = cache_na_k.reshape(DEC_BATCH, DEPTH, PAST_LEN, NA_W)
    na_cv = cache_na_v.reshape(DEC_BATCH, DEPTH, PAST_LEN, NA_W)
    gq_ck = cache_gqa_k.reshape(DEC_BATCH, DEPTH, PAST_LEN, GQA_KV_W)
    gq_cv = cache_gqa_v.reshape(DEC_BATCH, DEPTH, PAST_LEN, GQA_KV_W)
    lat_n0 = state_mlstm_n.reshape(DEC_BATCH, DEPTH, 2 * ML_HEADS, HEAD_DIM)
    lat_m0 = jnp.broadcast_to(state_mlstm_m.reshape(DEC_BATCH, DEPTH, 2 * ML_HEADS, 1),
                              (DEC_BATCH, DEPTH, 2 * ML_HEADS, LANE))
    zcs, states = [], []

    for l in range(DEPTH):
        zc = in_projection(xc, l, mods, _ctx_mod_row, w_pad, bias_row, gains)
        zl = in_projection(xl, l, mods, _lat_tile_mod_row, w_pad, bias_row, gains, cos, sin)

        out_a_c, out_c_c = ctx_attention(zc)
        hf_c, hb_c, sc, sn, sm = mlstm(zc, BATCH, SEQ)
        out_a_l = lat_na(zl, na_ck, na_cv, na_tab, l)
        out_c_l = lat_gqa(zl, gq_ck, gq_cv, l)
        hf_l, hb_l = mlstm(zl, DEC_BATCH, DEC_SEQ, (state_mlstm_c, lat_n0, lat_m0), l)[:2]
        zcs.append(zc)
        states.append((sc, sn, sm))

        x1_c, h2_c, lg_c = out_projection(xc, out_a_c, hf_c, hb_c, zc, out_c_c, l, w_out_b, ml_gain, mods,
                                          _ctx_mod_row, ln_g4, ln_b4, router_t)
        x1_l, h2_l, lg_l = out_projection(xl, out_a_l, hf_l, hb_l, zl, out_c_l, l, w_out_b, ml_gain, mods,
                                          _lat_tile_mod_row, ln_g4, ln_b4, router_t)
        crow_c, code_c, aff_c = route(lg_c, SEQ, CAP_CTX)
        crow_l, code_l, aff_l = route(lg_l, DEC_SEQ, CAP_LAT)
        xe_c = gather_tokens(crow_c, h2_c, SEQ, CAP_CTX)
        xe_l = gather_tokens(crow_l, h2_l, DEC_SEQ, CAP_LAT)
        ye_c, ye_l = experts(xe_c, xe_l, w_gate, w_up, w_down, l)
        xc = combine(code_c, aff_c, ye_c, x1_c, l, mods, _ctx_mod_row, ln_g4, ln_b4, SEQ, CAP_CTX)
        xl = combine(code_l, aff_l, ye_l, x1_l, l, mods, _lat_set_mod_row, ln_g4, ln_b4, DEC_SEQ, CAP_LAT)

    y_prompt = xc.reshape(BATCH, SEQ, D_MODEL)
    y_sample = xl.reshape(DEC_BATCH, DEC_SEQ, D_MODEL)
    cols = lambda off, heads: jnp.concatenate(
        [zc[:, off:off + heads * HEAD_DIM].reshape(BATCH, 1, SEQ, heads, HEAD_DIM) for zc in zcs], 1)
    new_c = jnp.stack([s[0] for s in states], 1)
    new_n = jnp.stack([s[1] for s in states], 1).reshape(BATCH, DEPTH, 2, ML_HEADS, HEAD_DIM)
    new_m = jnp.stack([s[2][:, :, 0] for s in states], 1).reshape(BATCH, DEPTH, 2, ML_HEADS)
    return (y_prompt, y_sample, cols(OFF_NA_K, NA_HEADS), cols(OFF_NA_V, NA_HEADS), cols(OFF_GQ_K, GQA_KV_HEADS),
            cols(OFF_GQ_V, GQA_KV_HEADS), new_c, new_n, new_m)
```

```python
import functools

import jax
import jax.numpy as jnp
import numpy as np
from jax import lax
from jax.experimental import pallas as pl
from jax.experimental.pallas import tpu as pltpu

D_MODEL = 1024
BATCH = 16
SEQ = 256
DEPTH = 2
DEC_BATCH = 2
DEC_SEQ = 2048
PAST_LEN = 256
GRID_W = 64
GRID_H = DEC_SEQ // GRID_W
HEAD_DIM = 64
NA_HEADS = 6
NA_WIN_ROWS = 8
NA_WIN_COLS = 16
ML_HEADS = 4
GQA_HEADS = 6
GQA_KV_HEADS = 2
ROPE_THETA = 10000.0
N_EXPERTS = 16
EC_CAPACITY = 2
EXPERT_HIDDEN = 2816
NORM_EPS = 1e-6
NA_W = NA_HEADS * HEAD_DIM
ML_W = ML_HEADS * HEAD_DIM
GQA_W = GQA_HEADS * HEAD_DIM
GQA_KV_W = GQA_KV_HEADS * HEAD_DIM
N_GATES = 4 * ML_HEADS
DEEPNORM_ALPHA = (2 * DEPTH) ** 0.25
ATTN_SCALE = HEAD_DIM ** -0.5
F32 = jnp.float32
BF16 = jnp.bfloat16

LANE = 128
N_CTX_TOK = BATCH * SEQ
N_LAT_TOK = DEC_BATCH * DEC_SEQ
NEG = -1e30

OFF_NA_Q = 0
OFF_NA_K = OFF_NA_Q + NA_W
OFF_NA_V = OFF_NA_K + NA_W
OFF_ML_Q = OFF_NA_V + NA_W
OFF_ML_K = OFF_ML_Q + ML_W
OFF_ML_V = OFF_ML_K + ML_W
OFF_ML_O = OFF_ML_V + ML_W
OFF_ML_G = OFF_ML_O + ML_W
OFF_GQ_Q = OFF_ML_G + LANE
OFF_GQ_K = OFF_GQ_Q + GQA_W
OFF_GQ_V = OFF_GQ_K + GQA_KV_W
PROJ_PAD = OFF_GQ_V + GQA_KV_W
ML_BLOCK_W = OFF_GQ_Q - OFF_ML_Q

TM = 512
ML_CHUNK = 256
EXP_SUB = 256
EXP_TF = EXPERT_HIDDEN // 2
CAP_CTX = EC_CAPACITY * SEQ // N_EXPERTS
CAP_LAT = EC_CAPACITY * DEC_SEQ // N_EXPERTS
SLOTS_CTX = BATCH * CAP_CTX
SLOTS_LAT = DEC_BATCH * CAP_LAT
VMEM_LIMIT = 56 * 1024 * 1024


def _cparams(*sem):
    return pltpu.CompilerParams(dimension_semantics=sem, vmem_limit_bytes=VMEM_LIMIT)


def _sigmoid(x):
    return 1.0 / (1.0 + jnp.exp(-x))


def _dot(a, b):
    return jnp.dot(a, b, preferred_element_type=F32)


def _dot_nt(a, b):
    return lax.dot_general(a, b, (((1,), (1,)), ((), ())), preferred_element_type=F32)


def _dot_tn(a, b):
    return lax.dot_general(a, b, (((0,), (0,)), ((), ())), preferred_element_type=F32)


def _low_half(rows):
    return lax.broadcasted_iota(jnp.int32, (rows, LANE), 1) < HEAD_DIM


def _pair_mean(x, low):
    s_lo = jnp.sum(jnp.where(low, x, 0.0), axis=-1, keepdims=True)
    s_hi = jnp.sum(jnp.where(low, 0.0, x), axis=-1, keepdims=True)
    return jnp.where(low, s_lo, s_hi) * (1.0 / HEAD_DIM)


def _adaln_kernel(c_ref, w_ref, b_ref, o_ref):
    c = c_ref[...]
    s = c * _sigmoid(c)
    o_ref[0] = _dot(s.astype(BF16), w_ref[0].astype(BF16)) + b_ref[0]


def adaln(cond8, ada_w, ada_b):
    tn = 1536
    return pl.pallas_call(
        _adaln_kernel,
        grid=(DEPTH, 6 * D_MODEL // tn),
        in_specs=[
            pl.BlockSpec((8, D_MODEL), lambda l, j: (0, 0)),
            pl.BlockSpec((1, D_MODEL, tn), lambda l, j: (l, 0, j)),
            pl.BlockSpec((1, 1, tn), lambda l, j: (l, 0, j)),
        ],
        out_specs=pl.BlockSpec((1, 8, tn), lambda l, j: (l, 0, j)),
        out_shape=jax.ShapeDtypeStruct((DEPTH, 8, 6 * D_MODEL), F32),
        compiler_params=_cparams("arbitrary", "arbitrary"),
        name="adaln",
    )(cond8, ada_w, ada_b.reshape(DEPTH, 1, 6 * D_MODEL))


def _inproj_kernel(x_ref, sh_ref, sc_ref, w_ref, b_ref, gq_ref, gk_ref, *rest, rope, n_alias):
    if rope:
        cos_ref, sin_ref, z_ref = rest
    else:
        z_ref, nakv_ref, gqkv_ref = rest[n_alias:]
    h = x_ref[...] * (1.0 + sc_ref[0]) + sh_ref[0]
    z = _dot(h.astype(BF16), w_ref[...]) + b_ref[...]
    z_ref[...] = z
    rows = z.shape[0]
    low = _low_half(rows)
    if rope:
        lane = lax.broadcasted_iota(jnp.int32, (rows, LANE), 1)
        first_quarter = (lane & (HEAD_DIM // 4)) == 0

    def norm_pair(x, gain):
        y = x * lax.rsqrt(_pair_mean(x * x, low) + NORM_EPS) * gain
        if rope:
            rot = jnp.where(first_quarter, -pltpu.roll(y, LANE - HEAD_DIM // 4, axis=1),
                            pltpu.roll(y, HEAD_DIM // 4, axis=1))
            y = y * cos_ref[...] + rot * sin_ref[...]
        return y

    for p in range(GQA_W // LANE):
        c0 = OFF_GQ_Q + p * LANE
        z_ref[:, c0:c0 + LANE] = norm_pair(z[:, c0:c0 + LANE], gq_ref[...])
    k_normed = norm_pair(z[:, OFF_GQ_K:OFF_GQ_K + LANE], gk_ref[...])
    z_ref[:, OFF_GQ_K:OFF_GQ_K + LANE] = k_normed
    if not rope:
        for s in range(rows // SEQ):
            r = slice(s * SEQ, (s + 1) * SEQ)
            nakv_ref[s] = z[r, OFF_NA_K:OFF_NA_K + 2 * NA_W]
            gqkv_ref[s, :, :GQA_KV_W] = k_normed[r]
            gqkv_ref[s, :, GQA_KV_W:] = z[r, OFF_GQ_V:OFF_GQ_V + GQA_KV_W]


def in_projection(x, layer, mods, mod_row, w_pad, bias_row, gains, cos=None, sin=None, kv_prev=None):
    n = x.shape[0]
    rope = cos is not None
    in_specs = [
        pl.BlockSpec((TM, D_MODEL), lambda i: (i, 0)),
        pl.BlockSpec((None, 1, 1, D_MODEL), lambda i: (layer, mod_row(i), 0, 0)),
        pl.BlockSpec((None, 1, 1, D_MODEL), lambda i: (layer, mod_row(i), 0, 1)),
        pl.BlockSpec((None, D_MODEL, PROJ_PAD), lambda i: (layer, 0, 0)),
        pl.BlockSpec((None, 1, PROJ_PAD), lambda i: (layer, 0, 0)),
        pl.BlockSpec((None, None, 1, LANE), lambda i: (layer, 0, 0, 0)),
        pl.BlockSpec((None, None, 1, LANE), lambda i: (layer, 1, 0, 0)),
    ]
    args = [x, mods, mods, w_pad, bias_row, gains, gains]
    z_spec = pl.BlockSpec((TM, PROJ_PAD), lambda i: (i, 0))
    z_shape = jax.ShapeDtypeStruct((n, PROJ_PAD), F32)
    if rope:
        per = cos.shape[0] // TM
        in_specs += [pl.BlockSpec((TM, LANE), lambda i: (i % per, 0))] * 2
        args += [cos, sin]
        return pl.pallas_call(
            functools.partial(_inproj_kernel, rope=True, n_alias=0),
            grid=(n // TM,),
            in_specs=in_specs,
            out_specs=z_spec,
            out_shape=z_shape,
            compiler_params=_cparams("arbitrary"),
            name="in_projection_rope",
        )(*args)
    seqs = TM // SEQ
    kv_spec = lambda w: pl.BlockSpec((seqs, None, SEQ, w), lambda i: (i, layer, 0, 0))
    kv_shape = lambda w: jax.ShapeDtypeStruct((n // SEQ, DEPTH, SEQ, w), F32)
    aliases = {}
    if kv_prev is not None:
        aliases = {len(args): 1, len(args) + 1: 2}
        in_specs += [pl.BlockSpec(memory_space=pl.ANY)] * 2
        args += list(kv_prev)
    return pl.pallas_call(
        functools.partial(_inproj_kernel, rope=False, n_alias=len(aliases)),
        grid=(n // TM,),
        in_specs=in_specs,
        out_specs=[z_spec, kv_spec(2 * NA_W), kv_spec(2 * GQA_KV_W)],
        out_shape=[z_shape, kv_shape(2 * NA_W), kv_shape(2 * GQA_KV_W)],
        input_output_aliases=aliases,
        compiler_params=_cparams("arbitrary"),
        name="in_projection",
    )(*args)


def _softmax_pv(scores, values):
    m = jnp.max(scores[0], axis=-1, keepdims=True)
    for s in scores[1:]:
        m = jnp.maximum(m, jnp.max(s, axis=-1, keepdims=True))
    l = 0.0
    o = 0.0
    for s, v in zip(scores, values):
        p = jnp.exp(s - m)
        l = l + jnp.sum(p, axis=-1, keepdims=True)
        o = o + _dot(p.astype(BF16), v)
    return o * (1.0 / l)


def _ctx_attn_kernel(naq_ref, nak_ref, nav_ref, gq_ref, gk_ref, gv_ref, oa_ref, oc_ref):
    low = _low_half(SEQ)
    for p in range(NA_W // LANE):
        cols = slice(p * LANE, (p + 1) * LANE)
        q = naq_ref[:, cols] * ATTN_SCALE
        k = nak_ref[:, cols].astype(BF16)
        v = nav_ref[:, cols].astype(BF16)
        outs = []
        for half in range(2):
            qm = jnp.where(low if half == 0 else ~low, q, 0.0).astype(BF16)
            outs.append(_softmax_pv([_dot_nt(qm, k)], [v]))
        oa_ref[:, cols] = jnp.where(low, outs[0], outs[1])
    k = gk_ref[...]
    v = gv_ref[...]
    k_at = (pltpu.roll(k, HEAD_DIM, axis=1).astype(BF16), k.astype(BF16))
    v_at = (pltpu.roll(v, HEAD_DIM, axis=1).astype(BF16), v.astype(BF16))
    group = GQA_HEADS // GQA_KV_HEADS
    for p in range(GQA_W // LANE):
        cols = slice(p * LANE, (p + 1) * LANE)
        q = gq_ref[:, cols] * ATTN_SCALE
        outs = []
        for half in range(2):
            g = (2 * p + half) // group
            qm = jnp.where(low if half == 0 else ~low, q, 0.0).astype(BF16)
            outs.append(_softmax_pv([_dot_nt(qm, k_at[g == half])], [v_at[g == half]]))
        oc_ref[:, cols] = jnp.where(low, outs[0], outs[1])


def ctx_attention(z):
    col = lambda w, off: pl.BlockSpec((SEQ, w), lambda b: (b, off // w))
    return pl.pallas_call(
        _ctx_attn_kernel,
        grid=(BATCH,),
        in_specs=[col(NA_W, OFF_NA_Q), col(NA_W, OFF_NA_K), col(NA_W, OFF_NA_V),
                  col(GQA_W, OFF_GQ_Q), col(LANE, OFF_GQ_K), col(LANE, OFF_GQ_V)],
        out_specs=[col(NA_W, 0), col(GQA_W, 0)],
        out_shape=[jax.ShapeDtypeStruct((N_CTX_TOK, NA_W), F32), jax.ShapeDtypeStruct((N_CTX_TOK, GQA_W), F32)],
        compiler_params=_cparams("arbitrary"),
        name="ctx_attention",
    )(z, z, z, z, z, z)


GQA_TQ = 512


def _lat_gqa_kernel(q_ref, kl_ref, vl_ref, kc_ref, vc_ref, o_ref):
    low = _low_half(GQA_TQ)
    at = lambda x: (pltpu.roll(x, HEAD_DIM, axis=1).astype(BF16), x.astype(BF16))
    kl_at, vl_at, kc_at, vc_at = at(kl_ref[...]), at(vl_ref[...]), at(kc_ref[0]), at(vc_ref[0])
    group = GQA_HEADS // GQA_KV_HEADS
    for p in range(GQA_W // LANE):
        cols = slice(p * LANE, (p + 1) * LANE)
        q = q_ref[:, cols] * ATTN_SCALE
        outs = []
        for half in range(2):
            same = ((2 * p + half) // group) == half
            qm = jnp.where(low if half == 0 else ~low, q, 0.0).astype(BF16)
            outs.append(_softmax_pv([_dot_nt(qm, kc_at[same]), _dot_nt(qm, kl_at[same])],
                                    [vc_at[same], vl_at[same]]))
        o_ref[:, cols] = jnp.where(low, outs[0], outs[1])


def lat_gqa(z, cache_k, cache_v, layer):
    per = DEC_SEQ // GQA_TQ
    kv = lambda off: pl.BlockSpec((DEC_SEQ, LANE), lambda b, i: (b, off // LANE))
    cache = pl.BlockSpec((1, None, PAST_LEN, LANE), lambda b, i: (b, layer, 0, 0))
    return pl.pallas_call(
        _lat_gqa_kernel,
        grid=(DEC_BATCH, per),
        in_specs=[pl.BlockSpec((GQA_TQ, GQA_W), lambda b, i: (b * per + i, OFF_GQ_Q // GQA_W)),
                  kv(OFF_GQ_K), kv(OFF_GQ_V), cache, cache],
        out_specs=pl.BlockSpec((GQA_TQ, GQA_W), lambda b, i: (b * per + i, 0)),
        out_shape=jax.ShapeDtypeStruct((N_LAT_TOK, GQA_W), F32),
        compiler_params=_cparams("arbitrary", "arbitrary"),
        name="lat_gqa",
    )(z, z, z, cache_k, cache_v)


NA_RB = 4
NA_UNION = NA_RB + NA_WIN_ROWS
NA_TQ = NA_RB * GRID_W
NA_TK = NA_UNION * GRID_W
NA_TILES = 2 * NA_WIN_ROWS


def _na_union_start(i):
    return jnp.clip(i * NA_RB - NA_WIN_ROWS // 2, 0, GRID_H - NA_UNION)


def _na_tile_ids(i):
    u0 = _na_union_start(i)
    ids = []
    for t in range(NA_RB):
        r = i * NA_RB + t
        rs = jnp.clip(r - NA_WIN_ROWS // 2, 0, GRID_H - NA_WIN_ROWS)
        row_ids = []
        for j in range(NA_UNION):
            kr = u0 + j
            inside = (kr >= rs) & (kr < rs + NA_WIN_ROWS)
            row_ids.append(jnp.where(inside, kr - r + NA_WIN_ROWS - 1, NA_TILES - 1))
        ids.append(row_ids)
    return ids


def _na_bias(tab_ref, head, ids):
    rows = [jnp.concatenate([tab_ref[head, 0, row_ids[j]] + tab_ref[head, 1, row_ids[j + 1]]
                             for j in range(0, NA_UNION, 2)], axis=1) for row_ids in ids]
    return jnp.concatenate(rows, axis=0)


def _lat_na_kernel(q_ref, k_ref, v_ref, kc_ref, vc_ref, tab_ref, o_ref):
    i = pl.program_id(1)
    start = pl.multiple_of(_na_union_start(i) * GRID_W, GRID_W)
    low = _low_half(NA_TQ)
    ids = _na_tile_ids(i)
    for p in range(NA_W // LANE):
        cols = slice(p * LANE, (p + 1) * LANE)
        q = q_ref[:, cols] * ATTN_SCALE
        kl = k_ref[pl.ds(start, NA_TK), cols].astype(BF16)
        vl = v_ref[pl.ds(start, NA_TK), cols].astype(BF16)
        kc = kc_ref[0, :, cols].astype(BF16)
        vc = vc_ref[0, :, cols].astype(BF16)
        outs = []
        for half in range(2):
            qm = jnp.where(low if half == 0 else ~low, q, 0.0).astype(BF16)
            s_loc = _dot_nt(qm, kl) + _na_bias(tab_ref, 2 * p + half, ids)
            s_ctx = _dot_nt(qm, kc)
            outs.append(_softmax_pv([s_loc, s_ctx], [vl, vc]))
        o_ref[:, cols] = jnp.where(low, outs[0], outs[1])


def na_bias_table(rpb):
    cq = np.arange(GRID_W)
    ck = np.arange(GRID_W)
    dc = np.clip(ck[None, :] - cq[:, None], -(NA_WIN_COLS - 1), NA_WIN_COLS - 1) + NA_WIN_COLS - 1
    col_start = np.clip(cq - NA_WIN_COLS // 2, 0, GRID_W - NA_WIN_COLS)
    in_win = (ck[None, :] >= col_start[:, None]) & (ck[None, :] < col_start[:, None] + NA_WIN_COLS)
    col_sel = (dc[:, :, None] == np.arange(2 * NA_WIN_COLS - 1)).astype(np.float32)
    tiles = jnp.einsum("lhrc,qkc->lhrqk", rpb.astype(F32), col_sel, precision=lax.Precision.HIGHEST)
    tiles = jnp.where(in_win, tiles, NEG)
    masked = jnp.full(tiles.shape[:2] + (1, GRID_W, GRID_W), NEG, F32)
    tiles = jnp.concatenate([tiles, masked], axis=2)
    zero = jnp.zeros_like(tiles)
    return jnp.stack([jnp.concatenate([tiles, zero], -1), jnp.concatenate([zero, tiles], -1)], axis=2)


def lat_na(z, cache_k, cache_v, tab, layer):
    per = GRID_H // NA_RB
    kv = lambda off: pl.BlockSpec((DEC_SEQ, NA_W), lambda b, i: (b, off // NA_W))
    cache = pl.BlockSpec((1, None, PAST_LEN, NA_W), lambda b, i: (b, layer, 0, 0))
    return pl.pallas_call(
        _lat_na_kernel,
        grid=(DEC_BATCH, per),
        in_specs=[pl.BlockSpec((NA_TQ, NA_W), lambda b, i: (b * per + i, 0)),
                  kv(OFF_NA_K), kv(OFF_NA_V), cache, cache,
                  pl.BlockSpec((None, NA_HEADS, 2, NA_TILES, GRID_W, LANE), lambda b, i: (layer, 0, 0, 0, 0, 0))],
        out_specs=pl.BlockSpec((NA_TQ, NA_W), lambda b, i: (b * per + i, 0)),
        out_shape=jax.ShapeDtypeStruct((N_LAT_TOK, NA_W), F32),
        compiler_params=_cparams("arbitrary", "arbitrary"),
        name="lat_na",
    )(z, z, z, cache_k, cache_v, tab)


def _log_sigmoid(x):
    return jnp.minimum(x, 0.0) - jnp.log1p(jnp.exp(-jnp.abs(x)))


def _split3(x):
    hi = x.astype(BF16)
    r = x - hi.astype(F32)
    mid = r.astype(BF16)
    lo = (r - mid.astype(F32)).astype(BF16)
    return hi, mid, lo


def _mlstm_gate_tables(gates, fwd):
    t = gates.shape[0]
    jj = lax.broadcasted_iota(jnp.int32, (t, t), 0)
    ss = lax.broadcasted_iota(jnp.int32, (t, t), 1)
    feeds = (jj <= ss) if fwd else (jj >= ss)
    feeds_b = jnp.where(feeds, 1.0, 0.0).astype(BF16)
    reached_b = jnp.where((ss <= jj) if fwd else (ss >= jj), 1.0, 0.0).astype(BF16)
    eye_b = jnp.where(jj == ss, 1.0, 0.0).astype(BF16)
    lf = _split3(_log_sigmoid(gates))
    gs = _split3(gates)
    b_col = _dot(reached_b, lf[0]) + _dot(reached_b, lf[1]) + _dot(reached_b, lf[2])
    b_row = _dot_tn(lf[0], feeds_b) + _dot_tn(lf[1], feeds_b) + _dot_tn(lf[2], feeds_b)
    ig_row = _dot_tn(gs[0], eye_b) + _dot_tn(gs[1], eye_b) + _dot_tn(gs[2], eye_b)
    return ig_row, b_row, b_col, feeds


def _mlstm_head(k, qt, vt, kt, u, b_row, ig_row, feeds, ct, n, m, half, fwd):
    t = k.shape[0]
    dmat = jnp.where(feeds, b_row + u, NEG)
    m_inter = b_row + m
    m_j = jnp.maximum(m_inter, jnp.max(dmat, axis=0, keepdims=True))
    w = jnp.exp(dmat - m_j)
    decay = jnp.exp(m_inter - m_j)
    first = lax.broadcasted_iota(jnp.int32, (LANE, t), 0) < HEAD_DIM
    mine = first if half == 0 else ~first
    qm = jnp.where(mine, qt, 0.0)
    qb = qm.astype(BF16)
    kb = k.astype(BF16)
    qk = _dot(kb, qb) * w
    num = decay * _dot(ct.astype(BF16), qb) + _dot(vt.astype(BF16), qk.astype(BF16))
    den = decay * jnp.sum(qm * n, axis=0, keepdims=True) + jnp.sum(qk, axis=0, keepdims=True)
    h = jnp.where(mine, num * (1.0 / jnp.maximum(jnp.abs(den), jnp.exp(-m_j))), 0.0)
    last = t - 1 if fwd else 0
    b_last = b_row[:, last:last + 1]
    g = b_last - b_row + ig_row
    m_new = jnp.maximum(b_last + m, jnp.max(g, axis=1, keepdims=True))
    ws = jnp.exp(g - m_new)
    d_last = jnp.exp(b_last + m - m_new)
    ct_add = _dot((vt * ws).astype(BF16), kb)
    n_add = jnp.sum(jnp.where(mine, kt * ws, 0.0), axis=1, keepdims=True)
    return h, d_last, ct_add, n_add, m_new


def _mlstm_kernel(zf_ref, zb_ref, *refs, has_init):
    if has_init:
        c0_ref, n0_ref, m0_ref, hf_ref, hb_ref, co_ref, no_ref, mo_ref, c_s, n_s, m_s = refs
    else:
        hf_ref, hb_ref, co_ref, no_ref, mo_ref, c_s, n_s, m_s = refs
    chunk = pl.program_id(1)
    pairs = ML_W // LANE
    zero = jnp.zeros((HEAD_DIM, HEAD_DIM), F32)

    ri = lax.broadcasted_iota(jnp.int32, (LANE, LANE), 0)
    ci = lax.broadcasted_iota(jnp.int32, (LANE, LANE), 1)
    eye = ri == ci
    first_block = (ri < HEAD_DIM) & (ci < HEAD_DIM)
    second_block = (ri >= HEAD_DIM) & (ci >= HEAD_DIM)
    first_rows = lax.broadcasted_iota(jnp.int32, (LANE, 1), 0) < HEAD_DIM

    @pl.when(chunk == 0)
    def _():
        if not has_init:
            c_s[...] = jnp.zeros_like(c_s)
            n_s[...] = jnp.zeros_like(n_s)
            m_s[...] = jnp.zeros_like(m_s)
            return
        for d in range(2):
            for p in range(pairs):
                top = jnp.concatenate([c0_ref[0, d, 2 * p], zero], axis=1)
                bot = jnp.concatenate([zero, c0_ref[0, d, 2 * p + 1]], axis=1)
                c_s[d * pairs + p] = jnp.concatenate([top, bot], axis=0).T
                e = d * ML_HEADS + 2 * p
                n_row = jnp.concatenate([n0_ref[0, e:e + 1, :], n0_ref[0, e + 1:e + 2, :]], axis=1)
                n_s[d * pairs + p] = jnp.sum(jnp.where(eye, n_row, 0.0), axis=1, keepdims=True)
            m_s[d] = m0_ref[0, d * ML_HEADS:(d + 1) * ML_HEADS, :]

    for d, (z_ref, h_ref) in enumerate(((zf_ref, hf_ref), (zb_ref, hb_ref))):
        gates = z_ref[:, OFF_ML_G - OFF_ML_Q:OFF_ML_G - OFF_ML_Q + LANE]
        ig_rows, b_rows, b_cols, feeds = _mlstm_gate_tables(gates, d == 0)
        u = gates - pltpu.roll(b_cols, LANE - ML_HEADS, axis=1)
        m_all = m_s[d]
        m_rows = []
        for p in range(pairs):
            k = z_ref[:, ML_W + p * LANE:ML_W + (p + 1) * LANE] * ATTN_SCALE
            qt = z_ref[:, p * LANE:(p + 1) * LANE].T
            vt = z_ref[:, 2 * ML_W + p * LANE:2 * ML_W + (p + 1) * LANE].T
            kt = k.T
            ct = c_s[d * pairs + p]
            n = n_s[d * pairs + p]
            res = []
            for half in range(2):
                hd = 2 * p + half
                gi = 2 * ML_HEADS * d + hd
                gf = gi + ML_HEADS
                res.append(_mlstm_head(k, qt, vt, kt, u[:, gi:gi + 1], b_rows[gf:gf + 1, :], ig_rows[gi:gi + 1, :],
                                       feeds, ct, n, m_all[hd:hd + 1, 0:1], half, d == 0))
            (h0, dl0, ca0, na0, mn0), (h1, dl1, ca1, na1, mn1) = res
            h_ref[:, p * LANE:(p + 1) * LANE] = (h0 + h1).T
            c_s[d * pairs + p] = jnp.where(first_block, dl0 * ct + ca0, jnp.where(second_block, dl1 * ct + ca1, 0.0))
            n_s[d * pairs + p] = jnp.where(first_rows, dl0 * n + na0, dl1 * n + na1)
            m_rows += [jnp.broadcast_to(mn0, (1, LANE)), jnp.broadcast_to(mn1, (1, LANE))]
        m_s[d] = jnp.concatenate(m_rows, axis=0)

    @pl.when(chunk == pl.num_programs(1) - 1)
    def _():
        for d in range(2):
            for p in range(pairs):
                c = c_s[d * pairs + p].T
                n_row = jnp.sum(jnp.where(eye, n_s[d * pairs + p], 0.0), axis=0, keepdims=True)
                co_ref[0, d, 2 * p] = c[:HEAD_DIM, :HEAD_DIM]
                co_ref[0, d, 2 * p + 1] = c[HEAD_DIM:, HEAD_DIM:]
                e = d * ML_HEADS + 2 * p
                no_ref[0, e:e + 1, :] = n_row[:, :HEAD_DIM]
                no_ref[0, e + 1:e + 2, :] = n_row[:, HEAD_DIM:]
            mo_ref[0, d * ML_HEADS:(d + 1) * ML_HEADS, :] = m_s[d]


def mlstm(z, batch, seq, init=None, layer=None):
    nc = seq // ML_CHUNK
    zcol = OFF_ML_Q // ML_BLOCK_W
    state_c = pl.BlockSpec((1, 2, ML_HEADS, HEAD_DIM, HEAD_DIM), lambda b, c: (b, 0, 0, 0, 0))
    state_n = pl.BlockSpec((1, 2 * ML_HEADS, HEAD_DIM), lambda b, c: (b, 0, 0))
    state_m = pl.BlockSpec((1, 2 * ML_HEADS, LANE), lambda b, c: (b, 0, 0))
    init_specs = [] if init is None else [
        pl.BlockSpec((1, None, 2, ML_HEADS, HEAD_DIM, HEAD_DIM), lambda b, c: (b, layer, 0, 0, 0, 0)),
        pl.BlockSpec((1, None, 2 * ML_HEADS, HEAD_DIM), lambda b, c: (b, layer, 0, 0)),
        pl.BlockSpec((1, None, 2 * ML_HEADS, LANE), lambda b, c: (b, layer, 0, 0)),
    ]
    pairs = ML_W // LANE
    return pl.pallas_call(
        functools.partial(_mlstm_kernel, has_init=init is not None),
        grid=(batch, nc),
        in_specs=[
            pl.BlockSpec((ML_CHUNK, ML_BLOCK_W), lambda b, c: (b * nc + c, zcol)),
            pl.BlockSpec((ML_CHUNK, ML_BLOCK_W), lambda b, c: (b * nc + nc - 1 - c, zcol)),
        ] + init_specs,
        out_specs=[
            pl.BlockSpec((ML_CHUNK, ML_W), lambda b, c: (b * nc + c, 0)),
            pl.BlockSpec((ML_CHUNK, ML_W), lambda b, c: (b * nc + nc - 1 - c, 0)),
            state_c, state_n, state_m,
        ],
        out_shape=[
            jax.ShapeDtypeStruct((batch * seq, ML_W), F32),
            jax.ShapeDtypeStruct((batch * seq, ML_W), F32),
            jax.ShapeDtypeStruct((batch, 2, ML_HEADS, HEAD_DIM, HEAD_DIM), F32),
            jax.ShapeDtypeStruct((batch, 2 * ML_HEADS, HEAD_DIM), F32),
            jax.ShapeDtypeStruct((batch, 2 * ML_HEADS, LANE), F32),
        ],
        scratch_shapes=[
            pltpu.VMEM((2 * pairs, LANE, LANE), F32),
            pltpu.VMEM((2 * pairs, LANE, 1), F32),
            pltpu.VMEM((2, ML_HEADS, LANE), F32),
        ],
        compiler_params=_cparams("arbitrary", "arbitrary"),
        name="mlstm",
    )(z, z, *(() if init is None else init))


def _layer_norm(u, g, b):
    mu = jnp.mean(u, axis=-1, keepdims=True)
    uc = u - mu
    var = jnp.mean(uc * uc, axis=-1, keepdims=True)
    return uc * lax.rsqrt(var + NORM_EPS) * g + b


def _outproj_kernel(x_ref, a_ref, hf_ref, hb_ref, og0_ref, og1_ref, c_ref, w_ref, mg_ref, g1_ref, sh2_ref, sc2_ref,
                    lng_ref, lnb_ref, rw_ref, x1_ref, h2_ref, lg_ref):
    low = _low_half(TM)
    mixed_b = []
    for p, og_ref in enumerate((og0_ref, og1_ref)):
        cols = slice(p * LANE, (p + 1) * LANE)
        h = hf_ref[:, cols] + hb_ref[:, cols]
        hc = h - _pair_mean(h, low)
        hn = hc * lax.rsqrt(_pair_mean(hc * hc, low) + NORM_EPS)
        mixed_b.append(hn * mg_ref[:, cols] * _sigmoid(og_ref[...]))
    mixed = jnp.concatenate([a_ref[...]] + mixed_b + [c_ref[...]], axis=-1).astype(BF16)
    y = _dot(mixed, w_ref[...])
    x1 = _layer_norm(DEEPNORM_ALPHA * x_ref[...] + g1_ref[0] * y, lng_ref[...], lnb_ref[...])
    h2 = (x1 * (1.0 + sc2_ref[0]) + sh2_ref[0]).astype(BF16)
    x1_ref[...] = x1
    h2_ref[...] = h2
    lg_ref[...] = _dot_nt(rw_ref[...], h2)


def out_projection(x, out_a, hf, hb, z, out_c, layer, w_out, ml_gain, mods, mod_row, ln_g, ln_b, router_t):
    n = x.shape[0]
    mod = lambda k: pl.BlockSpec((None, 1, 1, D_MODEL), lambda i: (layer, mod_row(i), 0, k))
    row = lambda w: pl.BlockSpec((TM, w), lambda i: (i, 0))
    zcol = lambda off: pl.BlockSpec((TM, LANE), lambda i: (i, off // LANE))
    per_layer = lambda r, c: pl.BlockSpec((None, r, c), lambda i: (layer, 0, 0))
    norm0 = pl.BlockSpec((None, None, 1, D_MODEL), lambda i: (layer, 0, 0, 0))
    return pl.pallas_call(
        _outproj_kernel,
        grid=(n // TM,),
        in_specs=[row(D_MODEL), row(NA_W), row(ML_W), row(ML_W), zcol(OFF_ML_O), zcol(OFF_ML_O + LANE), row(GQA_W),
                  per_layer(D_MODEL, D_MODEL), per_layer(1, ML_W), mod(2), mod(3), mod(4), norm0, norm0,
                  per_layer(N_EXPERTS, D_MODEL)],
        out_specs=[row(D_MODEL), row(D_MODEL), pl.BlockSpec((N_EXPERTS, TM), lambda i: (0, i))],
        out_shape=[
            jax.ShapeDtypeStruct((n, D_MODEL), F32),
            jax.ShapeDtypeStruct((n, D_MODEL), BF16),
            jax.ShapeDtypeStruct((N_EXPERTS, n), F32),
        ],
        compiler_params=_cparams("arbitrary"),
        name="out_projection",
    )(x, out_a, hf, hb, z, z, out_c, w_out, ml_gain, mods, mods, mods, ln_g, ln_b, router_t)


ROUTE_BLK = 256


def _prefix_count(x, tri):
    n = x.shape[1]
    outs = []
    carry = jnp.zeros((N_EXPERTS, 1), F32)
    for i in range(n // ROUTE_BLK):
        blk = x[:, i * ROUTE_BLK:(i + 1) * ROUTE_BLK]
        outs.append(_dot(blk, tri) + carry)
        carry = carry + jnp.sum(blk.astype(F32), axis=1, keepdims=True)
    return jnp.concatenate(outs, axis=1) if len(outs) > 1 else outs[0]


def _route_kernel(lg_ref, coder_ref, codet_ref, afft_ref, *, n, cap):
    sets = lg_ref.shape[1] // n
    lg = lg_ref[...]
    ex = jnp.exp(lg - jnp.max(lg, axis=0, keepdims=True))
    aff_all = ex / jnp.sum(ex, axis=0, keepdims=True)
    affs = [aff_all[:, s * n:(s + 1) * n] for s in range(sets)]

    def search(i, bits):
        bit = lax.shift_left(jnp.int32(1), 30 - i)
        out = []
        for aff, b in zip(affs, bits):
            cand = b | bit
            cnt = jnp.sum(jnp.where(aff >= pltpu.bitcast(cand, F32), 1.0, 0.0), axis=1, keepdims=True)
            out.append(jnp.where(cnt >= cap, cand, b))
        return tuple(out)

    floors = lax.fori_loop(0, 31, search, tuple(jnp.zeros((N_EXPERTS, 1), jnp.int32) for _ in range(sets)))
    ti = lax.broadcasted_iota(jnp.int32, (ROUTE_BLK, ROUTE_BLK), 0)
    tj = lax.broadcasted_iota(jnp.int32, (ROUTE_BLK, ROUTE_BLK), 1)
    tri = jnp.where(ti < tj, 1.0, 0.0).astype(BF16)
    pad = jnp.zeros((LANE - N_EXPERTS, n), F32)
    for s, (aff, floor_bits) in enumerate(zip(affs, floors)):
        thr = jnp.min(jnp.where(aff >= pltpu.bitcast(floor_bits, F32), aff, 2.0), axis=1, keepdims=True)
        gt = aff > thr
        eq = aff == thr
        need = cap - jnp.sum(jnp.where(gt, 1.0, 0.0), axis=1, keepdims=True)
        eq_rank = _prefix_count(jnp.where(eq, 1.0, 0.0).astype(BF16), tri)
        sel = jnp.where(gt, 1.0, jnp.where(eq & (eq_rank < need), 1.0, 0.0))
        pos = _prefix_count(sel.astype(BF16), tri)
        code = jnp.where(sel > 0.5, pos, -1.0)
        coder_ref[s] = code
        codet_ref[s] = jnp.concatenate([code, pad], axis=0).T
        afft_ref[s] = jnp.concatenate([aff, pad], axis=0).T


def route(logits_t, n, cap):
    tokens = logits_t.shape[1]
    sets = tokens // n
    whole = lambda shape: pl.BlockSpec(shape, lambda i: (0,) * len(shape))
    return pl.pallas_call(
        functools.partial(_route_kernel, n=n, cap=cap),
        grid=(1,),
        in_specs=[whole((N_EXPERTS, tokens))],
        out_specs=[whole((sets, N_EXPERTS, n)), whole((sets, n, LANE)), whole((sets, n, LANE))],
        out_shape=[
            jax.ShapeDtypeStruct((sets, N_EXPERTS, n), F32),
            jax.ShapeDtypeStruct((sets, n, LANE), F32),
            jax.ShapeDtypeStruct((sets, n, LANE), F32),
        ],
        compiler_params=_cparams("arbitrary"),
        name="route",
    )(logits_t)


def _gather_kernel(coder_ref, h_ref, xe_ref, *, cap, epb):
    e0 = pl.program_id(1) * epb
    n = coder_ref.shape[2]
    ci = lax.broadcasted_iota(jnp.int32, (cap, n), 0).astype(F32)
    onehot = [jnp.where(ci == coder_ref[0, pl.ds(e0 + k, 1), :], 1.0, 0.0).astype(BF16) for k in range(epb)]
    onehot = jnp.concatenate(onehot, axis=0) if epb > 1 else onehot[0]
    rows = _dot(onehot, h_ref[...]).astype(BF16)
    for k in range(epb):
        xe_ref[k] = rows[k * cap:(k + 1) * cap]


def gather_tokens(code_rows, h2, n, cap):
    sets = code_rows.shape[0]
    epb = max(1, 512 // cap)
    return pl.pallas_call(
        functools.partial(_gather_kernel, cap=cap, epb=epb),
        grid=(sets, N_EXPERTS // epb),
        in_specs=[
            pl.BlockSpec((1, N_EXPERTS, n), lambda s, e: (s, 0, 0)),
            pl.BlockSpec((n, D_MODEL), lambda s, e: (s, 0)),
        ],
        out_specs=pl.BlockSpec((epb, cap, D_MODEL), lambda s, e: (e, s, 0)),
        out_shape=jax.ShapeDtypeStruct((N_EXPERTS, sets * cap, D_MODEL), BF16),
        compiler_params=_cparams("arbitrary", "arbitrary"),
        name="gather_tokens",
    )(code_rows, h2)


def _expert_kernel(xi_ref, xl_ref, wg_ref, wu_ref, wd_ref, yi_ref, yl_ref, acci_ref, accl_ref):
    j = pl.program_id(1)

    @pl.when(j == 0)
    def _():
        acci_ref[...] = jnp.zeros_like(acci_ref)
        accl_ref[...] = jnp.zeros_like(accl_ref)

    xs = (xi_ref[0], xl_ref[0])

    sums = [None, None]
    for s0 in range(0, EXP_TF, EXP_SUB):
        cols = slice(s0, min(s0 + EXP_SUB, EXP_TF))
        wg = wg_ref[0, 0, :, cols].astype(BF16)
        wu = wu_ref[0, 0, :, cols].astype(BF16)
        wd = wd_ref[0, 0, cols, :].astype(BF16)
        for g in range(2):
            a = _dot(xs[g], wg)
            b = _dot(xs[g], wu)
            hid = (a * _sigmoid(a) * b).astype(BF16)
            part = _dot(hid, wd)
            sums[g] = part if sums[g] is None else sums[g] + part
    acci_ref[...] += sums[0]
    accl_ref[...] += sums[1]

    @pl.when(j == pl.num_programs(1) - 1)
    def _():
        yi_ref[0] = acci_ref[...].astype(BF16)
        yl_ref[0] = accl_ref[...].astype(BF16)


def experts(xe_ctx, xe_lat, w_gate, w_up, w_down, layer):
    xin = lambda s: pl.BlockSpec((1, s, D_MODEL), lambda e, j: (e, 0, 0))
    return pl.pallas_call(
        _expert_kernel,
        grid=(N_EXPERTS, EXPERT_HIDDEN // EXP_TF),
        in_specs=[
            xin(SLOTS_CTX), xin(SLOTS_LAT),
            pl.BlockSpec((1, 1, D_MODEL, EXP_TF), lambda e, j: (layer, e, 0, j)),
            pl.BlockSpec((1, 1, D_MODEL, EXP_TF), lambda e, j: (layer, e, 0, j)),
            pl.BlockSpec((1, 1, EXP_TF, D_MODEL), lambda e, j: (layer, e, j, 0)),
        ],
        out_specs=[xin(SLOTS_CTX), xin(SLOTS_LAT)],
        out_shape=[
            jax.ShapeDtypeStruct((N_EXPERTS, SLOTS_CTX, D_MODEL), BF16),
            jax.ShapeDtypeStruct((N_EXPERTS, SLOTS_LAT, D_MODEL), BF16),
        ],
        scratch_shapes=[pltpu.VMEM((SLOTS_CTX, D_MODEL), F32), pltpu.VMEM((SLOTS_LAT, D_MODEL), F32)],
        compiler_params=_cparams("arbitrary", "arbitrary"),
        name="experts",
    )(xe_ctx, xe_lat, w_gate, w_up, w_down)


COMB_TT = 256


def _scatter_per_expert(code, aff, ye_ref, cap):
    tt = code.shape[0]
    li = lax.broadcasted_iota(jnp.int32, (tt, cap), 1).astype(F32)
    acc = jnp.zeros((tt, D_MODEL), F32)
    for e in range(N_EXPERTS):
        onehot = jnp.where(li == code[:, e:e + 1], 1.0, 0.0).astype(BF16)
        acc = acc + aff[:, e:e + 1] * _dot(onehot, ye_ref[e])
    return acc


def _scatter_merged(code, aff, ye_ref, cap):
    tt = code.shape[0]
    slots = N_EXPERTS * cap
    shift = cap.bit_length() - 1
    ei = lax.broadcasted_iota(jnp.int32, (LANE, slots), 0)
    si = lax.broadcasted_iota(jnp.int32, (LANE, slots), 1)
    expand = jnp.where(lax.shift_right_logical(si, shift) == ei, 1.0, 0.0).astype(BF16)
    slot = (lax.broadcasted_iota(jnp.int32, (tt, slots), 1) & (cap - 1)).astype(F32)
    hit = _dot(code.astype(BF16), expand) == slot
    a_hi = aff.astype(BF16)
    a_lo = (aff - a_hi.astype(F32)).astype(BF16)
    ye = ye_ref[...].reshape(slots, D_MODEL)
    acc = _dot(jnp.where(hit, _dot(a_hi, expand), 0.0).astype(BF16), ye)
    return acc + _dot(jnp.where(hit, _dot(a_lo, expand), 0.0).astype(BF16), ye)


def _combine_kernel(code_ref, aff_ref, ye_ref, x1_ref, g2_ref, lng_ref, lnb_ref, o_ref, *, cap):
    scatter = _scatter_merged if cap < LANE else _scatter_per_expert
    acc = scatter(code_ref[0], aff_ref[0], ye_ref, cap)
    u = DEEPNORM_ALPHA * x1_ref[...] + g2_ref[0] * acc
    o_ref[...] = _layer_norm(u, lng_ref[...], lnb_ref[...])


def combine(code, aff, ye, x1, layer, mods, set_mod_row, ln_g, ln_b, n, cap):
    sets = code.shape[0]
    per = n // COMB_TT
    tok = pl.BlockSpec((1, COMB_TT, LANE), lambda s, i: (s, i, 0))
    norm1 = pl.BlockSpec((None, None, 1, D_MODEL), lambda s, i: (layer, 1, 0, 0))
    return pl.pallas_call(
        functools.partial(_combine_kernel, cap=cap),
        grid=(sets, per),
        in_specs=[
            tok, tok,
            pl.BlockSpec((N_EXPERTS, cap, D_MODEL), lambda s, i: (0, s, 0)),
            pl.BlockSpec((COMB_TT, D_MODEL), lambda s, i: (s * per + i, 0)),
            pl.BlockSpec((None, 1, 1, D_MODEL), lambda s, i: (layer, set_mod_row(s), 0, 5)),
            norm1, norm1,
        ],
        out_specs=pl.BlockSpec((COMB_TT, D_MODEL), lambda s, i: (s * per + i, 0)),
        out_shape=jax.ShapeDtypeStruct((sets * n, D_MODEL), F32),
        compiler_params=_cparams("arbitrary", "arbitrary"),
        name="combine",
    )(code, aff, ye, x1, mods, ln_g, ln_b)


def _axial_rope_tables():
    t = np.arange(DEC_SEQ)
    row = (t // GRID_W).astype(np.float32)
    col = (t % GRID_W).astype(np.float32)
    half = HEAD_DIM // 2
    inv = (ROPE_THETA ** (-np.arange(0, half, 2, dtype=np.float32) / half)).astype(np.float32)
    ang_r = row[:, None] * inv
    ang_c = col[:, None] * inv
    ang = np.concatenate([ang_r, ang_r, ang_c, ang_c] * 2, -1)
    return jnp.asarray(np.cos(ang), F32), jnp.asarray(np.sin(ang), F32)


def _ctx_mod_row(i):
    return 0


def _lat_tile_mod_row(i):
    return 1 + i // (DEC_SEQ // TM)


def _lat_set_mod_row(s):
    return 1 + s


def kernel(x_prompt, x_sample, c, cache_na_k, cache_na_v, cache_gqa_k, cache_gqa_v, state_mlstm_c, state_mlstm_n,
           state_mlstm_m, c_ctx, ada_w, ada_b, w_in, b_gate, w_out, na_rpb, qk_norm_g, ml_norm_g, ln_g, ln_b,
           router_w, w_gate, w_up, w_down):
    cond8 = jnp.concatenate([c_ctx[None, :], c, jnp.zeros((8 - 1 - DEC_BATCH, D_MODEL), F32)], 0)
    mods = adaln(cond8, ada_w, ada_b).reshape(DEPTH, 8, 1, 6 * D_MODEL)
    xc = x_prompt.reshape(N_CTX_TOK, D_MODEL)
    xl = x_sample.reshape(N_LAT_TOK, D_MODEL)
    cos, sin = _axial_rope_tables()
    na_tab = na_bias_table(na_rpb)

    split = OFF_ML_G + N_GATES
    w_pad = jnp.concatenate([w_in[:, :, :split], jnp.zeros((DEPTH, D_MODEL, LANE - N_GATES), F32),
                             w_in[:, :, split:]], 2).astype(BF16)
    bias_row = jnp.pad(b_gate, ((0, 0), (OFF_ML_G, PROJ_PAD - OFF_ML_G - N_GATES))).reshape(DEPTH, 1, PROJ_PAD)
    gains = jnp.tile(qk_norm_g, (1, 1, 2)).reshape(DEPTH, 2, 1, LANE)
    w_out_b = w_out.astype(BF16)
    router_t = jnp.swapaxes(router_w, 1, 2).astype(BF16)
    ml_gain = ml_norm_g.reshape(DEPTH, 1, ML_W)
    ln_g4 = ln_g.reshape(DEPTH, 2, 1, D_MODEL)
    ln_b4 = ln_b.reshape(DEPTH, 2, 1, D_MODEL)
    na_ck = cache_na_k.reshape(DEC_BATCH, DEPTH, PAST_LEN, NA_W)
    na_cv = cache_na_v.reshape(DEC_BATCH, DEPTH, PAST_LEN, NA_W)
    gq_ck = cache_gqa_k.reshape(DEC_BATCH, DEPTH, PAST_LEN, GQA_KV_W)
    gq_cv = cache_gqa_v.reshape(DEC_BATCH, DEPTH, PAST_LEN, GQA_KV_W)
    lat_n0 = state_mlstm_n.reshape(DEC_BATCH, DEPTH, 2 * ML_HEADS, HEAD_DIM)
    lat_m0 = jnp.broadcast_to(state_mlstm_m.reshape(DEC_BATCH, DEPTH, 2 * ML_HEADS, 1),
                              (DEC_BATCH, DEPTH, 2 * ML_HEADS, LANE))
    kv, states = None, []

    for l in range(DEPTH):
        zc, *kv = in_projection(xc, l, mods, _ctx_mod_row, w_pad, bias_row, gains, kv_prev=kv)
        zl = in_projection(xl, l, mods, _lat_tile_mod_row, w_pad, bias_row, gains, cos, sin)

        out_a_c, out_c_c = ctx_attention(zc)
        hf_c, hb_c, sc, sn, sm = mlstm(zc, BATCH, SEQ)
        out_a_l = lat_na(zl, na_ck, na_cv, na_tab, l)
        out_c_l = lat_gqa(zl, gq_ck, gq_cv, l)
        hf_l, hb_l = mlstm(zl, DEC_BATCH, DEC_SEQ, (state_mlstm_c, lat_n0, lat_m0), l)[:2]
        states.append((sc, sn, sm))

        x1_c, h2_c, lg_c = out_projection(xc, out_a_c, hf_c, hb_c, zc, out_c_c, l, w_out_b, ml_gain, mods,
                                          _ctx_mod_row, ln_g4, ln_b4, router_t)
        x1_l, h2_l, lg_l = out_projection(xl, out_a_l, hf_l, hb_l, zl, out_c_l, l, w_out_b, ml_gain, mods,
                                          _lat_tile_mod_row, ln_g4, ln_b4, router_t)
        crow_c, code_c, aff_c = route(lg_c, SEQ, CAP_CTX)
        crow_l, code_l, aff_l = route(lg_l, DEC_SEQ, CAP_LAT)
        xe_c = gather_tokens(crow_c, h2_c, SEQ, CAP_CTX)
        xe_l = gather_tokens(crow_l, h2_l, DEC_SEQ, CAP_LAT)
        ye_c, ye_l = experts(xe_c, xe_l, w_gate, w_up, w_down, l)
        xc = combine(code_c, aff_c, ye_c, x1_c, l, mods, _ctx_mod_row, ln_g4, ln_b4, SEQ, CAP_CTX)
        xl = combine(code_l, aff_l, ye_l, x1_l, l, mods, _lat_set_mod_row, ln_g4, ln_b4, DEC_SEQ, CAP_LAT)

    y_prompt = xc.reshape(BATCH, SEQ, D_MODEL)
    y_sample = xl.reshape(DEC_BATCH, DEC_SEQ, D_MODEL)
    na_kv, gq_kv = kv
    heads = lambda a, h: a.reshape(BATCH, DEPTH, SEQ, h, HEAD_DIM)
    new_c = jnp.stack([s[0] for s in states], 1)
    new_n = jnp.stack([s[1] for s in states], 1).reshape(BATCH, DEPTH, 2, ML_HEADS, HEAD_DIM)
    new_m = jnp.stack([s[2][:, :, 0] for s in states], 1).reshape(BATCH, DEPTH, 2, ML_HEADS)
    return (y_prompt, y_sample, heads(na_kv[..., :NA_W], NA_HEADS), heads(na_kv[..., NA_W:], NA_HEADS),
            heads(gq_kv[..., :GQA_KV_W], GQA_KV_HEADS), heads(gq_kv[..., GQA_KV_W:], GQA_KV_HEADS),
            new_c, new_n, new_m)
```

```python
import functools

import jax
import jax.numpy as jnp
import numpy as np
from jax import lax
from jax.experimental import pallas as pl
from jax.experimental.pallas import tpu as pltpu

D_MODEL = 1024
BATCH = 16
SEQ = 256
DEPTH = 2
DEC_BATCH = 2
DEC_SEQ = 2048
PAST_LEN = 256
GRID_W = 64
GRID_H = DEC_SEQ // GRID_W
HEAD_DIM = 64
NA_HEADS = 6
NA_WIN_ROWS = 8
NA_WIN_COLS = 16
ML_HEADS = 4
GQA_HEADS = 6
GQA_KV_HEADS = 2
ROPE_THETA = 10000.0
N_EXPERTS = 16
EC_CAPACITY = 2
EXPERT_HIDDEN = 2816
NORM_EPS = 1e-6
NA_W = NA_HEADS * HEAD_DIM
ML_W = ML_HEADS * HEAD_DIM
GQA_W = GQA_HEADS * HEAD_DIM
GQA_KV_W = GQA_KV_HEADS * HEAD_DIM
N_GATES = 4 * ML_HEADS
DEEPNORM_ALPHA = (2 * DEPTH) ** 0.25
ATTN_SCALE = HEAD_DIM ** -0.5
F32 = jnp.float32
BF16 = jnp.bfloat16

LANE = 128
N_CTX_TOK = BATCH * SEQ
N_LAT_TOK = DEC_BATCH * DEC_SEQ
NEG = -1e30

OFF_NA_Q = 0
OFF_NA_K = OFF_NA_Q + NA_W
OFF_NA_V = OFF_NA_K + NA_W
OFF_ML_Q = OFF_NA_V + NA_W
OFF_ML_K = OFF_ML_Q + ML_W
OFF_ML_V = OFF_ML_K + ML_W
OFF_ML_O = OFF_ML_V + ML_W
OFF_ML_G = OFF_ML_O + ML_W
OFF_GQ_Q = OFF_ML_G + LANE
OFF_GQ_K = OFF_GQ_Q + GQA_W
OFF_GQ_V = OFF_GQ_K + GQA_KV_W
PROJ_PAD = OFF_GQ_V + GQA_KV_W
ML_BLOCK_W = OFF_GQ_Q - OFF_ML_Q

TM = 512
TM_OUT = 1024
ML_CHUNK = 256
EXP_SUB = 256
EXP_TF = EXPERT_HIDDEN // 2
CAP_CTX = EC_CAPACITY * SEQ // N_EXPERTS
CAP_LAT = EC_CAPACITY * DEC_SEQ // N_EXPERTS
SLOTS_CTX = BATCH * CAP_CTX
SLOTS_LAT = DEC_BATCH * CAP_LAT
VMEM_LIMIT = 56 * 1024 * 1024


def _cparams(*sem):
    return pltpu.CompilerParams(dimension_semantics=sem, vmem_limit_bytes=VMEM_LIMIT)


def _sigmoid(x):
    return 1.0 / (1.0 + jnp.exp(-x))


def _dot(a, b):
    return jnp.dot(a, b, preferred_element_type=F32)


def _dot_nt(a, b):
    return lax.dot_general(a, b, (((1,), (1,)), ((), ())), preferred_element_type=F32)


def _dot_tn(a, b):
    return lax.dot_general(a, b, (((0,), (0,)), ((), ())), preferred_element_type=F32)


def _low_half(rows):
    return lax.broadcasted_iota(jnp.int32, (rows, LANE), 1) < HEAD_DIM


def _pair_mean(x, low):
    s_lo = jnp.sum(jnp.where(low, x, 0.0), axis=-1, keepdims=True)
    s_hi = jnp.sum(jnp.where(low, 0.0, x), axis=-1, keepdims=True)
    return jnp.where(low, s_lo, s_hi) * (1.0 / HEAD_DIM)


def _adaln_kernel(c_ref, w_ref, b_ref, o_ref):
    c = c_ref[...]
    s = c * _sigmoid(c)
    o_ref[0] = _dot(s.astype(BF16), w_ref[0].astype(BF16)) + b_ref[0]


def adaln(cond8, ada_w, ada_b):
    tn = 1536
    return pl.pallas_call(
        _adaln_kernel,
        grid=(DEPTH, 6 * D_MODEL // tn),
        in_specs=[
            pl.BlockSpec((8, D_MODEL), lambda l, j: (0, 0)),
            pl.BlockSpec((1, D_MODEL, tn), lambda l, j: (l, 0, j)),
            pl.BlockSpec((1, 1, tn), lambda l, j: (l, 0, j)),
        ],
        out_specs=pl.BlockSpec((1, 8, tn), lambda l, j: (l, 0, j)),
        out_shape=jax.ShapeDtypeStruct((DEPTH, 8, 6 * D_MODEL), F32),
        compiler_params=_cparams("arbitrary", "arbitrary"),
        name="adaln",
    )(cond8, ada_w, ada_b.reshape(DEPTH, 1, 6 * D_MODEL))


def _inproj_kernel(x_ref, sh_ref, sc_ref, w_ref, b_ref, gq_ref, gk_ref, *rest, rope, n_alias):
    if rope:
        cos_ref, sin_ref, z_ref = rest
    else:
        z_ref, nakv_ref, gqkv_ref = rest[n_alias:]
    h = x_ref[...] * (1.0 + sc_ref[0]) + sh_ref[0]
    z = _dot(h.astype(BF16), w_ref[...]) + b_ref[...]
    z_ref[...] = z
    rows = z.shape[0]
    low = _low_half(rows)
    if rope:
        lane = lax.broadcasted_iota(jnp.int32, (rows, LANE), 1)
        first_quarter = (lane & (HEAD_DIM // 4)) == 0

    def norm_pair(x, gain):
        y = x * lax.rsqrt(_pair_mean(x * x, low) + NORM_EPS) * gain
        if rope:
            rot = jnp.where(first_quarter, -pltpu.roll(y, LANE - HEAD_DIM // 4, axis=1),
                            pltpu.roll(y, HEAD_DIM // 4, axis=1))
            y = y * cos_ref[...] + rot * sin_ref[...]
        return y

    for p in range(GQA_W // LANE):
        c0 = OFF_GQ_Q + p * LANE
        z_ref[:, c0:c0 + LANE] = norm_pair(z[:, c0:c0 + LANE], gq_ref[...])
    k_normed = norm_pair(z[:, OFF_GQ_K:OFF_GQ_K + LANE], gk_ref[...])
    z_ref[:, OFF_GQ_K:OFF_GQ_K + LANE] = k_normed
    if not rope:
        for s in range(rows // SEQ):
            r = slice(s * SEQ, (s + 1) * SEQ)
            nakv_ref[s] = z[r, OFF_NA_K:OFF_NA_K + 2 * NA_W]
            gqkv_ref[s, :, :GQA_KV_W] = k_normed[r]
            gqkv_ref[s, :, GQA_KV_W:] = z[r, OFF_GQ_V:OFF_GQ_V + GQA_KV_W]


def in_projection(x, layer, mods, mod_row, w_pad, bias_row, gains, cos=None, sin=None, kv_prev=None):
    n = x.shape[0]
    rope = cos is not None
    in_specs = [
        pl.BlockSpec((TM, D_MODEL), lambda i: (i, 0)),
        pl.BlockSpec((None, 1, 1, D_MODEL), lambda i: (layer, mod_row(i), 0, 0)),
        pl.BlockSpec((None, 1, 1, D_MODEL), lambda i: (layer, mod_row(i), 0, 1)),
        pl.BlockSpec((None, D_MODEL, PROJ_PAD), lambda i: (layer, 0, 0)),
        pl.BlockSpec((None, 1, PROJ_PAD), lambda i: (layer, 0, 0)),
        pl.BlockSpec((None, None, 1, LANE), lambda i: (layer, 0, 0, 0)),
        pl.BlockSpec((None, None, 1, LANE), lambda i: (layer, 1, 0, 0)),
    ]
    args = [x, mods, mods, w_pad, bias_row, gains, gains]
    z_spec = pl.BlockSpec((TM, PROJ_PAD), lambda i: (i, 0))
    z_shape = jax.ShapeDtypeStruct((n, PROJ_PAD), F32)
    if rope:
        per = cos.shape[0] // TM
        in_specs += [pl.BlockSpec((TM, LANE), lambda i: (i % per, 0))] * 2
        args += [cos, sin]
        return pl.pallas_call(
            functools.partial(_inproj_kernel, rope=True, n_alias=0),
            grid=(n // TM,),
            in_specs=in_specs,
            out_specs=z_spec,
            out_shape=z_shape,
            compiler_params=_cparams("arbitrary"),
            name="in_projection_rope",
        )(*args)
    seqs = TM // SEQ
    kv_spec = lambda w: pl.BlockSpec((seqs, None, SEQ, w), lambda i: (i, layer, 0, 0))
    kv_shape = lambda w: jax.ShapeDtypeStruct((n // SEQ, DEPTH, SEQ, w), F32)
    aliases = {}
    if kv_prev is not None:
        aliases = {len(args): 1, len(args) + 1: 2}
        in_specs += [pl.BlockSpec(memory_space=pl.ANY)] * 2
        args += list(kv_prev)
    return pl.pallas_call(
        functools.partial(_inproj_kernel, rope=False, n_alias=len(aliases)),
        grid=(n // TM,),
        in_specs=in_specs,
        out_specs=[z_spec, kv_spec(2 * NA_W), kv_spec(2 * GQA_KV_W)],
        out_shape=[z_shape, kv_shape(2 * NA_W), kv_shape(2 * GQA_KV_W)],
        input_output_aliases=aliases,
        compiler_params=_cparams("arbitrary"),
        name="in_projection",
    )(*args)


def _softmax_pv(scores, values):
    m = jnp.max(scores[0], axis=-1, keepdims=True)
    for s in scores[1:]:
        m = jnp.maximum(m, jnp.max(s, axis=-1, keepdims=True))
    l = 0.0
    o = 0.0
    for s, v in zip(scores, values):
        p = jnp.exp(s - m)
        l = l + jnp.sum(p, axis=-1, keepdims=True)
        o = o + _dot(p.astype(BF16), v)
    return o * (1.0 / l)


def _ctx_attn_kernel(naq_ref, nak_ref, nav_ref, gq_ref, gk_ref, gv_ref, oa_ref, oc_ref):
    low = _low_half(SEQ)
    for p in range(NA_W // LANE):
        cols = slice(p * LANE, (p + 1) * LANE)
        q = naq_ref[:, cols] * ATTN_SCALE
        k = nak_ref[:, cols].astype(BF16)
        v = nav_ref[:, cols].astype(BF16)
        outs = []
        for half in range(2):
            qm = jnp.where(low if half == 0 else ~low, q, 0.0).astype(BF16)
            outs.append(_softmax_pv([_dot_nt(qm, k)], [v]))
        oa_ref[:, cols] = jnp.where(low, outs[0], outs[1])
    k = gk_ref[...]
    v = gv_ref[...]
    k_at = (pltpu.roll(k, HEAD_DIM, axis=1).astype(BF16), k.astype(BF16))
    v_at = (pltpu.roll(v, HEAD_DIM, axis=1).astype(BF16), v.astype(BF16))
    group = GQA_HEADS // GQA_KV_HEADS
    for p in range(GQA_W // LANE):
        cols = slice(p * LANE, (p + 1) * LANE)
        q = gq_ref[:, cols] * ATTN_SCALE
        outs = []
        for half in range(2):
            g = (2 * p + half) // group
            qm = jnp.where(low if half == 0 else ~low, q, 0.0).astype(BF16)
            outs.append(_softmax_pv([_dot_nt(qm, k_at[g == half])], [v_at[g == half]]))
        oc_ref[:, cols] = jnp.where(low, outs[0], outs[1])


def ctx_attention(z):
    col = lambda w, off: pl.BlockSpec((SEQ, w), lambda b: (b, off // w))
    return pl.pallas_call(
        _ctx_attn_kernel,
        grid=(BATCH,),
        in_specs=[col(NA_W, OFF_NA_Q), col(NA_W, OFF_NA_K), col(NA_W, OFF_NA_V),
                  col(GQA_W, OFF_GQ_Q), col(LANE, OFF_GQ_K), col(LANE, OFF_GQ_V)],
        out_specs=[col(NA_W, 0), col(GQA_W, 0)],
        out_shape=[jax.ShapeDtypeStruct((N_CTX_TOK, NA_W), F32), jax.ShapeDtypeStruct((N_CTX_TOK, GQA_W), F32)],
        compiler_params=_cparams("arbitrary"),
        name="ctx_attention",
    )(z, z, z, z, z, z)


GQA_TQ = 512


def _lat_gqa_kernel(q_ref, kl_ref, vl_ref, kc_ref, vc_ref, o_ref):
    low = _low_half(GQA_TQ)
    at = lambda x: (pltpu.roll(x, HEAD_DIM, axis=1).astype(BF16), x.astype(BF16))
    kl_at, vl_at, kc_at, vc_at = at(kl_ref[...]), at(vl_ref[...]), at(kc_ref[0]), at(vc_ref[0])
    group = GQA_HEADS // GQA_KV_HEADS
    for p in range(GQA_W // LANE):
        cols = slice(p * LANE, (p + 1) * LANE)
        q = q_ref[:, cols] * ATTN_SCALE
        outs = []
        for half in range(2):
            same = ((2 * p + half) // group) == half
            qm = jnp.where(low if half == 0 else ~low, q, 0.0).astype(BF16)
            outs.append(_softmax_pv([_dot_nt(qm, kc_at[same]), _dot_nt(qm, kl_at[same])],
                                    [vc_at[same], vl_at[same]]))
        o_ref[:, cols] = jnp.where(low, outs[0], outs[1])


def lat_gqa(z, cache_k, cache_v, layer):
    per = DEC_SEQ // GQA_TQ
    kv = lambda off: pl.BlockSpec((DEC_SEQ, LANE), lambda b, i: (b, off // LANE))
    cache = pl.BlockSpec((1, None, PAST_LEN, LANE), lambda b, i: (b, layer, 0, 0))
    return pl.pallas_call(
        _lat_gqa_kernel,
        grid=(DEC_BATCH, per),
        in_specs=[pl.BlockSpec((GQA_TQ, GQA_W), lambda b, i: (b * per + i, OFF_GQ_Q // GQA_W)),
                  kv(OFF_GQ_K), kv(OFF_GQ_V), cache, cache],
        out_specs=pl.BlockSpec((GQA_TQ, GQA_W), lambda b, i: (b * per + i, 0)),
        out_shape=jax.ShapeDtypeStruct((N_LAT_TOK, GQA_W), F32),
        compiler_params=_cparams("arbitrary", "arbitrary"),
        name="lat_gqa",
    )(z, z, z, cache_k, cache_v)


NA_RB = 4
NA_UNION = NA_RB + NA_WIN_ROWS
NA_TQ = NA_RB * GRID_W
NA_TK = NA_UNION * GRID_W
NA_TILES = 2 * NA_WIN_ROWS


def _na_union_start(i):
    return jnp.clip(i * NA_RB - NA_WIN_ROWS // 2, 0, GRID_H - NA_UNION)


def _na_tile_ids(i):
    u0 = _na_union_start(i)
    ids = []
    for t in range(NA_RB):
        r = i * NA_RB + t
        rs = jnp.clip(r - NA_WIN_ROWS // 2, 0, GRID_H - NA_WIN_ROWS)
        row_ids = []
        for j in range(NA_UNION):
            kr = u0 + j
            inside = (kr >= rs) & (kr < rs + NA_WIN_ROWS)
            row_ids.append(jnp.where(inside, kr - r + NA_WIN_ROWS - 1, NA_TILES - 1))
        ids.append(row_ids)
    return ids


def _na_bias(tab_ref, head, ids):
    rows = [jnp.concatenate([tab_ref[head, 0, row_ids[j]] + tab_ref[head, 1, row_ids[j + 1]]
                             for j in range(0, NA_UNION, 2)], axis=1) for row_ids in ids]
    return jnp.concatenate(rows, axis=0)


def _lat_na_kernel(q_ref, k_ref, v_ref, kc_ref, vc_ref, tab_ref, o_ref):
    i = pl.program_id(1)
    start = pl.multiple_of(_na_union_start(i) * GRID_W, GRID_W)
    low = _low_half(NA_TQ)
    ids = _na_tile_ids(i)
    for p in range(NA_W // LANE):
        cols = slice(p * LANE, (p + 1) * LANE)
        q = q_ref[:, cols] * ATTN_SCALE
        kl = k_ref[pl.ds(start, NA_TK), cols].astype(BF16)
        vl = v_ref[pl.ds(start, NA_TK), cols].astype(BF16)
        kc = kc_ref[0, :, cols].astype(BF16)
        vc = vc_ref[0, :, cols].astype(BF16)
        outs = []
        for half in range(2):
            qm = jnp.where(low if half == 0 else ~low, q, 0.0).astype(BF16)
            s_loc = _dot_nt(qm, kl) + _na_bias(tab_ref, 2 * p + half, ids)
            s_ctx = _dot_nt(qm, kc)
            outs.append(_softmax_pv([s_loc, s_ctx], [vl, vc]))
        o_ref[:, cols] = jnp.where(low, outs[0], outs[1])


def na_bias_table(rpb):
    cq = np.arange(GRID_W)
    ck = np.arange(GRID_W)
    dc = np.clip(ck[None, :] - cq[:, None], -(NA_WIN_COLS - 1), NA_WIN_COLS - 1) + NA_WIN_COLS - 1
    col_start = np.clip(cq - NA_WIN_COLS // 2, 0, GRID_W - NA_WIN_COLS)
    in_win = (ck[None, :] >= col_start[:, None]) & (ck[None, :] < col_start[:, None] + NA_WIN_COLS)
    col_sel = (dc[:, :, None] == np.arange(2 * NA_WIN_COLS - 1)).astype(np.float32)
    tiles = jnp.einsum("lhrc,qkc->lhrqk", rpb.astype(F32), col_sel, precision=lax.Precision.HIGHEST)
    tiles = jnp.where(in_win, tiles, NEG)
    masked = jnp.full(tiles.shape[:2] + (1, GRID_W, GRID_W), NEG, F32)
    tiles = jnp.concatenate([tiles, masked], axis=2)
    zero = jnp.zeros_like(tiles)
    return jnp.stack([jnp.concatenate([tiles, zero], -1), jnp.concatenate([zero, tiles], -1)], axis=2)


def lat_na(z, cache_k, cache_v, tab, layer):
    per = GRID_H // NA_RB
    kv = lambda off: pl.BlockSpec((DEC_SEQ, NA_W), lambda b, i: (b, off // NA_W))
    cache = pl.BlockSpec((1, None, PAST_LEN, NA_W), lambda b, i: (b, layer, 0, 0))
    return pl.pallas_call(
        _lat_na_kernel,
        grid=(DEC_BATCH, per),
        in_specs=[pl.BlockSpec((NA_TQ, NA_W), lambda b, i: (b * per + i, 0)),
                  kv(OFF_NA_K), kv(OFF_NA_V), cache, cache,
                  pl.BlockSpec((None, NA_HEADS, 2, NA_TILES, GRID_W, LANE), lambda b, i: (layer, 0, 0, 0, 0, 0))],
        out_specs=pl.BlockSpec((NA_TQ, NA_W), lambda b, i: (b * per + i, 0)),
        out_shape=jax.ShapeDtypeStruct((N_LAT_TOK, NA_W), F32),
        compiler_params=_cparams("arbitrary", "arbitrary"),
        name="lat_na",
    )(z, z, z, cache_k, cache_v, tab)


def _log_sigmoid(x):
    return jnp.minimum(x, 0.0) - jnp.log1p(jnp.exp(-jnp.abs(x)))


def _split3(x):
    hi = x.astype(BF16)
    r = x - hi.astype(F32)
    mid = r.astype(BF16)
    lo = (r - mid.astype(F32)).astype(BF16)
    return hi, mid, lo


def _mlstm_gate_tables(gates, fwd):
    t = gates.shape[0]
    jj = lax.broadcasted_iota(jnp.int32, (t, t), 0)
    ss = lax.broadcasted_iota(jnp.int32, (t, t), 1)
    feeds = (jj <= ss) if fwd else (jj >= ss)
    feeds_b = jnp.where(feeds, 1.0, 0.0).astype(BF16)
    reached_b = jnp.where((ss <= jj) if fwd else (ss >= jj), 1.0, 0.0).astype(BF16)
    eye_b = jnp.where(jj == ss, 1.0, 0.0).astype(BF16)
    lf = _split3(_log_sigmoid(gates))
    gs = _split3(gates)
    b_col = _dot(reached_b, lf[0]) + _dot(reached_b, lf[1]) + _dot(reached_b, lf[2])
    b_row = _dot_tn(lf[0], feeds_b) + _dot_tn(lf[1], feeds_b) + _dot_tn(lf[2], feeds_b)
    ig_row = _dot_tn(gs[0], eye_b) + _dot_tn(gs[1], eye_b) + _dot_tn(gs[2], eye_b)
    return ig_row, b_row, b_col, feeds


def _mlstm_head(k, qt, vt, kt, u, b_row, ig_row, feeds, ct, n, m, half, fwd):
    t = k.shape[0]
    dmat = jnp.where(feeds, b_row + u, NEG)
    m_inter = b_row + m
    m_j = jnp.maximum(m_inter, jnp.max(dmat, axis=0, keepdims=True))
    w = jnp.exp(dmat - m_j)
    decay = jnp.exp(m_inter - m_j)
    first = lax.broadcasted_iota(jnp.int32, (LANE, t), 0) < HEAD_DIM
    mine = first if half == 0 else ~first
    qm = jnp.where(mine, qt, 0.0)
    qb = qm.astype(BF16)
    kb = k.astype(BF16)
    qk = _dot(kb, qb) * w
    num = decay * _dot(ct.astype(BF16), qb) + _dot(vt.astype(BF16), qk.astype(BF16))
    den = decay * jnp.sum(qm * n, axis=0, keepdims=True) + jnp.sum(qk, axis=0, keepdims=True)
    h = jnp.where(mine, num * (1.0 / jnp.maximum(jnp.abs(den), jnp.exp(-m_j))), 0.0)
    last = t - 1 if fwd else 0
    b_last = b_row[:, last:last + 1]
    g = b_last - b_row + ig_row
    m_new = jnp.maximum(b_last + m, jnp.max(g, axis=1, keepdims=True))
    ws = jnp.exp(g - m_new)
    d_last = jnp.exp(b_last + m - m_new)
    ct_add = _dot((vt * ws).astype(BF16), kb)
    n_add = jnp.sum(jnp.where(mine, kt * ws, 0.0), axis=1, keepdims=True)
    return h, d_last, ct_add, n_add, m_new


def _mlstm_kernel(zf_ref, zb_ref, *refs, has_init):
    if has_init:
        c0_ref, n0_ref, m0_ref, hf_ref, hb_ref, co_ref, no_ref, mo_ref, c_s, n_s, m_s = refs
    else:
        hf_ref, hb_ref, co_ref, no_ref, mo_ref, c_s, n_s, m_s = refs
    chunk = pl.program_id(1)
    pairs = ML_W // LANE
    zero = jnp.zeros((HEAD_DIM, HEAD_DIM), F32)

    ri = lax.broadcasted_iota(jnp.int32, (LANE, LANE), 0)
    ci = lax.broadcasted_iota(jnp.int32, (LANE, LANE), 1)
    eye = ri == ci
    first_block = (ri < HEAD_DIM) & (ci < HEAD_DIM)
    second_block = (ri >= HEAD_DIM) & (ci >= HEAD_DIM)
    first_rows = lax.broadcasted_iota(jnp.int32, (LANE, 1), 0) < HEAD_DIM

    @pl.when(chunk == 0)
    def _():
        if not has_init:
            c_s[...] = jnp.zeros_like(c_s)
            n_s[...] = jnp.zeros_like(n_s)
            m_s[...] = jnp.zeros_like(m_s)
            return
        for d in range(2):
            for p in range(pairs):
                top = jnp.concatenate([c0_ref[0, d, 2 * p], zero], axis=1)
                bot = jnp.concatenate([zero, c0_ref[0, d, 2 * p + 1]], axis=1)
                c_s[d * pairs + p] = jnp.concatenate([top, bot], axis=0).T
                e = d * ML_HEADS + 2 * p
                n_row = jnp.concatenate([n0_ref[0, e:e + 1, :], n0_ref[0, e + 1:e + 2, :]], axis=1)
                n_s[d * pairs + p] = jnp.sum(jnp.where(eye, n_row, 0.0), axis=1, keepdims=True)
            m_s[d] = m0_ref[0, d * ML_HEADS:(d + 1) * ML_HEADS, :]

    for d, (z_ref, h_ref) in enumerate(((zf_ref, hf_ref), (zb_ref, hb_ref))):
        gates = z_ref[:, OFF_ML_G - OFF_ML_Q:OFF_ML_G - OFF_ML_Q + LANE]
        ig_rows, b_rows, b_cols, feeds = _mlstm_gate_tables(gates, d == 0)
        u = gates - pltpu.roll(b_cols, LANE - ML_HEADS, axis=1)
        m_all = m_s[d]
        m_rows = []
        for p in range(pairs):
            k = z_ref[:, ML_W + p * LANE:ML_W + (p + 1) * LANE] * ATTN_SCALE
            qt = z_ref[:, p * LANE:(p + 1) * LANE].T
            vt = z_ref[:, 2 * ML_W + p * LANE:2 * ML_W + (p + 1) * LANE].T
            kt = k.T
            ct = c_s[d * pairs + p]
            n = n_s[d * pairs + p]
            res = []
            for half in range(2):
                hd = 2 * p + half
                gi = 2 * ML_HEADS * d + hd
                gf = gi + ML_HEADS
                res.append(_mlstm_head(k, qt, vt, kt, u[:, gi:gi + 1], b_rows[gf:gf + 1, :], ig_rows[gi:gi + 1, :],
                                       feeds, ct, n, m_all[hd:hd + 1, 0:1], half, d == 0))
            (h0, dl0, ca0, na0, mn0), (h1, dl1, ca1, na1, mn1) = res
            h_ref[:, p * LANE:(p + 1) * LANE] = (h0 + h1).T
            c_s[d * pairs + p] = jnp.where(first_block, dl0 * ct + ca0, jnp.where(second_block, dl1 * ct + ca1, 0.0))
            n_s[d * pairs + p] = jnp.where(first_rows, dl0 * n + na0, dl1 * n + na1)
            m_rows += [jnp.broadcast_to(mn0, (1, LANE)), jnp.broadcast_to(mn1, (1, LANE))]
        m_s[d] = jnp.concatenate(m_rows, axis=0)

    @pl.when(chunk == pl.num_programs(1) - 1)
    def _():
        for d in range(2):
            for p in range(pairs):
                c = c_s[d * pairs + p].T
                n_row = jnp.sum(jnp.where(eye, n_s[d * pairs + p], 0.0), axis=0, keepdims=True)
                co_ref[0, d, 2 * p] = c[:HEAD_DIM, :HEAD_DIM]
                co_ref[0, d, 2 * p + 1] = c[HEAD_DIM:, HEAD_DIM:]
                e = d * ML_HEADS + 2 * p
                no_ref[0, e:e + 1, :] = n_row[:, :HEAD_DIM]
                no_ref[0, e + 1:e + 2, :] = n_row[:, HEAD_DIM:]
            mo_ref[0, d * ML_HEADS:(d + 1) * ML_HEADS, :] = m_s[d]


def mlstm(z, batch, seq, init=None, layer=None):
    nc = seq // ML_CHUNK
    zcol = OFF_ML_Q // ML_BLOCK_W
    state_c = pl.BlockSpec((1, 2, ML_HEADS, HEAD_DIM, HEAD_DIM), lambda b, c: (b, 0, 0, 0, 0))
    state_n = pl.BlockSpec((1, 2 * ML_HEADS, HEAD_DIM), lambda b, c: (b, 0, 0))
    state_m = pl.BlockSpec((1, 2 * ML_HEADS, LANE), lambda b, c: (b, 0, 0))
    init_specs = [] if init is None else [
        pl.BlockSpec((1, None, 2, ML_HEADS, HEAD_DIM, HEAD_DIM), lambda b, c: (b, layer, 0, 0, 0, 0)),
        pl.BlockSpec((1, None, 2 * ML_HEADS, HEAD_DIM), lambda b, c: (b, layer, 0, 0)),
        pl.BlockSpec((1, None, 2 * ML_HEADS, LANE), lambda b, c: (b, layer, 0, 0)),
    ]
    pairs = ML_W // LANE
    return pl.pallas_call(
        functools.partial(_mlstm_kernel, has_init=init is not None),
        grid=(batch, nc),
        in_specs=[
            pl.BlockSpec((ML_CHUNK, ML_BLOCK_W), lambda b, c: (b * nc + c, zcol)),
            pl.BlockSpec((ML_CHUNK, ML_BLOCK_W), lambda b, c: (b * nc + nc - 1 - c, zcol)),
        ] + init_specs,
        out_specs=[
            pl.BlockSpec((ML_CHUNK, ML_W), lambda b, c: (b * nc + c, 0)),
            pl.BlockSpec((ML_CHUNK, ML_W), lambda b, c: (b * nc + nc - 1 - c, 0)),
            state_c, state_n, state_m,
        ],
        out_shape=[
            jax.ShapeDtypeStruct((batch * seq, ML_W), F32),
            jax.ShapeDtypeStruct((batch * seq, ML_W), F32),
            jax.ShapeDtypeStruct((batch, 2, ML_HEADS, HEAD_DIM, HEAD_DIM), F32),
            jax.ShapeDtypeStruct((batch, 2 * ML_HEADS, HEAD_DIM), F32),
            jax.ShapeDtypeStruct((batch, 2 * ML_HEADS, LANE), F32),
        ],
        scratch_shapes=[
            pltpu.VMEM((2 * pairs, LANE, LANE), F32),
            pltpu.VMEM((2 * pairs, LANE, 1), F32),
            pltpu.VMEM((2, ML_HEADS, LANE), F32),
        ],
        compiler_params=_cparams("arbitrary", "arbitrary"),
        name="mlstm",
    )(z, z, *(() if init is None else init))


def _layer_norm(u, g, b):
    mu = jnp.mean(u, axis=-1, keepdims=True)
    uc = u - mu
    var = jnp.mean(uc * uc, axis=-1, keepdims=True)
    return uc * lax.rsqrt(var + NORM_EPS) * g + b


def _outproj_kernel(x_ref, a_ref, hf_ref, hb_ref, og0_ref, og1_ref, c_ref, w_ref, mg_ref, g1_ref, sh2_ref, sc2_ref,
                    lng_ref, lnb_ref, rw_ref, x1_ref, h2_ref, lg_ref):
    low = _low_half(TM_OUT)
    mixed_b = []
    for p, og_ref in enumerate((og0_ref, og1_ref)):
        cols = slice(p * LANE, (p + 1) * LANE)
        h = hf_ref[:, cols] + hb_ref[:, cols]
        hc = h - _pair_mean(h, low)
        hn = hc * lax.rsqrt(_pair_mean(hc * hc, low) + NORM_EPS)
        mixed_b.append(hn * mg_ref[:, cols] * _sigmoid(og_ref[...]))
    mixed = jnp.concatenate([a_ref[...]] + mixed_b + [c_ref[...]], axis=-1).astype(BF16)
    y = _dot(mixed, w_ref[...])
    x1 = _layer_norm(DEEPNORM_ALPHA * x_ref[...] + g1_ref[0] * y, lng_ref[...], lnb_ref[...])
    h2 = (x1 * (1.0 + sc2_ref[0]) + sh2_ref[0]).astype(BF16)
    x1_ref[...] = x1
    h2_ref[...] = h2
    lg_ref[...] = _dot_nt(rw_ref[...], h2)


def out_projection(x, out_a, hf, hb, z, out_c, layer, w_out, ml_gain, mods, mod_row, ln_g, ln_b, router_t):
    n = x.shape[0]
    mod = lambda k: pl.BlockSpec((None, 1, 1, D_MODEL), lambda i: (layer, mod_row(i), 0, k))
    row = lambda w: pl.BlockSpec((TM_OUT, w), lambda i: (i, 0))
    zcol = lambda off: pl.BlockSpec((TM_OUT, LANE), lambda i: (i, off // LANE))
    per_layer = lambda r, c: pl.BlockSpec((None, r, c), lambda i: (layer, 0, 0))
    norm0 = pl.BlockSpec((None, None, 1, D_MODEL), lambda i: (layer, 0, 0, 0))
    return pl.pallas_call(
        _outproj_kernel,
        grid=(n // TM_OUT,),
        in_specs=[row(D_MODEL), row(NA_W), row(ML_W), row(ML_W), zcol(OFF_ML_O), zcol(OFF_ML_O + LANE), row(GQA_W),
                  per_layer(D_MODEL, D_MODEL), per_layer(1, ML_W), mod(2), mod(3), mod(4), norm0, norm0,
                  per_layer(N_EXPERTS, D_MODEL)],
        out_specs=[row(D_MODEL), row(D_MODEL), pl.BlockSpec((N_EXPERTS, TM_OUT), lambda i: (0, i))],
        out_shape=[
            jax.ShapeDtypeStruct((n, D_MODEL), F32),
            jax.ShapeDtypeStruct((n, D_MODEL), BF16),
            jax.ShapeDtypeStruct((N_EXPERTS, n), F32),
        ],
        compiler_params=_cparams("arbitrary"),
        name="out_projection",
    )(x, out_a, hf, hb, z, z, out_c, w_out, ml_gain, mods, mods, mods, ln_g, ln_b, router_t)


ROUTE_BLK = 256


def _prefix_count(x, tri):
    n = x.shape[1]
    outs = []
    carry = jnp.zeros((N_EXPERTS, 1), F32)
    for i in range(n // ROUTE_BLK):
        blk = x[:, i * ROUTE_BLK:(i + 1) * ROUTE_BLK]
        outs.append(_dot(blk, tri) + carry)
        carry = carry + jnp.sum(blk.astype(F32), axis=1, keepdims=True)
    return jnp.concatenate(outs, axis=1) if len(outs) > 1 else outs[0]


def _route_kernel(lg_ref, coder_ref, codet_ref, afft_ref, *, n, cap):
    sets = lg_ref.shape[1] // n
    lg = lg_ref[...]
    ex = jnp.exp(lg - jnp.max(lg, axis=0, keepdims=True))
    aff_all = ex / jnp.sum(ex, axis=0, keepdims=True)
    affs = [aff_all[:, s * n:(s + 1) * n] for s in range(sets)]

    def search(i, bits):
        bit = lax.shift_left(jnp.int32(1), 30 - i)
        out = []
        for aff, b in zip(affs, bits):
            cand = b | bit
            cnt = jnp.sum(jnp.where(aff >= pltpu.bitcast(cand, F32), 1.0, 0.0), axis=1, keepdims=True)
            out.append(jnp.where(cnt >= cap, cand, b))
        return tuple(out)

    floors = lax.fori_loop(0, 31, search, tuple(jnp.zeros((N_EXPERTS, 1), jnp.int32) for _ in range(sets)))
    ti = lax.broadcasted_iota(jnp.int32, (ROUTE_BLK, ROUTE_BLK), 0)
    tj = lax.broadcasted_iota(jnp.int32, (ROUTE_BLK, ROUTE_BLK), 1)
    tri = jnp.where(ti < tj, 1.0, 0.0).astype(BF16)
    pad = jnp.zeros((LANE - N_EXPERTS, n), F32)
    for s, (aff, floor_bits) in enumerate(zip(affs, floors)):
        thr = jnp.min(jnp.where(aff >= pltpu.bitcast(floor_bits, F32), aff, 2.0), axis=1, keepdims=True)
        gt = aff > thr
        eq = aff == thr
        need = cap - jnp.sum(jnp.where(gt, 1.0, 0.0), axis=1, keepdims=True)
        eq_rank = _prefix_count(jnp.where(eq, 1.0, 0.0).astype(BF16), tri)
        sel = jnp.where(gt, 1.0, jnp.where(eq & (eq_rank < need), 1.0, 0.0))
        pos = _prefix_count(sel.astype(BF16), tri)
        code = jnp.where(sel > 0.5, pos, -1.0)
        coder_ref[s] = code
        codet_ref[s] = jnp.concatenate([code, pad], axis=0).T
        afft_ref[s] = jnp.concatenate([aff, pad], axis=0).T


def route(logits_t, n, cap):
    tokens = logits_t.shape[1]
    sets = tokens // n
    whole = lambda shape: pl.BlockSpec(shape, lambda i: (0,) * len(shape))
    return pl.pallas_call(
        functools.partial(_route_kernel, n=n, cap=cap),
        grid=(1,),
        in_specs=[whole((N_EXPERTS, tokens))],
        out_specs=[whole((sets, N_EXPERTS, n)), whole((sets, n, LANE)), whole((sets, n, LANE))],
        out_shape=[
            jax.ShapeDtypeStruct((sets, N_EXPERTS, n), F32),
            jax.ShapeDtypeStruct((sets, n, LANE), F32),
            jax.ShapeDtypeStruct((sets, n, LANE), F32),
        ],
        compiler_params=_cparams("arbitrary"),
        name="route",
    )(logits_t)


def _gather_kernel(coder_ref, h_ref, xe_ref, *, cap, epb):
    e0 = pl.program_id(1) * epb
    n = coder_ref.shape[2]
    ci = lax.broadcasted_iota(jnp.int32, (cap, n), 0).astype(F32)
    onehot = [jnp.where(ci == coder_ref[0, pl.ds(e0 + k, 1), :], 1.0, 0.0).astype(BF16) for k in range(epb)]
    onehot = jnp.concatenate(onehot, axis=0) if epb > 1 else onehot[0]
    rows = _dot(onehot, h_ref[...]).astype(BF16)
    for k in range(epb):
        xe_ref[k] = rows[k * cap:(k + 1) * cap]


def gather_tokens(code_rows, h2, n, cap):
    sets = code_rows.shape[0]
    epb = min(N_EXPERTS, max(1, 1024 // cap))
    return pl.pallas_call(
        functools.partial(_gather_kernel, cap=cap, epb=epb),
        grid=(sets, N_EXPERTS // epb),
        in_specs=[
            pl.BlockSpec((1, N_EXPERTS, n), lambda s, e: (s, 0, 0)),
            pl.BlockSpec((n, D_MODEL), lambda s, e: (s, 0)),
        ],
        out_specs=pl.BlockSpec((epb, cap, D_MODEL), lambda s, e: (e, s, 0)),
        out_shape=jax.ShapeDtypeStruct((N_EXPERTS, sets * cap, D_MODEL), BF16),
        compiler_params=_cparams("arbitrary", "arbitrary"),
        name="gather_tokens",
    )(code_rows, h2)


def _expert_kernel(xi_ref, xl_ref, wg_ref, wu_ref, wd_ref, yi_ref, yl_ref, acci_ref, accl_ref):
    j = pl.program_id(1)

    @pl.when(j == 0)
    def _():
        acci_ref[...] = jnp.zeros_like(acci_ref)
        accl_ref[...] = jnp.zeros_like(accl_ref)

    xs = (xi_ref[0], xl_ref[0])

    sums = [None, None]
    for s0 in range(0, EXP_TF, EXP_SUB):
        cols = slice(s0, min(s0 + EXP_SUB, EXP_TF))
        wg = wg_ref[0, 0, :, cols].astype(BF16)
        wu = wu_ref[0, 0, :, cols].astype(BF16)
        wd = wd_ref[0, 0, cols, :].astype(BF16)
        for g in range(2):
            a = _dot(xs[g], wg)
            b = _dot(xs[g], wu)
            hid = (a * _sigmoid(a) * b).astype(BF16)
            part = _dot(hid, wd)
            sums[g] = part if sums[g] is None else sums[g] + part
    acci_ref[...] += sums[0]
    accl_ref[...] += sums[1]

    @pl.when(j == pl.num_programs(1) - 1)
    def _():
        yi_ref[0] = acci_ref[...].astype(BF16)
        yl_ref[0] = accl_ref[...].astype(BF16)


def experts(xe_ctx, xe_lat, w_gate, w_up, w_down, layer):
    xin = lambda s: pl.BlockSpec((1, s, D_MODEL), lambda e, j: (e, 0, 0))
    return pl.pallas_call(
        _expert_kernel,
        grid=(N_EXPERTS, EXPERT_HIDDEN // EXP_TF),
        in_specs=[
            xin(SLOTS_CTX), xin(SLOTS_LAT),
            pl.BlockSpec((1, 1, D_MODEL, EXP_TF), lambda e, j: (layer, e, 0, j)),
            pl.BlockSpec((1, 1, D_MODEL, EXP_TF), lambda e, j: (layer, e, 0, j)),
            pl.BlockSpec((1, 1, EXP_TF, D_MODEL), lambda e, j: (layer, e, j, 0)),
        ],
        out_specs=[xin(SLOTS_CTX), xin(SLOTS_LAT)],
        out_shape=[
            jax.ShapeDtypeStruct((N_EXPERTS, SLOTS_CTX, D_MODEL), BF16),
            jax.ShapeDtypeStruct((N_EXPERTS, SLOTS_LAT, D_MODEL), BF16),
        ],
        scratch_shapes=[pltpu.VMEM((SLOTS_CTX, D_MODEL), F32), pltpu.VMEM((SLOTS_LAT, D_MODEL), F32)],
        compiler_params=_cparams("arbitrary", "arbitrary"),
        name="experts",
    )(xe_ctx, xe_lat, w_gate, w_up, w_down)


COMB_TT = 512


def _scatter_per_expert(code, aff, ye_ref, cap):
    tt = code.shape[0]
    li = lax.broadcasted_iota(jnp.int32, (tt, cap), 1).astype(F32)
    acc = jnp.zeros((tt, D_MODEL), F32)
    for e in range(N_EXPERTS):
        onehot = jnp.where(li == code[:, e:e + 1], 1.0, 0.0).astype(BF16)
        acc = acc + aff[:, e:e + 1] * _dot(onehot, ye_ref[e])
    return acc


def _scatter_merged(code, aff, ye_ref, cap):
    tt = code.shape[0]
    slots = N_EXPERTS * cap
    shift = cap.bit_length() - 1
    ei = lax.broadcasted_iota(jnp.int32, (LANE, slots), 0)
    si = lax.broadcasted_iota(jnp.int32, (LANE, slots), 1)
    expand = jnp.where(lax.shift_right_logical(si, shift) == ei, 1.0, 0.0).astype(BF16)
    slot = (lax.broadcasted_iota(jnp.int32, (tt, slots), 1) & (cap - 1)).astype(F32)
    hit = _dot(code.astype(BF16), expand) == slot
    a_hi = aff.astype(BF16)
    a_lo = (aff - a_hi.astype(F32)).astype(BF16)
    ye = ye_ref[...].reshape(slots, D_MODEL)
    acc = _dot(jnp.where(hit, _dot(a_hi, expand), 0.0).astype(BF16), ye)
    return acc + _dot(jnp.where(hit, _dot(a_lo, expand), 0.0).astype(BF16), ye)


def _combine_kernel(code_ref, aff_ref, ye_ref, x1_ref, g2_ref, lng_ref, lnb_ref, o_ref, *, cap):
    scatter = _scatter_merged if cap < LANE else _scatter_per_expert
    acc = scatter(code_ref[0], aff_ref[0], ye_ref, cap)
    u = DEEPNORM_ALPHA * x1_ref[...] + g2_ref[0] * acc
    o_ref[...] = _layer_norm(u, lng_ref[...], lnb_ref[...])


def combine(code, aff, ye, x1, layer, mods, set_mod_row, ln_g, ln_b, n, cap):
    sets = code.shape[0]
    tt = min(COMB_TT, n)
    per = n // tt
    tok = pl.BlockSpec((1, tt, LANE), lambda s, i: (s, i, 0))
    norm1 = pl.BlockSpec((None, None, 1, D_MODEL), lambda s, i: (layer, 1, 0, 0))
    return pl.pallas_call(
        functools.partial(_combine_kernel, cap=cap),
        grid=(sets, per),
        in_specs=[
            tok, tok,
            pl.BlockSpec((N_EXPERTS, cap, D_MODEL), lambda s, i: (0, s, 0)),
            pl.BlockSpec((tt, D_MODEL), lambda s, i: (s * per + i, 0)),
            pl.BlockSpec((None, 1, 1, D_MODEL), lambda s, i: (layer, set_mod_row(s), 0, 5)),
            norm1, norm1,
        ],
        out_specs=pl.BlockSpec((tt, D_MODEL), lambda s, i: (s * per + i, 0)),
        out_shape=jax.ShapeDtypeStruct((sets * n, D_MODEL), F32),
        compiler_params=_cparams("arbitrary", "arbitrary"),
        name="combine",
    )(code, aff, ye, x1, mods, ln_g, ln_b)


def _axial_rope_tables():
    t = np.arange(DEC_SEQ)
    row = (t // GRID_W).astype(np.float32)
    col = (t % GRID_W).astype(np.float32)
    half = HEAD_DIM // 2
    inv = (ROPE_THETA ** (-np.arange(0, half, 2, dtype=np.float32) / half)).astype(np.float32)
    ang_r = row[:, None] * inv
    ang_c = col[:, None] * inv
    ang = np.concatenate([ang_r, ang_r, ang_c, ang_c] * 2, -1)
    return jnp.asarray(np.cos(ang), F32), jnp.asarray(np.sin(ang), F32)


def _ctx_mod_row(i):
    return 0


def _lat_tile_mod_row(i):
    return 1 + i // (DEC_SEQ // TM)


def _lat_out_tile_mod_row(i):
    return 1 + i // (DEC_SEQ // TM_OUT)


def _lat_set_mod_row(s):
    return 1 + s


def kernel(x_prompt, x_sample, c, cache_na_k, cache_na_v, cache_gqa_k, cache_gqa_v, state_mlstm_c, state_mlstm_n,
           state_mlstm_m, c_ctx, ada_w, ada_b, w_in, b_gate, w_out, na_rpb, qk_norm_g, ml_norm_g, ln_g, ln_b,
           router_w, w_gate, w_up, w_down):
    cond8 = jnp.concatenate([c_ctx[None, :], c, jnp.zeros((8 - 1 - DEC_BATCH, D_MODEL), F32)], 0)
    mods = adaln(cond8, ada_w, ada_b).reshape(DEPTH, 8, 1, 6 * D_MODEL)
    xc = x_prompt.reshape(N_CTX_TOK, D_MODEL)
    xl = x_sample.reshape(N_LAT_TOK, D_MODEL)
    cos, sin = _axial_rope_tables()
    na_tab = na_bias_table(na_rpb)

    split = OFF_ML_G + N_GATES
    w_pad = jnp.concatenate([w_in[:, :, :split], jnp.zeros((DEPTH, D_MODEL, LANE - N_GATES), F32),
                             w_in[:, :, split:]], 2).astype(BF16)
    bias_row = jnp.pad(b_gate, ((0, 0), (OFF_ML_G, PROJ_PAD - OFF_ML_G - N_GATES))).reshape(DEPTH, 1, PROJ_PAD)
    gains = jnp.tile(qk_norm_g, (1, 1, 2)).reshape(DEPTH, 2, 1, LANE)
    w_out_b = w_out.astype(BF16)
    router_t = jnp.swapaxes(router_w, 1, 2).astype(BF16)
    ml_gain = ml_norm_g.reshape(DEPTH, 1, ML_W)
    ln_g4 = ln_g.reshape(DEPTH, 2, 1, D_MODEL)
    ln_b4 = ln_b.reshape(DEPTH, 2, 1, D_MODEL)
    na_ck = cache_na_k.reshape(DEC_BATCH, DEPTH, PAST_LEN, NA_W)
    na_cv = cache_na_v.reshape(DEC_BATCH, DEPTH, PAST_LEN, NA_W)
    gq_ck = cache_gqa_k.reshape(DEC_BATCH, DEPTH, PAST_LEN, GQA_KV_W)
    gq_cv = cache_gqa_v.reshape(DEC_BATCH, DEPTH, PAST_LEN, GQA_KV_W)
    lat_n0 = state_mlstm_n.reshape(DEC_BATCH, DEPTH, 2 * ML_HEADS, HEAD_DIM)
    lat_m0 = jnp.broadcast_to(state_mlstm_m.reshape(DEC_BATCH, DEPTH, 2 * ML_HEADS, 1),
                              (DEC_BATCH, DEPTH, 2 * ML_HEADS, LANE))
    kv, states = None, []

    for l in range(DEPTH):
        zc, *kv = in_projection(xc, l, mods, _ctx_mod_row, w_pad, bias_row, gains, kv_prev=kv)
        zl = in_projection(xl, l, mods, _lat_tile_mod_row, w_pad, bias_row, gains, cos, sin)

        out_a_c, out_c_c = ctx_attention(zc)
        hf_c, hb_c, sc, sn, sm = mlstm(zc, BATCH, SEQ)
        out_a_l = lat_na(zl, na_ck, na_cv, na_tab, l)
        out_c_l = lat_gqa(zl, gq_ck, gq_cv, l)
        hf_l, hb_l = mlstm(zl, DEC_BATCH, DEC_SEQ, (state_mlstm_c, lat_n0, lat_m0), l)[:2]
        states.append((sc, sn, sm))

        x1_c, h2_c, lg_c = out_projection(xc, out_a_c, hf_c, hb_c, zc, out_c_c, l, w_out_b, ml_gain, mods,
                                          _ctx_mod_row, ln_g4, ln_b4, router_t)
        x1_l, h2_l, lg_l = out_projection(xl, out_a_l, hf_l, hb_l, zl, out_c_l, l, w_out_b, ml_gain, mods,
                                          _lat_out_tile_mod_row, ln_g4, ln_b4, router_t)
        crow_c, code_c, aff_c = route(lg_c, SEQ, CAP_CTX)
        crow_l, code_l, aff_l = route(lg_l, DEC_SEQ, CAP_LAT)
        xe_c = gather_tokens(crow_c, h2_c, SEQ, CAP_CTX)
        xe_l = gather_tokens(crow_l, h2_l, DEC_SEQ, CAP_LAT)
        ye_c, ye_l = experts(xe_c, xe_l, w_gate, w_up, w_down, l)
        xc = combine(code_c, aff_c, ye_c, x1_c, l, mods, _ctx_mod_row, ln_g4, ln_b4, SEQ, CAP_CTX)
        xl = combine(code_l, aff_l, ye_l, x1_l, l, mods, _lat_set_mod_row, ln_g4, ln_b4, DEC_SEQ, CAP_LAT)

    y_prompt = xc.reshape(BATCH, SEQ, D_MODEL)
    y_sample = xl.reshape(DEC_BATCH, DEC_SEQ, D_MODEL)
    na_kv, gq_kv = kv
    heads = lambda a, h: a.reshape(BATCH, DEPTH, SEQ, h, HEAD_DIM)
    new_c = jnp.stack([s[0] for s in states], 1)
    new_n = jnp.stack([s[1] for s in states], 1).reshape(BATCH, DEPTH, 2, ML_HEADS, HEAD_DIM)
    new_m = jnp.stack([s[2][:, :, 0] for s in states], 1).reshape(BATCH, DEPTH, 2, ML_HEADS)
    return (y_prompt, y_sample, heads(na_kv[..., :NA_W], NA_HEADS), heads(na_kv[..., NA_W:], NA_HEADS),
            heads(gq_kv[..., :GQA_KV_W], GQA_KV_HEADS), heads(gq_kv[..., GQA_KV_W:], GQA_KV_HEADS),
            new_c, new_n, new_m)
```

```python
import functools

import jax
import jax.numpy as jnp
import numpy as np
from jax import lax
from jax.experimental import pallas as pl
from jax.experimental.pallas import tpu as pltpu

D_MODEL = 1024
BATCH = 16
SEQ = 256
DEPTH = 2
DEC_BATCH = 2
DEC_SEQ = 2048
PAST_LEN = 256
GRID_W = 64
GRID_H = DEC_SEQ // GRID_W
HEAD_DIM = 64
NA_HEADS = 6
NA_WIN_ROWS = 8
NA_WIN_COLS = 16
ML_HEADS = 4
GQA_HEADS = 6
GQA_KV_HEADS = 2
ROPE_THETA = 10000.0
N_EXPERTS = 16
EC_CAPACITY = 2
EXPERT_HIDDEN = 2816
NORM_EPS = 1e-6
NA_W = NA_HEADS * HEAD_DIM
ML_W = ML_HEADS * HEAD_DIM
GQA_W = GQA_HEADS * HEAD_DIM
GQA_KV_W = GQA_KV_HEADS * HEAD_DIM
N_GATES = 4 * ML_HEADS
DEEPNORM_ALPHA = (2 * DEPTH) ** 0.25
ATTN_SCALE = HEAD_DIM ** -0.5
F32 = jnp.float32
BF16 = jnp.bfloat16

LANE = 128
N_CTX_TOK = BATCH * SEQ
N_LAT_TOK = DEC_BATCH * DEC_SEQ
NEG = -1e30

OFF_NA_Q = 0
OFF_NA_K = OFF_NA_Q + NA_W
OFF_NA_V = OFF_NA_K + NA_W
OFF_ML_Q = OFF_NA_V + NA_W
OFF_ML_K = OFF_ML_Q + ML_W
OFF_ML_V = OFF_ML_K + ML_W
OFF_ML_O = OFF_ML_V + ML_W
OFF_ML_G = OFF_ML_O + ML_W
OFF_GQ_Q = OFF_ML_G + LANE
OFF_GQ_K = OFF_GQ_Q + GQA_W
OFF_GQ_V = OFF_GQ_K + GQA_KV_W
PROJ_PAD = OFF_GQ_V + GQA_KV_W
ML_BLOCK_W = OFF_GQ_Q - OFF_ML_Q

TM = 512
TM_OUT = 1024
ML_CHUNK = 256
EXP_SUB = 256
EXP_TF = EXPERT_HIDDEN // 2
CAP_CTX = EC_CAPACITY * SEQ // N_EXPERTS
CAP_LAT = EC_CAPACITY * DEC_SEQ // N_EXPERTS
SLOTS_CTX = BATCH * CAP_CTX
SLOTS_LAT = DEC_BATCH * CAP_LAT
VMEM_LIMIT = 56 * 1024 * 1024


def _cparams(*sem):
    return pltpu.CompilerParams(dimension_semantics=sem, vmem_limit_bytes=VMEM_LIMIT)


def _sigmoid(x):
    return 1.0 / (1.0 + jnp.exp(-x))


def _dot(a, b):
    return jnp.dot(a, b, preferred_element_type=F32)


def _dot_nt(a, b):
    return lax.dot_general(a, b, (((1,), (1,)), ((), ())), preferred_element_type=F32)


def _dot_tn(a, b):
    return lax.dot_general(a, b, (((0,), (0,)), ((), ())), preferred_element_type=F32)


def _low_half(rows):
    return lax.broadcasted_iota(jnp.int32, (rows, LANE), 1) < HEAD_DIM


def _pair_mean(x, low):
    s_lo = jnp.sum(jnp.where(low, x, 0.0), axis=-1, keepdims=True)
    s_hi = jnp.sum(jnp.where(low, 0.0, x), axis=-1, keepdims=True)
    return jnp.where(low, s_lo, s_hi) * (1.0 / HEAD_DIM)


def _adaln_kernel(c_ref, w_ref, b_ref, o_ref):
    c = c_ref[...]
    s = c * _sigmoid(c)
    o_ref[0] = _dot(s.astype(BF16), w_ref[0].astype(BF16)) + b_ref[0]


def adaln(cond8, ada_w, ada_b):
    tn = 1536
    return pl.pallas_call(
        _adaln_kernel,
        grid=(DEPTH, 6 * D_MODEL // tn),
        in_specs=[
            pl.BlockSpec((8, D_MODEL), lambda l, j: (0, 0)),
            pl.BlockSpec((1, D_MODEL, tn), lambda l, j: (l, 0, j)),
            pl.BlockSpec((1, 1, tn), lambda l, j: (l, 0, j)),
        ],
        out_specs=pl.BlockSpec((1, 8, tn), lambda l, j: (l, 0, j)),
        out_shape=jax.ShapeDtypeStruct((DEPTH, 8, 6 * D_MODEL), F32),
        compiler_params=_cparams("arbitrary", "arbitrary"),
        name="adaln",
    )(cond8, ada_w, ada_b.reshape(DEPTH, 1, 6 * D_MODEL))


def _inproj_kernel(x_ref, sh_ref, sc_ref, w_ref, b_ref, gq_ref, gk_ref, *rest, rope, n_alias, own_layer=0):
    first_layer = n_alias == 0
    if rope:
        cos_ref, sin_ref, z_ref = rest
    else:
        z_ref, nakv_ref, gqkv_ref = rest[n_alias:]
    h = x_ref[...] * (1.0 + sc_ref[0]) + sh_ref[0]
    z = _dot(h.astype(BF16), w_ref[...]) + b_ref[...]
    z_ref[...] = z
    rows = z.shape[0]
    low = _low_half(rows)
    if rope:
        lane = lax.broadcasted_iota(jnp.int32, (rows, LANE), 1)
        first_quarter = (lane & (HEAD_DIM // 4)) == 0

    def norm_pair(x, gain):
        y = x * lax.rsqrt(_pair_mean(x * x, low) + NORM_EPS) * gain
        if rope:
            rot = jnp.where(first_quarter, -pltpu.roll(y, LANE - HEAD_DIM // 4, axis=1),
                            pltpu.roll(y, HEAD_DIM // 4, axis=1))
            y = y * cos_ref[...] + rot * sin_ref[...]
        return y

    for p in range(GQA_W // LANE):
        c0 = OFF_GQ_Q + p * LANE
        z_ref[:, c0:c0 + LANE] = norm_pair(z[:, c0:c0 + LANE], gq_ref[...])
    k_normed = norm_pair(z[:, OFF_GQ_K:OFF_GQ_K + LANE], gk_ref[...])
    z_ref[:, OFF_GQ_K:OFF_GQ_K + LANE] = k_normed
    if not rope:
        for s in range(rows // SEQ):
            r = slice(s * SEQ, (s + 1) * SEQ)
            na, gq = (nakv_ref.at[s, own_layer], gqkv_ref.at[s, own_layer]) if first_layer else (
                nakv_ref.at[s], gqkv_ref.at[s])
            na[...] = z[r, OFF_NA_K:OFF_NA_K + 2 * NA_W]
            gq[:, :GQA_KV_W] = k_normed[r]
            gq[:, GQA_KV_W:] = z[r, OFF_GQ_V:OFF_GQ_V + GQA_KV_W]
            if first_layer:
                for other in range(DEPTH):
                    if other != own_layer:
                        nakv_ref[s, other] = jnp.zeros((SEQ, 2 * NA_W), F32)
                        gqkv_ref[s, other] = jnp.zeros((SEQ, 2 * GQA_KV_W), F32)


def in_projection(x, layer, mods, mod_row, w_pad, bias_row, gains, cos=None, sin=None, kv_prev=None):
    n = x.shape[0]
    rope = cos is not None
    in_specs = [
        pl.BlockSpec((TM, D_MODEL), lambda i: (i, 0)),
        pl.BlockSpec((None, 1, 1, D_MODEL), lambda i: (layer, mod_row(i), 0, 0)),
        pl.BlockSpec((None, 1, 1, D_MODEL), lambda i: (layer, mod_row(i), 0, 1)),
        pl.BlockSpec((None, D_MODEL, PROJ_PAD), lambda i: (layer, 0, 0)),
        pl.BlockSpec((None, 1, PROJ_PAD), lambda i: (layer, 0, 0)),
        pl.BlockSpec((None, None, 1, LANE), lambda i: (layer, 0, 0, 0)),
        pl.BlockSpec((None, None, 1, LANE), lambda i: (layer, 1, 0, 0)),
    ]
    args = [x, mods, mods, w_pad, bias_row, gains, gains]
    z_spec = pl.BlockSpec((TM, PROJ_PAD), lambda i: (i, 0))
    z_shape = jax.ShapeDtypeStruct((n, PROJ_PAD), F32)
    if rope:
        per = cos.shape[0] // TM
        in_specs += [pl.BlockSpec((TM, LANE), lambda i: (i % per, 0))] * 2
        args += [cos, sin]
        return pl.pallas_call(
            functools.partial(_inproj_kernel, rope=True, n_alias=0),
            grid=(n // TM,),
            in_specs=in_specs,
            out_specs=z_spec,
            out_shape=z_shape,
            compiler_params=_cparams("arbitrary"),
            name="in_projection_rope",
        )(*args)
    seqs = TM // SEQ
    kv_shape = lambda w: jax.ShapeDtypeStruct((n // SEQ, DEPTH, SEQ, w), F32)
    aliases = {}
    if kv_prev is None:
        kv_spec = lambda w: pl.BlockSpec((seqs, DEPTH, SEQ, w), lambda i: (i, 0, 0, 0))
    else:
        kv_spec = lambda w: pl.BlockSpec((seqs, None, SEQ, w), lambda i: (i, layer, 0, 0))
        aliases = {len(args): 1, len(args) + 1: 2}
        in_specs += [pl.BlockSpec(memory_space=pl.ANY)] * 2
        args += list(kv_prev)
    return pl.pallas_call(
        functools.partial(_inproj_kernel, rope=False, n_alias=len(aliases), own_layer=layer),
        grid=(n // TM,),
        in_specs=in_specs,
        out_specs=[z_spec, kv_spec(2 * NA_W), kv_spec(2 * GQA_KV_W)],
        out_shape=[z_shape, kv_shape(2 * NA_W), kv_shape(2 * GQA_KV_W)],
        input_output_aliases=aliases,
        compiler_params=_cparams("arbitrary"),
        name="in_projection",
    )(*args)


def _softmax_pv(scores, values):
    m = jnp.max(scores[0], axis=-1, keepdims=True)
    for s in scores[1:]:
        m = jnp.maximum(m, jnp.max(s, axis=-1, keepdims=True))
    l = 0.0
    o = 0.0
    for s, v in zip(scores, values):
        p = jnp.exp(s - m)
        l = l + jnp.sum(p, axis=-1, keepdims=True)
        o = o + _dot(p.astype(BF16), v)
    return o * (1.0 / l)


def _ctx_attn_kernel(naq_ref, nak_ref, nav_ref, gq_ref, gk_ref, gv_ref, oa_ref, oc_ref):
    low = _low_half(SEQ)
    for p in range(NA_W // LANE):
        cols = slice(p * LANE, (p + 1) * LANE)
        q = naq_ref[:, cols] * ATTN_SCALE
        k = nak_ref[:, cols].astype(BF16)
        v = nav_ref[:, cols].astype(BF16)
        outs = []
        for half in range(2):
            qm = jnp.where(low if half == 0 else ~low, q, 0.0).astype(BF16)
            outs.append(_softmax_pv([_dot_nt(qm, k)], [v]))
        oa_ref[:, cols] = jnp.where(low, outs[0], outs[1])
    k = gk_ref[...]
    v = gv_ref[...]
    k_at = (pltpu.roll(k, HEAD_DIM, axis=1).astype(BF16), k.astype(BF16))
    v_at = (pltpu.roll(v, HEAD_DIM, axis=1).astype(BF16), v.astype(BF16))
    group = GQA_HEADS // GQA_KV_HEADS
    for p in range(GQA_W // LANE):
        cols = slice(p * LANE, (p + 1) * LANE)
        q = gq_ref[:, cols] * ATTN_SCALE
        outs = []
        for half in range(2):
            g = (2 * p + half) // group
            qm = jnp.where(low if half == 0 else ~low, q, 0.0).astype(BF16)
            outs.append(_softmax_pv([_dot_nt(qm, k_at[g == half])], [v_at[g == half]]))
        oc_ref[:, cols] = jnp.where(low, outs[0], outs[1])


def ctx_attention(z):
    col = lambda w, off: pl.BlockSpec((SEQ, w), lambda b: (b, off // w))
    return pl.pallas_call(
        _ctx_attn_kernel,
        grid=(BATCH,),
        in_specs=[col(NA_W, OFF_NA_Q), col(NA_W, OFF_NA_K), col(NA_W, OFF_NA_V),
                  col(GQA_W, OFF_GQ_Q), col(LANE, OFF_GQ_K), col(LANE, OFF_GQ_V)],
        out_specs=[col(NA_W, 0), col(GQA_W, 0)],
        out_shape=[jax.ShapeDtypeStruct((N_CTX_TOK, NA_W), F32), jax.ShapeDtypeStruct((N_CTX_TOK, GQA_W), F32)],
        compiler_params=_cparams("arbitrary"),
        name="ctx_attention",
    )(z, z, z, z, z, z)


GQA_TQ = 512


def _lat_gqa_kernel(q_ref, kl_ref, vl_ref, kc_ref, vc_ref, o_ref):
    low = _low_half(GQA_TQ)
    at = lambda x: (pltpu.roll(x, HEAD_DIM, axis=1).astype(BF16), x.astype(BF16))
    kl_at, vl_at, kc_at, vc_at = at(kl_ref[...]), at(vl_ref[...]), at(kc_ref[0]), at(vc_ref[0])
    group = GQA_HEADS // GQA_KV_HEADS
    for p in range(GQA_W // LANE):
        cols = slice(p * LANE, (p + 1) * LANE)
        q = q_ref[:, cols] * ATTN_SCALE
        outs = []
        for half in range(2):
            same = ((2 * p + half) // group) == half
            qm = jnp.where(low if half == 0 else ~low, q, 0.0).astype(BF16)
            outs.append(_softmax_pv([_dot_nt(qm, kc_at[same]), _dot_nt(qm, kl_at[same])],
                                    [vc_at[same], vl_at[same]]))
        o_ref[:, cols] = jnp.where(low, outs[0], outs[1])


def lat_gqa(z, cache_k, cache_v, layer):
    per = DEC_SEQ // GQA_TQ
    kv = lambda off: pl.BlockSpec((DEC_SEQ, LANE), lambda b, i: (b, off // LANE))
    cache = pl.BlockSpec((1, None, PAST_LEN, LANE), lambda b, i: (b, layer, 0, 0))
    return pl.pallas_call(
        _lat_gqa_kernel,
        grid=(DEC_BATCH, per),
        in_specs=[pl.BlockSpec((GQA_TQ, GQA_W), lambda b, i: (b * per + i, OFF_GQ_Q // GQA_W)),
                  kv(OFF_GQ_K), kv(OFF_GQ_V), cache, cache],
        out_specs=pl.BlockSpec((GQA_TQ, GQA_W), lambda b, i: (b * per + i, 0)),
        out_shape=jax.ShapeDtypeStruct((N_LAT_TOK, GQA_W), F32),
        compiler_params=_cparams("arbitrary", "arbitrary"),
        name="lat_gqa",
    )(z, z, z, cache_k, cache_v)


NA_RB = 4
NA_UNION = NA_RB + NA_WIN_ROWS
NA_TQ = NA_RB * GRID_W
NA_TK = NA_UNION * GRID_W
NA_TILES = 2 * NA_WIN_ROWS


def _na_union_start(i):
    return jnp.clip(i * NA_RB - NA_WIN_ROWS // 2, 0, GRID_H - NA_UNION)


def _na_tile_ids(i):
    u0 = _na_union_start(i)
    ids = []
    for t in range(NA_RB):
        r = i * NA_RB + t
        rs = jnp.clip(r - NA_WIN_ROWS // 2, 0, GRID_H - NA_WIN_ROWS)
        row_ids = []
        for j in range(NA_UNION):
            kr = u0 + j
            inside = (kr >= rs) & (kr < rs + NA_WIN_ROWS)
            row_ids.append(jnp.where(inside, kr - r + NA_WIN_ROWS - 1, NA_TILES - 1))
        ids.append(row_ids)
    return ids


def _na_bias(tab_ref, head, ids):
    rows = [jnp.concatenate([tab_ref[head, 0, row_ids[j]] + tab_ref[head, 1, row_ids[j + 1]]
                             for j in range(0, NA_UNION, 2)], axis=1) for row_ids in ids]
    return jnp.concatenate(rows, axis=0)


def _lat_na_kernel(q_ref, k_ref, v_ref, kc_ref, vc_ref, tab_ref, o_ref):
    i = pl.program_id(1)
    start = pl.multiple_of(_na_union_start(i) * GRID_W, GRID_W)
    low = _low_half(NA_TQ)
    ids = _na_tile_ids(i)
    for p in range(NA_W // LANE):
        cols = slice(p * LANE, (p + 1) * LANE)
        q = q_ref[:, cols] * ATTN_SCALE
        kl = k_ref[pl.ds(start, NA_TK), cols].astype(BF16)
        vl = v_ref[pl.ds(start, NA_TK), cols].astype(BF16)
        kc = kc_ref[0, :, cols].astype(BF16)
        vc = vc_ref[0, :, cols].astype(BF16)
        outs = []
        for half in range(2):
            qm = jnp.where(low if half == 0 else ~low, q, 0.0).astype(BF16)
            s_loc = _dot_nt(qm, kl) + _na_bias(tab_ref, 2 * p + half, ids)
            s_ctx = _dot_nt(qm, kc)
            outs.append(_softmax_pv([s_loc, s_ctx], [vl, vc]))
        o_ref[:, cols] = jnp.where(low, outs[0], outs[1])


def na_bias_table(rpb):
    cq = np.arange(GRID_W)
    ck = np.arange(GRID_W)
    dc = np.clip(ck[None, :] - cq[:, None], -(NA_WIN_COLS - 1), NA_WIN_COLS - 1) + NA_WIN_COLS - 1
    col_start = np.clip(cq - NA_WIN_COLS // 2, 0, GRID_W - NA_WIN_COLS)
    in_win = (ck[None, :] >= col_start[:, None]) & (ck[None, :] < col_start[:, None] + NA_WIN_COLS)
    col_sel = (dc[:, :, None] == np.arange(2 * NA_WIN_COLS - 1)).astype(np.float32)
    tiles = jnp.einsum("lhrc,qkc->lhrqk", rpb.astype(F32), col_sel, precision=lax.Precision.HIGHEST)
    tiles = jnp.where(in_win, tiles, NEG)
    masked = jnp.full(tiles.shape[:2] + (1, GRID_W, GRID_W), NEG, F32)
    tiles = jnp.concatenate([tiles, masked], axis=2)
    zero = jnp.zeros_like(tiles)
    return jnp.stack([jnp.concatenate([tiles, zero], -1), jnp.concatenate([zero, tiles], -1)], axis=2)


def lat_na(z, cache_k, cache_v, tab, layer):
    per = GRID_H // NA_RB
    kv = lambda off: pl.BlockSpec((DEC_SEQ, NA_W), lambda b, i: (b, off // NA_W))
    cache = pl.BlockSpec((1, None, PAST_LEN, NA_W), lambda b, i: (b, layer, 0, 0))
    return pl.pallas_call(
        _lat_na_kernel,
        grid=(DEC_BATCH, per),
        in_specs=[pl.BlockSpec((NA_TQ, NA_W), lambda b, i: (b * per + i, 0)),
                  kv(OFF_NA_K), kv(OFF_NA_V), cache, cache,
                  pl.BlockSpec((None, NA_HEADS, 2, NA_TILES, GRID_W, LANE), lambda b, i: (layer, 0, 0, 0, 0, 0))],
        out_specs=pl.BlockSpec((NA_TQ, NA_W), lambda b, i: (b * per + i, 0)),
        out_shape=jax.ShapeDtypeStruct((N_LAT_TOK, NA_W), F32),
        compiler_params=_cparams("arbitrary", "arbitrary"),
        name="lat_na",
    )(z, z, z, cache_k, cache_v, tab)


def _log_sigmoid(x):
    return jnp.minimum(x, 0.0) - jnp.log1p(jnp.exp(-jnp.abs(x)))


def _split3(x):
    hi = x.astype(BF16)
    r = x - hi.astype(F32)
    mid = r.astype(BF16)
    lo = (r - mid.astype(F32)).astype(BF16)
    return hi, mid, lo


def _mlstm_gate_tables(gates, fwd):
    t = gates.shape[0]
    jj = lax.broadcasted_iota(jnp.int32, (t, t), 0)
    ss = lax.broadcasted_iota(jnp.int32, (t, t), 1)
    feeds = (jj <= ss) if fwd else (jj >= ss)
    feeds_b = jnp.where(feeds, 1.0, 0.0).astype(BF16)
    reached_b = jnp.where((ss <= jj) if fwd else (ss >= jj), 1.0, 0.0).astype(BF16)
    eye_b = jnp.where(jj == ss, 1.0, 0.0).astype(BF16)
    lf = _split3(_log_sigmoid(gates))
    gs = _split3(gates)
    b_col = _dot(reached_b, lf[0]) + _dot(reached_b, lf[1]) + _dot(reached_b, lf[2])
    b_row = _dot_tn(lf[0], feeds_b) + _dot_tn(lf[1], feeds_b) + _dot_tn(lf[2], feeds_b)
    ig_row = _dot_tn(gs[0], eye_b) + _dot_tn(gs[1], eye_b) + _dot_tn(gs[2], eye_b)
    return ig_row, b_row, b_col, feeds


def _mlstm_head(k, qt, vt, kt, u, b_row, ig_row, feeds, ct, n, m, half, fwd):
    t = k.shape[0]
    dmat = jnp.where(feeds, b_row + u, NEG)
    m_inter = b_row + m
    m_j = jnp.maximum(m_inter, jnp.max(dmat, axis=0, keepdims=True))
    w = jnp.exp(dmat - m_j)
    decay = jnp.exp(m_inter - m_j)
    first = lax.broadcasted_iota(jnp.int32, (LANE, t), 0) < HEAD_DIM
    mine = first if half == 0 else ~first
    qm = jnp.where(mine, qt, 0.0)
    qb = qm.astype(BF16)
    kb = k.astype(BF16)
    qk = _dot(kb, qb) * w
    num = decay * _dot(ct.astype(BF16), qb) + _dot(vt.astype(BF16), qk.astype(BF16))
    den = decay * jnp.sum(qm * n, axis=0, keepdims=True) + jnp.sum(qk, axis=0, keepdims=True)
    h = jnp.where(mine, num * (1.0 / jnp.maximum(jnp.abs(den), jnp.exp(-m_j))), 0.0)
    last = t - 1 if fwd else 0
    b_last = b_row[:, last:last + 1]
    g = b_last - b_row + ig_row
    m_new = jnp.maximum(b_last + m, jnp.max(g, axis=1, keepdims=True))
    ws = jnp.exp(g - m_new)
    d_last = jnp.exp(b_last + m - m_new)
    ct_add = _dot((vt * ws).astype(BF16), kb)
    n_add = jnp.sum(jnp.where(mine, kt * ws, 0.0), axis=1, keepdims=True)
    return h, d_last, ct_add, n_add, m_new


def _mlstm_kernel(zf_ref, zb_ref, *refs, has_init):
    if has_init:
        c0_ref, n0_ref, m0_ref, hf_ref, hb_ref, co_ref, no_ref, mo_ref, c_s, n_s, m_s = refs
    else:
        hf_ref, hb_ref, co_ref, no_ref, mo_ref, c_s, n_s, m_s = refs
    chunk = pl.program_id(1)
    pairs = ML_W // LANE
    zero = jnp.zeros((HEAD_DIM, HEAD_DIM), F32)

    ri = lax.broadcasted_iota(jnp.int32, (LANE, LANE), 0)
    ci = lax.broadcasted_iota(jnp.int32, (LANE, LANE), 1)
    eye = ri == ci
    first_block = (ri < HEAD_DIM) & (ci < HEAD_DIM)
    second_block = (ri >= HEAD_DIM) & (ci >= HEAD_DIM)
    first_rows = lax.broadcasted_iota(jnp.int32, (LANE, 1), 0) < HEAD_DIM

    @pl.when(chunk == 0)
    def _():
        if not has_init:
            c_s[...] = jnp.zeros_like(c_s)
            n_s[...] = jnp.zeros_like(n_s)
            m_s[...] = jnp.zeros_like(m_s)
            return
        for d in range(2):
            for p in range(pairs):
                top = jnp.concatenate([c0_ref[0, d, 2 * p], zero], axis=1)
                bot = jnp.concatenate([zero, c0_ref[0, d, 2 * p + 1]], axis=1)
                c_s[d * pairs + p] = jnp.concatenate([top, bot], axis=0).T
                e = d * ML_HEADS + 2 * p
                n_row = jnp.concatenate([n0_ref[0, e:e + 1, :], n0_ref[0, e + 1:e + 2, :]], axis=1)
                n_s[d * pairs + p] = jnp.sum(jnp.where(eye, n_row, 0.0), axis=1, keepdims=True)
            m_s[d] = m0_ref[0, d * ML_HEADS:(d + 1) * ML_HEADS, :]

    for d, (z_ref, h_ref) in enumerate(((zf_ref, hf_ref), (zb_ref, hb_ref))):
        gates = z_ref[:, OFF_ML_G - OFF_ML_Q:OFF_ML_G - OFF_ML_Q + LANE]
        ig_rows, b_rows, b_cols, feeds = _mlstm_gate_tables(gates, d == 0)
        u = gates - pltpu.roll(b_cols, LANE - ML_HEADS, axis=1)
        m_all = m_s[d]
        m_rows = []
        for p in range(pairs):
            k = z_ref[:, ML_W + p * LANE:ML_W + (p + 1) * LANE] * ATTN_SCALE
            qt = z_ref[:, p * LANE:(p + 1) * LANE].T
            vt = z_ref[:, 2 * ML_W + p * LANE:2 * ML_W + (p + 1) * LANE].T
            kt = k.T
            ct = c_s[d * pairs + p]
            n = n_s[d * pairs + p]
            res = []
            for half in range(2):
                hd = 2 * p + half
                gi = 2 * ML_HEADS * d + hd
                gf = gi + ML_HEADS
                res.append(_mlstm_head(k, qt, vt, kt, u[:, gi:gi + 1], b_rows[gf:gf + 1, :], ig_rows[gi:gi + 1, :],
                                       feeds, ct, n, m_all[hd:hd + 1, 0:1], half, d == 0))
            (h0, dl0, ca0, na0, mn0), (h1, dl1, ca1, na1, mn1) = res
            h_ref[:, p * LANE:(p + 1) * LANE] = (h0 + h1).T
            c_s[d * pairs + p] = jnp.where(first_block, dl0 * ct + ca0, jnp.where(second_block, dl1 * ct + ca1, 0.0))
            n_s[d * pairs + p] = jnp.where(first_rows, dl0 * n + na0, dl1 * n + na1)
            m_rows += [jnp.broadcast_to(mn0, (1, LANE)), jnp.broadcast_to(mn1, (1, LANE))]
        m_s[d] = jnp.concatenate(m_rows, axis=0)

    @pl.when(chunk == pl.num_programs(1) - 1)
    def _():
        for d in range(2):
            for p in range(pairs):
                c = c_s[d * pairs + p].T
                n_row = jnp.sum(jnp.where(eye, n_s[d * pairs + p], 0.0), axis=0, keepdims=True)
                co_ref[0, d, 2 * p] = c[:HEAD_DIM, :HEAD_DIM]
                co_ref[0, d, 2 * p + 1] = c[HEAD_DIM:, HEAD_DIM:]
                e = d * ML_HEADS + 2 * p
                no_ref[0, e:e + 1, :] = n_row[:, :HEAD_DIM]
                no_ref[0, e + 1:e + 2, :] = n_row[:, HEAD_DIM:]
            mo_ref[0, d * ML_HEADS:(d + 1) * ML_HEADS, :] = m_s[d]


def mlstm(z, batch, seq, init=None, layer=None):
    nc = seq // ML_CHUNK
    zcol = OFF_ML_Q // ML_BLOCK_W
    state_c = pl.BlockSpec((1, 2, ML_HEADS, HEAD_DIM, HEAD_DIM), lambda b, c: (b, 0, 0, 0, 0))
    state_n = pl.BlockSpec((1, 2 * ML_HEADS, HEAD_DIM), lambda b, c: (b, 0, 0))
    state_m = pl.BlockSpec((1, 2 * ML_HEADS, LANE), lambda b, c: (b, 0, 0))
    init_specs = [] if init is None else [
        pl.BlockSpec((1, None, 2, ML_HEADS, HEAD_DIM, HEAD_DIM), lambda b, c: (b, layer, 0, 0, 0, 0)),
        pl.BlockSpec((1, None, 2 * ML_HEADS, HEAD_DIM), lambda b, c: (b, layer, 0, 0)),
        pl.BlockSpec((1, None, 2 * ML_HEADS, LANE), lambda b, c: (b, layer, 0, 0)),
    ]
    pairs = ML_W // LANE
    return pl.pallas_call(
        functools.partial(_mlstm_kernel, has_init=init is not None),
        grid=(batch, nc),
        in_specs=[
            pl.BlockSpec((ML_CHUNK, ML_BLOCK_W), lambda b, c: (b * nc + c, zcol)),
            pl.BlockSpec((ML_CHUNK, ML_BLOCK_W), lambda b, c: (b * nc + nc - 1 - c, zcol)),
        ] + init_specs,
        out_specs=[
            pl.BlockSpec((ML_CHUNK, ML_W), lambda b, c: (b * nc + c, 0)),
            pl.BlockSpec((ML_CHUNK, ML_W), lambda b, c: (b * nc + nc - 1 - c, 0)),
            state_c, state_n, state_m,
        ],
        out_shape=[
            jax.ShapeDtypeStruct((batch * seq, ML_W), F32),
            jax.ShapeDtypeStruct((batch * seq, ML_W), F32),
            jax.ShapeDtypeStruct((batch, 2, ML_HEADS, HEAD_DIM, HEAD_DIM), F32),
            jax.ShapeDtypeStruct((batch, 2 * ML_HEADS, HEAD_DIM), F32),
            jax.ShapeDtypeStruct((batch, 2 * ML_HEADS, LANE), F32),
        ],
        scratch_shapes=[
            pltpu.VMEM((2 * pairs, LANE, LANE), F32),
            pltpu.VMEM((2 * pairs, LANE, 1), F32),
            pltpu.VMEM((2, ML_HEADS, LANE), F32),
        ],
        compiler_params=_cparams("arbitrary", "arbitrary"),
        name="mlstm",
    )(z, z, *(() if init is None else init))


def _layer_norm(u, g, b):
    mu = jnp.mean(u, axis=-1, keepdims=True)
    uc = u - mu
    var = jnp.mean(uc * uc, axis=-1, keepdims=True)
    return uc * lax.rsqrt(var + NORM_EPS) * g + b


def _outproj_kernel(x_ref, a_ref, hf_ref, hb_ref, og0_ref, og1_ref, c_ref, w_ref, mg_ref, g1_ref, sh2_ref, sc2_ref,
                    lng_ref, lnb_ref, rw_ref, x1_ref, h2_ref, lg_ref):
    low = _low_half(TM_OUT)
    mixed_b = []
    for p, og_ref in enumerate((og0_ref, og1_ref)):
        cols = slice(p * LANE, (p + 1) * LANE)
        h = hf_ref[:, cols] + hb_ref[:, cols]
        hc = h - _pair_mean(h, low)
        hn = hc * lax.rsqrt(_pair_mean(hc * hc, low) + NORM_EPS)
        mixed_b.append(hn * mg_ref[:, cols] * _sigmoid(og_ref[...]))
    mixed = jnp.concatenate([a_ref[...]] + mixed_b + [c_ref[...]], axis=-1).astype(BF16)
    y = _dot(mixed, w_ref[...])
    x1 = _layer_norm(DEEPNORM_ALPHA * x_ref[...] + g1_ref[0] * y, lng_ref[...], lnb_ref[...])
    h2 = (x1 * (1.0 + sc2_ref[0]) + sh2_ref[0]).astype(BF16)
    x1_ref[...] = x1
    h2_ref[...] = h2
    lg_ref[...] = _dot_nt(rw_ref[...], h2)


def out_projection(x, out_a, hf, hb, z, out_c, layer, w_out, ml_gain, mods, mod_row, ln_g, ln_b, router_t):
    n = x.shape[0]
    mod = lambda k: pl.BlockSpec((None, 1, 1, D_MODEL), lambda i: (layer, mod_row(i), 0, k))
    row = lambda w: pl.BlockSpec((TM_OUT, w), lambda i: (i, 0))
    zcol = lambda off: pl.BlockSpec((TM_OUT, LANE), lambda i: (i, off // LANE))
    per_layer = lambda r, c: pl.BlockSpec((None, r, c), lambda i: (layer, 0, 0))
    norm0 = pl.BlockSpec((None, None, 1, D_MODEL), lambda i: (layer, 0, 0, 0))
    return pl.pallas_call(
        _outproj_kernel,
        grid=(n // TM_OUT,),
        in_specs=[row(D_MODEL), row(NA_W), row(ML_W), row(ML_W), zcol(OFF_ML_O), zcol(OFF_ML_O + LANE), row(GQA_W),
                  per_layer(D_MODEL, D_MODEL), per_layer(1, ML_W), mod(2), mod(3), mod(4), norm0, norm0,
                  per_layer(N_EXPERTS, D_MODEL)],
        out_specs=[row(D_MODEL), row(D_MODEL), pl.BlockSpec((N_EXPERTS, TM_OUT), lambda i: (0, i))],
        out_shape=[
            jax.ShapeDtypeStruct((n, D_MODEL), F32),
            jax.ShapeDtypeStruct((n, D_MODEL), BF16),
            jax.ShapeDtypeStruct((N_EXPERTS, n), F32),
        ],
        compiler_params=_cparams("arbitrary"),
        name="out_projection",
    )(x, out_a, hf, hb, z, z, out_c, w_out, ml_gain, mods, mods, mods, ln_g, ln_b, router_t)


ROUTE_BLK = 256


def _prefix_count(x, tri):
    n = x.shape[1]
    outs = []
    carry = jnp.zeros((N_EXPERTS, 1), F32)
    for i in range(n // ROUTE_BLK):
        blk = x[:, i * ROUTE_BLK:(i + 1) * ROUTE_BLK]
        outs.append(_dot(blk, tri) + carry)
        carry = carry + jnp.sum(blk.astype(F32), axis=1, keepdims=True)
    return jnp.concatenate(outs, axis=1) if len(outs) > 1 else outs[0]


def _route_kernel(lg_ref, coder_ref, codet_ref, afft_ref, *, n, cap):
    sets = lg_ref.shape[1] // n
    lg = lg_ref[...]
    ex = jnp.exp(lg - jnp.max(lg, axis=0, keepdims=True))
    aff_all = ex / jnp.sum(ex, axis=0, keepdims=True)
    affs = [aff_all[:, s * n:(s + 1) * n] for s in range(sets)]

    def search(i, bits):
        bit = lax.shift_left(jnp.int32(1), 30 - i)
        out = []
        for aff, b in zip(affs, bits):
            cand = b | bit
            cnt = jnp.sum(jnp.where(aff >= pltpu.bitcast(cand, F32), 1.0, 0.0), axis=1, keepdims=True)
            out.append(jnp.where(cnt >= cap, cand, b))
        return tuple(out)

    floors = lax.fori_loop(0, 31, search, tuple(jnp.zeros((N_EXPERTS, 1), jnp.int32) for _ in range(sets)))
    ti = lax.broadcasted_iota(jnp.int32, (ROUTE_BLK, ROUTE_BLK), 0)
    tj = lax.broadcasted_iota(jnp.int32, (ROUTE_BLK, ROUTE_BLK), 1)
    tri = jnp.where(ti < tj, 1.0, 0.0).astype(BF16)
    pad = jnp.zeros((LANE - N_EXPERTS, n), F32)
    for s, (aff, floor_bits) in enumerate(zip(affs, floors)):
        thr = jnp.min(jnp.where(aff >= pltpu.bitcast(floor_bits, F32), aff, 2.0), axis=1, keepdims=True)
        gt = aff > thr
        eq = aff == thr
        need = cap - jnp.sum(jnp.where(gt, 1.0, 0.0), axis=1, keepdims=True)
        eq_rank = _prefix_count(jnp.where(eq, 1.0, 0.0).astype(BF16), tri)
        sel = jnp.where(gt, 1.0, jnp.where(eq & (eq_rank < need), 1.0, 0.0))
        pos = _prefix_count(sel.astype(BF16), tri)
        code = jnp.where(sel > 0.5, pos, -1.0)
        coder_ref[s] = code
        codet_ref[s] = jnp.concatenate([code, pad], axis=0).T
        afft_ref[s] = jnp.concatenate([aff, pad], axis=0).T


def route(logits_t, n, cap):
    tokens = logits_t.shape[1]
    sets = tokens // n
    whole = lambda shape: pl.BlockSpec(shape, lambda i: (0,) * len(shape))
    return pl.pallas_call(
        functools.partial(_route_kernel, n=n, cap=cap),
        grid=(1,),
        in_specs=[whole((N_EXPERTS, tokens))],
        out_specs=[whole((sets, N_EXPERTS, n)), whole((sets, n, LANE)), whole((sets, n, LANE))],
        out_shape=[
            jax.ShapeDtypeStruct((sets, N_EXPERTS, n), F32),
            jax.ShapeDtypeStruct((sets, n, LANE), F32),
            jax.ShapeDtypeStruct((sets, n, LANE), F32),
        ],
        compiler_params=_cparams("arbitrary"),
        name="route",
    )(logits_t)


def _gather_kernel(coder_ref, h_ref, xe_ref, *, cap, epb):
    e0 = pl.program_id(1) * epb
    n = coder_ref.shape[2]
    ci = lax.broadcasted_iota(jnp.int32, (cap, n), 0).astype(F32)
    onehot = [jnp.where(ci == coder_ref[0, pl.ds(e0 + k, 1), :], 1.0, 0.0).astype(BF16) for k in range(epb)]
    onehot = jnp.concatenate(onehot, axis=0) if epb > 1 else onehot[0]
    rows = _dot(onehot, h_ref[...]).astype(BF16)
    for k in range(epb):
        xe_ref[k] = rows[k * cap:(k + 1) * cap]


def gather_tokens(code_rows, h2, n, cap):
    sets = code_rows.shape[0]
    epb = min(N_EXPERTS, max(1, 1024 // cap))
    return pl.pallas_call(
        functools.partial(_gather_kernel, cap=cap, epb=epb),
        grid=(sets, N_EXPERTS // epb),
        in_specs=[
            pl.BlockSpec((1, N_EXPERTS, n), lambda s, e: (s, 0, 0)),
            pl.BlockSpec((n, D_MODEL), lambda s, e: (s, 0)),
        ],
        out_specs=pl.BlockSpec((epb, cap, D_MODEL), lambda s, e: (e, s, 0)),
        out_shape=jax.ShapeDtypeStruct((N_EXPERTS, sets * cap, D_MODEL), BF16),
        compiler_params=_cparams("arbitrary", "arbitrary"),
        name="gather_tokens",
    )(code_rows, h2)


def _expert_kernel(xi_ref, xl_ref, wg_ref, wu_ref, wd_ref, yi_ref, yl_ref, acci_ref, accl_ref):
    j = pl.program_id(1)

    @pl.when(j == 0)
    def _():
        acci_ref[...] = jnp.zeros_like(acci_ref)
        accl_ref[...] = jnp.zeros_like(accl_ref)

    xs = (xi_ref[0], xl_ref[0])

    sums = [None, None]
    for s0 in range(0, EXP_TF, EXP_SUB):
        cols = slice(s0, min(s0 + EXP_SUB, EXP_TF))
        wg = wg_ref[0, 0, :, cols].astype(BF16)
        wu = wu_ref[0, 0, :, cols].astype(BF16)
        wd = wd_ref[0, 0, cols, :].astype(BF16)
        for g in range(2):
            a = _dot(xs[g], wg)
            b = _dot(xs[g], wu)
            hid = (a * _sigmoid(a) * b).astype(BF16)
            part = _dot(hid, wd)
            sums[g] = part if sums[g] is None else sums[g] + part
    acci_ref[...] += sums[0]
    accl_ref[...] += sums[1]

    @pl.when(j == pl.num_programs(1) - 1)
    def _():
        yi_ref[0] = acci_ref[...].astype(BF16)
        yl_ref[0] = accl_ref[...].astype(BF16)


def experts(xe_ctx, xe_lat, w_gate, w_up, w_down, layer):
    xin = lambda s: pl.BlockSpec((1, s, D_MODEL), lambda e, j: (e, 0, 0))
    return pl.pallas_call(
        _expert_kernel,
        grid=(N_EXPERTS, EXPERT_HIDDEN // EXP_TF),
        in_specs=[
            xin(SLOTS_CTX), xin(SLOTS_LAT),
            pl.BlockSpec((1, 1, D_MODEL, EXP_TF), lambda e, j: (layer, e, 0, j)),
            pl.BlockSpec((1, 1, D_MODEL, EXP_TF), lambda e, j: (layer, e, 0, j)),
            pl.BlockSpec((1, 1, EXP_TF, D_MODEL), lambda e, j: (layer, e, j, 0)),
        ],
        out_specs=[xin(SLOTS_CTX), xin(SLOTS_LAT)],
        out_shape=[
            jax.ShapeDtypeStruct((N_EXPERTS, SLOTS_CTX, D_MODEL), BF16),
            jax.ShapeDtypeStruct((N_EXPERTS, SLOTS_LAT, D_MODEL), BF16),
        ],
        scratch_shapes=[pltpu.VMEM((SLOTS_CTX, D_MODEL), F32), pltpu.VMEM((SLOTS_LAT, D_MODEL), F32)],
        compiler_params=_cparams("arbitrary", "arbitrary"),
        name="experts",
    )(xe_ctx, xe_lat, w_gate, w_up, w_down)


COMB_TT = 512


def _scatter_per_expert(code, aff, ye_ref, cap):
    tt = code.shape[0]
    li = lax.broadcasted_iota(jnp.int32, (tt, cap), 1).astype(F32)
    acc = jnp.zeros((tt, D_MODEL), F32)
    for e in range(N_EXPERTS):
        onehot = jnp.where(li == code[:, e:e + 1], 1.0, 0.0).astype(BF16)
        acc = acc + aff[:, e:e + 1] * _dot(onehot, ye_ref[e])
    return acc


def _scatter_merged(code, aff, ye_ref, cap):
    tt = code.shape[0]
    slots = N_EXPERTS * cap
    shift = cap.bit_length() - 1
    ei = lax.broadcasted_iota(jnp.int32, (LANE, slots), 0)
    si = lax.broadcasted_iota(jnp.int32, (LANE, slots), 1)
    expand = jnp.where(lax.shift_right_logical(si, shift) == ei, 1.0, 0.0).astype(BF16)
    slot = (lax.broadcasted_iota(jnp.int32, (tt, slots), 1) & (cap - 1)).astype(F32)
    hit = _dot(code.astype(BF16), expand) == slot
    a_hi = aff.astype(BF16)
    a_lo = (aff - a_hi.astype(F32)).astype(BF16)
    ye = ye_ref[...].reshape(slots, D_MODEL)
    acc = _dot(jnp.where(hit, _dot(a_hi, expand), 0.0).astype(BF16), ye)
    return acc + _dot(jnp.where(hit, _dot(a_lo, expand), 0.0).astype(BF16), ye)


def _combine_kernel(code_ref, aff_ref, ye_ref, x1_ref, g2_ref, lng_ref, lnb_ref, o_ref, *, cap):
    scatter = _scatter_merged if cap < LANE else _scatter_per_expert
    acc = scatter(code_ref[0], aff_ref[0], ye_ref, cap)
    u = DEEPNORM_ALPHA * x1_ref[...] + g2_ref[0] * acc
    o_ref[...] = _layer_norm(u, lng_ref[...], lnb_ref[...])


def combine(code, aff, ye, x1, layer, mods, set_mod_row, ln_g, ln_b, n, cap):
    sets = code.shape[0]
    tt = min(COMB_TT, n)
    per = n // tt
    tok = pl.BlockSpec((1, tt, LANE), lambda s, i: (s, i, 0))
    norm1 = pl.BlockSpec((None, None, 1, D_MODEL), lambda s, i: (layer, 1, 0, 0))
    return pl.pallas_call(
        functools.partial(_combine_kernel, cap=cap),
        grid=(sets, per),
        in_specs=[
            tok, tok,
            pl.BlockSpec((N_EXPERTS, cap, D_MODEL), lambda s, i: (0, s, 0)),
            pl.BlockSpec((tt, D_MODEL), lambda s, i: (s * per + i, 0)),
            pl.BlockSpec((None, 1, 1, D_MODEL), lambda s, i: (layer, set_mod_row(s), 0, 5)),
            norm1, norm1,
        ],
        out_specs=pl.BlockSpec((tt, D_MODEL), lambda s, i: (s * per + i, 0)),
        out_shape=jax.ShapeDtypeStruct((sets * n, D_MODEL), F32),
        compiler_params=_cparams("arbitrary", "arbitrary"),
        name="combine",
    )(code, aff, ye, x1, mods, ln_g, ln_b)


def _axial_rope_tables():
    t = np.arange(DEC_SEQ)
    row = (t // GRID_W).astype(np.float32)
    col = (t % GRID_W).astype(np.float32)
    half = HEAD_DIM // 2
    inv = (ROPE_THETA ** (-np.arange(0, half, 2, dtype=np.float32) / half)).astype(np.float32)
    ang_r = row[:, None] * inv
    ang_c = col[:, None] * inv
    ang = np.concatenate([ang_r, ang_r, ang_c, ang_c] * 2, -1)
    return jnp.asarray(np.cos(ang), F32), jnp.asarray(np.sin(ang), F32)


def _ctx_mod_row(i):
    return 0


def _lat_tile_mod_row(i):
    return 1 + i // (DEC_SEQ // TM)


def _lat_out_tile_mod_row(i):
    return 1 + i // (DEC_SEQ // TM_OUT)


def _lat_set_mod_row(s):
    return 1 + s


def kernel(x_prompt, x_sample, c, cache_na_k, cache_na_v, cache_gqa_k, cache_gqa_v, state_mlstm_c, state_mlstm_n,
           state_mlstm_m, c_ctx, ada_w, ada_b, w_in, b_gate, w_out, na_rpb, qk_norm_g, ml_norm_g, ln_g, ln_b,
           router_w, w_gate, w_up, w_down):
    cond8 = jnp.concatenate([c_ctx[None, :], c, jnp.zeros((8 - 1 - DEC_BATCH, D_MODEL), F32)], 0)
    mods = adaln(cond8, ada_w, ada_b).reshape(DEPTH, 8, 1, 6 * D_MODEL)
    xc = x_prompt.reshape(N_CTX_TOK, D_MODEL)
    xl = x_sample.reshape(N_LAT_TOK, D_MODEL)
    cos, sin = _axial_rope_tables()
    na_tab = na_bias_table(na_rpb)

    split = OFF_ML_G + N_GATES
    w_pad = jnp.concatenate([w_in[:, :, :split], jnp.zeros((DEPTH, D_MODEL, LANE - N_GATES), F32),
                             w_in[:, :, split:]], 2).astype(BF16)
    bias_row = jnp.pad(b_gate, ((0, 0), (OFF_ML_G, PROJ_PAD - OFF_ML_G - N_GATES))).reshape(DEPTH, 1, PROJ_PAD)
    gains = jnp.tile(qk_norm_g, (1, 1, 2)).reshape(DEPTH, 2, 1, LANE)
    w_out_b = w_out.astype(BF16)
    router_t = jnp.swapaxes(router_w, 1, 2).astype(BF16)
    ml_gain = ml_norm_g.reshape(DEPTH, 1, ML_W)
    ln_g4 = ln_g.reshape(DEPTH, 2, 1, D_MODEL)
    ln_b4 = ln_b.reshape(DEPTH, 2, 1, D_MODEL)
    na_ck = cache_na_k.reshape(DEC_BATCH, DEPTH, PAST_LEN, NA_W)
    na_cv = cache_na_v.reshape(DEC_BATCH, DEPTH, PAST_LEN, NA_W)
    gq_ck = cache_gqa_k.reshape(DEC_BATCH, DEPTH, PAST_LEN, GQA_KV_W)
    gq_cv = cache_gqa_v.reshape(DEC_BATCH, DEPTH, PAST_LEN, GQA_KV_W)
    lat_n0 = state_mlstm_n.reshape(DEC_BATCH, DEPTH, 2 * ML_HEADS, HEAD_DIM)
    lat_m0 = jnp.broadcast_to(state_mlstm_m.reshape(DEC_BATCH, DEPTH, 2 * ML_HEADS, 1),
                              (DEC_BATCH, DEPTH, 2 * ML_HEADS, LANE))
    kv, states = None, []

    for l in range(DEPTH):
        zc, *kv = in_projection(xc, l, mods, _ctx_mod_row, w_pad, bias_row, gains, kv_prev=kv)
        zl = in_projection(xl, l, mods, _lat_tile_mod_row, w_pad, bias_row, gains, cos, sin)

        out_a_c, out_c_c = ctx_attention(zc)
        hf_c, hb_c, sc, sn, sm = mlstm(zc, BATCH, SEQ)
        out_a_l = lat_na(zl, na_ck, na_cv, na_tab, l)
        out_c_l = lat_gqa(zl, gq_ck, gq_cv, l)
        hf_l, hb_l = mlstm(zl, DEC_BATCH, DEC_SEQ, (state_mlstm_c, lat_n0, lat_m0), l)[:2]
        states.append((sc, sn, sm))

        x1_c, h2_c, lg_c = out_projection(xc, out_a_c, hf_c, hb_c, zc, out_c_c, l, w_out_b, ml_gain, mods,
                                          _ctx_mod_row, ln_g4, ln_b4, router_t)
        x1_l, h2_l, lg_l = out_projection(xl, out_a_l, hf_l, hb_l, zl, out_c_l, l, w_out_b, ml_gain, mods,
                                          _lat_out_tile_mod_row, ln_g4, ln_b4, router_t)
        crow_c, code_c, aff_c = route(lg_c, SEQ, CAP_CTX)
        crow_l, code_l, aff_l = route(lg_l, DEC_SEQ, CAP_LAT)
        xe_c = gather_tokens(crow_c, h2_c, SEQ, CAP_CTX)
        xe_l = gather_tokens(crow_l, h2_l, DEC_SEQ, CAP_LAT)
        ye_c, ye_l = experts(xe_c, xe_l, w_gate, w_up, w_down, l)
        xc = combine(code_c, aff_c, ye_c, x1_c, l, mods, _ctx_mod_row, ln_g4, ln_b4, SEQ, CAP_CTX)
        xl = combine(code_l, aff_l, ye_l, x1_l, l, mods, _lat_set_mod_row, ln_g4, ln_b4, DEC_SEQ, CAP_LAT)

    y_prompt = xc.reshape(BATCH, SEQ, D_MODEL)
    y_sample = xl.reshape(DEC_BATCH, DEC_SEQ, D_MODEL)
    na_kv, gq_kv = kv
    heads = lambda a, h: a.reshape(BATCH, DEPTH, SEQ, h, HEAD_DIM)
    new_c = jnp.stack([s[0] for s in states], 1)
    new_n = jnp.stack([s[1] for s in states], 1).reshape(BATCH, DEPTH, 2, ML_HEADS, HEAD_DIM)
    new_m = jnp.stack([s[2][:, :, 0] for s in states], 1).reshape(BATCH, DEPTH, 2, ML_HEADS)
    return (y_prompt, y_sample, heads(na_kv[..., :NA_W], NA_HEADS), heads(na_kv[..., NA_W:], NA_HEADS),
            heads(gq_kv[..., :GQA_KV_W], GQA_KV_HEADS), heads(gq_kv[..., GQA_KV_W:], GQA_KV_HEADS),
            new_c, new_n, new_m)
```

```python
import functools

import jax
import jax.numpy as jnp
import numpy as np
from jax import lax
from jax.experimental import pallas as pl
from jax.experimental.pallas import tpu as pltpu

D_MODEL = 1024
BATCH = 16
SEQ = 256
DEPTH = 2
DEC_BATCH = 2
DEC_SEQ = 2048
PAST_LEN = 256
GRID_W = 64
GRID_H = DEC_SEQ // GRID_W
HEAD_DIM = 64
NA_HEADS = 6
NA_WIN_ROWS = 8
NA_WIN_COLS = 16
ML_HEADS = 4
GQA_HEADS = 6
GQA_KV_HEADS = 2
ROPE_THETA = 10000.0
N_EXPERTS = 16
EC_CAPACITY = 2
EXPERT_HIDDEN = 2816
NORM_EPS = 1e-6
NA_W = NA_HEADS * HEAD_DIM
ML_W = ML_HEADS * HEAD_DIM
GQA_W = GQA_HEADS * HEAD_DIM
GQA_KV_W = GQA_KV_HEADS * HEAD_DIM
N_GATES = 4 * ML_HEADS
DEEPNORM_ALPHA = (2 * DEPTH) ** 0.25
ATTN_SCALE = HEAD_DIM ** -0.5
F32 = jnp.float32
BF16 = jnp.bfloat16

LANE = 128
N_CTX_TOK = BATCH * SEQ
N_LAT_TOK = DEC_BATCH * DEC_SEQ
NEG = -1e30

OFF_NA_Q = 0
OFF_NA_K = OFF_NA_Q + NA_W
OFF_NA_V = OFF_NA_K + NA_W
OFF_ML_Q = OFF_NA_V + NA_W
OFF_ML_K = OFF_ML_Q + ML_W
OFF_ML_V = OFF_ML_K + ML_W
OFF_ML_O = OFF_ML_V + ML_W
OFF_ML_G = OFF_ML_O + ML_W
OFF_GQ_Q = OFF_ML_G + LANE
OFF_GQ_K = OFF_GQ_Q + GQA_W
OFF_GQ_V = OFF_GQ_K + GQA_KV_W
PROJ_PAD = OFF_GQ_V + GQA_KV_W
ML_BLOCK_W = OFF_GQ_Q - OFF_ML_Q

TM = 512
TM_OUT = 1024
ML_CHUNK = 256
EXP_SUB = 256
EXP_TF = EXPERT_HIDDEN // 2
CAP_CTX = EC_CAPACITY * SEQ // N_EXPERTS
CAP_LAT = EC_CAPACITY * DEC_SEQ // N_EXPERTS
SLOTS_CTX = BATCH * CAP_CTX
SLOTS_LAT = DEC_BATCH * CAP_LAT
VMEM_LIMIT = 56 * 1024 * 1024


def _cparams(*sem):
    return pltpu.CompilerParams(dimension_semantics=sem, vmem_limit_bytes=VMEM_LIMIT)


def _sigmoid(x):
    return 1.0 / (1.0 + jnp.exp(-x))


def _dot(a, b):
    return jnp.dot(a, b, preferred_element_type=F32)


def _dot_nt(a, b):
    return lax.dot_general(a, b, (((1,), (1,)), ((), ())), preferred_element_type=F32)


def _dot_tn(a, b):
    return lax.dot_general(a, b, (((0,), (0,)), ((), ())), preferred_element_type=F32)


def _low_half(rows):
    return lax.broadcasted_iota(jnp.int32, (rows, LANE), 1) < HEAD_DIM


def _pair_mean(x, low):
    s_lo = jnp.sum(jnp.where(low, x, 0.0), axis=-1, keepdims=True)
    s_hi = jnp.sum(jnp.where(low, 0.0, x), axis=-1, keepdims=True)
    return jnp.where(low, s_lo, s_hi) * (1.0 / HEAD_DIM)


def _adaln_kernel(c_ref, w_ref, b_ref, o_ref):
    c = c_ref[...]
    s = c * _sigmoid(c)
    o_ref[0] = _dot(s.astype(BF16), w_ref[0].astype(BF16)) + b_ref[0]


def adaln(cond8, ada_w, ada_b):
    tn = 1536
    return pl.pallas_call(
        _adaln_kernel,
        grid=(DEPTH, 6 * D_MODEL // tn),
        in_specs=[
            pl.BlockSpec((8, D_MODEL), lambda l, j: (0, 0)),
            pl.BlockSpec((1, D_MODEL, tn), lambda l, j: (l, 0, j)),
            pl.BlockSpec((1, 1, tn), lambda l, j: (l, 0, j)),
        ],
        out_specs=pl.BlockSpec((1, 8, tn), lambda l, j: (l, 0, j)),
        out_shape=jax.ShapeDtypeStruct((DEPTH, 8, 6 * D_MODEL), F32),
        compiler_params=_cparams("arbitrary", "arbitrary"),
        name="adaln",
    )(cond8, ada_w, ada_b.reshape(DEPTH, 1, 6 * D_MODEL))


def _inproj_kernel(x_ref, sh_ref, sc_ref, w_ref, b_ref, gq_ref, gk_ref, *rest, rope, n_alias, own_layer=0):
    first_layer = n_alias == 0
    if rope:
        cos_ref, sin_ref, z_ref = rest
    else:
        z_ref, nakv_ref, gqkv_ref = rest[n_alias:]
    h = x_ref[...] * (1.0 + sc_ref[0]) + sh_ref[0]
    z = _dot(h.astype(BF16), w_ref[...]) + b_ref[...]
    z_ref[...] = z
    rows = z.shape[0]
    low = _low_half(rows)
    if rope:
        lane = lax.broadcasted_iota(jnp.int32, (rows, LANE), 1)
        first_quarter = (lane & (HEAD_DIM // 4)) == 0

    def norm_pair(x, gain):
        y = x * lax.rsqrt(_pair_mean(x * x, low) + NORM_EPS) * gain
        if rope:
            rot = jnp.where(first_quarter, -pltpu.roll(y, LANE - HEAD_DIM // 4, axis=1),
                            pltpu.roll(y, HEAD_DIM // 4, axis=1))
            y = y * cos_ref[...] + rot * sin_ref[...]
        return y

    for p in range(GQA_W // LANE):
        c0 = OFF_GQ_Q + p * LANE
        z_ref[:, c0:c0 + LANE] = norm_pair(z[:, c0:c0 + LANE], gq_ref[...])
    k_normed = norm_pair(z[:, OFF_GQ_K:OFF_GQ_K + LANE], gk_ref[...])
    z_ref[:, OFF_GQ_K:OFF_GQ_K + LANE] = k_normed
    if not rope:
        for s in range(rows // SEQ):
            r = slice(s * SEQ, (s + 1) * SEQ)
            na, gq = (nakv_ref.at[s, own_layer], gqkv_ref.at[s, own_layer]) if first_layer else (
                nakv_ref.at[s], gqkv_ref.at[s])
            na[...] = z[r, OFF_NA_K:OFF_NA_K + 2 * NA_W]
            gq[:, :GQA_KV_W] = k_normed[r]
            gq[:, GQA_KV_W:] = z[r, OFF_GQ_V:OFF_GQ_V + GQA_KV_W]
            if first_layer:
                for other in range(DEPTH):
                    if other != own_layer:
                        nakv_ref[s, other] = jnp.zeros((SEQ, 2 * NA_W), F32)
                        gqkv_ref[s, other] = jnp.zeros((SEQ, 2 * GQA_KV_W), F32)


def in_projection(x, layer, mods, mod_row, w_pad, bias_row, gains, cos=None, sin=None, kv_prev=None):
    n = x.shape[0]
    rope = cos is not None
    in_specs = [
        pl.BlockSpec((TM, D_MODEL), lambda i: (i, 0)),
        pl.BlockSpec((None, 1, 1, D_MODEL), lambda i: (layer, mod_row(i), 0, 0)),
        pl.BlockSpec((None, 1, 1, D_MODEL), lambda i: (layer, mod_row(i), 0, 1)),
        pl.BlockSpec((None, D_MODEL, PROJ_PAD), lambda i: (layer, 0, 0)),
        pl.BlockSpec((None, 1, PROJ_PAD), lambda i: (layer, 0, 0)),
        pl.BlockSpec((None, None, 1, LANE), lambda i: (layer, 0, 0, 0)),
        pl.BlockSpec((None, None, 1, LANE), lambda i: (layer, 1, 0, 0)),
    ]
    args = [x, mods, mods, w_pad, bias_row, gains, gains]
    z_spec = pl.BlockSpec((TM, PROJ_PAD), lambda i: (i, 0))
    z_shape = jax.ShapeDtypeStruct((n, PROJ_PAD), F32)
    if rope:
        per = cos.shape[0] // TM
        in_specs += [pl.BlockSpec((TM, LANE), lambda i: (i % per, 0))] * 2
        args += [cos, sin]
        return pl.pallas_call(
            functools.partial(_inproj_kernel, rope=True, n_alias=0),
            grid=(n // TM,),
            in_specs=in_specs,
            out_specs=z_spec,
            out_shape=z_shape,
            compiler_params=_cparams("arbitrary"),
            name="in_projection_rope",
        )(*args)
    seqs = TM // SEQ
    kv_shape = lambda w: jax.ShapeDtypeStruct((n // SEQ, DEPTH, SEQ, w), F32)
    aliases = {}
    if kv_prev is None:
        kv_spec = lambda w: pl.BlockSpec((seqs, DEPTH, SEQ, w), lambda i: (i, 0, 0, 0))
    else:
        kv_spec = lambda w: pl.BlockSpec((seqs, None, SEQ, w), lambda i: (i, layer, 0, 0))
        aliases = {len(args): 1, len(args) + 1: 2}
        in_specs += [pl.BlockSpec(memory_space=pl.ANY)] * 2
        args += list(kv_prev)
    return pl.pallas_call(
        functools.partial(_inproj_kernel, rope=False, n_alias=len(aliases), own_layer=layer),
        grid=(n // TM,),
        in_specs=in_specs,
        out_specs=[z_spec, kv_spec(2 * NA_W), kv_spec(2 * GQA_KV_W)],
        out_shape=[z_shape, kv_shape(2 * NA_W), kv_shape(2 * GQA_KV_W)],
        input_output_aliases=aliases,
        compiler_params=_cparams("arbitrary"),
        name="in_projection",
    )(*args)


def _softmax_pv(scores, values):
    m = jnp.max(scores[0], axis=-1, keepdims=True)
    for s in scores[1:]:
        m = jnp.maximum(m, jnp.max(s, axis=-1, keepdims=True))
    l = 0.0
    o = 0.0
    for s, v in zip(scores, values):
        p = jnp.exp(s - m)
        l = l + jnp.sum(p, axis=-1, keepdims=True)
        o = o + _dot(p.astype(BF16), v)
    return o * (1.0 / l)


CTX_SEQS = 4


def _ctx_attn_kernel(naq_ref, nak_ref, nav_ref, gq_ref, gk_ref, gv_ref, oa_ref, oc_ref):
    low = _low_half(SEQ)
    group = GQA_HEADS // GQA_KV_HEADS
    for s in range(CTX_SEQS):
        rows = slice(s * SEQ, (s + 1) * SEQ)
        for p in range(NA_W // LANE):
            cols = slice(p * LANE, (p + 1) * LANE)
            q = naq_ref[rows, cols] * ATTN_SCALE
            k = nak_ref[rows, cols].astype(BF16)
            v = nav_ref[rows, cols].astype(BF16)
            outs = []
            for half in range(2):
                qm = jnp.where(low if half == 0 else ~low, q, 0.0).astype(BF16)
                outs.append(_softmax_pv([_dot_nt(qm, k)], [v]))
            oa_ref[rows, cols] = jnp.where(low, outs[0], outs[1])
        k = gk_ref[rows, :]
        v = gv_ref[rows, :]
        k_at = (pltpu.roll(k, HEAD_DIM, axis=1).astype(BF16), k.astype(BF16))
        v_at = (pltpu.roll(v, HEAD_DIM, axis=1).astype(BF16), v.astype(BF16))
        for p in range(GQA_W // LANE):
            cols = slice(p * LANE, (p + 1) * LANE)
            q = gq_ref[rows, cols] * ATTN_SCALE
            outs = []
            for half in range(2):
                g = (2 * p + half) // group
                qm = jnp.where(low if half == 0 else ~low, q, 0.0).astype(BF16)
                outs.append(_softmax_pv([_dot_nt(qm, k_at[g == half])], [v_at[g == half]]))
            oc_ref[rows, cols] = jnp.where(low, outs[0], outs[1])


def ctx_attention(z):
    col = lambda w, off: pl.BlockSpec((CTX_SEQS * SEQ, w), lambda b: (b, off // w))
    return pl.pallas_call(
        _ctx_attn_kernel,
        grid=(BATCH // CTX_SEQS,),
        in_specs=[col(NA_W, OFF_NA_Q), col(NA_W, OFF_NA_K), col(NA_W, OFF_NA_V),
                  col(GQA_W, OFF_GQ_Q), col(LANE, OFF_GQ_K), col(LANE, OFF_GQ_V)],
        out_specs=[col(NA_W, 0), col(GQA_W, 0)],
        out_shape=[jax.ShapeDtypeStruct((N_CTX_TOK, NA_W), F32), jax.ShapeDtypeStruct((N_CTX_TOK, GQA_W), F32)],
        compiler_params=_cparams("arbitrary"),
        name="ctx_attention",
    )(z, z, z, z, z, z)


GQA_TQ = 512


def _lat_gqa_kernel(q_ref, kl_ref, vl_ref, kc_ref, vc_ref, o_ref):
    low = _low_half(GQA_TQ)
    at = lambda x: (pltpu.roll(x, HEAD_DIM, axis=1).astype(BF16), x.astype(BF16))
    kl_at, vl_at, kc_at, vc_at = at(kl_ref[...]), at(vl_ref[...]), at(kc_ref[0]), at(vc_ref[0])
    group = GQA_HEADS // GQA_KV_HEADS
    for p in range(GQA_W // LANE):
        cols = slice(p * LANE, (p + 1) * LANE)
        q = q_ref[:, cols] * ATTN_SCALE
        outs = []
        for half in range(2):
            same = ((2 * p + half) // group) == half
            qm = jnp.where(low if half == 0 else ~low, q, 0.0).astype(BF16)
            outs.append(_softmax_pv([_dot_nt(qm, kc_at[same]), _dot_nt(qm, kl_at[same])],
                                    [vc_at[same], vl_at[same]]))
        o_ref[:, cols] = jnp.where(low, outs[0], outs[1])


def lat_gqa(z, cache_k, cache_v, layer):
    per = DEC_SEQ // GQA_TQ
    kv = lambda off: pl.BlockSpec((DEC_SEQ, LANE), lambda b, i: (b, off // LANE))
    cache = pl.BlockSpec((1, None, PAST_LEN, LANE), lambda b, i: (b, layer, 0, 0))
    return pl.pallas_call(
        _lat_gqa_kernel,
        grid=(DEC_BATCH, per),
        in_specs=[pl.BlockSpec((GQA_TQ, GQA_W), lambda b, i: (b * per + i, OFF_GQ_Q // GQA_W)),
                  kv(OFF_GQ_K), kv(OFF_GQ_V), cache, cache],
        out_specs=pl.BlockSpec((GQA_TQ, GQA_W), lambda b, i: (b * per + i, 0)),
        out_shape=jax.ShapeDtypeStruct((N_LAT_TOK, GQA_W), F32),
        compiler_params=_cparams("arbitrary", "arbitrary"),
        name="lat_gqa",
    )(z, z, z, cache_k, cache_v)


NA_RB = 4
NA_UNION = NA_RB + NA_WIN_ROWS
NA_TQ = NA_RB * GRID_W
NA_TK = NA_UNION * GRID_W
NA_TILES = 2 * NA_WIN_ROWS


def _na_union_start(i):
    return jnp.clip(i * NA_RB - NA_WIN_ROWS // 2, 0, GRID_H - NA_UNION)


def _na_tile_ids(i):
    u0 = _na_union_start(i)
    ids = []
    for t in range(NA_RB):
        r = i * NA_RB + t
        rs = jnp.clip(r - NA_WIN_ROWS // 2, 0, GRID_H - NA_WIN_ROWS)
        row_ids = []
        for j in range(NA_UNION):
            kr = u0 + j
            inside = (kr >= rs) & (kr < rs + NA_WIN_ROWS)
            row_ids.append(jnp.where(inside, kr - r + NA_WIN_ROWS - 1, NA_TILES - 1))
        ids.append(row_ids)
    return ids


def _na_bias(tab_ref, head, ids):
    rows = [jnp.concatenate([tab_ref[head, 0, row_ids[j]] + tab_ref[head, 1, row_ids[j + 1]]
                             for j in range(0, NA_UNION, 2)], axis=1) for row_ids in ids]
    return jnp.concatenate(rows, axis=0)


NA_BLOCKS = 2


def _lat_na_kernel(q_ref, k_ref, v_ref, kc_ref, vc_ref, tab_ref, o_ref):
    low = _low_half(NA_TQ)
    for t in range(NA_BLOCKS):
        i = pl.program_id(1) * NA_BLOCKS + t
        rows = slice(t * NA_TQ, (t + 1) * NA_TQ)
        start = pl.multiple_of(_na_union_start(i) * GRID_W, GRID_W)
        ids = _na_tile_ids(i)
        for p in range(NA_W // LANE):
            cols = slice(p * LANE, (p + 1) * LANE)
            q = q_ref[rows, cols] * ATTN_SCALE
            kl = k_ref[pl.ds(start, NA_TK), cols].astype(BF16)
            vl = v_ref[pl.ds(start, NA_TK), cols].astype(BF16)
            kc = kc_ref[0, :, cols].astype(BF16)
            vc = vc_ref[0, :, cols].astype(BF16)
            outs = []
            for half in range(2):
                qm = jnp.where(low if half == 0 else ~low, q, 0.0).astype(BF16)
                s_loc = _dot_nt(qm, kl) + _na_bias(tab_ref, 2 * p + half, ids)
                s_ctx = _dot_nt(qm, kc)
                outs.append(_softmax_pv([s_loc, s_ctx], [vl, vc]))
            o_ref[rows, cols] = jnp.where(low, outs[0], outs[1])


def na_bias_table(rpb):
    cq = np.arange(GRID_W)
    ck = np.arange(GRID_W)
    dc = np.clip(ck[None, :] - cq[:, None], -(NA_WIN_COLS - 1), NA_WIN_COLS - 1) + NA_WIN_COLS - 1
    col_start = np.clip(cq - NA_WIN_COLS // 2, 0, GRID_W - NA_WIN_COLS)
    in_win = (ck[None, :] >= col_start[:, None]) & (ck[None, :] < col_start[:, None] + NA_WIN_COLS)
    col_sel = (dc[:, :, None] == np.arange(2 * NA_WIN_COLS - 1)).astype(np.float32)
    tiles = jnp.einsum("lhrc,qkc->lhrqk", rpb.astype(F32), col_sel, precision=lax.Precision.HIGHEST)
    tiles = jnp.where(in_win, tiles, NEG)
    masked = jnp.full(tiles.shape[:2] + (1, GRID_W, GRID_W), NEG, F32)
    tiles = jnp.concatenate([tiles, masked], axis=2)
    zero = jnp.zeros_like(tiles)
    return jnp.stack([jnp.concatenate([tiles, zero], -1), jnp.concatenate([zero, tiles], -1)], axis=2)


def lat_na(z, cache_k, cache_v, tab, layer):
    per = GRID_H // (NA_RB * NA_BLOCKS)
    kv = lambda off: pl.BlockSpec((DEC_SEQ, NA_W), lambda b, i: (b, off // NA_W))
    cache = pl.BlockSpec((1, None, PAST_LEN, NA_W), lambda b, i: (b, layer, 0, 0))
    return pl.pallas_call(
        _lat_na_kernel,
        grid=(DEC_BATCH, per),
        in_specs=[pl.BlockSpec((NA_BLOCKS * NA_TQ, NA_W), lambda b, i: (b * per + i, 0)),
                  kv(OFF_NA_K), kv(OFF_NA_V), cache, cache,
                  pl.BlockSpec((None, NA_HEADS, 2, NA_TILES, GRID_W, LANE), lambda b, i: (layer, 0, 0, 0, 0, 0))],
        out_specs=pl.BlockSpec((NA_BLOCKS * NA_TQ, NA_W), lambda b, i: (b * per + i, 0)),
        out_shape=jax.ShapeDtypeStruct((N_LAT_TOK, NA_W), F32),
        compiler_params=_cparams("arbitrary", "arbitrary"),
        name="lat_na",
    )(z, z, z, cache_k, cache_v, tab)


def _log_sigmoid(x):
    return jnp.minimum(x, 0.0) - jnp.log1p(jnp.exp(-jnp.abs(x)))


def _split3(x):
    hi = x.astype(BF16)
    r = x - hi.astype(F32)
    mid = r.astype(BF16)
    lo = (r - mid.astype(F32)).astype(BF16)
    return hi, mid, lo


def _mlstm_gate_tables(gates, fwd):
    t = gates.shape[0]
    jj = lax.broadcasted_iota(jnp.int32, (t, t), 0)
    ss = lax.broadcasted_iota(jnp.int32, (t, t), 1)
    feeds = (jj <= ss) if fwd else (jj >= ss)
    feeds_b = jnp.where(feeds, 1.0, 0.0).astype(BF16)
    reached_b = jnp.where((ss <= jj) if fwd else (ss >= jj), 1.0, 0.0).astype(BF16)
    eye_b = jnp.where(jj == ss, 1.0, 0.0).astype(BF16)
    lf = _split3(_log_sigmoid(gates))
    gs = _split3(gates)
    b_col = _dot(reached_b, lf[0]) + _dot(reached_b, lf[1]) + _dot(reached_b, lf[2])
    b_row = _dot_tn(lf[0], feeds_b) + _dot_tn(lf[1], feeds_b) + _dot_tn(lf[2], feeds_b)
    ig_row = _dot_tn(gs[0], eye_b) + _dot_tn(gs[1], eye_b) + _dot_tn(gs[2], eye_b)
    return ig_row, b_row, b_col, feeds


def _mlstm_head(k, qt, vt, kt, u, b_row, ig_row, feeds, ct, n, m, half, fwd):
    t = k.shape[0]
    dmat = jnp.where(feeds, b_row + u, NEG)
    m_inter = b_row + m
    m_j = jnp.maximum(m_inter, jnp.max(dmat, axis=0, keepdims=True))
    w = jnp.exp(dmat - m_j)
    decay = jnp.exp(m_inter - m_j)
    first = lax.broadcasted_iota(jnp.int32, (LANE, t), 0) < HEAD_DIM
    mine = first if half == 0 else ~first
    qm = jnp.where(mine, qt, 0.0)
    qb = qm.astype(BF16)
    kb = k.astype(BF16)
    qk = _dot(kb, qb) * w
    num = decay * _dot(ct.astype(BF16), qb) + _dot(vt.astype(BF16), qk.astype(BF16))
    den = decay * jnp.sum(qm * n, axis=0, keepdims=True) + jnp.sum(qk, axis=0, keepdims=True)
    h = jnp.where(mine, num * (1.0 / jnp.maximum(jnp.abs(den), jnp.exp(-m_j))), 0.0)
    last = t - 1 if fwd else 0
    b_last = b_row[:, last:last + 1]
    g = b_last - b_row + ig_row
    m_new = jnp.maximum(b_last + m, jnp.max(g, axis=1, keepdims=True))
    ws = jnp.exp(g - m_new)
    d_last = jnp.exp(b_last + m - m_new)
    ct_add = _dot((vt * ws).astype(BF16), kb)
    n_add = jnp.sum(jnp.where(mine, kt * ws, 0.0), axis=1, keepdims=True)
    return h, d_last, ct_add, n_add, m_new


def _mlstm_kernel(zf_ref, zb_ref, *refs, has_init):
    if has_init:
        c0_ref, n0_ref, m0_ref, hf_ref, hb_ref, co_ref, no_ref, mo_ref, c_s, n_s, m_s = refs
    else:
        hf_ref, hb_ref, co_ref, no_ref, mo_ref, c_s, n_s, m_s = refs
    chunk = pl.program_id(1)
    pairs = ML_W // LANE
    zero = jnp.zeros((HEAD_DIM, HEAD_DIM), F32)

    ri = lax.broadcasted_iota(jnp.int32, (LANE, LANE), 0)
    ci = lax.broadcasted_iota(jnp.int32, (LANE, LANE), 1)
    eye = ri == ci
    first_block = (ri < HEAD_DIM) & (ci < HEAD_DIM)
    second_block = (ri >= HEAD_DIM) & (ci >= HEAD_DIM)
    first_rows = lax.broadcasted_iota(jnp.int32, (LANE, 1), 0) < HEAD_DIM

    @pl.when(chunk == 0)
    def _():
        if not has_init:
            c_s[...] = jnp.zeros_like(c_s)
            n_s[...] = jnp.zeros_like(n_s)
            m_s[...] = jnp.zeros_like(m_s)
            return
        for d in range(2):
            for p in range(pairs):
                top = jnp.concatenate([c0_ref[0, d, 2 * p], zero], axis=1)
                bot = jnp.concatenate([zero, c0_ref[0, d, 2 * p + 1]], axis=1)
                c_s[d * pairs + p] = jnp.concatenate([top, bot], axis=0).T
                e = d * ML_HEADS + 2 * p
                n_row = jnp.concatenate([n0_ref[0, e:e + 1, :], n0_ref[0, e + 1:e + 2, :]], axis=1)
                n_s[d * pairs + p] = jnp.sum(jnp.where(eye, n_row, 0.0), axis=1, keepdims=True)
            m_s[d] = m0_ref[0, d * ML_HEADS:(d + 1) * ML_HEADS, :]

    for d, (z_ref, h_ref) in enumerate(((zf_ref, hf_ref), (zb_ref, hb_ref))):
        gates = z_ref[:, OFF_ML_G - OFF_ML_Q:OFF_ML_G - OFF_ML_Q + LANE]
        ig_rows, b_rows, b_cols, feeds = _mlstm_gate_tables(gates, d == 0)
        u = gates - pltpu.roll(b_cols, LANE - ML_HEADS, axis=1)
        m_all = m_s[d]
        m_rows = []
        for p in range(pairs):
            k = z_ref[:, ML_W + p * LANE:ML_W + (p + 1) * LANE] * ATTN_SCALE
            qt = z_ref[:, p * LANE:(p + 1) * LANE].T
            vt = z_ref[:, 2 * ML_W + p * LANE:2 * ML_W + (p + 1) * LANE].T
            kt = k.T
            ct = c_s[d * pairs + p]
            n = n_s[d * pairs + p]
            res = []
            for half in range(2):
                hd = 2 * p + half
                gi = 2 * ML_HEADS * d + hd
                gf = gi + ML_HEADS
                res.append(_mlstm_head(k, qt, vt, kt, u[:, gi:gi + 1], b_rows[gf:gf + 1, :], ig_rows[gi:gi + 1, :],
                                       feeds, ct, n, m_all[hd:hd + 1, 0:1], half, d == 0))
            (h0, dl0, ca0, na0, mn0), (h1, dl1, ca1, na1, mn1) = res
            h_ref[:, p * LANE:(p + 1) * LANE] = (h0 + h1).T
            c_s[d * pairs + p] = jnp.where(first_block, dl0 * ct + ca0, jnp.where(second_block, dl1 * ct + ca1, 0.0))
            n_s[d * pairs + p] = jnp.where(first_rows, dl0 * n + na0, dl1 * n + na1)
            m_rows += [jnp.broadcast_to(mn0, (1, LANE)), jnp.broadcast_to(mn1, (1, LANE))]
        m_s[d] = jnp.concatenate(m_rows, axis=0)

    @pl.when(chunk == pl.num_programs(1) - 1)
    def _():
        for d in range(2):
            for p in range(pairs):
                c = c_s[d * pairs + p].T
                n_row = jnp.sum(jnp.where(eye, n_s[d * pairs + p], 0.0), axis=0, keepdims=True)
                co_ref[0, d, 2 * p] = c[:HEAD_DIM, :HEAD_DIM]
                co_ref[0, d, 2 * p + 1] = c[HEAD_DIM:, HEAD_DIM:]
                e = d * ML_HEADS + 2 * p
                no_ref[0, e:e + 1, :] = n_row[:, :HEAD_DIM]
                no_ref[0, e + 1:e + 2, :] = n_row[:, HEAD_DIM:]
            mo_ref[0, d * ML_HEADS:(d + 1) * ML_HEADS, :] = m_s[d]


def mlstm(z, batch, seq, init=None, layer=None):
    nc = seq // ML_CHUNK
    zcol = OFF_ML_Q // ML_BLOCK_W
    state_c = pl.BlockSpec((1, 2, ML_HEADS, HEAD_DIM, HEAD_DIM), lambda b, c: (b, 0, 0, 0, 0))
    state_n = pl.BlockSpec((1, 2 * ML_HEADS, HEAD_DIM), lambda b, c: (b, 0, 0))
    state_m = pl.BlockSpec((1, 2 * ML_HEADS, LANE), lambda b, c: (b, 0, 0))
    init_specs = [] if init is None else [
        pl.BlockSpec((1, None, 2, ML_HEADS, HEAD_DIM, HEAD_DIM), lambda b, c: (b, layer, 0, 0, 0, 0)),
        pl.BlockSpec((1, None, 2 * ML_HEADS, HEAD_DIM), lambda b, c: (b, layer, 0, 0)),
        pl.BlockSpec((1, None, 2 * ML_HEADS, LANE), lambda b, c: (b, layer, 0, 0)),
    ]
    pairs = ML_W // LANE
    return pl.pallas_call(
        functools.partial(_mlstm_kernel, has_init=init is not None),
        grid=(batch, nc),
        in_specs=[
            pl.BlockSpec((ML_CHUNK, ML_BLOCK_W), lambda b, c: (b * nc + c, zcol)),
            pl.BlockSpec((ML_CHUNK, ML_BLOCK_W), lambda b, c: (b * nc + nc - 1 - c, zcol)),
        ] + init_specs,
        out_specs=[
            pl.BlockSpec((ML_CHUNK, ML_W), lambda b, c: (b * nc + c, 0)),
            pl.BlockSpec((ML_CHUNK, ML_W), lambda b, c: (b * nc + nc - 1 - c, 0)),
            state_c, state_n, state_m,
        ],
        out_shape=[
            jax.ShapeDtypeStruct((batch * seq, ML_W), F32),
            jax.ShapeDtypeStruct((batch * seq, ML_W), F32),
            jax.ShapeDtypeStruct((batch, 2, ML_HEADS, HEAD_DIM, HEAD_DIM), F32),
            jax.ShapeDtypeStruct((batch, 2 * ML_HEADS, HEAD_DIM), F32),
            jax.ShapeDtypeStruct((batch, 2 * ML_HEADS, LANE), F32),
        ],
        scratch_shapes=[
            pltpu.VMEM((2 * pairs, LANE, LANE), F32),
            pltpu.VMEM((2 * pairs, LANE, 1), F32),
            pltpu.VMEM((2, ML_HEADS, LANE), F32),
        ],
        compiler_params=_cparams("arbitrary", "arbitrary"),
        name="mlstm",
    )(z, z, *(() if init is None else init))


def _layer_norm(u, g, b):
    mu = jnp.mean(u, axis=-1, keepdims=True)
    uc = u - mu
    var = jnp.mean(uc * uc, axis=-1, keepdims=True)
    return uc * lax.rsqrt(var + NORM_EPS) * g + b


def _outproj_kernel(x_ref, a_ref, hf_ref, hb_ref, og0_ref, og1_ref, c_ref, w_ref, mg_ref, g1_ref, sh2_ref, sc2_ref,
                    lng_ref, lnb_ref, rw_ref, x1_ref, h2_ref, lg_ref):
    low = _low_half(TM_OUT)
    mixed_b = []
    for p, og_ref in enumerate((og0_ref, og1_ref)):
        cols = slice(p * LANE, (p + 1) * LANE)
        h = hf_ref[:, cols] + hb_ref[:, cols]
        hc = h - _pair_mean(h, low)
        hn = hc * lax.rsqrt(_pair_mean(hc * hc, low) + NORM_EPS)
        mixed_b.append(hn * mg_ref[:, cols] * _sigmoid(og_ref[...]))
    mixed = jnp.concatenate([a_ref[...]] + mixed_b + [c_ref[...]], axis=-1).astype(BF16)
    y = _dot(mixed, w_ref[...])
    x1 = _layer_norm(DEEPNORM_ALPHA * x_ref[...] + g1_ref[0] * y, lng_ref[...], lnb_ref[...])
    h2 = (x1 * (1.0 + sc2_ref[0]) + sh2_ref[0]).astype(BF16)
    x1_ref[...] = x1
    h2_ref[...] = h2
    lg_ref[...] = _dot_nt(rw_ref[...], h2)


def out_projection(x, out_a, hf, hb, z, out_c, layer, w_out, ml_gain, mods, mod_row, ln_g, ln_b, router_t):
    n = x.shape[0]
    mod = lambda k: pl.BlockSpec((None, 1, 1, D_MODEL), lambda i: (layer, mod_row(i), 0, k))
    row = lambda w: pl.BlockSpec((TM_OUT, w), lambda i: (i, 0))
    zcol = lambda off: pl.BlockSpec((TM_OUT, LANE), lambda i: (i, off // LANE))
    per_layer = lambda r, c: pl.BlockSpec((None, r, c), lambda i: (layer, 0, 0))
    norm0 = pl.BlockSpec((None, None, 1, D_MODEL), lambda i: (layer, 0, 0, 0))
    return pl.pallas_call(
        _outproj_kernel,
        grid=(n // TM_OUT,),
        in_specs=[row(D_MODEL), row(NA_W), row(ML_W), row(ML_W), zcol(OFF_ML_O), zcol(OFF_ML_O + LANE), row(GQA_W),
                  per_layer(D_MODEL, D_MODEL), per_layer(1, ML_W), mod(2), mod(3), mod(4), norm0, norm0,
                  per_layer(N_EXPERTS, D_MODEL)],
        out_specs=[row(D_MODEL), row(D_MODEL), pl.BlockSpec((N_EXPERTS, TM_OUT), lambda i: (0, i))],
        out_shape=[
            jax.ShapeDtypeStruct((n, D_MODEL), F32),
            jax.ShapeDtypeStruct((n, D_MODEL), BF16),
            jax.ShapeDtypeStruct((N_EXPERTS, n), F32),
        ],
        compiler_params=_cparams("arbitrary"),
        name="out_projection",
    )(x, out_a, hf, hb, z, z, out_c, w_out, ml_gain, mods, mods, mods, ln_g, ln_b, router_t)


ROUTE_BLK = 256


def _prefix_count(x, tri):
    n = x.shape[1]
    outs = []
    carry = jnp.zeros((N_EXPERTS, 1), F32)
    for i in range(n // ROUTE_BLK):
        blk = x[:, i * ROUTE_BLK:(i + 1) * ROUTE_BLK]
        outs.append(_dot(blk, tri) + carry)
        carry = carry + jnp.sum(blk.astype(F32), axis=1, keepdims=True)
    return jnp.concatenate(outs, axis=1) if len(outs) > 1 else outs[0]


def _route_kernel(lg_ref, coder_ref, codet_ref, afft_ref, *, n, cap):
    sets = lg_ref.shape[1] // n
    lg = lg_ref[...]
    ex = jnp.exp(lg - jnp.max(lg, axis=0, keepdims=True))
    aff_all = ex / jnp.sum(ex, axis=0, keepdims=True)
    affs = [aff_all[:, s * n:(s + 1) * n] for s in range(sets)]

    def search(i, bits):
        bit = lax.shift_left(jnp.int32(1), 30 - i)
        out = []
        for aff, b in zip(affs, bits):
            cand = b | bit
            cnt = jnp.sum(jnp.where(aff >= pltpu.bitcast(cand, F32), 1.0, 0.0), axis=1, keepdims=True)
            out.append(jnp.where(cnt >= cap, cand, b))
        return tuple(out)

    floors = lax.fori_loop(0, 31, search, tuple(jnp.zeros((N_EXPERTS, 1), jnp.int32) for _ in range(sets)))
    ti = lax.broadcasted_iota(jnp.int32, (ROUTE_BLK, ROUTE_BLK), 0)
    tj = lax.broadcasted_iota(jnp.int32, (ROUTE_BLK, ROUTE_BLK), 1)
    tri = jnp.where(ti < tj, 1.0, 0.0).astype(BF16)
    pad = jnp.zeros((LANE - N_EXPERTS, n), F32)
    for s, (aff, floor_bits) in enumerate(zip(affs, floors)):
        thr = jnp.min(jnp.where(aff >= pltpu.bitcast(floor_bits, F32), aff, 2.0), axis=1, keepdims=True)
        gt = aff > thr
        eq = aff == thr
        need = cap - jnp.sum(jnp.where(gt, 1.0, 0.0), axis=1, keepdims=True)
        eq_rank = _prefix_count(jnp.where(eq, 1.0, 0.0).astype(BF16), tri)
        sel = jnp.where(gt, 1.0, jnp.where(eq & (eq_rank < need), 1.0, 0.0))
        pos = _prefix_count(sel.astype(BF16), tri)
        code = jnp.where(sel > 0.5, pos, -1.0)
        coder_ref[s] = code
        codet_ref[s] = jnp.concatenate([code, pad], axis=0).T
        afft_ref[s] = jnp.concatenate([aff, pad], axis=0).T


def route(logits_t, n, cap):
    tokens = logits_t.shape[1]
    sets = tokens // n
    whole = lambda shape: pl.BlockSpec(shape, lambda i: (0,) * len(shape))
    return pl.pallas_call(
        functools.partial(_route_kernel, n=n, cap=cap),
        grid=(1,),
        in_specs=[whole((N_EXPERTS, tokens))],
        out_specs=[whole((sets, N_EXPERTS, n)), whole((sets, n, LANE)), whole((sets, n, LANE))],
        out_shape=[
            jax.ShapeDtypeStruct((sets, N_EXPERTS, n), F32),
            jax.ShapeDtypeStruct((sets, n, LANE), F32),
            jax.ShapeDtypeStruct((sets, n, LANE), F32),
        ],
        compiler_params=_cparams("arbitrary"),
        name="route",
    )(logits_t)


def _gather_kernel(coder_ref, h_ref, xe_ref, *, cap, epb):
    e0 = pl.program_id(1) * epb
    n = coder_ref.shape[2]
    ci = lax.broadcasted_iota(jnp.int32, (cap, n), 0).astype(F32)
    onehot = [jnp.where(ci == coder_ref[0, pl.ds(e0 + k, 1), :], 1.0, 0.0).astype(BF16) for k in range(epb)]
    onehot = jnp.concatenate(onehot, axis=0) if epb > 1 else onehot[0]
    rows = _dot(onehot, h_ref[...]).astype(BF16)
    for k in range(epb):
        xe_ref[k] = rows[k * cap:(k + 1) * cap]


def gather_tokens(code_rows, h2, n, cap):
    sets = code_rows.shape[0]
    epb = min(N_EXPERTS, max(1, 1024 // cap))
    return pl.pallas_call(
        functools.partial(_gather_kernel, cap=cap, epb=epb),
        grid=(sets, N_EXPERTS // epb),
        in_specs=[
            pl.BlockSpec((1, N_EXPERTS, n), lambda s, e: (s, 0, 0)),
            pl.BlockSpec((n, D_MODEL), lambda s, e: (s, 0)),
        ],
        out_specs=pl.BlockSpec((epb, cap, D_MODEL), lambda s, e: (e, s, 0)),
        out_shape=jax.ShapeDtypeStruct((N_EXPERTS, sets * cap, D_MODEL), BF16),
        compiler_params=_cparams("arbitrary", "arbitrary"),
        name="gather_tokens",
    )(code_rows, h2)


def _expert_kernel(xi_ref, xl_ref, wg_ref, wu_ref, wd_ref, yi_ref, yl_ref, acci_ref, accl_ref):
    j = pl.program_id(1)

    @pl.when(j == 0)
    def _():
        acci_ref[...] = jnp.zeros_like(acci_ref)
        accl_ref[...] = jnp.zeros_like(accl_ref)

    xs = (xi_ref[0], xl_ref[0])

    sums = [None, None]
    for s0 in range(0, EXP_TF, EXP_SUB):
        cols = slice(s0, min(s0 + EXP_SUB, EXP_TF))
        wg = wg_ref[0, 0, :, cols].astype(BF16)
        wu = wu_ref[0, 0, :, cols].astype(BF16)
        wd = wd_ref[0, 0, cols, :].astype(BF16)
        for g in range(2):
            a = _dot(xs[g], wg)
            b = _dot(xs[g], wu)
            hid = (a * _sigmoid(a) * b).astype(BF16)
            part = _dot(hid, wd)
            sums[g] = part if sums[g] is None else sums[g] + part
    acci_ref[...] += sums[0]
    accl_ref[...] += sums[1]

    @pl.when(j == pl.num_programs(1) - 1)
    def _():
        yi_ref[0] = acci_ref[...].astype(BF16)
        yl_ref[0] = accl_ref[...].astype(BF16)


def experts(xe_ctx, xe_lat, w_gate, w_up, w_down, layer):
    xin = lambda s: pl.BlockSpec((1, s, D_MODEL), lambda e, j: (e, 0, 0))
    return pl.pallas_call(
        _expert_kernel,
        grid=(N_EXPERTS, EXPERT_HIDDEN // EXP_TF),
        in_specs=[
            xin(SLOTS_CTX), xin(SLOTS_LAT),
            pl.BlockSpec((1, 1, D_MODEL, EXP_TF), lambda e, j: (layer, e, 0, j)),
            pl.BlockSpec((1, 1, D_MODEL, EXP_TF), lambda e, j: (layer, e, 0, j)),
            pl.BlockSpec((1, 1, EXP_TF, D_MODEL), lambda e, j: (layer, e, j, 0)),
        ],
        out_specs=[xin(SLOTS_CTX), xin(SLOTS_LAT)],
        out_shape=[
            jax.ShapeDtypeStruct((N_EXPERTS, SLOTS_CTX, D_MODEL), BF16),
            jax.ShapeDtypeStruct((N_EXPERTS, SLOTS_LAT, D_MODEL), BF16),
        ],
        scratch_shapes=[pltpu.VMEM((SLOTS_CTX, D_MODEL), F32), pltpu.VMEM((SLOTS_LAT, D_MODEL), F32)],
        compiler_params=_cparams("arbitrary", "arbitrary"),
        name="experts",
    )(xe_ctx, xe_lat, w_gate, w_up, w_down)


COMB_TT = 512


def _scatter_per_expert(code, aff, ye_ref, cap):
    tt = code.shape[0]
    li = lax.broadcasted_iota(jnp.int32, (tt, cap), 1).astype(F32)
    acc = jnp.zeros((tt, D_MODEL), F32)
    for e in range(N_EXPERTS):
        onehot = jnp.where(li == code[:, e:e + 1], 1.0, 0.0).astype(BF16)
        acc = acc + aff[:, e:e + 1] * _dot(onehot, ye_ref[e])
    return acc


def _scatter_merged(code, aff, ye_ref, cap):
    tt = code.shape[0]
    slots = N_EXPERTS * cap
    shift = cap.bit_length() - 1
    ei = lax.broadcasted_iota(jnp.int32, (LANE, slots), 0)
    si = lax.broadcasted_iota(jnp.int32, (LANE, slots), 1)
    expand = jnp.where(lax.shift_right_logical(si, shift) == ei, 1.0, 0.0).astype(BF16)
    slot = (lax.broadcasted_iota(jnp.int32, (tt, slots), 1) & (cap - 1)).astype(F32)
    hit = _dot(code.astype(BF16), expand) == slot
    a_hi = aff.astype(BF16)
    a_lo = (aff - a_hi.astype(F32)).astype(BF16)
    ye = ye_ref[...].reshape(slots, D_MODEL)
    acc = _dot(jnp.where(hit, _dot(a_hi, expand), 0.0).astype(BF16), ye)
    return acc + _dot(jnp.where(hit, _dot(a_lo, expand), 0.0).astype(BF16), ye)


def _combine_kernel(code_ref, aff_ref, ye_ref, x1_ref, g2_ref, lng_ref, lnb_ref, o_ref, *, cap):
    scatter = _scatter_merged if cap < LANE else _scatter_per_expert
    acc = scatter(code_ref[0], aff_ref[0], ye_ref, cap)
    u = DEEPNORM_ALPHA * x1_ref[...] + g2_ref[0] * acc
    o_ref[...] = _layer_norm(u, lng_ref[...], lnb_ref[...])


def combine(code, aff, ye, x1, layer, mods, set_mod_row, ln_g, ln_b, n, cap):
    sets = code.shape[0]
    tt = min(COMB_TT, n)
    per = n // tt
    tok = pl.BlockSpec((1, tt, LANE), lambda s, i: (s, i, 0))
    norm1 = pl.BlockSpec((None, None, 1, D_MODEL), lambda s, i: (layer, 1, 0, 0))
    return pl.pallas_call(
        functools.partial(_combine_kernel, cap=cap),
        grid=(sets, per),
        in_specs=[
            tok, tok,
            pl.BlockSpec((N_EXPERTS, cap, D_MODEL), lambda s, i: (0, s, 0)),
            pl.BlockSpec((tt, D_MODEL), lambda s, i: (s * per + i, 0)),
            pl.BlockSpec((None, 1, 1, D_MODEL), lambda s, i: (layer, set_mod_row(s), 0, 5)),
            norm1, norm1,
        ],
        out_specs=pl.BlockSpec((tt, D_MODEL), lambda s, i: (s * per + i, 0)),
        out_shape=jax.ShapeDtypeStruct((sets * n, D_MODEL), F32),
        compiler_params=_cparams("arbitrary", "arbitrary"),
        name="combine",
    )(code, aff, ye, x1, mods, ln_g, ln_b)


def _axial_rope_tables():
    t = np.arange(DEC_SEQ)
    row = (t // GRID_W).astype(np.float32)
    col = (t % GRID_W).astype(np.float32)
    half = HEAD_DIM // 2
    inv = (ROPE_THETA ** (-np.arange(0, half, 2, dtype=np.float32) / half)).astype(np.float32)
    ang_r = row[:, None] * inv
    ang_c = col[:, None] * inv
    ang = np.concatenate([ang_r, ang_r, ang_c, ang_c] * 2, -1)
    return jnp.asarray(np.cos(ang), F32), jnp.asarray(np.sin(ang), F32)


def _ctx_mod_row(i):
    return 0


def _lat_tile_mod_row(i):
    return 1 + i // (DEC_SEQ // TM)


def _lat_out_tile_mod_row(i):
    return 1 + i // (DEC_SEQ // TM_OUT)


def _lat_set_mod_row(s):
    return 1 + s


def kernel(x_prompt, x_sample, c, cache_na_k, cache_na_v, cache_gqa_k, cache_gqa_v, state_mlstm_c, state_mlstm_n,
           state_mlstm_m, c_ctx, ada_w, ada_b, w_in, b_gate, w_out, na_rpb, qk_norm_g, ml_norm_g, ln_g, ln_b,
           router_w, w_gate, w_up, w_down):
    cond8 = jnp.concatenate([c_ctx[None, :], c, jnp.zeros((8 - 1 - DEC_BATCH, D_MODEL), F32)], 0)
    mods = adaln(cond8, ada_w, ada_b).reshape(DEPTH, 8, 1, 6 * D_MODEL)
    xc = x_prompt.reshape(N_CTX_TOK, D_MODEL)
    xl = x_sample.reshape(N_LAT_TOK, D_MODEL)
    cos, sin = _axial_rope_tables()
    na_tab = na_bias_table(na_rpb)

    split = OFF_ML_G + N_GATES
    w_pad = jnp.concatenate([w_in[:, :, :split], jnp.zeros((DEPTH, D_MODEL, LANE - N_GATES), F32),
                             w_in[:, :, split:]], 2).astype(BF16)
    bias_row = jnp.pad(b_gate, ((0, 0), (OFF_ML_G, PROJ_PAD - OFF_ML_G - N_GATES))).reshape(DEPTH, 1, PROJ_PAD)
    gains = jnp.tile(qk_norm_g, (1, 1, 2)).reshape(DEPTH, 2, 1, LANE)
    w_out_b = w_out.astype(BF16)
    router_t = jnp.swapaxes(router_w, 1, 2).astype(BF16)
    ml_gain = ml_norm_g.reshape(DEPTH, 1, ML_W)
    ln_g4 = ln_g.reshape(DEPTH, 2, 1, D_MODEL)
    ln_b4 = ln_b.reshape(DEPTH, 2, 1, D_MODEL)
    na_ck = cache_na_k.reshape(DEC_BATCH, DEPTH, PAST_LEN, NA_W)
    na_cv = cache_na_v.reshape(DEC_BATCH, DEPTH, PAST_LEN, NA_W)
    gq_ck = cache_gqa_k.reshape(DEC_BATCH, DEPTH, PAST_LEN, GQA_KV_W)
    gq_cv = cache_gqa_v.reshape(DEC_BATCH, DEPTH, PAST_LEN, GQA_KV_W)
    lat_n0 = state_mlstm_n.reshape(DEC_BATCH, DEPTH, 2 * ML_HEADS, HEAD_DIM)
    lat_m0 = jnp.broadcast_to(state_mlstm_m.reshape(DEC_BATCH, DEPTH, 2 * ML_HEADS, 1),
                              (DEC_BATCH, DEPTH, 2 * ML_HEADS, LANE))
    kv, states = None, []

    for l in range(DEPTH):
        zc, *kv = in_projection(xc, l, mods, _ctx_mod_row, w_pad, bias_row, gains, kv_prev=kv)
        zl = in_projection(xl, l, mods, _lat_tile_mod_row, w_pad, bias_row, gains, cos, sin)

        out_a_c, out_c_c = ctx_attention(zc)
        hf_c, hb_c, sc, sn, sm = mlstm(zc, BATCH, SEQ)
        out_a_l = lat_na(zl, na_ck, na_cv, na_tab, l)
        out_c_l = lat_gqa(zl, gq_ck, gq_cv, l)
        hf_l, hb_l = mlstm(zl, DEC_BATCH, DEC_SEQ, (state_mlstm_c, lat_n0, lat_m0), l)[:2]
        states.append((sc, sn, sm))

        x1_c, h2_c, lg_c = out_projection(xc, out_a_c, hf_c, hb_c, zc, out_c_c, l, w_out_b, ml_gain, mods,
                                          _ctx_mod_row, ln_g4, ln_b4, router_t)
        x1_l, h2_l, lg_l = out_projection(xl, out_a_l, hf_l, hb_l, zl, out_c_l, l, w_out_b, ml_gain, mods,
                                          _lat_out_tile_mod_row, ln_g4, ln_b4, router_t)
        crow_c, code_c, aff_c = route(lg_c, SEQ, CAP_CTX)
        crow_l, code_l, aff_l = route(lg_l, DEC_SEQ, CAP_LAT)
        xe_c = gather_tokens(crow_c, h2_c, SEQ, CAP_CTX)
        xe_l = gather_tokens(crow_l, h2_l, DEC_SEQ, CAP_LAT)
        ye_c, ye_l = experts(xe_c, xe_l, w_gate, w_up, w_down, l)
        xc = combine(code_c, aff_c, ye_c, x1_c, l, mods, _ctx_mod_row, ln_g4, ln_b4, SEQ, CAP_CTX)
        xl = combine(code_l, aff_l, ye_l, x1_l, l, mods, _lat_set_mod_row, ln_g4, ln_b4, DEC_SEQ, CAP_LAT)

    y_prompt = xc.reshape(BATCH, SEQ, D_MODEL)
    y_sample = xl.reshape(DEC_BATCH, DEC_SEQ, D_MODEL)
    na_kv, gq_kv = kv
    heads = lambda a, h: a.reshape(BATCH, DEPTH, SEQ, h, HEAD_DIM)
    new_c = jnp.stack([s[0] for s in states], 1)
    new_n = jnp.stack([s[1] for s in states], 1).reshape(BATCH, DEPTH, 2, ML_HEADS, HEAD_DIM)
    new_m = jnp.stack([s[2][:, :, 0] for s in states], 1).reshape(BATCH, DEPTH, 2, ML_HEADS)
    return (y_prompt, y_sample, heads(na_kv[..., :NA_W], NA_HEADS), heads(na_kv[..., NA_W:], NA_HEADS),
            heads(gq_kv[..., :GQA_KV_W], GQA_KV_HEADS), heads(gq_kv[..., GQA_KV_W:], GQA_KV_HEADS),
            new_c, new_n, new_m)
```

```python
import functools

import jax
import jax.numpy as jnp
import numpy as np
from jax import lax
from jax.experimental import pallas as pl
from jax.experimental.pallas import tpu as pltpu

D_MODEL = 1024
BATCH = 16
SEQ = 256
DEPTH = 2
DEC_BATCH = 2
DEC_SEQ = 2048
PAST_LEN = 256
GRID_W = 64
GRID_H = DEC_SEQ // GRID_W
HEAD_DIM = 64
NA_HEADS = 6
NA_WIN_ROWS = 8
NA_WIN_COLS = 16
ML_HEADS = 4
GQA_HEADS = 6
GQA_KV_HEADS = 2
ROPE_THETA = 10000.0
N_EXPERTS = 16
EC_CAPACITY = 2
EXPERT_HIDDEN = 2816
NORM_EPS = 1e-6
NA_W = NA_HEADS * HEAD_DIM
ML_W = ML_HEADS * HEAD_DIM
GQA_W = GQA_HEADS * HEAD_DIM
GQA_KV_W = GQA_KV_HEADS * HEAD_DIM
N_GATES = 4 * ML_HEADS
DEEPNORM_ALPHA = (2 * DEPTH) ** 0.25
ATTN_SCALE = HEAD_DIM ** -0.5
F32 = jnp.float32
BF16 = jnp.bfloat16

LANE = 128
N_CTX_TOK = BATCH * SEQ
N_LAT_TOK = DEC_BATCH * DEC_SEQ
NEG = -1e30

OFF_NA_Q = 0
OFF_NA_K = OFF_NA_Q + NA_W
OFF_NA_V = OFF_NA_K + NA_W
OFF_ML_Q = OFF_NA_V + NA_W
OFF_ML_K = OFF_ML_Q + ML_W
OFF_ML_V = OFF_ML_K + ML_W
OFF_ML_O = OFF_ML_V + ML_W
OFF_ML_G = OFF_ML_O + ML_W
OFF_GQ_Q = OFF_ML_G + LANE
OFF_GQ_K = OFF_GQ_Q + GQA_W
OFF_GQ_V = OFF_GQ_K + GQA_KV_W
PROJ_PAD = OFF_GQ_V + GQA_KV_W
ML_BLOCK_W = OFF_GQ_Q - OFF_ML_Q

TM = 512
TM_OUT = 1024
ML_CHUNK = 256
EXP_SUB = 256
EXP_TF = EXPERT_HIDDEN // 2
CAP_CTX = EC_CAPACITY * SEQ // N_EXPERTS
CAP_LAT = EC_CAPACITY * DEC_SEQ // N_EXPERTS
SLOTS_CTX = BATCH * CAP_CTX
SLOTS_LAT = DEC_BATCH * CAP_LAT
VMEM_LIMIT = 56 * 1024 * 1024


def _cparams(*sem):
    return pltpu.CompilerParams(dimension_semantics=sem, vmem_limit_bytes=VMEM_LIMIT)


def _sigmoid(x):
    return 1.0 / (1.0 + jnp.exp(-x))


def _dot(a, b):
    return jnp.dot(a, b, preferred_element_type=F32)


def _dot_nt(a, b):
    return lax.dot_general(a, b, (((1,), (1,)), ((), ())), preferred_element_type=F32)


def _dot_tn(a, b):
    return lax.dot_general(a, b, (((0,), (0,)), ((), ())), preferred_element_type=F32)


def _low_half(rows):
    return lax.broadcasted_iota(jnp.int32, (rows, LANE), 1) < HEAD_DIM


def _pair_mean(x, low):
    s_lo = jnp.sum(jnp.where(low, x, 0.0), axis=-1, keepdims=True)
    s_hi = jnp.sum(jnp.where(low, 0.0, x), axis=-1, keepdims=True)
    return jnp.where(low, s_lo, s_hi) * (1.0 / HEAD_DIM)


def _adaln_kernel(c_ref, w_ref, b_ref, o_ref):
    c = c_ref[...]
    s = c * _sigmoid(c)
    o_ref[0] = _dot(s.astype(BF16), w_ref[0].astype(BF16)) + b_ref[0]


def adaln(cond8, ada_w, ada_b):
    tn = 1536
    return pl.pallas_call(
        _adaln_kernel,
        grid=(DEPTH, 6 * D_MODEL // tn),
        in_specs=[
            pl.BlockSpec((8, D_MODEL), lambda l, j: (0, 0)),
            pl.BlockSpec((1, D_MODEL, tn), lambda l, j: (l, 0, j)),
            pl.BlockSpec((1, 1, tn), lambda l, j: (l, 0, j)),
        ],
        out_specs=pl.BlockSpec((1, 8, tn), lambda l, j: (l, 0, j)),
        out_shape=jax.ShapeDtypeStruct((DEPTH, 8, 6 * D_MODEL), F32),
        compiler_params=_cparams("arbitrary", "arbitrary"),
        name="adaln",
    )(cond8, ada_w, ada_b.reshape(DEPTH, 1, 6 * D_MODEL))


def _inproj_kernel(x_ref, sh_ref, sc_ref, w_ref, b_ref, gq_ref, gk_ref, *rest, rope, n_alias, own_layer=0):
    first_layer = n_alias == 0
    if rope:
        cos_ref, sin_ref, z_ref = rest
    else:
        z_ref, nakv_ref, gqkv_ref = rest[n_alias:]
    h = x_ref[...] * (1.0 + sc_ref[0]) + sh_ref[0]
    z = _dot(h.astype(BF16), w_ref[...]) + b_ref[...]
    z_ref[...] = z
    rows = z.shape[0]
    low = _low_half(rows)
    if rope:
        lane = lax.broadcasted_iota(jnp.int32, (rows, LANE), 1)
        first_quarter = (lane & (HEAD_DIM // 4)) == 0

    def norm_pair(x, gain):
        y = x * lax.rsqrt(_pair_mean(x * x, low) + NORM_EPS) * gain
        if rope:
            rot = jnp.where(first_quarter, -pltpu.roll(y, LANE - HEAD_DIM // 4, axis=1),
                            pltpu.roll(y, HEAD_DIM // 4, axis=1))
            y = y * cos_ref[...] + rot * sin_ref[...]
        return y

    for p in range(GQA_W // LANE):
        c0 = OFF_GQ_Q + p * LANE
        z_ref[:, c0:c0 + LANE] = norm_pair(z[:, c0:c0 + LANE], gq_ref[...])
    k_normed = norm_pair(z[:, OFF_GQ_K:OFF_GQ_K + LANE], gk_ref[...])
    z_ref[:, OFF_GQ_K:OFF_GQ_K + LANE] = k_normed
    if not rope:
        for s in range(rows // SEQ):
            r = slice(s * SEQ, (s + 1) * SEQ)
            na, gq = (nakv_ref.at[s, own_layer], gqkv_ref.at[s, own_layer]) if first_layer else (
                nakv_ref.at[s], gqkv_ref.at[s])
            na[...] = z[r, OFF_NA_K:OFF_NA_K + 2 * NA_W]
            gq[:, :GQA_KV_W] = k_normed[r]
            gq[:, GQA_KV_W:] = z[r, OFF_GQ_V:OFF_GQ_V + GQA_KV_W]
            if first_layer:
                for other in range(DEPTH):
                    if other != own_layer:
                        nakv_ref[s, other] = jnp.zeros((SEQ, 2 * NA_W), F32)
                        gqkv_ref[s, other] = jnp.zeros((SEQ, 2 * GQA_KV_W), F32)


def in_projection(x, layer, mods, mod_row, w_pad, bias_row, gains, cos=None, sin=None, kv_prev=None):
    n = x.shape[0]
    rope = cos is not None
    in_specs = [
        pl.BlockSpec((TM, D_MODEL), lambda i: (i, 0)),
        pl.BlockSpec((None, 1, 1, D_MODEL), lambda i: (layer, mod_row(i), 0, 0)),
        pl.BlockSpec((None, 1, 1, D_MODEL), lambda i: (layer, mod_row(i), 0, 1)),
        pl.BlockSpec((None, D_MODEL, PROJ_PAD), lambda i: (layer, 0, 0)),
        pl.BlockSpec((None, 1, PROJ_PAD), lambda i: (layer, 0, 0)),
        pl.BlockSpec((None, None, 1, LANE), lambda i: (layer, 0, 0, 0)),
        pl.BlockSpec((None, None, 1, LANE), lambda i: (layer, 1, 0, 0)),
    ]
    args = [x, mods, mods, w_pad, bias_row, gains, gains]
    z_spec = pl.BlockSpec((TM, PROJ_PAD), lambda i: (i, 0))
    z_shape = jax.ShapeDtypeStruct((n, PROJ_PAD), F32)
    if rope:
        per = cos.shape[0] // TM
        in_specs += [pl.BlockSpec((TM, LANE), lambda i: (i % per, 0))] * 2
        args += [cos, sin]
        return pl.pallas_call(
            functools.partial(_inproj_kernel, rope=True, n_alias=0),
            grid=(n // TM,),
            in_specs=in_specs,
            out_specs=z_spec,
            out_shape=z_shape,
            compiler_params=_cparams("arbitrary"),
            name="in_projection_rope",
        )(*args)
    seqs = TM // SEQ
    kv_shape = lambda w: jax.ShapeDtypeStruct((n // SEQ, DEPTH, SEQ, w), F32)
    aliases = {}
    if kv_prev is None:
        kv_spec = lambda w: pl.BlockSpec((seqs, DEPTH, SEQ, w), lambda i: (i, 0, 0, 0))
    else:
        kv_spec = lambda w: pl.BlockSpec((seqs, None, SEQ, w), lambda i: (i, layer, 0, 0))
        aliases = {len(args): 1, len(args) + 1: 2}
        in_specs += [pl.BlockSpec(memory_space=pl.ANY)] * 2
        args += list(kv_prev)
    return pl.pallas_call(
        functools.partial(_inproj_kernel, rope=False, n_alias=len(aliases), own_layer=layer),
        grid=(n // TM,),
        in_specs=in_specs,
        out_specs=[z_spec, kv_spec(2 * NA_W), kv_spec(2 * GQA_KV_W)],
        out_shape=[z_shape, kv_shape(2 * NA_W), kv_shape(2 * GQA_KV_W)],
        input_output_aliases=aliases,
        compiler_params=_cparams("arbitrary"),
        name="in_projection",
    )(*args)


def _softmax_pv(scores, values):
    m = jnp.max(scores[0], axis=-1, keepdims=True)
    for s in scores[1:]:
        m = jnp.maximum(m, jnp.max(s, axis=-1, keepdims=True))
    l = 0.0
    o = 0.0
    for s, v in zip(scores, values):
        p = jnp.exp(s - m)
        l = l + jnp.sum(p, axis=-1, keepdims=True)
        o = o + _dot(p.astype(BF16), v)
    return o * (1.0 / l)


CTX_SEQS = 4


def _ctx_attn_kernel(naq_ref, nak_ref, nav_ref, gq_ref, gk_ref, gv_ref, oa_ref, oc_ref):
    low = _low_half(SEQ)
    group = GQA_HEADS // GQA_KV_HEADS
    for s in range(CTX_SEQS):
        rows = slice(s * SEQ, (s + 1) * SEQ)
        for p in range(NA_W // LANE):
            cols = slice(p * LANE, (p + 1) * LANE)
            q = naq_ref[rows, cols] * ATTN_SCALE
            k = nak_ref[rows, cols].astype(BF16)
            v = nav_ref[rows, cols].astype(BF16)
            outs = []
            for half in range(2):
                qm = jnp.where(low if half == 0 else ~low, q, 0.0).astype(BF16)
                outs.append(_softmax_pv([_dot_nt(qm, k)], [v]))
            oa_ref[rows, cols] = jnp.where(low, outs[0], outs[1]).astype(BF16)
        k = gk_ref[rows, :]
        v = gv_ref[rows, :]
        k_at = (pltpu.roll(k, HEAD_DIM, axis=1).astype(BF16), k.astype(BF16))
        v_at = (pltpu.roll(v, HEAD_DIM, axis=1).astype(BF16), v.astype(BF16))
        for p in range(GQA_W // LANE):
            cols = slice(p * LANE, (p + 1) * LANE)
            q = gq_ref[rows, cols] * ATTN_SCALE
            outs = []
            for half in range(2):
                g = (2 * p + half) // group
                qm = jnp.where(low if half == 0 else ~low, q, 0.0).astype(BF16)
                outs.append(_softmax_pv([_dot_nt(qm, k_at[g == half])], [v_at[g == half]]))
            oc_ref[rows, cols] = jnp.where(low, outs[0], outs[1]).astype(BF16)


def ctx_attention(z):
    col = lambda w, off: pl.BlockSpec((CTX_SEQS * SEQ, w), lambda b: (b, off // w))
    return pl.pallas_call(
        _ctx_attn_kernel,
        grid=(BATCH // CTX_SEQS,),
        in_specs=[col(NA_W, OFF_NA_Q), col(NA_W, OFF_NA_K), col(NA_W, OFF_NA_V),
                  col(GQA_W, OFF_GQ_Q), col(LANE, OFF_GQ_K), col(LANE, OFF_GQ_V)],
        out_specs=[col(NA_W, 0), col(GQA_W, 0)],
        out_shape=[jax.ShapeDtypeStruct((N_CTX_TOK, NA_W), BF16), jax.ShapeDtypeStruct((N_CTX_TOK, GQA_W), BF16)],
        compiler_params=_cparams("arbitrary"),
        name="ctx_attention",
    )(z, z, z, z, z, z)


GQA_TQ = 512


def _lat_gqa_kernel(q_ref, kl_ref, vl_ref, kc_ref, vc_ref, o_ref):
    low = _low_half(GQA_TQ)
    at = lambda x: (pltpu.roll(x, HEAD_DIM, axis=1).astype(BF16), x.astype(BF16))
    kl_at, vl_at, kc_at, vc_at = at(kl_ref[...]), at(vl_ref[...]), at(kc_ref[0]), at(vc_ref[0])
    group = GQA_HEADS // GQA_KV_HEADS
    for p in range(GQA_W // LANE):
        cols = slice(p * LANE, (p + 1) * LANE)
        q = q_ref[:, cols] * ATTN_SCALE
        outs = []
        for half in range(2):
            same = ((2 * p + half) // group) == half
            qm = jnp.where(low if half == 0 else ~low, q, 0.0).astype(BF16)
            outs.append(_softmax_pv([_dot_nt(qm, kc_at[same]), _dot_nt(qm, kl_at[same])],
                                    [vc_at[same], vl_at[same]]))
        o_ref[:, cols] = jnp.where(low, outs[0], outs[1]).astype(BF16)


def lat_gqa(z, cache_k, cache_v, layer):
    per = DEC_SEQ // GQA_TQ
    kv = lambda off: pl.BlockSpec((DEC_SEQ, LANE), lambda b, i: (b, off // LANE))
    cache = pl.BlockSpec((1, None, PAST_LEN, LANE), lambda b, i: (b, layer, 0, 0))
    return pl.pallas_call(
        _lat_gqa_kernel,
        grid=(DEC_BATCH, per),
        in_specs=[pl.BlockSpec((GQA_TQ, GQA_W), lambda b, i: (b * per + i, OFF_GQ_Q // GQA_W)),
                  kv(OFF_GQ_K), kv(OFF_GQ_V), cache, cache],
        out_specs=pl.BlockSpec((GQA_TQ, GQA_W), lambda b, i: (b * per + i, 0)),
        out_shape=jax.ShapeDtypeStruct((N_LAT_TOK, GQA_W), BF16),
        compiler_params=_cparams("arbitrary", "arbitrary"),
        name="lat_gqa",
    )(z, z, z, cache_k, cache_v)


NA_RB = 4
NA_UNION = NA_RB + NA_WIN_ROWS
NA_TQ = NA_RB * GRID_W
NA_TK = NA_UNION * GRID_W
NA_TILES = 2 * NA_WIN_ROWS


def _na_union_start(i):
    return jnp.clip(i * NA_RB - NA_WIN_ROWS // 2, 0, GRID_H - NA_UNION)


def _na_tile_ids(i):
    u0 = _na_union_start(i)
    ids = []
    for t in range(NA_RB):
        r = i * NA_RB + t
        rs = jnp.clip(r - NA_WIN_ROWS // 2, 0, GRID_H - NA_WIN_ROWS)
        row_ids = []
        for j in range(NA_UNION):
            kr = u0 + j
            inside = (kr >= rs) & (kr < rs + NA_WIN_ROWS)
            row_ids.append(jnp.where(inside, kr - r + NA_WIN_ROWS - 1, NA_TILES - 1))
        ids.append(row_ids)
    return ids


def _na_bias(tab_ref, head, ids):
    rows = [jnp.concatenate([tab_ref[head, 0, row_ids[j]] + tab_ref[head, 1, row_ids[j + 1]]
                             for j in range(0, NA_UNION, 2)], axis=1) for row_ids in ids]
    return jnp.concatenate(rows, axis=0)


NA_BLOCKS = 2


def _lat_na_kernel(q_ref, k_ref, v_ref, kc_ref, vc_ref, tab_ref, o_ref):
    low = _low_half(NA_TQ)
    for t in range(NA_BLOCKS):
        i = pl.program_id(1) * NA_BLOCKS + t
        rows = slice(t * NA_TQ, (t + 1) * NA_TQ)
        start = pl.multiple_of(_na_union_start(i) * GRID_W, GRID_W)
        ids = _na_tile_ids(i)
        for p in range(NA_W // LANE):
            cols = slice(p * LANE, (p + 1) * LANE)
            q = q_ref[rows, cols] * ATTN_SCALE
            kl = k_ref[pl.ds(start, NA_TK), cols].astype(BF16)
            vl = v_ref[pl.ds(start, NA_TK), cols].astype(BF16)
            kc = kc_ref[0, :, cols].astype(BF16)
            vc = vc_ref[0, :, cols].astype(BF16)
            outs = []
            for half in range(2):
                qm = jnp.where(low if half == 0 else ~low, q, 0.0).astype(BF16)
                s_loc = _dot_nt(qm, kl) + _na_bias(tab_ref, 2 * p + half, ids)
                s_ctx = _dot_nt(qm, kc)
                outs.append(_softmax_pv([s_loc, s_ctx], [vl, vc]))
            o_ref[rows, cols] = jnp.where(low, outs[0], outs[1]).astype(BF16)


def na_bias_table(rpb):
    cq = np.arange(GRID_W)
    ck = np.arange(GRID_W)
    dc = np.clip(ck[None, :] - cq[:, None], -(NA_WIN_COLS - 1), NA_WIN_COLS - 1) + NA_WIN_COLS - 1
    col_start = np.clip(cq - NA_WIN_COLS // 2, 0, GRID_W - NA_WIN_COLS)
    in_win = (ck[None, :] >= col_start[:, None]) & (ck[None, :] < col_start[:, None] + NA_WIN_COLS)
    col_sel = (dc[:, :, None] == np.arange(2 * NA_WIN_COLS - 1)).astype(np.float32)
    tiles = jnp.einsum("lhrc,qkc->lhrqk", rpb.astype(F32), col_sel, precision=lax.Precision.HIGHEST)
    tiles = jnp.where(in_win, tiles, NEG)
    masked = jnp.full(tiles.shape[:2] + (1, GRID_W, GRID_W), NEG, F32)
    tiles = jnp.concatenate([tiles, masked], axis=2)
    zero = jnp.zeros_like(tiles)
    return jnp.stack([jnp.concatenate([tiles, zero], -1), jnp.concatenate([zero, tiles], -1)], axis=2)


def lat_na(z, cache_k, cache_v, tab, layer):
    per = GRID_H // (NA_RB * NA_BLOCKS)
    kv = lambda off: pl.BlockSpec((DEC_SEQ, NA_W), lambda b, i: (b, off // NA_W))
    cache = pl.BlockSpec((1, None, PAST_LEN, NA_W), lambda b, i: (b, layer, 0, 0))
    return pl.pallas_call(
        _lat_na_kernel,
        grid=(DEC_BATCH, per),
        in_specs=[pl.BlockSpec((NA_BLOCKS * NA_TQ, NA_W), lambda b, i: (b * per + i, 0)),
                  kv(OFF_NA_K), kv(OFF_NA_V), cache, cache,
                  pl.BlockSpec((None, NA_HEADS, 2, NA_TILES, GRID_W, LANE), lambda b, i: (layer, 0, 0, 0, 0, 0))],
        out_specs=pl.BlockSpec((NA_BLOCKS * NA_TQ, NA_W), lambda b, i: (b * per + i, 0)),
        out_shape=jax.ShapeDtypeStruct((N_LAT_TOK, NA_W), BF16),
        compiler_params=_cparams("arbitrary", "arbitrary"),
        name="lat_na",
    )(z, z, z, cache_k, cache_v, tab)


def _log_sigmoid(x):
    return jnp.minimum(x, 0.0) - jnp.log1p(jnp.exp(-jnp.abs(x)))


def _split3(x):
    hi = x.astype(BF16)
    r = x - hi.astype(F32)
    mid = r.astype(BF16)
    lo = (r - mid.astype(F32)).astype(BF16)
    return hi, mid, lo


def _mlstm_gate_tables(gates, fwd):
    t = gates.shape[0]
    jj = lax.broadcasted_iota(jnp.int32, (t, t), 0)
    ss = lax.broadcasted_iota(jnp.int32, (t, t), 1)
    feeds = (jj <= ss) if fwd else (jj >= ss)
    feeds_b = jnp.where(feeds, 1.0, 0.0).astype(BF16)
    reached_b = jnp.where((ss <= jj) if fwd else (ss >= jj), 1.0, 0.0).astype(BF16)
    eye_b = jnp.where(jj == ss, 1.0, 0.0).astype(BF16)
    lf = _split3(_log_sigmoid(gates))
    gs = _split3(gates)
    b_col = _dot(reached_b, lf[0]) + _dot(reached_b, lf[1]) + _dot(reached_b, lf[2])
    b_row = _dot_tn(lf[0], feeds_b) + _dot_tn(lf[1], feeds_b) + _dot_tn(lf[2], feeds_b)
    ig_row = _dot_tn(gs[0], eye_b) + _dot_tn(gs[1], eye_b) + _dot_tn(gs[2], eye_b)
    return ig_row, b_row, b_col, feeds


def _mlstm_head(k, qt, vt, kt, u, b_row, ig_row, feeds, ct, n, m, half, fwd):
    t = k.shape[0]
    dmat = jnp.where(feeds, b_row + u, NEG)
    m_inter = b_row + m
    m_j = jnp.maximum(m_inter, jnp.max(dmat, axis=0, keepdims=True))
    w = jnp.exp(dmat - m_j)
    decay = jnp.exp(m_inter - m_j)
    first = lax.broadcasted_iota(jnp.int32, (LANE, t), 0) < HEAD_DIM
    mine = first if half == 0 else ~first
    qm = jnp.where(mine, qt, 0.0)
    qb = qm.astype(BF16)
    kb = k.astype(BF16)
    qk = _dot(kb, qb) * w
    num = decay * _dot(ct.astype(BF16), qb) + _dot(vt.astype(BF16), qk.astype(BF16))
    den = decay * jnp.sum(qm * n, axis=0, keepdims=True) + jnp.sum(qk, axis=0, keepdims=True)
    h = jnp.where(mine, num * (1.0 / jnp.maximum(jnp.abs(den), jnp.exp(-m_j))), 0.0)
    last = t - 1 if fwd else 0
    b_last = b_row[:, last:last + 1]
    g = b_last - b_row + ig_row
    m_new = jnp.maximum(b_last + m, jnp.max(g, axis=1, keepdims=True))
    ws = jnp.exp(g - m_new)
    d_last = jnp.exp(b_last + m - m_new)
    ct_add = _dot((vt * ws).astype(BF16), kb)
    n_add = jnp.sum(jnp.where(mine, kt * ws, 0.0), axis=1, keepdims=True)
    return h, d_last, ct_add, n_add, m_new


def _mlstm_kernel(zf_ref, zb_ref, *refs, has_init):
    if has_init:
        c0_ref, n0_ref, m0_ref, hf_ref, hb_ref, co_ref, no_ref, mo_ref, c_s, n_s, m_s = refs
    else:
        hf_ref, hb_ref, co_ref, no_ref, mo_ref, c_s, n_s, m_s = refs
    chunk = pl.program_id(1)
    pairs = ML_W // LANE
    zero = jnp.zeros((HEAD_DIM, HEAD_DIM), F32)

    ri = lax.broadcasted_iota(jnp.int32, (LANE, LANE), 0)
    ci = lax.broadcasted_iota(jnp.int32, (LANE, LANE), 1)
    eye = ri == ci
    first_block = (ri < HEAD_DIM) & (ci < HEAD_DIM)
    second_block = (ri >= HEAD_DIM) & (ci >= HEAD_DIM)
    first_rows = lax.broadcasted_iota(jnp.int32, (LANE, 1), 0) < HEAD_DIM

    @pl.when(chunk == 0)
    def _():
        if not has_init:
            c_s[...] = jnp.zeros_like(c_s)
            n_s[...] = jnp.zeros_like(n_s)
            m_s[...] = jnp.zeros_like(m_s)
            return
        for d in range(2):
            for p in range(pairs):
                top = jnp.concatenate([c0_ref[0, d, 2 * p], zero], axis=1)
                bot = jnp.concatenate([zero, c0_ref[0, d, 2 * p + 1]], axis=1)
                c_s[d * pairs + p] = jnp.concatenate([top, bot], axis=0).T
                e = d * ML_HEADS + 2 * p
                n_row = jnp.concatenate([n0_ref[0, e:e + 1, :], n0_ref[0, e + 1:e + 2, :]], axis=1)
                n_s[d * pairs + p] = jnp.sum(jnp.where(eye, n_row, 0.0), axis=1, keepdims=True)
            m_s[d] = m0_ref[0, d * ML_HEADS:(d + 1) * ML_HEADS, :]

    for d, (z_ref, h_ref) in enumerate(((zf_ref, hf_ref), (zb_ref, hb_ref))):
        gates = z_ref[:, OFF_ML_G - OFF_ML_Q:OFF_ML_G - OFF_ML_Q + LANE]
        ig_rows, b_rows, b_cols, feeds = _mlstm_gate_tables(gates, d == 0)
        u = gates - pltpu.roll(b_cols, LANE - ML_HEADS, axis=1)
        m_all = m_s[d]
        m_rows = []
        for p in range(pairs):
            k = z_ref[:, ML_W + p * LANE:ML_W + (p + 1) * LANE] * ATTN_SCALE
            qt = z_ref[:, p * LANE:(p + 1) * LANE].T
            vt = z_ref[:, 2 * ML_W + p * LANE:2 * ML_W + (p + 1) * LANE].T
            kt = k.T
            ct = c_s[d * pairs + p]
            n = n_s[d * pairs + p]
            res = []
            for half in range(2):
                hd = 2 * p + half
                gi = 2 * ML_HEADS * d + hd
                gf = gi + ML_HEADS
                res.append(_mlstm_head(k, qt, vt, kt, u[:, gi:gi + 1], b_rows[gf:gf + 1, :], ig_rows[gi:gi + 1, :],
                                       feeds, ct, n, m_all[hd:hd + 1, 0:1], half, d == 0))
            (h0, dl0, ca0, na0, mn0), (h1, dl1, ca1, na1, mn1) = res
            h_ref[:, p * LANE:(p + 1) * LANE] = (h0 + h1).T
            c_s[d * pairs + p] = jnp.where(first_block, dl0 * ct + ca0, jnp.where(second_block, dl1 * ct + ca1, 0.0))
            n_s[d * pairs + p] = jnp.where(first_rows, dl0 * n + na0, dl1 * n + na1)
            m_rows += [jnp.broadcast_to(mn0, (1, LANE)), jnp.broadcast_to(mn1, (1, LANE))]
        m_s[d] = jnp.concatenate(m_rows, axis=0)

    @pl.when(chunk == pl.num_programs(1) - 1)
    def _():
        for d in range(2):
            for p in range(pairs):
                c = c_s[d * pairs + p].T
                n_row = jnp.sum(jnp.where(eye, n_s[d * pairs + p], 0.0), axis=0, keepdims=True)
                co_ref[0, d, 2 * p] = c[:HEAD_DIM, :HEAD_DIM]
                co_ref[0, d, 2 * p + 1] = c[HEAD_DIM:, HEAD_DIM:]
                e = d * ML_HEADS + 2 * p
                no_ref[0, e:e + 1, :] = n_row[:, :HEAD_DIM]
                no_ref[0, e + 1:e + 2, :] = n_row[:, HEAD_DIM:]
            mo_ref[0, d * ML_HEADS:(d + 1) * ML_HEADS, :] = m_s[d]


def mlstm(z, batch, seq, init=None, layer=None):
    nc = seq // ML_CHUNK
    zcol = OFF_ML_Q // ML_BLOCK_W
    state_c = pl.BlockSpec((1, 2, ML_HEADS, HEAD_DIM, HEAD_DIM), lambda b, c: (b, 0, 0, 0, 0))
    state_n = pl.BlockSpec((1, 2 * ML_HEADS, HEAD_DIM), lambda b, c: (b, 0, 0))
    state_m = pl.BlockSpec((1, 2 * ML_HEADS, LANE), lambda b, c: (b, 0, 0))
    init_specs = [] if init is None else [
        pl.BlockSpec((1, None, 2, ML_HEADS, HEAD_DIM, HEAD_DIM), lambda b, c: (b, layer, 0, 0, 0, 0)),
        pl.BlockSpec((1, None, 2 * ML_HEADS, HEAD_DIM), lambda b, c: (b, layer, 0, 0)),
        pl.BlockSpec((1, None, 2 * ML_HEADS, LANE), lambda b, c: (b, layer, 0, 0)),
    ]
    pairs = ML_W // LANE
    return pl.pallas_call(
        functools.partial(_mlstm_kernel, has_init=init is not None),
        grid=(batch, nc),
        in_specs=[
            pl.BlockSpec((ML_CHUNK, ML_BLOCK_W), lambda b, c: (b * nc + c, zcol)),
            pl.BlockSpec((ML_CHUNK, ML_BLOCK_W), lambda b, c: (b * nc + nc - 1 - c, zcol)),
        ] + init_specs,
        out_specs=[
            pl.BlockSpec((ML_CHUNK, ML_W), lambda b, c: (b * nc + c, 0)),
            pl.BlockSpec((ML_CHUNK, ML_W), lambda b, c: (b * nc + nc - 1 - c, 0)),
            state_c, state_n, state_m,
        ],
        out_shape=[
            jax.ShapeDtypeStruct((batch * seq, ML_W), F32),
            jax.ShapeDtypeStruct((batch * seq, ML_W), F32),
            jax.ShapeDtypeStruct((batch, 2, ML_HEADS, HEAD_DIM, HEAD_DIM), F32),
            jax.ShapeDtypeStruct((batch, 2 * ML_HEADS, HEAD_DIM), F32),
            jax.ShapeDtypeStruct((batch, 2 * ML_HEADS, LANE), F32),
        ],
        scratch_shapes=[
            pltpu.VMEM((2 * pairs, LANE, LANE), F32),
            pltpu.VMEM((2 * pairs, LANE, 1), F32),
            pltpu.VMEM((2, ML_HEADS, LANE), F32),
        ],
        compiler_params=_cparams("arbitrary", "arbitrary"),
        name="mlstm",
    )(z, z, *(() if init is None else init))


def _layer_norm(u, g, b):
    mu = jnp.mean(u, axis=-1, keepdims=True)
    uc = u - mu
    var = jnp.mean(uc * uc, axis=-1, keepdims=True)
    return uc * lax.rsqrt(var + NORM_EPS) * g + b


def _outproj_kernel(x_ref, a_ref, hf_ref, hb_ref, og0_ref, og1_ref, c_ref, w_ref, mg_ref, g1_ref, sh2_ref, sc2_ref,
                    lng_ref, lnb_ref, rw_ref, x1_ref, h2_ref, lg_ref):
    low = _low_half(TM_OUT)
    mixed_b = []
    for p, og_ref in enumerate((og0_ref, og1_ref)):
        cols = slice(p * LANE, (p + 1) * LANE)
        h = hf_ref[:, cols] + hb_ref[:, cols]
        hc = h - _pair_mean(h, low)
        hn = hc * lax.rsqrt(_pair_mean(hc * hc, low) + NORM_EPS)
        mixed_b.append(hn * mg_ref[:, cols] * _sigmoid(og_ref[...]))
    mixed = jnp.concatenate([a_ref[...]] + [m.astype(BF16) for m in mixed_b] + [c_ref[...]], axis=-1)
    y = _dot(mixed, w_ref[...])
    x1 = _layer_norm(DEEPNORM_ALPHA * x_ref[...] + g1_ref[0] * y, lng_ref[...], lnb_ref[...])
    h2 = (x1 * (1.0 + sc2_ref[0]) + sh2_ref[0]).astype(BF16)
    x1_ref[...] = x1
    h2_ref[...] = h2
    lg_ref[...] = _dot_nt(rw_ref[...], h2)


def out_projection(x, out_a, hf, hb, z, out_c, layer, w_out, ml_gain, mods, mod_row, ln_g, ln_b, router_t):
    n = x.shape[0]
    mod = lambda k: pl.BlockSpec((None, 1, 1, D_MODEL), lambda i: (layer, mod_row(i), 0, k))
    row = lambda w: pl.BlockSpec((TM_OUT, w), lambda i: (i, 0))
    zcol = lambda off: pl.BlockSpec((TM_OUT, LANE), lambda i: (i, off // LANE))
    per_layer = lambda r, c: pl.BlockSpec((None, r, c), lambda i: (layer, 0, 0))
    norm0 = pl.BlockSpec((None, None, 1, D_MODEL), lambda i: (layer, 0, 0, 0))
    return pl.pallas_call(
        _outproj_kernel,
        grid=(n // TM_OUT,),
        in_specs=[row(D_MODEL), row(NA_W), row(ML_W), row(ML_W), zcol(OFF_ML_O), zcol(OFF_ML_O + LANE), row(GQA_W),
                  per_layer(D_MODEL, D_MODEL), per_layer(1, ML_W), mod(2), mod(3), mod(4), norm0, norm0,
                  per_layer(N_EXPERTS, D_MODEL)],
        out_specs=[row(D_MODEL), row(D_MODEL), pl.BlockSpec((N_EXPERTS, TM_OUT), lambda i: (0, i))],
        out_shape=[
            jax.ShapeDtypeStruct((n, D_MODEL), F32),
            jax.ShapeDtypeStruct((n, D_MODEL), BF16),
            jax.ShapeDtypeStruct((N_EXPERTS, n), F32),
        ],
        compiler_params=_cparams("arbitrary"),
        name="out_projection",
    )(x, out_a, hf, hb, z, z, out_c, w_out, ml_gain, mods, mods, mods, ln_g, ln_b, router_t)


ROUTE_BLK = 256


def _prefix_count(x, tri):
    n = x.shape[1]
    outs = []
    carry = jnp.zeros((N_EXPERTS, 1), F32)
    for i in range(n // ROUTE_BLK):
        blk = x[:, i * ROUTE_BLK:(i + 1) * ROUTE_BLK]
        outs.append(_dot(blk, tri) + carry)
        carry = carry + jnp.sum(blk.astype(F32), axis=1, keepdims=True)
    return jnp.concatenate(outs, axis=1) if len(outs) > 1 else outs[0]


def _route_kernel(lg_ref, coder_ref, codet_ref, afft_ref, *, n, cap):
    sets = lg_ref.shape[1] // n
    lg = lg_ref[...]
    ex = jnp.exp(lg - jnp.max(lg, axis=0, keepdims=True))
    aff_all = ex / jnp.sum(ex, axis=0, keepdims=True)
    affs = [aff_all[:, s * n:(s + 1) * n] for s in range(sets)]

    def search(i, bits):
        bit = lax.shift_left(jnp.int32(1), 30 - i)
        out = []
        for aff, b in zip(affs, bits):
            cand = b | bit
            cnt = jnp.sum(jnp.where(aff >= pltpu.bitcast(cand, F32), 1.0, 0.0), axis=1, keepdims=True)
            out.append(jnp.where(cnt >= cap, cand, b))
        return tuple(out)

    floors = lax.fori_loop(0, 31, search, tuple(jnp.zeros((N_EXPERTS, 1), jnp.int32) for _ in range(sets)))
    ti = lax.broadcasted_iota(jnp.int32, (ROUTE_BLK, ROUTE_BLK), 0)
    tj = lax.broadcasted_iota(jnp.int32, (ROUTE_BLK, ROUTE_BLK), 1)
    tri = jnp.where(ti < tj, 1.0, 0.0).astype(BF16)
    pad = jnp.zeros((LANE - N_EXPERTS, n), F32)
    for s, (aff, floor_bits) in enumerate(zip(affs, floors)):
        thr = jnp.min(jnp.where(aff >= pltpu.bitcast(floor_bits, F32), aff, 2.0), axis=1, keepdims=True)
        gt = aff > thr
        eq = aff == thr
        need = cap - jnp.sum(jnp.where(gt, 1.0, 0.0), axis=1, keepdims=True)
        eq_rank = _prefix_count(jnp.where(eq, 1.0, 0.0).astype(BF16), tri)
        sel = jnp.where(gt, 1.0, jnp.where(eq & (eq_rank < need), 1.0, 0.0))
        pos = _prefix_count(sel.astype(BF16), tri)
        code = jnp.where(sel > 0.5, pos, -1.0)
        coder_ref[s] = code
        codet_ref[s] = jnp.concatenate([code, pad], axis=0).T
        afft_ref[s] = jnp.concatenate([aff, pad], axis=0).T


def route(logits_t, n, cap):
    tokens = logits_t.shape[1]
    sets = tokens // n
    whole = lambda shape: pl.BlockSpec(shape, lambda i: (0,) * len(shape))
    return pl.pallas_call(
        functools.partial(_route_kernel, n=n, cap=cap),
        grid=(1,),
        in_specs=[whole((N_EXPERTS, tokens))],
        out_specs=[whole((sets, N_EXPERTS, n)), whole((sets, n, LANE)), whole((sets, n, LANE))],
        out_shape=[
            jax.ShapeDtypeStruct((sets, N_EXPERTS, n), F32),
            jax.ShapeDtypeStruct((sets, n, LANE), F32),
            jax.ShapeDtypeStruct((sets, n, LANE), F32),
        ],
        compiler_params=_cparams("arbitrary"),
        name="route",
    )(logits_t)


def _gather_kernel(coder_ref, h_ref, xe_ref, *, cap, epb):
    e0 = pl.program_id(1) * epb
    n = coder_ref.shape[2]
    ci = lax.broadcasted_iota(jnp.int32, (cap, n), 0).astype(F32)
    onehot = [jnp.where(ci == coder_ref[0, pl.ds(e0 + k, 1), :], 1.0, 0.0).astype(BF16) for k in range(epb)]
    onehot = jnp.concatenate(onehot, axis=0) if epb > 1 else onehot[0]
    rows = _dot(onehot, h_ref[...]).astype(BF16)
    for k in range(epb):
        xe_ref[k] = rows[k * cap:(k + 1) * cap]


def gather_tokens(code_rows, h2, n, cap):
    sets = code_rows.shape[0]
    epb = min(N_EXPERTS, max(1, 1024 // cap))
    return pl.pallas_call(
        functools.partial(_gather_kernel, cap=cap, epb=epb),
        grid=(sets, N_EXPERTS // epb),
        in_specs=[
            pl.BlockSpec((1, N_EXPERTS, n), lambda s, e: (s, 0, 0)),
            pl.BlockSpec((n, D_MODEL), lambda s, e: (s, 0)),
        ],
        out_specs=pl.BlockSpec((epb, cap, D_MODEL), lambda s, e: (e, s, 0)),
        out_shape=jax.ShapeDtypeStruct((N_EXPERTS, sets * cap, D_MODEL), BF16),
        compiler_params=_cparams("arbitrary", "arbitrary"),
        name="gather_tokens",
    )(code_rows, h2)


def _expert_kernel(xi_ref, xl_ref, wg_ref, wu_ref, wd_ref, yi_ref, yl_ref, acci_ref, accl_ref):
    j = pl.program_id(1)

    @pl.when(j == 0)
    def _():
        acci_ref[...] = jnp.zeros_like(acci_ref)
        accl_ref[...] = jnp.zeros_like(accl_ref)

    xs = (xi_ref[0], xl_ref[0])

    sums = [None, None]
    for s0 in range(0, EXP_TF, EXP_SUB):
        cols = slice(s0, min(s0 + EXP_SUB, EXP_TF))
        wg = wg_ref[0, 0, :, cols].astype(BF16)
        wu = wu_ref[0, 0, :, cols].astype(BF16)
        wd = wd_ref[0, 0, cols, :].astype(BF16)
        for g in range(2):
            a = _dot(xs[g], wg)
            b = _dot(xs[g], wu)
            hid = (a * _sigmoid(a) * b).astype(BF16)
            part = _dot(hid, wd)
            sums[g] = part if sums[g] is None else sums[g] + part
    acci_ref[...] += sums[0]
    accl_ref[...] += sums[1]

    @pl.when(j == pl.num_programs(1) - 1)
    def _():
        yi_ref[0] = acci_ref[...].astype(BF16)
        yl_ref[0] = accl_ref[...].astype(BF16)


def experts(xe_ctx, xe_lat, w_gate, w_up, w_down, layer):
    xin = lambda s: pl.BlockSpec((1, s, D_MODEL), lambda e, j: (e, 0, 0))
    return pl.pallas_call(
        _expert_kernel,
        grid=(N_EXPERTS, EXPERT_HIDDEN // EXP_TF),
        in_specs=[
            xin(SLOTS_CTX), xin(SLOTS_LAT),
            pl.BlockSpec((1, 1, D_MODEL, EXP_TF), lambda e, j: (layer, e, 0, j)),
            pl.BlockSpec((1, 1, D_MODEL, EXP_TF), lambda e, j: (layer, e, 0, j)),
            pl.BlockSpec((1, 1, EXP_TF, D_MODEL), lambda e, j: (layer, e, j, 0)),
        ],
        out_specs=[xin(SLOTS_CTX), xin(SLOTS_LAT)],
        out_shape=[
            jax.ShapeDtypeStruct((N_EXPERTS, SLOTS_CTX, D_MODEL), BF16),
            jax.ShapeDtypeStruct((N_EXPERTS, SLOTS_LAT, D_MODEL), BF16),
        ],
        scratch_shapes=[pltpu.VMEM((SLOTS_CTX, D_MODEL), F32), pltpu.VMEM((SLOTS_LAT, D_MODEL), F32)],
        compiler_params=_cparams("arbitrary", "arbitrary"),
        name="experts",
    )(xe_ctx, xe_lat, w_gate, w_up, w_down)


COMB_TT = 512


def _scatter_per_expert(code, aff, ye_ref, cap):
    tt = code.shape[0]
    li = lax.broadcasted_iota(jnp.int32, (tt, cap), 1).astype(F32)
    acc = jnp.zeros((tt, D_MODEL), F32)
    for e in range(N_EXPERTS):
        onehot = jnp.where(li == code[:, e:e + 1], 1.0, 0.0).astype(BF16)
        acc = acc + aff[:, e:e + 1] * _dot(onehot, ye_ref[e])
    return acc


def _scatter_merged(code, aff, ye_ref, cap):
    tt = code.shape[0]
    slots = N_EXPERTS * cap
    shift = cap.bit_length() - 1
    ei = lax.broadcasted_iota(jnp.int32, (LANE, slots), 0)
    si = lax.broadcasted_iota(jnp.int32, (LANE, slots), 1)
    expand = jnp.where(lax.shift_right_logical(si, shift) == ei, 1.0, 0.0).astype(BF16)
    slot = (lax.broadcasted_iota(jnp.int32, (tt, slots), 1) & (cap - 1)).astype(F32)
    hit = _dot(code.astype(BF16), expand) == slot
    a_hi = aff.astype(BF16)
    a_lo = (aff - a_hi.astype(F32)).astype(BF16)
    ye = ye_ref[...].reshape(slots, D_MODEL)
    acc = _dot(jnp.where(hit, _dot(a_hi, expand), 0.0).astype(BF16), ye)
    return acc + _dot(jnp.where(hit, _dot(a_lo, expand), 0.0).astype(BF16), ye)


def _combine_kernel(code_ref, aff_ref, ye_ref, x1_ref, g2_ref, lng_ref, lnb_ref, o_ref, *, cap):
    scatter = _scatter_merged if cap < LANE else _scatter_per_expert
    acc = scatter(code_ref[0], aff_ref[0], ye_ref, cap)
    u = DEEPNORM_ALPHA * x1_ref[...] + g2_ref[0] * acc
    o_ref[...] = _layer_norm(u, lng_ref[...], lnb_ref[...])


def combine(code, aff, ye, x1, layer, mods, set_mod_row, ln_g, ln_b, n, cap):
    sets = code.shape[0]
    tt = min(COMB_TT, n)
    per = n // tt
    tok = pl.BlockSpec((1, tt, LANE), lambda s, i: (s, i, 0))
    norm1 = pl.BlockSpec((None, None, 1, D_MODEL), lambda s, i: (layer, 1, 0, 0))
    return pl.pallas_call(
        functools.partial(_combine_kernel, cap=cap),
        grid=(sets, per),
        in_specs=[
            tok, tok,
            pl.BlockSpec((N_EXPERTS, cap, D_MODEL), lambda s, i: (0, s, 0)),
            pl.BlockSpec((tt, D_MODEL), lambda s, i: (s * per + i, 0)),
            pl.BlockSpec((None, 1, 1, D_MODEL), lambda s, i: (layer, set_mod_row(s), 0, 5)),
            norm1, norm1,
        ],
        out_specs=pl.BlockSpec((tt, D_MODEL), lambda s, i: (s * per + i, 0)),
        out_shape=jax.ShapeDtypeStruct((sets * n, D_MODEL), F32),
        compiler_params=_cparams("arbitrary", "arbitrary"),
        name="combine",
    )(code, aff, ye, x1, mods, ln_g, ln_b)


def _axial_rope_tables():
    t = np.arange(DEC_SEQ)
    row = (t // GRID_W).astype(np.float32)
    col = (t % GRID_W).astype(np.float32)
    half = HEAD_DIM // 2
    inv = (ROPE_THETA ** (-np.arange(0, half, 2, dtype=np.float32) / half)).astype(np.float32)
    ang_r = row[:, None] * inv
    ang_c = col[:, None] * inv
    ang = np.concatenate([ang_r, ang_r, ang_c, ang_c] * 2, -1)
    return jnp.asarray(np.cos(ang), F32), jnp.asarray(np.sin(ang), F32)


def _ctx_mod_row(i):
    return 0


def _lat_tile_mod_row(i):
    return 1 + i // (DEC_SEQ // TM)


def _lat_out_tile_mod_row(i):
    return 1 + i // (DEC_SEQ // TM_OUT)


def _lat_set_mod_row(s):
    return 1 + s


def kernel(x_prompt, x_sample, c, cache_na_k, cache_na_v, cache_gqa_k, cache_gqa_v, state_mlstm_c, state_mlstm_n,
           state_mlstm_m, c_ctx, ada_w, ada_b, w_in, b_gate, w_out, na_rpb, qk_norm_g, ml_norm_g, ln_g, ln_b,
           router_w, w_gate, w_up, w_down):
    cond8 = jnp.concatenate([c_ctx[None, :], c, jnp.zeros((8 - 1 - DEC_BATCH, D_MODEL), F32)], 0)
    mods = adaln(cond8, ada_w, ada_b).reshape(DEPTH, 8, 1, 6 * D_MODEL)
    xc = x_prompt.reshape(N_CTX_TOK, D_MODEL)
    xl = x_sample.reshape(N_LAT_TOK, D_MODEL)
    cos, sin = _axial_rope_tables()
    na_tab = na_bias_table(na_rpb)

    split = OFF_ML_G + N_GATES
    w_pad = jnp.concatenate([w_in[:, :, :split], jnp.zeros((DEPTH, D_MODEL, LANE - N_GATES), F32),
                             w_in[:, :, split:]], 2).astype(BF16)
    bias_row = jnp.pad(b_gate, ((0, 0), (OFF_ML_G, PROJ_PAD - OFF_ML_G - N_GATES))).reshape(DEPTH, 1, PROJ_PAD)
    gains = jnp.tile(qk_norm_g, (1, 1, 2)).reshape(DEPTH, 2, 1, LANE)
    w_out_b = w_out.astype(BF16)
    router_t = jnp.swapaxes(router_w, 1, 2).astype(BF16)
    ml_gain = ml_norm_g.reshape(DEPTH, 1, ML_W)
    ln_g4 = ln_g.reshape(DEPTH, 2, 1, D_MODEL)
    ln_b4 = ln_b.reshape(DEPTH, 2, 1, D_MODEL)
    na_ck = cache_na_k.reshape(DEC_BATCH, DEPTH, PAST_LEN, NA_W)
    na_cv = cache_na_v.reshape(DEC_BATCH, DEPTH, PAST_LEN, NA_W)
    gq_ck = cache_gqa_k.reshape(DEC_BATCH, DEPTH, PAST_LEN, GQA_KV_W)
    gq_cv = cache_gqa_v.reshape(DEC_BATCH, DEPTH, PAST_LEN, GQA_KV_W)
    lat_n0 = state_mlstm_n.reshape(DEC_BATCH, DEPTH, 2 * ML_HEADS, HEAD_DIM)
    lat_m0 = jnp.broadcast_to(state_mlstm_m.reshape(DEC_BATCH, DEPTH, 2 * ML_HEADS, 1),
                              (DEC_BATCH, DEPTH, 2 * ML_HEADS, LANE))
    kv, states = None, []

    for l in range(DEPTH):
        zc, *kv = in_projection(xc, l, mods, _ctx_mod_row, w_pad, bias_row, gains, kv_prev=kv)
        zl = in_projection(xl, l, mods, _lat_tile_mod_row, w_pad, bias_row, gains, cos, sin)

        out_a_c, out_c_c = ctx_attention(zc)
        hf_c, hb_c, sc, sn, sm = mlstm(zc, BATCH, SEQ)
        out_a_l = lat_na(zl, na_ck, na_cv, na_tab, l)
        out_c_l = lat_gqa(zl, gq_ck, gq_cv, l)
        hf_l, hb_l = mlstm(zl, DEC_BATCH, DEC_SEQ, (state_mlstm_c, lat_n0, lat_m0), l)[:2]
        states.append((sc, sn, sm))

        x1_c, h2_c, lg_c = out_projection(xc, out_a_c, hf_c, hb_c, zc, out_c_c, l, w_out_b, ml_gain, mods,
                                          _ctx_mod_row, ln_g4, ln_b4, router_t)
        x1_l, h2_l, lg_l = out_projection(xl, out_a_l, hf_l, hb_l, zl, out_c_l, l, w_out_b, ml_gain, mods,
                                          _lat_out_tile_mod_row, ln_g4, ln_b4, router_t)
        crow_c, code_c, aff_c = route(lg_c, SEQ, CAP_CTX)
        crow_l, code_l, aff_l = route(lg_l, DEC_SEQ, CAP_LAT)
        xe_c = gather_tokens(crow_c, h2_c, SEQ, CAP_CTX)
        xe_l = gather_tokens(crow_l, h2_l, DEC_SEQ, CAP_LAT)
        ye_c, ye_l = experts(xe_c, xe_l, w_gate, w_up, w_down, l)
        xc = combine(code_c, aff_c, ye_c, x1_c, l, mods, _ctx_mod_row, ln_g4, ln_b4, SEQ, CAP_CTX)
        xl = combine(code_l, aff_l, ye_l, x1_l, l, mods, _lat_set_mod_row, ln_g4, ln_b4, DEC_SEQ, CAP_LAT)

    y_prompt = xc.reshape(BATCH, SEQ, D_MODEL)
    y_sample = xl.reshape(DEC_BATCH, DEC_SEQ, D_MODEL)
    na_kv, gq_kv = kv
    heads = lambda a, h: a.reshape(BATCH, DEPTH, SEQ, h, HEAD_DIM)
    new_c = jnp.stack([s[0] for s in states], 1)
    new_n = jnp.stack([s[1] for s in states], 1).reshape(BATCH, DEPTH, 2, ML_HEADS, HEAD_DIM)
    new_m = jnp.stack([s[2][:, :, 0] for s in states], 1).reshape(BATCH, DEPTH, 2, ML_HEADS)
    return (y_prompt, y_sample, heads(na_kv[..., :NA_W], NA_HEADS), heads(na_kv[..., NA_W:], NA_HEADS),
            heads(gq_kv[..., :GQA_KV_W], GQA_KV_HEADS), heads(gq_kv[..., GQA_KV_W:], GQA_KV_HEADS),
            new_c, new_n, new_m)
```
